```python
import jax, jax.numpy as jnp
from jax import lax
import numpy as np

D_MODEL = 1024
BATCH = 8
SEQ = 2048
DEPTH = 4
DEC_BATCH = 16
DEC_SEQ = 32
PAST_LEN = 4096

CHUNK = 64
N_META = 16
HA = D_MODEL // 128
DK = 128
DV = 128
QK_W = HA * DK
V_W = HA * DV
QKV_W = 2 * QK_W + V_W
LRU_W = D_MODEL
NB = D_MODEL // 128
BW = LRU_W // NB
CONV_W = 4
LRU_C = 8.0
D_FF = ((8 * D_MODEL // 3 + 127) // 128) * 128
EPS = 1e-6

OFF_Z = QKV_W
OFF_BETA = OFF_Z + V_W
OFF_ALPHA = OFF_BETA + HA
OFF_LX = OFF_ALPHA + HA
OFF_LY = OFF_LX + LRU_W
OFF_GA = OFF_LY + LRU_W
OFF_GB = OFF_GA + D_MODEL
IN_COLS = OFF_GB + D_MODEL

kernel_name = "hybrid_gdn_rglru_macaron_stream_step"


def _rmsnorm(x, w):
    xf = x.astype(jnp.float32)
    y = xf * lax.rsqrt(jnp.mean(xf * xf, axis=-1, keepdims=True) + EPS)
    return (y * w.astype(jnp.float32)).astype(x.dtype)


def _l2norm(t):
    return t * lax.rsqrt(jnp.sum(t * t, axis=-1, keepdims=True) + EPS)


def _swiglu(xn, w_gu, w_down):
    gate, up = jnp.split(xn @ w_gu, 2, axis=-1)
    return (jax.nn.silu(gate) * up) @ w_down


def _causal_conv(x, buf, w):
    L = x.shape[1]
    xp = jnp.concatenate([buf.astype(x.dtype), x], axis=1)
    y = xp[:, 0:L] * w[0]
    for j in range(1, CONV_W):
        y = y + xp[:, j:j + L] * w[j]
    return y, xp[:, -(CONV_W - 1):]


def _gated_delta_rule(q, k, v, g, beta, S0, chunk):
    B, H, L, _ = q.shape
    n = L // chunk
    rs = lambda t: t.reshape((B, H, n, chunk) + t.shape[3:])
    q, k, v, g, beta = rs(q), rs(k), rs(v), rs(g), rs(beta)
    g = jnp.cumsum(g, axis=-1)
    tri_incl = jnp.tril(jnp.ones((chunk, chunk), bool))
    tri_strict = jnp.tril(jnp.ones((chunk, chunk), bool), -1)
    diff = g[..., :, None] - g[..., None, :]
    decay = jnp.where(tri_incl, jnp.exp(jnp.where(tri_incl, diff, 0.0)), 0.0)
    k_beta = k * beta[..., None]
    v_beta = v * beta[..., None]
    low = jnp.where(tri_strict, jnp.einsum('bhnid,bhnjd->bhnij', k_beta, k) * decay, 0.0)
    eye = jnp.eye(chunk, dtype=q.dtype)
    T = lax.linalg.triangular_solve(eye + low, jnp.broadcast_to(eye, low.shape),
                                    left_side=True, lower=True)
    u = jnp.einsum('bhnij,bhnjd->bhnid', T, v_beta)
    w = jnp.einsum('bhnij,bhnjd->bhnid', T, k_beta * jnp.exp(g)[..., None])
    a_intra = jnp.where(tri_incl, jnp.einsum('bhnid,bhnjd->bhnij', q, k) * decay, 0.0)
    q_dec = q * jnp.exp(g)[..., None]
    k_dec = k * jnp.exp(g[..., -1:] - g)[..., None]
    g_last = jnp.exp(g[..., -1])

    def step(S, xs):
        u_i, w_i, qd_i, kd_i, a_i, gl_i = xs
        v_new = u_i - jnp.einsum('bhck,bhkv->bhcv', w_i, S)
        o = jnp.einsum('bhck,bhkv->bhcv', qd_i, S) + jnp.einsum('bhij,bhjv->bhiv', a_i, v_new)
        S = S * gl_i[..., None, None] + jnp.einsum('bhck,bhcv->bhkv', kd_i, v_new)
        return S, o

    mv = lambda t: jnp.moveaxis(t, 2, 0)
    S, o = lax.scan(step, S0, (mv(u), mv(w), mv(q_dec), mv(k_dec), mv(a_intra), mv(g_last)))
    o = jnp.moveaxis(o, 0, 2).reshape(B, H, L, v.shape[-1])
    return o, S


def _gated_delta_branch(proj, S0, conv_buf, chunk, conv_w, A_log, dt_bias, out_norm):
    f32 = jnp.float32
    B, L, _ = proj.shape
    qkv, new_buf = _causal_conv(proj[..., :QKV_W], conv_buf, conv_w)
    qkv = jax.nn.silu(qkv.astype(f32))
    q = _l2norm(qkv[..., :QK_W].reshape(B, L, HA, DK)) * (DK ** -0.5)
    k = _l2norm(qkv[..., QK_W:2 * QK_W].reshape(B, L, HA, DK))
    v = qkv[..., 2 * QK_W:].reshape(B, L, HA, DV)
    beta = jax.nn.sigmoid(proj[..., OFF_BETA:OFF_ALPHA].astype(f32))
    g = -jnp.exp(A_log.astype(f32)) * jax.nn.softplus(
        proj[..., OFF_ALPHA:OFF_LX].astype(f32) + dt_bias.astype(f32))
    pad = (-L) % chunk

    def prep(t):
        t = jnp.moveaxis(t, 2, 1)
        return jnp.pad(t, ((0, 0), (0, 0), (pad, 0)) + ((0, 0),) * (t.ndim - 3))

    o, S = _gated_delta_rule(prep(q), prep(k), prep(v), prep(g), prep(beta), S0.astype(f32), chunk)
    o = jnp.moveaxis(o[:, :, pad:], 1, 2)
    z = proj[..., OFF_Z:OFF_BETA].astype(f32).reshape(B, L, HA, DV)
    o = o * lax.rsqrt(jnp.mean(o * o, axis=-1, keepdims=True) + EPS) * out_norm.astype(f32) * jax.nn.silu(z)
    return o.reshape(B, L, V_W).astype(proj.dtype), S, new_buf


def _linear_scan(a, b, h0):
    b = b.at[:, 0].add(a[:, 0] * h0)

    def comb(x, y):
        return x[0] * y[0], y[0] * x[1] + y[1]

    _, h = lax.associative_scan(comb, (a, b), axis=1)
    return h


def _rglru_branch(proj, h0, conv_buf, conv_w, conv_b, w_r, b_r, w_i, b_i, lam):
    f32 = jnp.float32
    B, L, _ = proj.shape
    xc, new_buf = _causal_conv(proj[..., OFF_LX:OFF_LY], conv_buf, conv_w)
    xc = xc.astype(f32) + conv_b.astype(f32)
    xb = xc.reshape(B, L, NB, BW)
    r = jax.nn.sigmoid(jnp.einsum('blnc,ncd->blnd', xb, w_r.astype(f32)).reshape(B, L, LRU_W) + b_r.astype(f32))
    i = jax.nn.sigmoid(jnp.einsum('blnc,ncd->blnd', xb, w_i.astype(f32)).reshape(B, L, LRU_W) + b_i.astype(f32))
    log_a = -LRU_C * r * jax.nn.softplus(-lam.astype(f32))
    a = jnp.exp(log_a)
    mult = jnp.sqrt(-jnp.expm1(2.0 * log_a))
    h = _linear_scan(a, mult * (i * xc), h0.astype(f32))
    y = h * jax.nn.gelu(proj[..., OFF_LY:OFF_GA].astype(f32))
    return y.astype(proj.dtype), h[:, -1], new_buf


def _trunk(x, S_all, cq_all, h_all, cx_all, chunk, weights):
    (ffn1_norm, ffn1_w_gu, ffn1_w_down, mix_norm, w_in, delta_conv_w, delta_A_log, delta_dt_bias,
     delta_out_norm, lru_conv_w, lru_conv_b, lru_w_r, lru_b_r, lru_w_i, lru_b_i, lru_lambda,
     w_branch_a, w_branch_b, w_out, ffn2_norm, ffn2_w_gu, ffn2_w_down) = weights
    new_S, new_cq, new_h, new_cx = [], [], [], []
    for l in range(DEPTH):
        x = x + 0.5 * _swiglu(_rmsnorm(x, ffn1_norm[l]), ffn1_w_gu[l], ffn1_w_down[l])
        proj = _rmsnorm(x, mix_norm[l]) @ w_in[l]
        o_a, S, cq = _gated_delta_branch(proj, S_all[l], cq_all[l], chunk, delta_conv_w[l],
                                         delta_A_log[l], delta_dt_bias[l], delta_out_norm[l])
        o_b, h, cx = _rglru_branch(proj, h_all[l], cx_all[l], lru_conv_w[l], lru_conv_b[l],
                                   lru_w_r[l], lru_b_r[l], lru_w_i[l], lru_b_i[l], lru_lambda[l])
        gate_a = jax.nn.sigmoid(proj[..., OFF_GA:OFF_GB].astype(jnp.float32))
        gate_b = jax.nn.sigmoid(proj[..., OFF_GB:IN_COLS].astype(jnp.float32))
        merged = gate_a * (o_a @ w_branch_a[l]) + gate_b * (o_b @ w_branch_b[l])
        x = x + merged.astype(x.dtype) @ w_out[l]
        x = x + 0.5 * _swiglu(_rmsnorm(x, ffn2_norm[l]), ffn2_w_gu[l], ffn2_w_down[l])
        new_S.append(S)
        new_cq.append(cq)
        new_h.append(h)
        new_cx.append(cx)
    return (x, jnp.stack(new_S).astype(S_all.dtype), jnp.stack(new_cq).astype(cq_all.dtype),
            jnp.stack(new_h).astype(h_all.dtype), jnp.stack(new_cx).astype(cx_all.dtype))


def setup_inputs(seed: int = 0) -> dict:
    key = jax.random.key(seed)
    ks = iter(jax.random.split(key, 32))
    f32 = jnp.float32
    nrm = lambda shape, scale: jax.random.normal(next(ks), shape, f32) * scale
    gain = lambda shape: 1.0 + nrm(shape, 0.02)
    lam_u = jax.random.uniform(next(ks), (DEPTH, LRU_W), f32, 0.9, 0.999)
    dt = jnp.exp(jax.random.uniform(next(ks), (DEPTH, HA), f32, np.log(1e-3), np.log(1e-1)))
    return {
        "x_prompt": nrm((BATCH, SEQ, D_MODEL), 1.0),
        "x_sample": nrm((DEC_BATCH, DEC_SEQ, D_MODEL), 1.0),
        "state_delta_S": nrm((DEPTH, DEC_BATCH, HA, DK, DV), 0.5),
        "state_delta_conv": nrm((DEPTH, DEC_BATCH, CONV_W - 1, QKV_W), 1.0),
        "state_lru_h": nrm((DEPTH, DEC_BATCH, LRU_W), 0.5),
        "state_lru_conv": nrm((DEPTH, DEC_BATCH, CONV_W - 1, LRU_W), 1.0),
        "meta_tokens": nrm((N_META, D_MODEL), 1.0),
        "ffn1_norm": gain((DEPTH, D_MODEL)),
        "ffn1_w_gu": nrm((DEPTH, D_MODEL, 2 * D_FF), D_MODEL ** -0.5),
        "ffn1_w_down": nrm((DEPTH, D_FF, D_MODEL), D_FF ** -0.5),
        "mix_norm": gain((DEPTH, D_MODEL)),
        "w_in": nrm((DEPTH, D_MODEL, IN_COLS), D_MODEL ** -0.5),
        "delta_conv_w": nrm((DEPTH, CONV_W, QKV_W), CONV_W ** -0.5),
        "delta_A_log": jnp.log(jax.random.uniform(next(ks), (DEPTH, HA), f32, 1.0, 16.0)),
        "delta_dt_bias": dt + jnp.log(-jnp.expm1(-dt)),
        "delta_out_norm": gain((DEPTH, DV)),
        "lru_conv_w": nrm((DEPTH, CONV_W, LRU_W), CONV_W ** -0.5),
        "lru_conv_b": nrm((DEPTH, LRU_W), 0.01),
        "lru_w_r": nrm((DEPTH, NB, BW, BW), BW ** -0.5),
        "lru_b_r": nrm((DEPTH, LRU_W), 0.01),
        "lru_w_i": nrm((DEPTH, NB, BW, BW), BW ** -0.5),
        "lru_b_i": nrm((DEPTH, LRU_W), 0.01),
        "lru_lambda": jnp.log(lam_u) - jnp.log1p(-lam_u),
        "w_branch_a": nrm((DEPTH, V_W, D_MODEL), V_W ** -0.5),
        "w_branch_b": nrm((DEPTH, LRU_W, D_MODEL), LRU_W ** -0.5),
        "w_out": nrm((DEPTH, D_MODEL, D_MODEL), D_MODEL ** -0.5),
        "ffn2_norm": gain((DEPTH, D_MODEL)),
        "ffn2_w_gu": nrm((DEPTH, D_MODEL, 2 * D_FF), D_MODEL ** -0.5),
        "ffn2_w_down": nrm((DEPTH, D_FF, D_MODEL), D_FF ** -0.5),
        "final_norm": gain((D_MODEL,)),
    }


def reference(x_prompt, x_sample, state_delta_S, state_delta_conv, state_lru_h, state_lru_conv,
              meta_tokens, ffn1_norm, ffn1_w_gu, ffn1_w_down, mix_norm, w_in, delta_conv_w,
              delta_A_log, delta_dt_bias, delta_out_norm, lru_conv_w, lru_conv_b, lru_w_r, lru_b_r,
              lru_w_i, lru_b_i, lru_lambda, w_branch_a, w_branch_b, w_out, ffn2_norm, ffn2_w_gu,
              ffn2_w_down, final_norm):
    weights = (ffn1_norm, ffn1_w_gu, ffn1_w_down, mix_norm, w_in, delta_conv_w, delta_A_log,
               delta_dt_bias, delta_out_norm, lru_conv_w, lru_conv_b, lru_w_r, lru_b_r, lru_w_i,
               lru_b_i, lru_lambda, w_branch_a, w_branch_b, w_out, ffn2_norm, ffn2_w_gu, ffn2_w_down)
    dt = x_prompt.dtype
    B = x_prompt.shape[0]
    meta = jnp.broadcast_to(meta_tokens.astype(dt)[None], (B, N_META, D_MODEL))
    xp = jnp.concatenate([meta, x_prompt], axis=1)
    xp, p_S, p_cq, p_h, p_cx = _trunk(
        xp,
        jnp.zeros((DEPTH, B, HA, DK, DV), dt),
        jnp.zeros((DEPTH, B, CONV_W - 1, QKV_W), dt),
        jnp.zeros((DEPTH, B, LRU_W), dt),
        jnp.zeros((DEPTH, B, CONV_W - 1, LRU_W), dt),
        CHUNK, weights)
    y_prompt = _rmsnorm(xp, final_norm)[:, N_META:]
    xs, s_S, s_cq, s_h, s_cx = _trunk(x_sample, state_delta_S, state_delta_conv, state_lru_h,
                                      state_lru_conv, x_sample.shape[1], weights)
    y_sample = _rmsnorm(xs, final_norm)
    return (y_prompt, y_sample, p_S, p_cq, p_h, p_cx, s_S, s_cq, s_h, s_cx)
```

```python
import functools

import jax
import jax.numpy as jnp
from jax import lax
from jax.experimental import pallas as pl
from jax.experimental.pallas import tpu as pltpu

F32 = jnp.float32
BF16 = jnp.bfloat16

D_MODEL = 1024
DEPTH = 4
N_META = 16
HA = 8
DK = 128
DV = 128
QK_W = HA * DK
V_W = HA * DV
QKV_W = 2 * QK_W + V_W
LRU_W = D_MODEL
NB = 8
BW = 128
CONV_W = 4
LRU_C = 8.0
D_FF = 2816
EPS = 1e-6

OFF_Z = QKV_W
OFF_BETA = OFF_Z + V_W
OFF_ALPHA = OFF_BETA + HA
OFF_LX = OFF_ALPHA + HA
OFF_LY = OFF_LX + LRU_W
OFF_GA = OFF_LY + LRU_W
OFF_GB = OFF_GA + D_MODEL
IN_COLS = OFF_GB + D_MODEL

CHUNK = 64
INV_BASE = 16
ROW_TILE = 512
FF_HALF = D_FF // 2
P_COLS = 8192
VMEM_LIMIT = 56 * 1024 * 1024


def _rms(x, w):
    return x * lax.rsqrt(jnp.mean(x * x, axis=-1, keepdims=True) + EPS) * w


def _mm(a, b):
    return jnp.dot(a.astype(BF16), b.astype(BF16), preferred_element_type=F32)


def _mm_nt(a, b):
    return lax.dot_general(a.astype(BF16), b.astype(BF16), (((1,), (1,)), ((), ())),
                           preferred_element_type=F32)


def _mm_tn(a, b):
    return lax.dot_general(a.astype(BF16), b.astype(BF16), (((0,), (0,)), ((), ())),
                           preferred_element_type=F32)


def _sigmoid(x):
    return 1.0 / (1.0 + jnp.exp(-x))


def _silu(x):
    return x * _sigmoid(x)


def _softplus(x):
    return jnp.maximum(x, 0.0) + jnp.log1p(jnp.exp(-jnp.abs(x)))


def _gelu_tanh(x):
    return 0.5 * x * (1.0 + jnp.tanh(0.7978845608028654 * (x + 0.044715 * (x * x * x))))


def _ffn(x, nw, wgu_ref, wd_ref, h_ref):
    xn = _rms(x, nw).astype(BF16)
    for c in range(2):
        lo = c * FF_HALF
        g = jnp.dot(xn, wgu_ref[:, lo:lo + FF_HALF], preferred_element_type=F32)
        u = jnp.dot(xn, wgu_ref[:, D_FF + lo:D_FF + lo + FF_HALF], preferred_element_type=F32)
        h_ref[:, lo:lo + FF_HALF] = (_silu(g) * u).astype(BF16)
    y = jnp.dot(h_ref[...], wd_ref[...], preferred_element_type=F32)
    return x + 0.5 * y


def _ffn_kernel(x_ref, nw_ref, wgu_ref, wd_ref, o_ref, h_ref):
    o_ref[...] = _ffn(x_ref[...], nw_ref[...], wgu_ref, wd_ref, h_ref)


def _inproj_kernel(x_ref, nw_ref, w_ref, wba_ref, p_ref, ba_ref):
    xn = _rms(x_ref[...], nw_ref[...]).astype(BF16)
    for c in range(P_COLS // 1024):
        lo = c * 1024
        p_ref[:, lo:lo + 1024] = jnp.dot(
            xn, w_ref[:, lo:lo + 1024], preferred_element_type=F32).astype(BF16)
    ba_ref[...] = jnp.dot(xn, wba_ref[...], preferred_element_type=F32)


def _outproj_kernel(final, x_ref, g_ref, oa_ref, ob_ref, wa_ref, wb_ref, wo_ref,
                    nw_ref, wgu_ref, wd_ref, fn_ref, o_ref, h_ref):
    ga = _sigmoid(g_ref[:, :D_MODEL].astype(F32))
    gb = _sigmoid(g_ref[:, D_MODEL:].astype(F32))
    ma = jnp.dot(oa_ref[...], wa_ref[...], preferred_element_type=F32)
    mb = jnp.dot(ob_ref[...], wb_ref[...], preferred_element_type=F32)
    m = (ga * ma + gb * mb).astype(BF16)
    x = x_ref[...] + jnp.dot(m, wo_ref[...], preferred_element_type=F32)
    x = _ffn(x, nw_ref[...], wgu_ref, wd_ref, h_ref)
    if final:
        x = _rms(x, fn_ref[...])
    o_ref[...] = x


def _resident(shape):
    nd = len(shape)
    return pl.BlockSpec(shape, lambda i: (0,) * nd, pipeline_mode=pl.Buffered(1))


def _token_params():
    return pltpu.CompilerParams(dimension_semantics=("parallel",),
                                vmem_limit_bytes=VMEM_LIMIT)


def _ffn_call(x, nw, wgu, wd):
    n = x.shape[0]
    row = lambda i: (i, 0)
    return pl.pallas_call(
        _ffn_kernel,
        grid=(n // ROW_TILE,),
        in_specs=[pl.BlockSpec((ROW_TILE, D_MODEL), row),
                  _resident((1, D_MODEL)),
                  _resident((D_MODEL, 2 * D_FF)),
                  _resident((D_FF, D_MODEL))],
        out_specs=pl.BlockSpec((ROW_TILE, D_MODEL), row),
        out_shape=jax.ShapeDtypeStruct((n, D_MODEL), F32),
        scratch_shapes=[pltpu.VMEM((ROW_TILE, D_FF), BF16)],
        compiler_params=_token_params(),
        name="ffn",
    )(x, nw, wgu, wd)


def _inproj_call(x, nw, w, wba):
    n = x.shape[0]
    row = lambda i: (i, 0)
    return pl.pallas_call(
        _inproj_kernel,
        grid=(n // ROW_TILE,),
        in_specs=[pl.BlockSpec((ROW_TILE, D_MODEL), row),
                  _resident((1, D_MODEL)),
                  _resident((D_MODEL, P_COLS)),
                  _resident((D_MODEL, 256))],
        out_specs=[pl.BlockSpec((ROW_TILE, P_COLS), row),
                   pl.BlockSpec((ROW_TILE, 256), row)],
        out_shape=[jax.ShapeDtypeStruct((n, P_COLS), BF16),
                   jax.ShapeDtypeStruct((n, 256), F32)],
        compiler_params=_token_params(),
        name="inproj",
    )(x, nw, w, wba)


def _outproj_call(x, p, oa, ob, wa, wb, wo, nw, wgu, wd, fn, final):
    n = x.shape[0]
    row = lambda i: (i, 0)
    return pl.pallas_call(
        functools.partial(_outproj_kernel, final),
        grid=(n // ROW_TILE,),
        in_specs=[pl.BlockSpec((ROW_TILE, D_MODEL), row),
                  pl.BlockSpec((ROW_TILE, 2 * D_MODEL), lambda i: (i, 3)),
                  pl.BlockSpec((ROW_TILE, V_W), row),
                  pl.BlockSpec((ROW_TILE, LRU_W), row),
                  _resident((V_W, D_MODEL)),
                  _resident((LRU_W, D_MODEL)),
                  _resident((D_MODEL, D_MODEL)),
                  _resident((1, D_MODEL)),
                  _resident((D_MODEL, 2 * D_FF)),
                  _resident((D_FF, D_MODEL)),
                  _resident((1, D_MODEL))],
        out_specs=pl.BlockSpec((ROW_TILE, D_MODEL), row),
        out_shape=jax.ShapeDtypeStruct((n, D_MODEL), F32),
        scratch_shapes=[pltpu.VMEM((ROW_TILE, D_FF), BF16)],
        compiler_params=_token_params(),
        name="outproj_ffn",
    )(x, p, oa, ob, wa, wb, wo, nw, wgu, wd, fn)


def _unit_lower_inverse(low, c):
    ri = lax.broadcasted_iota(jnp.int32, (c, c), 0)
    ci = lax.broadcasted_iota(jnp.int32, (c, c), 1)
    same_block = (ri // INV_BASE) == (ci // INV_BASE)
    eye = (ri == ci).astype(F32)
    ld = jnp.where(same_block, low, 0.0)
    lo = low - ld
    x = eye - ld
    p = ld
    span = 2
    while span < INV_BASE:
        p = _mm(p, p)
        x = x + _mm(x, p)
        span *= 2
    e = _mm(x, lo)
    y = eye - e
    q = e
    span = 2
    while span < c // INV_BASE:
        q = _mm(q, q)
        y = y + _mm(y, q)
        span *= 2
    return _mm(y, x)


def _shift_rows(x, d, fill):
    rows = lax.broadcasted_iota(jnp.int32, x.shape, 0)
    return jnp.where(rows >= d, pltpu.roll(x, d, 0), fill)


def _cumsum_rows(x, c):
    d = 1
    while d < c:
        x = x + _shift_rows(x, d, 0.0)
        d *= 2
    return x


def _linear_scan_rows(a, b, h0, c):
    d = 1
    while d < c:
        b = b + a * _shift_rows(b, d, 0.0)
        a = a * _shift_rows(a, d, 1.0)
        d *= 2
    return b + a * h0


def _mixer_kernel(n_prompt_steps, chunks_per_prompt,
                  qkv_ref, z_ref, lx_ref, ly_ref, ba_ref,
                  s0_ref, cq0_ref, h0_ref, cx0_ref,
                  cw_ref, alog_ref, dtb_ref, onorm_ref,
                  lcw_ref, lcb_ref, wri_ref, br_ref, bi_ref, lam_ref,
                  oa_ref, ob_ref, sout_ref, cqout_ref, hout_ref, cxout_ref,
                  s_scr, xq_scr, h_scr, xl_scr):
    c = CHUNK
    i = pl.program_id(0)
    is_prompt = i < n_prompt_steps
    pos = i % chunks_per_prompt
    first = jnp.logical_or(jnp.logical_not(is_prompt), pos == 0)
    last = jnp.logical_or(jnp.logical_not(is_prompt), pos == chunks_per_prompt - 1)
    prompt_null = c - N_META
    sample_null = c // 2
    null = jnp.where(first, jnp.where(is_prompt, prompt_null, sample_null), 0)

    xq_scr[8:8 + c, :] = qkv_ref[...].astype(F32)
    xl_scr[8:8 + c, :] = lx_ref[...].astype(F32)

    @pl.when(jnp.logical_and(first, is_prompt))
    def _():
        s_scr[...] = jnp.zeros_like(s_scr)
        h_scr[...] = jnp.zeros_like(h_scr)
        xq_scr[0:8 + prompt_null, :] = jnp.zeros((8 + prompt_null, QKV_W), F32)
        xl_scr[0:8 + prompt_null, :] = jnp.zeros((8 + prompt_null, LRU_W), F32)

    @pl.when(jnp.logical_not(is_prompt))
    def _():
        s_scr[...] = s0_ref[0]
        h_scr[...] = jnp.broadcast_to(h0_ref[0], h_scr.shape)
        xq_scr[0:8 + sample_null, :] = jnp.zeros((8 + sample_null, QKV_W), F32)
        xl_scr[0:8 + sample_null, :] = jnp.zeros((8 + sample_null, LRU_W), F32)
        xq_scr[8 + sample_null - 3:8 + sample_null, :] = cq0_ref[0]
        xl_scr[8 + sample_null - 3:8 + sample_null, :] = cx0_ref[0]

    rows1 = lax.broadcasted_iota(jnp.int32, (c, 1), 0)
    valid = rows1 >= null

    conv = xq_scr[5:5 + c, :] * cw_ref[0:1, :]
    for j in range(1, CONV_W):
        conv = conv + xq_scr[5 + j:5 + j + c, :] * cw_ref[j:j + 1, :]
    act = jnp.where(valid, _silu(conv), 0.0)

    ba = ba_ref[...]
    beta_all = jnp.where(valid, _sigmoid(ba[:, :128]), 0.0)
    g_all = -jnp.exp(alog_ref[...]) * _softplus(ba[:, 128:] + dtb_ref[...])
    g_all = jnp.where(valid, g_all, 0.0)
    gc_all = _cumsum_rows(g_all, c)
    g_last = gc_all[c - 1:c, :]
    eg_all = jnp.exp(gc_all)
    ekd_all = jnp.exp(g_last - gc_all)
    egl_all = jnp.exp(g_last)
    gc_t = jnp.concatenate([gc_all, jnp.zeros((128 - c, 128), F32)], axis=0).T

    ri = lax.broadcasted_iota(jnp.int32, (c, c), 0)
    ci = lax.broadcasted_iota(jnp.int32, (c, c), 1)
    tri_incl = ri >= ci
    tri_strict = ri > ci

    for h in range(HA):
        qs = act[:, h * DK:(h + 1) * DK]
        ks = act[:, QK_W + h * DK:QK_W + (h + 1) * DK]
        v = act[:, 2 * QK_W + h * DV:2 * QK_W + (h + 1) * DV]
        q = qs * lax.rsqrt(jnp.sum(qs * qs, axis=-1, keepdims=True) + EPS) * (DK ** -0.5)
        k = ks * lax.rsqrt(jnp.sum(ks * ks, axis=-1, keepdims=True) + EPS)
        beta = beta_all[:, h:h + 1]
        gcol = gc_all[:, h:h + 1]
        grow = gc_t[h:h + 1, :c]
        decay = jnp.where(tri_incl, jnp.exp(jnp.where(tri_incl, gcol - grow, 0.0)), 0.0)
        kb = k * beta
        sc = _mm_nt(jnp.concatenate([kb, q], axis=0), k)
        low = jnp.where(tri_strict, sc[:c] * decay, 0.0)
        a_intra = jnp.where(tri_incl, sc[c:] * decay, 0.0)
        t_inv = _unit_lower_inverse(low, c)
        eg = eg_all[:, h:h + 1]
        uw = _mm(t_inv, jnp.concatenate([v * beta, kb * eg], axis=1))
        u = uw[:, :DV]
        w = uw[:, DV:]
        qd = q * eg
        kd = k * ekd_all[:, h:h + 1]
        s_old = s_scr[h]
        wq = _mm(jnp.concatenate([w, qd], axis=0), s_old)
        v_new = u - wq[:c]
        o = wq[c:] + _mm(a_intra, v_new)
        s_scr[h] = s_old * egl_all[:, h:h + 1] + _mm_tn(kd, v_new)
        zh = z_ref[:, h * DV:(h + 1) * DV].astype(F32)
        o = o * lax.rsqrt(jnp.mean(o * o, axis=-1, keepdims=True) + EPS) * onorm_ref[...] * _silu(zh)
        oa_ref[:, h * DV:(h + 1) * DV] = o.astype(oa_ref.dtype)

    xc = xl_scr[5:5 + c, :] * lcw_ref[0:1, :]
    for j in range(1, CONV_W):
        xc = xc + xl_scr[5 + j:5 + j + c, :] * lcw_ref[j:j + 1, :]
    xc = xc + lcb_ref[...]
    sp = _softplus(-lam_ref[...])
    for n in range(NB):
        sl = slice(n * BW, (n + 1) * BW)
        xn = xc[:, sl]
        gates = _mm(xn, wri_ref[n])
        r = _sigmoid(gates[:, :BW] + br_ref[:, sl])
        ig = _sigmoid(gates[:, BW:] + bi_ref[:, sl])
        log_a = -LRU_C * r * sp[:, sl]
        a = jnp.where(valid, jnp.exp(log_a), 1.0)
        th = jnp.tanh(log_a)
        mult = jnp.sqrt(-2.0 * th / (1.0 - th))
        b = jnp.where(valid, mult * (ig * xn), 0.0)
        hs = _linear_scan_rows(a, b, h_scr[0:1, sl], c)
        ob_ref[:, sl] = (hs * _gelu_tanh(ly_ref[:, sl].astype(F32))).astype(ob_ref.dtype)
        h_scr[:, sl] = jnp.broadcast_to(hs[c - 1:c, :], (8, BW))

    @pl.when(last)
    def _():
        sout_ref[0] = s_scr[...]
        cqout_ref[0] = xq_scr[8 + c - 3:8 + c, :]
        hout_ref[0] = h_scr[0:1, :]
        cxout_ref[0] = xl_scr[8 + c - 3:8 + c, :]

    xq_scr[0:8, :] = xq_scr[c:c + 8, :]
    xl_scr[0:8, :] = xl_scr[c:c + 8, :]


def _mixer_call(p, ba, s0, cq0, h0, cx0, cw, alog, dtb, onorm, lcw, lcb, wri, br, bi, lam,
                n_prompt, n_sample, chunks_per_prompt):
    n_prompt_steps = n_prompt * chunks_per_prompt
    steps = n_prompt_steps + n_sample
    n_seq = n_prompt + n_sample
    c = CHUNK

    def seq(i):
        return jnp.where(i < n_prompt_steps, i // chunks_per_prompt, i - n_prompt_steps + n_prompt)

    def sample_seq(i):
        return jnp.maximum(i - n_prompt_steps, 0)

    const2 = lambda i: (0, 0)
    in_specs = [
        pl.BlockSpec((c, QKV_W), lambda i: (i, 0)),
        pl.BlockSpec((c, V_W), lambda i: (i, 3)),
        pl.BlockSpec((c, LRU_W), lambda i: (i, 4)),
        pl.BlockSpec((c, LRU_W), lambda i: (i, 5)),
        pl.BlockSpec((c, 256), lambda i: (i, 0)),
        pl.BlockSpec((1, HA, DK, DV), lambda i: (sample_seq(i), 0, 0, 0)),
        pl.BlockSpec((1, CONV_W - 1, QKV_W), lambda i: (sample_seq(i), 0, 0)),
        pl.BlockSpec((1, 1, LRU_W), lambda i: (sample_seq(i), 0, 0)),
        pl.BlockSpec((1, CONV_W - 1, LRU_W), lambda i: (sample_seq(i), 0, 0)),
        pl.BlockSpec((CONV_W, QKV_W), const2),
        pl.BlockSpec((1, 128), const2),
        pl.BlockSpec((1, 128), const2),
        pl.BlockSpec((1, DV), const2),
        pl.BlockSpec((CONV_W, LRU_W), const2),
        pl.BlockSpec((1, LRU_W), const2),
        pl.BlockSpec((NB, BW, 2 * BW), lambda i: (0, 0, 0)),
        pl.BlockSpec((1, LRU_W), const2),
        pl.BlockSpec((1, LRU_W), const2),
        pl.BlockSpec((1, LRU_W), const2),
    ]
    out_specs = [
        pl.BlockSpec((c, V_W), lambda i: (i, 0)),
        pl.BlockSpec((c, LRU_W), lambda i: (i, 0)),
        pl.BlockSpec((1, HA, DK, DV), lambda i: (seq(i), 0, 0, 0)),
        pl.BlockSpec((1, CONV_W - 1, QKV_W), lambda i: (seq(i), 0, 0)),
        pl.BlockSpec((1, 1, LRU_W), lambda i: (seq(i), 0, 0)),
        pl.BlockSpec((1, CONV_W - 1, LRU_W), lambda i: (seq(i), 0, 0)),
    ]
    n = steps * c
    out_shape = [
        jax.ShapeDtypeStruct((n, V_W), BF16),
        jax.ShapeDtypeStruct((n, LRU_W), BF16),
        jax.ShapeDtypeStruct((n_seq, HA, DK, DV), F32),
        jax.ShapeDtypeStruct((n_seq, CONV_W - 1, QKV_W), F32),
        jax.ShapeDtypeStruct((n_seq, 1, LRU_W), F32),
        jax.ShapeDtypeStruct((n_seq, CONV_W - 1, LRU_W), F32),
    ]
    return pl.pallas_call(
        functools.partial(_mixer_kernel, n_prompt_steps, chunks_per_prompt),
        grid=(steps,),
        in_specs=in_specs,
        out_specs=out_specs,
        out_shape=out_shape,
        scratch_shapes=[pltpu.VMEM((HA, DK, DV), F32),
                        pltpu.VMEM((c + 8, QKV_W), F32),
                        pltpu.VMEM((8, LRU_W), F32),
                        pltpu.VMEM((c + 8, LRU_W), F32)],
        compiler_params=pltpu.CompilerParams(dimension_semantics=("arbitrary",),
                                             vmem_limit_bytes=VMEM_LIMIT),
        name="mixer",
    )(p, p, p, p, ba, s0, cq0, h0, cx0, cw, alog, dtb, onorm, lcw, lcb, wri, br, bi, lam)


def _pad_lanes(v, width):
    return jnp.pad(v, ((0, 0), (0, width - v.shape[-1])))


def kernel(x_prompt, x_sample, state_delta_S, state_delta_conv, state_lru_h, state_lru_conv,
           meta_tokens, ffn1_norm, ffn1_w_gu, ffn1_w_down, mix_norm, w_in, delta_conv_w,
           delta_A_log, delta_dt_bias, delta_out_norm, lru_conv_w, lru_conv_b, lru_w_r, lru_b_r,
           lru_w_i, lru_b_i, lru_lambda, w_branch_a, w_branch_b, w_out, ffn2_norm, ffn2_w_gu,
           ffn2_w_down, final_norm):
    n_prompt, seq_len, _ = x_prompt.shape
    n_sample, dec_len, _ = x_sample.shape
    assert dec_len == CHUNK // 2 and (N_META + seq_len) % CHUNK == N_META
    prompt_rows = CHUNK - N_META + N_META + seq_len
    chunks_per_prompt = prompt_rows // CHUNK
    dt = x_prompt.dtype

    meta = jnp.broadcast_to(meta_tokens.astype(dt)[None], (n_prompt, N_META, D_MODEL))
    xp = jnp.concatenate([jnp.zeros((n_prompt, CHUNK - N_META, D_MODEL), dt), meta, x_prompt], axis=1)
    xs = jnp.concatenate([jnp.zeros((n_sample, CHUNK - dec_len, D_MODEL), dt), x_sample], axis=1)
    x = jnp.concatenate([xp.reshape(-1, D_MODEL), xs.reshape(-1, D_MODEL)], axis=0).astype(F32)
    n_rows = x.shape[0]
    assert n_rows % ROW_TILE == 0

    wgu1 = ffn1_w_gu.astype(BF16)
    wd1 = ffn1_w_down.astype(BF16)
    wgu2 = ffn2_w_gu.astype(BF16)
    wd2 = ffn2_w_down.astype(BF16)
    w_main = jnp.concatenate([w_in[:, :, :OFF_BETA], w_in[:, :, OFF_LX:]], axis=-1).astype(BF16)
    w_ba = jnp.concatenate(
        [jnp.pad(w_in[:, :, OFF_BETA:OFF_ALPHA], ((0, 0), (0, 0), (0, 128 - HA))),
         jnp.pad(w_in[:, :, OFF_ALPHA:OFF_LX], ((0, 0), (0, 0), (0, 128 - HA)))], axis=-1).astype(BF16)
    wa = w_branch_a.astype(BF16)
    wb = w_branch_b.astype(BF16)
    wo = w_out.astype(BF16)
    w_ri = jnp.concatenate([lru_w_r, lru_w_i], axis=-1).astype(BF16)

    outs_s, outs_cq, outs_h, outs_cx = [], [], [], []
    for l in range(DEPTH):
        x = _ffn_call(x, ffn1_norm[l][None].astype(F32), wgu1[l], wd1[l])
        p, ba = _inproj_call(x, mix_norm[l][None].astype(F32), w_main[l], w_ba[l])
        oa, ob, s_new, cq_new, h_new, cx_new = _mixer_call(
            p, ba,
            state_delta_S[l].astype(F32), state_delta_conv[l].astype(F32),
            state_lru_h[l][:, None, :].astype(F32), state_lru_conv[l].astype(F32),
            delta_conv_w[l].astype(F32),
            _pad_lanes(delta_A_log[l][None].astype(F32), 128),
            _pad_lanes(delta_dt_bias[l][None].astype(F32), 128),
            delta_out_norm[l][None].astype(F32),
            lru_conv_w[l].astype(F32), lru_conv_b[l][None].astype(F32), w_ri[l],
            lru_b_r[l][None].astype(F32), lru_b_i[l][None].astype(F32),
            lru_lambda[l][None].astype(F32),
            n_prompt, n_sample, chunks_per_prompt)
        x = _outproj_call(x, p, oa, ob, wa[l], wb[l], wo[l], ffn2_norm[l][None].astype(F32),
                          wgu2[l], wd2[l], final_norm[None].astype(F32), final=(l == DEPTH - 1))
        outs_s.append(s_new)
        outs_cq.append(cq_new)
        outs_h.append(h_new[:, 0, :])
        outs_cx.append(cx_new)

    n_prompt_rows = n_prompt * prompt_rows
    y_prompt = x[:n_prompt_rows].reshape(n_prompt, prompt_rows, D_MODEL)[:, CHUNK:]
    y_sample = x[n_prompt_rows:].reshape(n_sample, CHUNK, D_MODEL)[:, CHUNK - dec_len:]
    s_all = jnp.stack(outs_s)
    cq_all = jnp.stack(outs_cq)
    h_all = jnp.stack(outs_h)
    cx_all = jnp.stack(outs_cx)
    sd = state_delta_S.dtype
    return (y_prompt.astype(dt), y_sample.astype(dt),
            s_all[:, :n_prompt].astype(dt), cq_all[:, :n_prompt].astype(dt),
            h_all[:, :n_prompt].astype(dt), cx_all[:, :n_prompt].astype(dt),
            s_all[:, n_prompt:].astype(sd), cq_all[:, n_prompt:].astype(state_delta_conv.dtype),
            h_all[:, n_prompt:].astype(state_lru_h.dtype), cx_all[:, n_prompt:].astype(state_lru_conv.dtype))
```

```python
import functools

import jax
import jax.numpy as jnp
from jax import lax
from jax.experimental import pallas as pl
from jax.experimental.pallas import tpu as pltpu

F32 = jnp.float32
BF16 = jnp.bfloat16

D_MODEL = 1024
DEPTH = 4
N_META = 16
HA = 8
DK = 128
DV = 128
QK_W = HA * DK
V_W = HA * DV
QKV_W = 2 * QK_W + V_W
LRU_W = D_MODEL
NB = 8
BW = 128
CONV_W = 4
LRU_C = 8.0
D_FF = 2816
EPS = 1e-6

OFF_Z = QKV_W
OFF_BETA = OFF_Z + V_W
OFF_ALPHA = OFF_BETA + HA
OFF_LX = OFF_ALPHA + HA
OFF_LY = OFF_LX + LRU_W
OFF_GA = OFF_LY + LRU_W
OFF_GB = OFF_GA + D_MODEL
IN_COLS = OFF_GB + D_MODEL

CHUNK = 64
INV_BASE = 16
ROW_TILE = 512
FF_HALF = D_FF // 2
P_COLS = 8192
VMEM_LIMIT = 56 * 1024 * 1024


def _rms(x, w):
    return x * lax.rsqrt(jnp.mean(x * x, axis=-1, keepdims=True) + EPS) * w


def _mm(a, b):
    return jnp.dot(a.astype(BF16), b.astype(BF16), preferred_element_type=F32)


def _mm_nt(a, b):
    return lax.dot_general(a.astype(BF16), b.astype(BF16), (((1,), (1,)), ((), ())),
                           preferred_element_type=F32)


def _mm_tn(a, b):
    return lax.dot_general(a.astype(BF16), b.astype(BF16), (((0,), (0,)), ((), ())),
                           preferred_element_type=F32)


def _sigmoid(x):
    return 1.0 / (1.0 + jnp.exp(-x))


def _silu(x):
    return x * _sigmoid(x)


def _softplus(x):
    return jnp.maximum(x, 0.0) + jnp.log1p(jnp.exp(-jnp.abs(x)))


def _gelu_tanh(x):
    return 0.5 * x * (1.0 + jnp.tanh(0.7978845608028654 * (x + 0.044715 * (x * x * x))))


def _ffn(x, nw, wgu_ref, wd_ref, h_ref):
    xn = _rms(x, nw).astype(BF16)
    for c in range(2):
        lo = c * FF_HALF
        g = jnp.dot(xn, wgu_ref[:, lo:lo + FF_HALF], preferred_element_type=F32)
        u = jnp.dot(xn, wgu_ref[:, D_FF + lo:D_FF + lo + FF_HALF], preferred_element_type=F32)
        h_ref[:, lo:lo + FF_HALF] = (_silu(g) * u).astype(BF16)
    y = jnp.dot(h_ref[...], wd_ref[...], preferred_element_type=F32)
    return x + 0.5 * y


def _ffn_kernel(x_ref, nw_ref, wgu_ref, wd_ref, o_ref, h_ref):
    o_ref[...] = _ffn(x_ref[...], nw_ref[...], wgu_ref, wd_ref, h_ref)


def _inproj_kernel(x_ref, nw_ref, w_ref, wba_ref, p_ref, ba_ref):
    xn = _rms(x_ref[...], nw_ref[...]).astype(BF16)
    for c in range(P_COLS // 1024):
        lo = c * 1024
        p_ref[:, lo:lo + 1024] = jnp.dot(
            xn, w_ref[:, lo:lo + 1024], preferred_element_type=F32).astype(BF16)
    ba_ref[...] = jnp.dot(xn, wba_ref[...], preferred_element_type=F32)


def _outproj_kernel(final, x_ref, g_ref, oa_ref, ob_ref, wa_ref, wb_ref, wo_ref,
                    nw_ref, wgu_ref, wd_ref, fn_ref, o_ref, h_ref):
    ga = _sigmoid(g_ref[:, :D_MODEL].astype(F32))
    gb = _sigmoid(g_ref[:, D_MODEL:].astype(F32))
    ma = jnp.dot(oa_ref[...], wa_ref[...], preferred_element_type=F32)
    mb = jnp.dot(ob_ref[...], wb_ref[...], preferred_element_type=F32)
    m = (ga * ma + gb * mb).astype(BF16)
    x = x_ref[...] + jnp.dot(m, wo_ref[...], preferred_element_type=F32)
    x = _ffn(x, nw_ref[...], wgu_ref, wd_ref, h_ref)
    if final:
        x = _rms(x, fn_ref[...])
    o_ref[...] = x


def _resident(shape):
    nd = len(shape)
    return pl.BlockSpec(shape, lambda i: (0,) * nd, pipeline_mode=pl.Buffered(1))


def _token_params():
    return pltpu.CompilerParams(dimension_semantics=("parallel",),
                                vmem_limit_bytes=VMEM_LIMIT)


def _ffn_call(x, nw, wgu, wd):
    n = x.shape[0]
    row = lambda i: (i, 0)
    return pl.pallas_call(
        _ffn_kernel,
        grid=(n // ROW_TILE,),
        in_specs=[pl.BlockSpec((ROW_TILE, D_MODEL), row),
                  _resident((1, D_MODEL)),
                  _resident((D_MODEL, 2 * D_FF)),
                  _resident((D_FF, D_MODEL))],
        out_specs=pl.BlockSpec((ROW_TILE, D_MODEL), row),
        out_shape=jax.ShapeDtypeStruct((n, D_MODEL), F32),
        scratch_shapes=[pltpu.VMEM((ROW_TILE, D_FF), BF16)],
        compiler_params=_token_params(),
        name="ffn",
    )(x, nw, wgu, wd)


def _inproj_call(x, nw, w, wba):
    n = x.shape[0]
    row = lambda i: (i, 0)
    return pl.pallas_call(
        _inproj_kernel,
        grid=(n // ROW_TILE,),
        in_specs=[pl.BlockSpec((ROW_TILE, D_MODEL), row),
                  _resident((1, D_MODEL)),
                  _resident((D_MODEL, P_COLS)),
                  _resident((D_MODEL, 256))],
        out_specs=[pl.BlockSpec((ROW_TILE, P_COLS), row),
                   pl.BlockSpec((ROW_TILE, 256), row)],
        out_shape=[jax.ShapeDtypeStruct((n, P_COLS), BF16),
                   jax.ShapeDtypeStruct((n, 256), F32)],
        compiler_params=_token_params(),
        name="inproj",
    )(x, nw, w, wba)


def _outproj_call(x, p, oa, ob, wa, wb, wo, nw, wgu, wd, fn, final):
    n = x.shape[0]
    row = lambda i: (i, 0)
    return pl.pallas_call(
        functools.partial(_outproj_kernel, final),
        grid=(n // ROW_TILE,),
        in_specs=[pl.BlockSpec((ROW_TILE, D_MODEL), row),
                  pl.BlockSpec((ROW_TILE, 2 * D_MODEL), lambda i: (i, 3)),
                  pl.BlockSpec((ROW_TILE, V_W), row),
                  pl.BlockSpec((ROW_TILE, LRU_W), row),
                  _resident((V_W, D_MODEL)),
                  _resident((LRU_W, D_MODEL)),
                  _resident((D_MODEL, D_MODEL)),
                  _resident((1, D_MODEL)),
                  _resident((D_MODEL, 2 * D_FF)),
                  _resident((D_FF, D_MODEL)),
                  _resident((1, D_MODEL))],
        out_specs=pl.BlockSpec((ROW_TILE, D_MODEL), row),
        out_shape=jax.ShapeDtypeStruct((n, D_MODEL), F32),
        scratch_shapes=[pltpu.VMEM((ROW_TILE, D_FF), BF16)],
        compiler_params=_token_params(),
        name="outproj_ffn",
    )(x, p, oa, ob, wa, wb, wo, nw, wgu, wd, fn)


def _unit_lower_inverse(lows, c):
    ri = lax.broadcasted_iota(jnp.int32, (c, c), 0)
    ci = lax.broadcasted_iota(jnp.int32, (c, c), 1)
    same_block = (ri // INV_BASE) == (ci // INV_BASE)
    eye = (ri == ci).astype(F32)
    ld = [jnp.where(same_block, low, 0.0) for low in lows]
    lo = [low - d for low, d in zip(lows, ld)]
    x = [eye - d for d in ld]
    p = ld
    span = 2
    while span < INV_BASE:
        p = [_mm(t, t) for t in p]
        x = [a + _mm(a, t) for a, t in zip(x, p)]
        span *= 2
    e = [_mm(a, b) for a, b in zip(x, lo)]
    y = [eye - t for t in e]
    q = e
    span = 2
    while span < c // INV_BASE:
        q = [_mm(t, t) for t in q]
        y = [a + _mm(a, t) for a, t in zip(y, q)]
        span *= 2
    return [_mm(a, b) for a, b in zip(y, x)]


def _shift_rows(x, d, fill):
    if d % 8 == 0:
        return jnp.concatenate([jnp.full((d, x.shape[1]), fill, x.dtype), x[:x.shape[0] - d]], axis=0)
    rows = lax.broadcasted_iota(jnp.int32, x.shape, 0)
    return jnp.where(rows >= d, pltpu.roll(x, d, 0), fill)


def _cumsum_rows(x, c):
    d = 1
    while d < c:
        x = x + _shift_rows(x, d, 0.0)
        d *= 2
    return x


def _linear_scan_rows(a, b, h0, c):
    d = 1
    while d < c:
        b = b + a * _shift_rows(b, d, 0.0)
        a = a * _shift_rows(a, d, 1.0)
        d *= 2
    return b + a * h0


def _mixer_kernel(n_prompt_steps, chunks_per_prompt,
                  qkv_ref, z_ref, lx_ref, ly_ref, ba_ref,
                  s0_ref, cq0_ref, h0_ref, cx0_ref,
                  cw_ref, alog_ref, dtb_ref, onorm_ref,
                  lcw_ref, lcb_ref, wri_ref, br_ref, bi_ref, lam_ref,
                  oa_ref, ob_ref, sout_ref, cqout_ref, hout_ref, cxout_ref,
                  s_scr, xq_scr, h_scr, xl_scr):
    c = CHUNK
    i = pl.program_id(0)
    is_prompt = i < n_prompt_steps
    pos = i % chunks_per_prompt
    first = jnp.logical_or(jnp.logical_not(is_prompt), pos == 0)
    last = jnp.logical_or(jnp.logical_not(is_prompt), pos == chunks_per_prompt - 1)
    prompt_null = c - N_META
    sample_null = c // 2
    null = jnp.where(first, jnp.where(is_prompt, prompt_null, sample_null), 0)

    xq_scr[8:8 + c, :] = qkv_ref[...].astype(F32)
    xl_scr[8:8 + c, :] = lx_ref[...].astype(F32)

    @pl.when(jnp.logical_and(first, is_prompt))
    def _():
        s_scr[...] = jnp.zeros_like(s_scr)
        h_scr[...] = jnp.zeros_like(h_scr)
        xq_scr[0:8 + prompt_null, :] = jnp.zeros((8 + prompt_null, QKV_W), F32)
        xl_scr[0:8 + prompt_null, :] = jnp.zeros((8 + prompt_null, LRU_W), F32)

    @pl.when(jnp.logical_not(is_prompt))
    def _():
        s_scr[...] = s0_ref[0]
        h_scr[...] = jnp.broadcast_to(h0_ref[0], h_scr.shape)
        xq_scr[0:8 + sample_null, :] = jnp.zeros((8 + sample_null, QKV_W), F32)
        xl_scr[0:8 + sample_null, :] = jnp.zeros((8 + sample_null, LRU_W), F32)
        xq_scr[8 + sample_null - 3:8 + sample_null, :] = cq0_ref[0]
        xl_scr[8 + sample_null - 3:8 + sample_null, :] = cx0_ref[0]

    rows1 = lax.broadcasted_iota(jnp.int32, (c, 1), 0)
    valid = rows1 >= null

    conv = xq_scr[5:5 + c, :] * cw_ref[0:1, :]
    for j in range(1, CONV_W):
        conv = conv + xq_scr[5 + j:5 + j + c, :] * cw_ref[j:j + 1, :]
    act = jnp.where(valid, _silu(conv), 0.0)

    ba = ba_ref[...]
    beta_all = jnp.where(valid, _sigmoid(ba[:, :128]), 0.0)
    g_all = -jnp.exp(alog_ref[...]) * _softplus(ba[:, 128:] + dtb_ref[...])
    g_all = jnp.where(valid, g_all, 0.0)
    gc_all = _cumsum_rows(g_all, c)
    g_last = gc_all[c - 1:c, :]
    eg_all = jnp.exp(gc_all)
    ekd_all = jnp.exp(g_last - gc_all)
    egl_all = jnp.exp(g_last)
    gc_t = jnp.concatenate([gc_all, jnp.zeros((128 - c, 128), F32)], axis=0).T

    ri = lax.broadcasted_iota(jnp.int32, (c, c), 0)
    ci = lax.broadcasted_iota(jnp.int32, (c, c), 1)
    tri_incl = ri >= ci
    tri_strict = ri > ci

    heads = range(HA)
    qs = [act[:, h * DK:(h + 1) * DK] for h in heads]
    ks = [act[:, QK_W + h * DK:QK_W + (h + 1) * DK] for h in heads]
    v = [act[:, 2 * QK_W + h * DV:2 * QK_W + (h + 1) * DV] for h in heads]
    q = [t * (lax.rsqrt(jnp.sum(t * t, axis=-1, keepdims=True) + EPS) * (DK ** -0.5)) for t in qs]
    k = [t * lax.rsqrt(jnp.sum(t * t, axis=-1, keepdims=True) + EPS) for t in ks]
    beta = [beta_all[:, h:h + 1] for h in heads]
    eg = [eg_all[:, h:h + 1] for h in heads]
    kb = [k[h] * beta[h] for h in heads]
    sc = [_mm_nt(jnp.concatenate([kb[h], q[h]], axis=0), k[h]) for h in heads]
    decay = [jnp.where(tri_incl, jnp.exp(jnp.where(tri_incl, gc_all[:, h:h + 1] - gc_t[h:h + 1, :c], 0.0)), 0.0)
             for h in heads]
    low = [jnp.where(tri_strict, sc[h][:c] * decay[h], 0.0) for h in heads]
    a_intra = [sc[h][c:] * decay[h] for h in heads]
    t_inv = _unit_lower_inverse(low, c)
    uw = [_mm(t_inv[h], jnp.concatenate([v[h] * beta[h], kb[h] * eg[h]], axis=1)) for h in heads]
    s_old = [s_scr[h] for h in heads]
    wq = [_mm(jnp.concatenate([uw[h][:, DV:], q[h] * eg[h]], axis=0), s_old[h]) for h in heads]
    v_new = [uw[h][:, :DV] - wq[h][:c] for h in heads]
    o_intra = [_mm(a_intra[h], v_new[h]) for h in heads]
    ds = [_mm_tn(k[h] * ekd_all[:, h:h + 1], v_new[h]) for h in heads]
    for h in heads:
        s_scr[h] = s_old[h] * egl_all[:, h:h + 1] + ds[h]
        o = wq[h][c:] + o_intra[h]
        zh = z_ref[:, h * DV:(h + 1) * DV].astype(F32)
        o = o * lax.rsqrt(jnp.mean(o * o, axis=-1, keepdims=True) + EPS) * onorm_ref[...] * _silu(zh)
        oa_ref[:, h * DV:(h + 1) * DV] = o.astype(oa_ref.dtype)

    xc = xl_scr[5:5 + c, :] * lcw_ref[0:1, :]
    for j in range(1, CONV_W):
        xc = xc + xl_scr[5 + j:5 + j + c, :] * lcw_ref[j:j + 1, :]
    xc = xc + lcb_ref[...]
    sp = _softplus(-lam_ref[...])
    for n in range(NB):
        sl = slice(n * BW, (n + 1) * BW)
        xn = xc[:, sl]
        gates = _mm(xn, wri_ref[n])
        r = _sigmoid(gates[:, :BW] + br_ref[:, sl])
        ig = _sigmoid(gates[:, BW:] + bi_ref[:, sl])
        log_a = -LRU_C * r * sp[:, sl]
        a = jnp.where(valid, jnp.exp(log_a), 1.0)
        th = jnp.tanh(log_a)
        mult = jnp.sqrt(-2.0 * th / (1.0 - th))
        b = jnp.where(valid, mult * (ig * xn), 0.0)
        hs = _linear_scan_rows(a, b, h_scr[0:1, sl], c)
        ob_ref[:, sl] = (hs * _gelu_tanh(ly_ref[:, sl].astype(F32))).astype(ob_ref.dtype)
        h_scr[:, sl] = jnp.broadcast_to(hs[c - 1:c, :], (8, BW))

    @pl.when(last)
    def _():
        sout_ref[0] = s_scr[...]
        cqout_ref[0] = xq_scr[8 + c - 3:8 + c, :]
        hout_ref[0] = h_scr[0:1, :]
        cxout_ref[0] = xl_scr[8 + c - 3:8 + c, :]

    xq_scr[0:8, :] = xq_scr[c:c + 8, :]
    xl_scr[0:8, :] = xl_scr[c:c + 8, :]


def _mixer_call(p, ba, s0, cq0, h0, cx0, cw, alog, dtb, onorm, lcw, lcb, wri, br, bi, lam,
                n_prompt, n_sample, chunks_per_prompt):
    n_prompt_steps = n_prompt * chunks_per_prompt
    steps = n_prompt_steps + n_sample
    n_seq = n_prompt + n_sample
    c = CHUNK

    def seq(i):
        return jnp.where(i < n_prompt_steps, i // chunks_per_prompt, i - n_prompt_steps + n_prompt)

    def sample_seq(i):
        return jnp.maximum(i - n_prompt_steps, 0)

    const2 = lambda i: (0, 0)
    in_specs = [
        pl.BlockSpec((c, QKV_W), lambda i: (i, 0)),
        pl.BlockSpec((c, V_W), lambda i: (i, 3)),
        pl.BlockSpec((c, LRU_W), lambda i: (i, 4)),
        pl.BlockSpec((c, LRU_W), lambda i: (i, 5)),
        pl.BlockSpec((c, 256), lambda i: (i, 0)),
        pl.BlockSpec((1, HA, DK, DV), lambda i: (sample_seq(i), 0, 0, 0)),
        pl.BlockSpec((1, CONV_W - 1, QKV_W), lambda i: (sample_seq(i), 0, 0)),
        pl.BlockSpec((1, 1, LRU_W), lambda i: (sample_seq(i), 0, 0)),
        pl.BlockSpec((1, CONV_W - 1, LRU_W), lambda i: (sample_seq(i), 0, 0)),
        pl.BlockSpec((CONV_W, QKV_W), const2),
        pl.BlockSpec((1, 128), const2),
        pl.BlockSpec((1, 128), const2),
        pl.BlockSpec((1, DV), const2),
        pl.BlockSpec((CONV_W, LRU_W), const2),
        pl.BlockSpec((1, LRU_W), const2),
        pl.BlockSpec((NB, BW, 2 * BW), lambda i: (0, 0, 0)),
        pl.BlockSpec((1, LRU_W), const2),
        pl.BlockSpec((1, LRU_W), const2),
        pl.BlockSpec((1, LRU_W), const2),
    ]
    out_specs = [
        pl.BlockSpec((c, V_W), lambda i: (i, 0)),
        pl.BlockSpec((c, LRU_W), lambda i: (i, 0)),
        pl.BlockSpec((1, HA, DK, DV), lambda i: (seq(i), 0, 0, 0)),
        pl.BlockSpec((1, CONV_W - 1, QKV_W), lambda i: (seq(i), 0, 0)),
        pl.BlockSpec((1, 1, LRU_W), lambda i: (seq(i), 0, 0)),
        pl.BlockSpec((1, CONV_W - 1, LRU_W), lambda i: (seq(i), 0, 0)),
    ]
    n = steps * c
    out_shape = [
        jax.ShapeDtypeStruct((n, V_W), BF16),
        jax.ShapeDtypeStruct((n, LRU_W), BF16),
        jax.ShapeDtypeStruct((n_seq, HA, DK, DV), F32),
        jax.ShapeDtypeStruct((n_seq, CONV_W - 1, QKV_W), F32),
        jax.ShapeDtypeStruct((n_seq, 1, LRU_W), F32),
        jax.ShapeDtypeStruct((n_seq, CONV_W - 1, LRU_W), F32),
    ]
    return pl.pallas_call(
        functools.partial(_mixer_kernel, n_prompt_steps, chunks_per_prompt),
        grid=(steps,),
        in_specs=in_specs,
        out_specs=out_specs,
        out_shape=out_shape,
        scratch_shapes=[pltpu.VMEM((HA, DK, DV), F32),
                        pltpu.VMEM((c + 8, QKV_W), F32),
                        pltpu.VMEM((8, LRU_W), F32),
                        pltpu.VMEM((c + 8, LRU_W), F32)],
        compiler_params=pltpu.CompilerParams(dimension_semantics=("arbitrary",),
                                             vmem_limit_bytes=VMEM_LIMIT),
        name="mixer",
    )(p, p, p, p, ba, s0, cq0, h0, cx0, cw, alog, dtb, onorm, lcw, lcb, wri, br, bi, lam)


def _pad_lanes(v, width):
    return jnp.pad(v, ((0, 0), (0, width - v.shape[-1])))


def kernel(x_prompt, x_sample, state_delta_S, state_delta_conv, state_lru_h, state_lru_conv,
           meta_tokens, ffn1_norm, ffn1_w_gu, ffn1_w_down, mix_norm, w_in, delta_conv_w,
           delta_A_log, delta_dt_bias, delta_out_norm, lru_conv_w, lru_conv_b, lru_w_r, lru_b_r,
           lru_w_i, lru_b_i, lru_lambda, w_branch_a, w_branch_b, w_out, ffn2_norm, ffn2_w_gu,
           ffn2_w_down, final_norm):
    n_prompt, seq_len, _ = x_prompt.shape
    n_sample, dec_len, _ = x_sample.shape
    assert dec_len == CHUNK // 2 and (N_META + seq_len) % CHUNK == N_META
    prompt_rows = CHUNK - N_META + N_META + seq_len
    chunks_per_prompt = prompt_rows // CHUNK
    dt = x_prompt.dtype

    meta = jnp.broadcast_to(meta_tokens.astype(dt)[None], (n_prompt, N_META, D_MODEL))
    xp = jnp.concatenate([jnp.zeros((n_prompt, CHUNK - N_META, D_MODEL), dt), meta, x_prompt], axis=1)
    xs = jnp.concatenate([jnp.zeros((n_sample, CHUNK - dec_len, D_MODEL), dt), x_sample], axis=1)
    x = jnp.concatenate([xp.reshape(-1, D_MODEL), xs.reshape(-1, D_MODEL)], axis=0).astype(F32)
    n_rows = x.shape[0]
    assert n_rows % ROW_TILE == 0

    wgu1 = ffn1_w_gu.astype(BF16)
    wd1 = ffn1_w_down.astype(BF16)
    wgu2 = ffn2_w_gu.astype(BF16)
    wd2 = ffn2_w_down.astype(BF16)
    w_main = jnp.concatenate([w_in[:, :, :OFF_BETA], w_in[:, :, OFF_LX:]], axis=-1).astype(BF16)
    w_ba = jnp.concatenate(
        [jnp.pad(w_in[:, :, OFF_BETA:OFF_ALPHA], ((0, 0), (0, 0), (0, 128 - HA))),
         jnp.pad(w_in[:, :, OFF_ALPHA:OFF_LX], ((0, 0), (0, 0), (0, 128 - HA)))], axis=-1).astype(BF16)
    wa = w_branch_a.astype(BF16)
    wb = w_branch_b.astype(BF16)
    wo = w_out.astype(BF16)
    w_ri = jnp.concatenate([lru_w_r, lru_w_i], axis=-1).astype(BF16)

    outs_s, outs_cq, outs_h, outs_cx = [], [], [], []
    for l in range(DEPTH):
        x = _ffn_call(x, ffn1_norm[l][None].astype(F32), wgu1[l], wd1[l])
        p, ba = _inproj_call(x, mix_norm[l][None].astype(F32), w_main[l], w_ba[l])
        oa, ob, s_new, cq_new, h_new, cx_new = _mixer_call(
            p, ba,
            state_delta_S[l].astype(F32), state_delta_conv[l].astype(F32),
            state_lru_h[l][:, None, :].astype(F32), state_lru_conv[l].astype(F32),
            delta_conv_w[l].astype(F32),
            _pad_lanes(delta_A_log[l][None].astype(F32), 128),
            _pad_lanes(delta_dt_bias[l][None].astype(F32), 128),
            delta_out_norm[l][None].astype(F32),
            lru_conv_w[l].astype(F32), lru_conv_b[l][None].astype(F32), w_ri[l],
            lru_b_r[l][None].astype(F32), lru_b_i[l][None].astype(F32),
            lru_lambda[l][None].astype(F32),
            n_prompt, n_sample, chunks_per_prompt)
        x = _outproj_call(x, p, oa, ob, wa[l], wb[l], wo[l], ffn2_norm[l][None].astype(F32),
                          wgu2[l], wd2[l], final_norm[None].astype(F32), final=(l == DEPTH - 1))
        outs_s.append(s_new)
        outs_cq.append(cq_new)
        outs_h.append(h_new[:, 0, :])
        outs_cx.append(cx_new)

    n_prompt_rows = n_prompt * prompt_rows
    y_prompt = x[:n_prompt_rows].reshape(n_prompt, prompt_rows, D_MODEL)[:, CHUNK:]
    y_sample = x[n_prompt_rows:].reshape(n_sample, CHUNK, D_MODEL)[:, CHUNK - dec_len:]
    s_all = jnp.stack(outs_s)
    cq_all = jnp.stack(outs_cq)
    h_all = jnp.stack(outs_h)
    cx_all = jnp.stack(outs_cx)
    sd = state_delta_S.dtype
    return (y_prompt.astype(dt), y_sample.astype(dt),
            s_all[:, :n_prompt].astype(dt), cq_all[:, :n_prompt].astype(dt),
            h_all[:, :n_prompt].astype(dt), cx_all[:, :n_prompt].astype(dt),
            s_all[:, n_prompt:].astype(sd), cq_all[:, n_prompt:].astype(state_delta_conv.dtype),
            h_all[:, n_prompt:].astype(state_lru_h.dtype), cx_all[:, n_prompt:].astype(state_lru_conv.dtype))
```

```python
import functools

import numpy as np
import jax
import jax.numpy as jnp
from jax import lax
from jax.experimental import pallas as pl
from jax.experimental.pallas import tpu as pltpu

F32 = jnp.float32
BF16 = jnp.bfloat16

D_MODEL = 1024
DEPTH = 4
N_META = 16
HA = 8
DK = 128
DV = 128
QK_W = HA * DK
V_W = HA * DV
QKV_W = 2 * QK_W + V_W
LRU_W = D_MODEL
NB = 8
BW = 128
CONV_W = 4
LRU_C = 8.0
D_FF = 2816
EPS = 1e-6

OFF_Z = QKV_W
OFF_BETA = OFF_Z + V_W
OFF_ALPHA = OFF_BETA + HA
OFF_LX = OFF_ALPHA + HA
OFF_LY = OFF_LX + LRU_W
OFF_GA = OFF_LY + LRU_W
OFF_GB = OFF_GA + D_MODEL
IN_COLS = OFF_GB + D_MODEL

CHUNK = 64
PAIR_ROWS = 2 * CHUNK
PROMPT_NULL = CHUNK - N_META
SAMPLE_NULL = CHUNK // 2
INV_BASE = 16
ROW_TILE = 512
INPROJ_TILE = 256
FF_HALF = D_FF // 2
CONV_COLS = QKV_W + LRU_W
P_Z, P_X, P_Y, P_G = 3072, 4096, 5120, 6144
P_COLS = 8192
VMEM_LIMIT = 56 * 1024 * 1024


def _rms(x, w):
    return x * lax.rsqrt(jnp.mean(x * x, axis=-1, keepdims=True) + EPS) * w


def _mm(a, b):
    return jnp.dot(a.astype(BF16), b.astype(BF16), preferred_element_type=F32)


def _mm_nt(a, b):
    return lax.dot_general(a.astype(BF16), b.astype(BF16), (((1,), (1,)), ((), ())),
                           preferred_element_type=F32)


def _mm_tn(a, b):
    return lax.dot_general(a.astype(BF16), b.astype(BF16), (((0,), (0,)), ((), ())),
                           preferred_element_type=F32)


def _sigmoid(x):
    return 1.0 / (1.0 + jnp.exp(-x))


def _silu(x):
    return x * _sigmoid(x)


def _softplus(x):
    return jnp.maximum(x, 0.0) + jnp.log1p(jnp.exp(-jnp.abs(x)))


def _gelu_tanh(x):
    return 0.5 * x * (1.0 + jnp.tanh(0.7978845608028654 * (x + 0.044715 * (x * x * x))))


class _Geometry:
    def __init__(self, n_prompt, n_sample, chunks_per_prompt):
        assert n_prompt % 2 == 0 and n_sample % 2 == 0
        self.n_prompt = n_prompt
        self.n_sample = n_sample
        self.cpp = chunks_per_prompt
        self.prompt_steps = (n_prompt // 2) * chunks_per_prompt
        self.steps = self.prompt_steps + n_sample // 2
        self.prompt_chunks = 2 * self.prompt_steps
        self.chunks = 2 * self.steps
        self.rows = self.chunks * CHUNK
        self.n_streams = n_prompt + n_sample

    def chunk_null(self, cid):
        is_prompt = cid < self.prompt_chunks
        first = jnp.logical_or(jnp.logical_not(is_prompt), (cid // 2) % self.cpp == 0)
        return jnp.where(first, jnp.where(is_prompt, PROMPT_NULL, SAMPLE_NULL), 0)

    def last_chunk_ids(self):
        ids = [2 * ((b // 2) * self.cpp + self.cpp - 1) + b % 2 for b in range(self.n_prompt)]
        ids += [self.prompt_chunks + t for t in range(self.n_sample)]
        return np.asarray(ids, np.int32)


def _valid_rows(null):
    return lax.broadcasted_iota(jnp.int32, (CHUNK, 1), 0) >= null


def _ffn(x, nw, wgu_ref, wd_ref, h_ref):
    xn = _rms(x, nw).astype(BF16)
    for c in range(2):
        lo = c * FF_HALF
        g = jnp.dot(xn, wgu_ref[:, lo:lo + FF_HALF], preferred_element_type=F32)
        u = jnp.dot(xn, wgu_ref[:, D_FF + lo:D_FF + lo + FF_HALF], preferred_element_type=F32)
        h_ref[:, lo:lo + FF_HALF] = (_silu(g) * u).astype(BF16)
    y = jnp.dot(h_ref[...], wd_ref[...], preferred_element_type=F32)
    return x + 0.5 * y


def _ffn_kernel(x_ref, nw_ref, wgu_ref, wd_ref, o_ref, h_ref):
    o_ref[...] = _ffn(x_ref[...], nw_ref[...], wgu_ref, wd_ref, h_ref)


def _inproj_kernel(geo, x_ref, nw_ref, w_ref, wba_ref, cw_ref, lcw_ref, lcb_ref, inj_ref,
                   p_ref, ba_ref, tails_ref, stage_all, carry_scr):
    i = pl.program_id(0)
    nch = INPROJ_TILE // CHUNK
    xn = _rms(x_ref[...], nw_ref[...]).astype(BF16)
    for lo in (P_Z, P_Y, P_G, P_G + 1024):
        p_ref[:, lo:lo + 1024] = jnp.dot(
            xn, w_ref[:, lo:lo + 1024], preferred_element_type=F32).astype(BF16)
    ba_ref[...] = jnp.dot(xn, wba_ref[...], preferred_element_type=F32)

    @pl.when(i == 0)
    def _():
        carry_scr[...] = jnp.zeros_like(carry_scr)

    valid = [_valid_rows(geo.chunk_null(i * nch + j)) for j in range(nch)]
    sample_tile0 = geo.prompt_chunks * CHUNK // INPROJ_TILE

    for kidx, (kind, lo, col) in enumerate(
            (("q", 0, 0), ("k", QK_W, QK_W), ("v", 2 * QK_W, 2 * QK_W), ("x", P_X, QKV_W))):
        stage = stage_all.at[kidx]
        pre = jnp.dot(xn, w_ref[:, lo:lo + 1024], preferred_element_type=F32)
        for j in range(nch):
            stage[j, 8:8 + CHUNK, :] = pre[j * CHUNK:(j + 1) * CHUNK]
            if j >= 2:
                stage[j, 0:8, :] = pre[(j - 1) * CHUNK - 8:(j - 1) * CHUNK]
            else:
                stage[j, 0:8, :] = carry_scr[j, :, col:col + 1024]

        @pl.when(i >= sample_tile0)
        def _():
            for j in range(nch):
                lo_r = 8 + SAMPLE_NULL - 8
                stage[j, lo_r:lo_r + 8, :] = (stage[j, lo_r:lo_r + 8, :]
                                                  + inj_ref[8 * j:8 * j + 8, col:col + 1024])

        for j in range(nch):
            tails_ref[j, :, col:col + 1024] = stage[j, CHUNK:CHUNK + 8, :]
        carry_scr[0, :, col:col + 1024] = stage[nch - 2, CHUNK:CHUNK + 8, :]
        carry_scr[1, :, col:col + 1024] = stage[nch - 1, CHUNK:CHUNK + 8, :]

        taps = lcw_ref[...] if kind == "x" else cw_ref[:, lo:lo + 1024]
        for j in range(nch):
            conv = stage[j, 5:5 + CHUNK, :] * taps[0:1, :]
            for t in range(1, CONV_W):
                conv = conv + stage[j, 5 + t:5 + t + CHUNK, :] * taps[t:t + 1, :]
            if kind == "x":
                out = conv + lcb_ref[...]
            else:
                act = _silu(conv)
                if kind != "v":
                    scale = DK ** -0.5 if kind == "q" else 1.0
                    segs = []
                    for h in range(HA):
                        seg = act[:, h * DK:(h + 1) * DK]
                        segs.append(seg * (lax.rsqrt(jnp.sum(seg * seg, axis=-1, keepdims=True) + EPS) * scale))
                    act = jnp.concatenate(segs, axis=1)
                out = jnp.where(valid[j], act, 0.0)
            p_ref[j * CHUNK:(j + 1) * CHUNK, lo:lo + 1024] = out.astype(BF16)


def _shift_rows(x, d, fill):
    if d % 8 == 0:
        return jnp.concatenate([jnp.full((d, x.shape[1]), fill, x.dtype), x[:x.shape[0] - d]], axis=0)
    rows = lax.broadcasted_iota(jnp.int32, x.shape, 0)
    return jnp.where(rows >= d, pltpu.roll(x, d, 0), fill)


def _cumsum_rows(x, c):
    d = 1
    while d < c:
        x = x + _shift_rows(x, d, 0.0)
        d *= 2
    return x


def _linear_scan_rows(a, b, h0, c):
    d = 1
    while d < c:
        b = b + a * _shift_rows(b, d, 0.0)
        a = a * _shift_rows(a, d, 1.0)
        d *= 2
    return b + a * h0


def _outproj_kernel(geo, final, x_ref, g_ref, oa_ref, xl_ref, h0_ref,
                    wri_ref, br_ref, bi_ref, lam_ref, wa_ref, wb_ref, wo_ref,
                    nw_ref, wgu_ref, wd_ref, fn_ref,
                    o_ref, hlast_ref, h_ref, hc_scr, ob_scr):
    i = pl.program_id(0)
    nch = ROW_TILE // CHUNK

    @pl.when(i == 0)
    def _():
        hc_scr[...] = jnp.zeros_like(hc_scr)

    sp = _softplus(-lam_ref[...])
    nulls = [geo.chunk_null(i * nch + j) for j in range(nch)]
    valid = [_valid_rows(n) for n in nulls]
    is_sample = [(i * nch + j) >= geo.prompt_chunks for j in range(nch)]
    for n in range(NB):
        sl = slice(n * BW, (n + 1) * BW)
        xb = xl_ref[:, sl]
        gates = jnp.dot(xb, wri_ref[n], preferred_element_type=F32)
        r = _sigmoid(gates[:, :BW] + br_ref[:, sl])
        ig = _sigmoid(gates[:, BW:] + bi_ref[:, sl])
        log_a = -LRU_C * r * sp[:, sl]
        a_all = jnp.exp(log_a)
        th = jnp.tanh(log_a)
        mult = jnp.sqrt(-2.0 * th / (1.0 - th))
        b_all = mult * (ig * xb.astype(F32))
        gy = _gelu_tanh(xl_ref[:, LRU_W + n * BW:LRU_W + (n + 1) * BW].astype(F32))
        carry = [hc_scr[0:1, sl], hc_scr[1:2, sl]]
        for j in range(nch):
            rows = slice(j * CHUNK, (j + 1) * CHUNK)
            a = jnp.where(valid[j], a_all[rows], 1.0)
            b = jnp.where(valid[j], b_all[rows], 0.0)
            h_in = jnp.where(nulls[j] > 0, jnp.where(is_sample[j], h0_ref[j:j + 1, sl], 0.0), carry[j % 2])
            hs = _linear_scan_rows(a, b, h_in, CHUNK)
            ob_scr[rows, sl] = (hs * gy[rows]).astype(BF16)
            carry[j % 2] = hs[CHUNK - 1:CHUNK, :]
            hlast_ref[j:j + 1, sl] = carry[j % 2]
        hc_scr[0:1, sl] = carry[0]
        hc_scr[1:2, sl] = carry[1]

    ga = _sigmoid(g_ref[:, :D_MODEL].astype(F32))
    gb = _sigmoid(g_ref[:, D_MODEL:].astype(F32))
    ma = jnp.dot(oa_ref[...], wa_ref[...], preferred_element_type=F32)
    mb = jnp.dot(ob_scr[...], wb_ref[...], preferred_element_type=F32)
    m = (ga * ma + gb * mb).astype(BF16)
    x = x_ref[...] + jnp.dot(m, wo_ref[...], preferred_element_type=F32)
    x = _ffn(x, nw_ref[...], wgu_ref, wd_ref, h_ref)
    if final:
        x = _rms(x, fn_ref[...])
    o_ref[...] = x


def _resident(shape):
    nd = len(shape)
    return pl.BlockSpec(shape, lambda i: (0,) * nd, pipeline_mode=pl.Buffered(1))


def _params(semantics):
    return pltpu.CompilerParams(dimension_semantics=(semantics,), vmem_limit_bytes=VMEM_LIMIT)


def _ffn_call(x, nw, wgu, wd):
    n = x.shape[0]
    row = lambda i: (i, 0)
    return pl.pallas_call(
        _ffn_kernel,
        grid=(n // ROW_TILE,),
        in_specs=[pl.BlockSpec((ROW_TILE, D_MODEL), row),
                  _resident((1, D_MODEL)),
                  _resident((D_MODEL, 2 * D_FF)),
                  _resident((D_FF, D_MODEL))],
        out_specs=pl.BlockSpec((ROW_TILE, D_MODEL), row),
        out_shape=jax.ShapeDtypeStruct((n, D_MODEL), F32),
        scratch_shapes=[pltpu.VMEM((ROW_TILE, D_FF), BF16)],
        compiler_params=_params("parallel"),
        name="ffn",
    )(x, nw, wgu, wd)


def _inproj_call(geo, x, nw, w, wba, cw, lcw, lcb, inj):
    n = x.shape[0]
    nch = INPROJ_TILE // CHUNK
    sample_tile0 = geo.prompt_chunks * CHUNK // INPROJ_TILE
    row = lambda i: (i, 0)
    return pl.pallas_call(
        functools.partial(_inproj_kernel, geo),
        grid=(n // INPROJ_TILE,),
        in_specs=[pl.BlockSpec((INPROJ_TILE, D_MODEL), row),
                  _resident((1, D_MODEL)),
                  _resident((D_MODEL, P_COLS)),
                  _resident((D_MODEL, 256)),
                  _resident((CONV_W, QKV_W)),
                  _resident((CONV_W, LRU_W)),
                  _resident((1, LRU_W)),
                  pl.BlockSpec((8 * nch, CONV_COLS), lambda i: (jnp.maximum(i - sample_tile0, 0), 0))],
        out_specs=[pl.BlockSpec((INPROJ_TILE, P_COLS), row),
                   pl.BlockSpec((INPROJ_TILE, 256), row),
                   pl.BlockSpec((nch, 8, CONV_COLS), lambda i: (i, 0, 0))],
        out_shape=[jax.ShapeDtypeStruct((n, P_COLS), BF16),
                   jax.ShapeDtypeStruct((n, 256), F32),
                   jax.ShapeDtypeStruct((n // CHUNK, 8, CONV_COLS), F32)],
        scratch_shapes=[pltpu.VMEM((4, nch, CHUNK + 8, 1024), F32),
                        pltpu.VMEM((2, 8, CONV_COLS), F32)],
        compiler_params=_params("arbitrary"),
        name="inproj",
    )(x, nw, w, wba, cw, lcw, lcb, inj)


def _outproj_call(geo, x, p, oa, h0, wri, br, bi, lam, wa, wb, wo, nw, wgu, wd, fn, final):
    n = x.shape[0]
    nch = ROW_TILE // CHUNK
    sample_tile0 = geo.prompt_chunks * CHUNK // ROW_TILE
    row = lambda i: (i, 0)
    return pl.pallas_call(
        functools.partial(_outproj_kernel, geo, final),
        grid=(n // ROW_TILE,),
        in_specs=[pl.BlockSpec((ROW_TILE, D_MODEL), row),
                  pl.BlockSpec((ROW_TILE, 2 * D_MODEL), lambda i: (i, P_G // 2048)),
                  pl.BlockSpec((ROW_TILE, V_W), row),
                  pl.BlockSpec((ROW_TILE, 2 * LRU_W), lambda i: (i, P_X // 2048)),
                  pl.BlockSpec((nch, LRU_W), lambda i: (jnp.maximum(i - sample_tile0, 0), 0)),
                  _resident((NB, BW, 2 * BW)),
                  _resident((1, LRU_W)),
                  _resident((1, LRU_W)),
                  _resident((1, LRU_W)),
                  _resident((V_W, D_MODEL)),
                  _resident((LRU_W, D_MODEL)),
                  _resident((D_MODEL, D_MODEL)),
                  _resident((1, D_MODEL)),
                  _resident((D_MODEL, 2 * D_FF)),
                  _resident((D_FF, D_MODEL)),
                  _resident((1, D_MODEL))],
        out_specs=[pl.BlockSpec((ROW_TILE, D_MODEL), row),
                   pl.BlockSpec((nch, LRU_W), row)],
        out_shape=[jax.ShapeDtypeStruct((n, D_MODEL), F32),
                   jax.ShapeDtypeStruct((n // CHUNK, LRU_W), F32)],
        scratch_shapes=[pltpu.VMEM((ROW_TILE, D_FF), BF16),
                        pltpu.VMEM((8, LRU_W), F32),
                        pltpu.VMEM((ROW_TILE, LRU_W), BF16)],
        compiler_params=_params("arbitrary"),
        name="outproj_ffn",
    )(x, p, oa, p, h0, wri, br, bi, lam, wa, wb, wo, nw, wgu, wd, fn)


def _unit_lower_inverse(lows, c):
    ri = lax.broadcasted_iota(jnp.int32, (c, c), 0)
    ci = lax.broadcasted_iota(jnp.int32, (c, c), 1)
    same_block = (ri // INV_BASE) == (ci // INV_BASE)
    eye = (ri == ci).astype(F32)
    ld = [jnp.where(same_block, low, 0.0) for low in lows]
    lo = [low - d for low, d in zip(lows, ld)]
    x = [eye - d for d in ld]
    p = ld
    span = 2
    while span < INV_BASE:
        p = [_mm(t, t) for t in p]
        x = [a + _mm(a, t) for a, t in zip(x, p)]
        span *= 2
    e = [_mm(a, b) for a, b in zip(x, lo)]
    y = [eye - t for t in e]
    q = e
    span = 2
    while span < c // INV_BASE:
        q = [_mm(t, t) for t in q]
        y = [a + _mm(a, t) for a, t in zip(y, q)]
        span *= 2
    return [_mm(a, b) for a, b in zip(y, x)]


def _delta_kernel(geo, qkv_ref, z_ref, ba_ref, s0_ref, alog_ref, dtb_ref, onorm_ref,
                  oa_ref, sout_ref, s_scr):
    c = CHUNK
    i = pl.program_id(0)
    is_prompt = i < geo.prompt_steps
    pos = i % geo.cpp
    first = jnp.logical_or(jnp.logical_not(is_prompt), pos == 0)
    last = jnp.logical_or(jnp.logical_not(is_prompt), pos == geo.cpp - 1)
    null = jnp.where(first, jnp.where(is_prompt, PROMPT_NULL, SAMPLE_NULL), 0)
    valid = _valid_rows(null)

    @pl.when(jnp.logical_and(first, is_prompt))
    def _():
        s_scr[...] = jnp.zeros_like(s_scr)

    @pl.when(jnp.logical_not(is_prompt))
    def _():
        s_scr[...] = s0_ref[...]

    ri = lax.broadcasted_iota(jnp.int32, (c, c), 0)
    ci = lax.broadcasted_iota(jnp.int32, (c, c), 1)
    tri_incl = ri >= ci
    tri_strict = ri > ci

    beta_all, gc_all, gc_t, eg_all, ekd_all, egl_all = [], [], [], [], [], []
    for s in range(2):
        ba = ba_ref[s * c:(s + 1) * c, :]
        beta_all.append(jnp.where(valid, _sigmoid(ba[:, :128]), 0.0))
        g = -jnp.exp(alog_ref[...]) * _softplus(ba[:, 128:] + dtb_ref[...])
        gc = _cumsum_rows(jnp.where(valid, g, 0.0), c)
        g_last = gc[c - 1:c, :]
        gc_all.append(gc)
        gc_t.append(jnp.concatenate([gc, jnp.zeros((128 - c, 128), F32)], axis=0).T)
        eg_all.append(jnp.exp(gc))
        ekd_all.append(jnp.exp(g_last - gc))
        egl_all.append(jnp.exp(g_last))

    probs = [(s, h) for s in range(2) for h in range(HA)]
    rows = lambda s: slice(s * c, (s + 1) * c)
    qb = [qkv_ref[rows(s), h * DK:(h + 1) * DK] for s, h in probs]
    kbf = [qkv_ref[rows(s), QK_W + h * DK:QK_W + (h + 1) * DK] for s, h in probs]
    vb = [qkv_ref[rows(s), 2 * QK_W + h * DV:2 * QK_W + (h + 1) * DV] for s, h in probs]
    q = [t.astype(F32) for t in qb]
    k = [t.astype(F32) for t in kbf]
    beta = [beta_all[s][:, h:h + 1] for s, h in probs]
    eg = [eg_all[s][:, h:h + 1] for s, h in probs]
    n = len(probs)
    kb = [k[p] * beta[p] for p in range(n)]
    sc = [_mm_nt(jnp.concatenate([kb[p].astype(BF16), qb[p]], axis=0), kbf[p]) for p in range(n)]
    decay = [jnp.where(tri_incl, jnp.exp(jnp.where(tri_incl, gc_all[s][:, h:h + 1] - gc_t[s][h:h + 1, :c], 0.0)), 0.0)
             for s, h in probs]
    low = [jnp.where(tri_strict, sc[p][:c] * decay[p], 0.0) for p in range(n)]
    a_intra = [sc[p][c:] * decay[p] for p in range(n)]
    t_inv = _unit_lower_inverse(low, c)
    uw = [_mm(t_inv[p], jnp.concatenate([vb[p].astype(F32) * beta[p], kb[p] * eg[p]], axis=1))
          for p in range(n)]
    s_old = [s_scr[s, h] for s, h in probs]
    wq = [_mm(jnp.concatenate([uw[p][:, DV:], q[p] * eg[p]], axis=0), s_old[p]) for p in range(n)]
    v_new = [uw[p][:, :DV] - wq[p][:c] for p in range(n)]
    o_intra = [_mm(a_intra[p], v_new[p]) for p in range(n)]
    ds = [_mm_tn(k[p] * ekd_all[s][:, h:h + 1], v_new[p]) for p, (s, h) in enumerate(probs)]
    for p, (s, h) in enumerate(probs):
        s_scr[s, h] = s_old[p] * egl_all[s][:, h:h + 1] + ds[p]
        o = wq[p][c:] + o_intra[p]
        zh = z_ref[rows(s), h * DV:(h + 1) * DV].astype(F32)
        o = o * lax.rsqrt(jnp.mean(o * o, axis=-1, keepdims=True) + EPS) * onorm_ref[...] * _silu(zh)
        oa_ref[rows(s), h * DV:(h + 1) * DV] = o.astype(oa_ref.dtype)

    @pl.when(last)
    def _():
        sout_ref[...] = s_scr[...]


def _delta_call(geo, p, ba, s0, alog, dtb, onorm):
    n_pp = geo.n_prompt // 2

    def pair(i):
        return jnp.where(i < geo.prompt_steps, i // geo.cpp, i - geo.prompt_steps + n_pp)

    const2 = lambda i: (0, 0)
    return pl.pallas_call(
        functools.partial(_delta_kernel, geo),
        grid=(geo.steps,),
        in_specs=[pl.BlockSpec((PAIR_ROWS, QKV_W), lambda i: (i, 0)),
                  pl.BlockSpec((PAIR_ROWS, V_W), lambda i: (i, P_Z // V_W)),
                  pl.BlockSpec((PAIR_ROWS, 256), lambda i: (i, 0)),
                  pl.BlockSpec((2, HA, DK, DV), lambda i: (jnp.maximum(i - geo.prompt_steps, 0), 0, 0, 0)),
                  pl.BlockSpec((1, 128), const2),
                  pl.BlockSpec((1, 128), const2),
                  pl.BlockSpec((1, DV), const2)],
        out_specs=[pl.BlockSpec((PAIR_ROWS, V_W), lambda i: (i, 0)),
                   pl.BlockSpec((2, HA, DK, DV), lambda i: (pair(i), 0, 0, 0))],
        out_shape=[jax.ShapeDtypeStruct((geo.rows, V_W), BF16),
                   jax.ShapeDtypeStruct((geo.n_streams, HA, DK, DV), F32)],
        scratch_shapes=[pltpu.VMEM((2, HA, DK, DV), F32)],
        compiler_params=_params("arbitrary"),
        name="delta",
    )(p, p, ba, s0, alog, dtb, onorm)


def _pad_lanes(v, width):
    return jnp.pad(v, ((0, 0), (0, width - v.shape[-1])))


def kernel(x_prompt, x_sample, state_delta_S, state_delta_conv, state_lru_h, state_lru_conv,
           meta_tokens, ffn1_norm, ffn1_w_gu, ffn1_w_down, mix_norm, w_in, delta_conv_w,
           delta_A_log, delta_dt_bias, delta_out_norm, lru_conv_w, lru_conv_b, lru_w_r, lru_b_r,
           lru_w_i, lru_b_i, lru_lambda, w_branch_a, w_branch_b, w_out, ffn2_norm, ffn2_w_gu,
           ffn2_w_down, final_norm):
    n_prompt, seq_len, _ = x_prompt.shape
    n_sample, dec_len, _ = x_sample.shape
    assert dec_len == CHUNK - SAMPLE_NULL and (N_META + seq_len) % CHUNK == N_META
    prompt_rows = PROMPT_NULL + N_META + seq_len
    cpp = prompt_rows // CHUNK
    geo = _Geometry(n_prompt, n_sample, cpp)
    assert geo.rows % ROW_TILE == 0 and (geo.prompt_chunks * CHUNK) % ROW_TILE == 0
    dt = x_prompt.dtype

    meta = jnp.broadcast_to(meta_tokens.astype(dt)[None], (n_prompt, N_META, D_MODEL))
    xp = jnp.concatenate([jnp.zeros((n_prompt, PROMPT_NULL, D_MODEL), dt), meta, x_prompt], axis=1)
    xp = xp.reshape(n_prompt // 2, 2, cpp, CHUNK, D_MODEL).transpose(0, 2, 1, 3, 4)
    xs = jnp.concatenate([jnp.zeros((n_sample, SAMPLE_NULL, D_MODEL), dt), x_sample], axis=1)
    x = jnp.concatenate([xp.reshape(-1, D_MODEL), xs.reshape(-1, D_MODEL)], axis=0).astype(F32)

    wgu1 = ffn1_w_gu.astype(BF16)
    wd1 = ffn1_w_down.astype(BF16)
    wgu2 = ffn2_w_gu.astype(BF16)
    wd2 = ffn2_w_down.astype(BF16)
    w_main = jnp.concatenate([w_in[:, :, :OFF_BETA], w_in[:, :, OFF_LX:]], axis=-1).astype(BF16)
    w_ba = jnp.concatenate(
        [jnp.pad(w_in[:, :, OFF_BETA:OFF_ALPHA], ((0, 0), (0, 0), (0, 128 - HA))),
         jnp.pad(w_in[:, :, OFF_ALPHA:OFF_LX], ((0, 0), (0, 0), (0, 128 - HA)))], axis=-1).astype(BF16)
    wa = w_branch_a.astype(BF16)
    wb = w_branch_b.astype(BF16)
    wo = w_out.astype(BF16)
    w_ri = jnp.concatenate([lru_w_r, lru_w_i], axis=-1).astype(BF16)
    inj = jnp.concatenate([state_delta_conv, state_lru_conv], axis=-1).astype(F32)
    inj = jnp.pad(inj, ((0, 0), (0, 0), (8 - (CONV_W - 1), 0), (0, 0))).reshape(DEPTH, n_sample * 8, CONV_COLS)

    last_ids = geo.last_chunk_ids()
    outs_s, outs_tail, outs_h = [], [], []
    for l in range(DEPTH):
        x = _ffn_call(x, ffn1_norm[l][None].astype(F32), wgu1[l], wd1[l])
        p, ba, tails = _inproj_call(geo, x, mix_norm[l][None].astype(F32), w_main[l], w_ba[l],
                                    delta_conv_w[l].astype(F32), lru_conv_w[l].astype(F32),
                                    lru_conv_b[l][None].astype(F32), inj[l])
        oa, s_new = _delta_call(geo, p, ba, state_delta_S[l].astype(F32),
                                _pad_lanes(delta_A_log[l][None].astype(F32), 128),
                                _pad_lanes(delta_dt_bias[l][None].astype(F32), 128),
                                delta_out_norm[l][None].astype(F32))
        x, hlast = _outproj_call(geo, x, p, oa, state_lru_h[l].astype(F32), w_ri[l],
                                 lru_b_r[l][None].astype(F32), lru_b_i[l][None].astype(F32),
                                 lru_lambda[l][None].astype(F32), wa[l], wb[l], wo[l],
                                 ffn2_norm[l][None].astype(F32), wgu2[l], wd2[l],
                                 final_norm[None].astype(F32), final=(l == DEPTH - 1))
        outs_s.append(s_new)
        outs_tail.append(tails[last_ids, 8 - (CONV_W - 1):, :])
        outs_h.append(hlast[last_ids])

    n_prompt_rows = geo.prompt_chunks * CHUNK
    y_prompt = x[:n_prompt_rows].reshape(n_prompt // 2, cpp, 2, CHUNK, D_MODEL).transpose(0, 2, 1, 3, 4)
    y_prompt = y_prompt.reshape(n_prompt, prompt_rows, D_MODEL)[:, CHUNK:]
    y_sample = x[n_prompt_rows:].reshape(n_sample, CHUNK, D_MODEL)[:, SAMPLE_NULL:]
    s_all = jnp.stack(outs_s)
    tail_all = jnp.stack(outs_tail)
    cq_all = tail_all[..., :QKV_W]
    cx_all = tail_all[..., QKV_W:]
    h_all = jnp.stack(outs_h)
    return (y_prompt.astype(dt), y_sample.astype(dt),
            s_all[:, :n_prompt].astype(dt), cq_all[:, :n_prompt].astype(dt),
            h_all[:, :n_prompt].astype(dt), cx_all[:, :n_prompt].astype(dt),
            s_all[:, n_prompt:].astype(state_delta_S.dtype),
            cq_all[:, n_prompt:].astype(state_delta_conv.dtype),
            h_all[:, n_prompt:].astype(state_lru_h.dtype),
            cx_all[:, n_prompt:].astype(state_lru_conv.dtype))
```

```python
import functools

import numpy as np
import jax
import jax.numpy as jnp
from jax import lax
from jax.experimental import pallas as pl
from jax.experimental.pallas import tpu as pltpu

F32 = jnp.float32
BF16 = jnp.bfloat16

D_MODEL = 1024
DEPTH = 4
N_META = 16
HA = 8
DK = 128
DV = 128
QK_W = HA * DK
V_W = HA * DV
QKV_W = 2 * QK_W + V_W
LRU_W = D_MODEL
NB = 8
BW = 128
CONV_W = 4
LRU_C = 8.0
D_FF = 2816
EPS = 1e-6

OFF_Z = QKV_W
OFF_BETA = OFF_Z + V_W
OFF_ALPHA = OFF_BETA + HA
OFF_LX = OFF_ALPHA + HA
OFF_LY = OFF_LX + LRU_W
OFF_GA = OFF_LY + LRU_W
OFF_GB = OFF_GA + D_MODEL
IN_COLS = OFF_GB + D_MODEL

CHUNK = 64
PAIR_ROWS = 2 * CHUNK
PROMPT_NULL = CHUNK - N_META
SAMPLE_NULL = CHUNK // 2
INV_BASE = 16
ROW_TILE = 512
OUT_TILE = 256
INPROJ_TILE = 256
FF_HALF = D_FF // 2
CONV_COLS = QKV_W + LRU_W
P_Z, P_X, P_Y, P_G = 3072, 4096, 5120, 6144
P_COLS = 8192
VMEM_LIMIT = 56 * 1024 * 1024


def _rms(x, w):
    return x * lax.rsqrt(jnp.mean(x * x, axis=-1, keepdims=True) + EPS) * w


def _mm(a, b):
    return jnp.dot(a.astype(BF16), b.astype(BF16), preferred_element_type=F32)


def _mm_nt(a, b):
    return lax.dot_general(a.astype(BF16), b.astype(BF16), (((1,), (1,)), ((), ())),
                           preferred_element_type=F32)


def _mm_tn(a, b):
    return lax.dot_general(a.astype(BF16), b.astype(BF16), (((0,), (0,)), ((), ())),
                           preferred_element_type=F32)


def _sigmoid(x):
    return 1.0 / (1.0 + jnp.exp(-x))


def _silu(x):
    return x * _sigmoid(x)


def _softplus(x):
    return jnp.maximum(x, 0.0) + jnp.log1p(jnp.exp(-jnp.abs(x)))


def _gelu_tanh(x):
    return 0.5 * x * (1.0 + jnp.tanh(0.7978845608028654 * (x + 0.044715 * (x * x * x))))


class _Geometry:
    def __init__(self, n_prompt, n_sample, chunks_per_prompt):
        assert n_prompt % 2 == 0 and n_sample % 2 == 0
        self.n_prompt = n_prompt
        self.n_sample = n_sample
        self.cpp = chunks_per_prompt
        self.prompt_steps = (n_prompt // 2) * chunks_per_prompt
        self.steps = self.prompt_steps + n_sample // 2
        self.prompt_chunks = 2 * self.prompt_steps
        self.chunks = 2 * self.steps
        self.rows = self.chunks * CHUNK
        self.n_streams = n_prompt + n_sample

    def chunk_null(self, cid):
        is_prompt = cid < self.prompt_chunks
        first = jnp.logical_or(jnp.logical_not(is_prompt), (cid // 2) % self.cpp == 0)
        return jnp.where(first, jnp.where(is_prompt, PROMPT_NULL, SAMPLE_NULL), 0)

    def last_chunk_ids(self):
        ids = [2 * ((b // 2) * self.cpp + self.cpp - 1) + b % 2 for b in range(self.n_prompt)]
        ids += [self.prompt_chunks + t for t in range(self.n_sample)]
        return np.asarray(ids, np.int32)


def _valid_rows(null):
    return lax.broadcasted_iota(jnp.int32, (CHUNK, 1), 0) >= null


def _ffn(x, nw, wgu_ref, wd_ref, h_ref, after_dot=lambda: None):
    xn = _rms(x, nw).astype(BF16)
    for c in range(2):
        lo = c * FF_HALF
        g = jnp.dot(xn, wgu_ref[:, lo:lo + FF_HALF], preferred_element_type=F32)
        after_dot()
        u = jnp.dot(xn, wgu_ref[:, D_FF + lo:D_FF + lo + FF_HALF], preferred_element_type=F32)
        after_dot()
        h_ref[:, lo:lo + FF_HALF] = (_silu(g) * u).astype(BF16)
    y = jnp.dot(h_ref[...], wd_ref[...], preferred_element_type=F32)
    after_dot()
    return x + 0.5 * y


def _ffn_kernel(x_ref, nw_ref, wgu_ref, wd_ref, o_ref, h_ref):
    o_ref[...] = _ffn(x_ref[...], nw_ref[...], wgu_ref, wd_ref, h_ref)


def _inproj_kernel(geo, x_ref, nw_ref, w_ref, wba_ref, cw_ref, lcw_ref, lcb_ref, inj_ref,
                   p_ref, ba_ref, tails_ref, stage_all, carry_scr):
    i = pl.program_id(0)
    nch = INPROJ_TILE // CHUNK
    xn = _rms(x_ref[...], nw_ref[...]).astype(BF16)

    @pl.when(i == 0)
    def _():
        carry_scr[...] = jnp.zeros_like(carry_scr)

    valid = [_valid_rows(geo.chunk_null(i * nch + j)) for j in range(nch)]
    in_sample = i >= geo.prompt_chunks * CHUNK // INPROJ_TILE
    kinds = (("q", 0, 0), ("k", QK_W, QK_W), ("v", 2 * QK_W, 2 * QK_W), ("x", P_X, QKV_W))

    def project(kidx):
        kind, lo, col = kinds[kidx]
        stage = stage_all.at[kidx]
        pre = jnp.dot(xn, w_ref[:, lo:lo + 1024], preferred_element_type=F32)
        for j in range(nch):
            stage[j, 8:8 + CHUNK, :] = pre[j * CHUNK:(j + 1) * CHUNK]
            if j >= 2:
                stage[j, 0:8, :] = pre[(j - 1) * CHUNK - 8:(j - 1) * CHUNK]
            else:
                stage[j, 0:8, :] = carry_scr[j, :, col:col + 1024]
            lo_r = 8 + SAMPLE_NULL - 8
            stage[j, lo_r:lo_r + 8, :] = stage[j, lo_r:lo_r + 8, :] + jnp.where(
                in_sample, inj_ref[8 * j:8 * j + 8, col:col + 1024], 0.0)
            tails_ref[j, :, col:col + 1024] = stage[j, CHUNK:CHUNK + 8, :]
        carry_scr[0, :, col:col + 1024] = stage[nch - 2, CHUNK:CHUNK + 8, :]
        carry_scr[1, :, col:col + 1024] = stage[nch - 1, CHUNK:CHUNK + 8, :]

    def convolve(kidx, j):
        kind, lo, col = kinds[kidx]
        stage = stage_all.at[kidx]
        taps = lcw_ref[...] if kind == "x" else cw_ref[:, lo:lo + 1024]
        conv = stage[j, 5:5 + CHUNK, :] * taps[0:1, :]
        for t in range(1, CONV_W):
            conv = conv + stage[j, 5 + t:5 + t + CHUNK, :] * taps[t:t + 1, :]
        if kind == "x":
            out = conv + lcb_ref[...]
        else:
            act = _silu(conv)
            if kind != "v":
                scale = DK ** -0.5 if kind == "q" else 1.0
                segs = []
                for h in range(HA):
                    seg = act[:, h * DK:(h + 1) * DK]
                    segs.append(seg * (lax.rsqrt(jnp.sum(seg * seg, axis=-1, keepdims=True) + EPS) * scale))
                act = jnp.concatenate(segs, axis=1)
            out = jnp.where(valid[j], act, 0.0)
        p_ref[j * CHUNK:(j + 1) * CHUNK, lo:lo + 1024] = out.astype(BF16)

    def plain(lo):
        p_ref[:, lo:lo + 1024] = jnp.dot(
            xn, w_ref[:, lo:lo + 1024], preferred_element_type=F32).astype(BF16)

    plain_cols = (P_Z, P_Y, P_G, P_G + 1024)
    project(0)
    for kidx in range(len(kinds)):
        plain(plain_cols[kidx])
        if kidx + 1 < len(kinds):
            project(kidx + 1)
        for j in range(nch):
            convolve(kidx, j)
    ba_ref[...] = jnp.dot(xn, wba_ref[...], preferred_element_type=F32)


def _shift_rows(x, d, fill):
    if d % 8 == 0:
        return jnp.concatenate([jnp.full((d, x.shape[1]), fill, x.dtype), x[:x.shape[0] - d]], axis=0)
    rows = lax.broadcasted_iota(jnp.int32, x.shape, 0)
    return jnp.where(rows >= d, pltpu.roll(x, d, 0), fill)


def _cumsum_rows(x, c):
    d = 1
    while d < c:
        x = x + _shift_rows(x, d, 0.0)
        d *= 2
    return x


def _linear_scan_rows(a, b, h0, c):
    d = 1
    while d < c:
        b = b + a * _shift_rows(b, d, 0.0)
        a = a * _shift_rows(a, d, 1.0)
        d *= 2
    return b + a * h0


def _lru_blocks(geo, tile, xl_ref, h0_ref, wri_ref, br_ref, bi_ref, lam_ref, ob_dst, hlast_ref, hc_scr, n_tiles):
    nch = OUT_TILE // CHUNK
    live = tile < n_tiles
    nulls = [geo.chunk_null(tile * nch + j) for j in range(nch)]
    valid = [_valid_rows(n) for n in nulls]
    is_sample = [(tile * nch + j) >= geo.prompt_chunks for j in range(nch)]
    hrow = pl.multiple_of(jnp.minimum(tile, n_tiles - 1) * 8, 8)

    def block(n):
        sl = slice(n * BW, (n + 1) * BW)
        xb = xl_ref[:, sl]
        gates = jnp.dot(xb, wri_ref[n], preferred_element_type=F32)
        r = _sigmoid(gates[:, :BW] + br_ref[:, sl])
        ig = _sigmoid(gates[:, BW:] + bi_ref[:, sl])
        log_a = -LRU_C * r * _softplus(-lam_ref[:, sl])
        a_all = jnp.exp(log_a)
        th = jnp.tanh(log_a)
        mult = jnp.sqrt(-2.0 * th / (1.0 - th))
        b_all = mult * (ig * xb.astype(F32))
        gy = _gelu_tanh(xl_ref[:, LRU_W + n * BW:LRU_W + (n + 1) * BW].astype(F32))
        carry = [hc_scr[0:1, sl], hc_scr[1:2, sl]]
        lasts = []
        for j in range(nch):
            rows = slice(j * CHUNK, (j + 1) * CHUNK)
            a = jnp.where(valid[j], a_all[rows], 1.0)
            b = jnp.where(valid[j], b_all[rows], 0.0)
            h_in = jnp.where(nulls[j] > 0, jnp.where(is_sample[j], h0_ref[j:j + 1, sl], 0.0), carry[j % 2])
            hs = _linear_scan_rows(a, b, h_in, CHUNK)
            ob_dst[rows, sl] = (hs * gy[rows]).astype(BF16)
            carry[j % 2] = hs[CHUNK - 1:CHUNK, :]
            lasts.append(carry[j % 2])
        hc_scr[0:1, sl] = carry[0]
        hc_scr[1:2, sl] = carry[1]
        lasts.append(jnp.zeros((8 - nch, BW), F32))
        hlast_ref[pl.ds(hrow, 8), sl] = jnp.where(live, jnp.concatenate(lasts, axis=0),
                                                  hlast_ref[pl.ds(hrow, 8), sl])

    return [functools.partial(block, n) for n in range(NB)]


def _outproj_kernel(geo, final, n_tiles, x_ref, g_ref, oa_ref, xl0_ref, xl_ref, h0_ref,
                    wri_ref, br_ref, bi_ref, lam_ref, wa_ref, wb_ref, wo_ref,
                    nw_ref, wgu_ref, wd_ref, fn_ref,
                    o_ref, hlast_ref, h_ref, hc_scr, ob_scr):
    i = pl.program_id(0)
    lru_refs = (wri_ref, br_ref, bi_ref, lam_ref)

    @pl.when(i == 0)
    def _():
        hc_scr[...] = jnp.zeros_like(hc_scr)
        hlast_ref[...] = jnp.zeros_like(hlast_ref)
        for block in _lru_blocks(geo, i, xl0_ref, h0_ref, *lru_refs, ob_scr.at[0], hlast_ref, hc_scr, n_tiles):
            block()

    blocks = iter(_lru_blocks(geo, i + 1, xl_ref, h0_ref, *lru_refs, ob_scr.at[(i + 1) % 2],
                              hlast_ref, hc_scr, n_tiles))

    def next_block():
        block = next(blocks, None)
        if block is not None:
            block()

    mb = jnp.dot(ob_scr[i % 2], wb_ref[...], preferred_element_type=F32)
    next_block()
    ma = jnp.dot(oa_ref[...], wa_ref[...], preferred_element_type=F32)
    next_block()
    ga = _sigmoid(g_ref[:, :D_MODEL].astype(F32))
    gb = _sigmoid(g_ref[:, D_MODEL:].astype(F32))
    m = (ga * ma + gb * mb).astype(BF16)
    x = x_ref[...] + jnp.dot(m, wo_ref[...], preferred_element_type=F32)
    next_block()
    x = _ffn(x, nw_ref[...], wgu_ref, wd_ref, h_ref, after_dot=next_block)
    for block in blocks:
        block()
    if final:
        x = _rms(x, fn_ref[...])
    o_ref[...] = x


def _resident(shape):
    nd = len(shape)
    return pl.BlockSpec(shape, lambda i: (0,) * nd, pipeline_mode=pl.Buffered(1))


def _params(semantics):
    return pltpu.CompilerParams(dimension_semantics=(semantics,), vmem_limit_bytes=VMEM_LIMIT)


def _ffn_call(x, nw, wgu, wd):
    n = x.shape[0]
    row = lambda i: (i, 0)
    return pl.pallas_call(
        _ffn_kernel,
        grid=(n // ROW_TILE,),
        in_specs=[pl.BlockSpec((ROW_TILE, D_MODEL), row),
                  _resident((1, D_MODEL)),
                  _resident((D_MODEL, 2 * D_FF)),
                  _resident((D_FF, D_MODEL))],
        out_specs=pl.BlockSpec((ROW_TILE, D_MODEL), row),
        out_shape=jax.ShapeDtypeStruct((n, D_MODEL), F32),
        scratch_shapes=[pltpu.VMEM((ROW_TILE, D_FF), BF16)],
        compiler_params=_params("parallel"),
        name="ffn",
    )(x, nw, wgu, wd)


def _inproj_call(geo, x, nw, w, wba, cw, lcw, lcb, inj):
    n = x.shape[0]
    nch = INPROJ_TILE // CHUNK
    sample_tile0 = geo.prompt_chunks * CHUNK // INPROJ_TILE
    row = lambda i: (i, 0)
    return pl.pallas_call(
        functools.partial(_inproj_kernel, geo),
        grid=(n // INPROJ_TILE,),
        in_specs=[pl.BlockSpec((INPROJ_TILE, D_MODEL), row),
                  _resident((1, D_MODEL)),
                  _resident((D_MODEL, P_COLS)),
                  _resident((D_MODEL, 256)),
                  _resident((CONV_W, QKV_W)),
                  _resident((CONV_W, LRU_W)),
                  _resident((1, LRU_W)),
                  pl.BlockSpec((8 * nch, CONV_COLS), lambda i: (jnp.maximum(i - sample_tile0, 0), 0))],
        out_specs=[pl.BlockSpec((INPROJ_TILE, P_COLS), row),
                   pl.BlockSpec((INPROJ_TILE, 256), row),
                   pl.BlockSpec((nch, 8, CONV_COLS), lambda i: (i, 0, 0))],
        out_shape=[jax.ShapeDtypeStruct((n, P_COLS), BF16),
                   jax.ShapeDtypeStruct((n, 256), F32),
                   jax.ShapeDtypeStruct((n // CHUNK, 8, CONV_COLS), F32)],
        scratch_shapes=[pltpu.VMEM((4, nch, CHUNK + 8, 1024), F32),
                        pltpu.VMEM((2, 8, CONV_COLS), F32)],
        compiler_params=_params("arbitrary"),
        name="inproj",
    )(x, nw, w, wba, cw, lcw, lcb, inj)


def _outproj_call(geo, x, p, oa, h0, wri, br, bi, lam, wa, wb, wo, nw, wgu, wd, fn, final):
    n = x.shape[0]
    n_tiles = n // OUT_TILE
    sample_tile0 = geo.prompt_chunks * CHUNK // OUT_TILE
    row = lambda i: (i, 0)
    nxt = lambda i: jnp.minimum(i + 1, n_tiles - 1)
    return pl.pallas_call(
        functools.partial(_outproj_kernel, geo, final, n_tiles),
        grid=(n_tiles,),
        in_specs=[pl.BlockSpec((OUT_TILE, D_MODEL), row),
                  pl.BlockSpec((OUT_TILE, 2 * D_MODEL), lambda i: (i, P_G // 2048)),
                  pl.BlockSpec((OUT_TILE, V_W), row),
                  pl.BlockSpec((OUT_TILE, 2 * LRU_W), lambda i: (0, P_X // 2048),
                               pipeline_mode=pl.Buffered(1)),
                  pl.BlockSpec((OUT_TILE, 2 * LRU_W), lambda i: (nxt(i), P_X // 2048)),
                  pl.BlockSpec((8, LRU_W), lambda i: (jnp.maximum(nxt(i) - sample_tile0, 0), 0)),
                  _resident((NB, BW, 2 * BW)),
                  _resident((1, LRU_W)),
                  _resident((1, LRU_W)),
                  _resident((1, LRU_W)),
                  _resident((V_W, D_MODEL)),
                  _resident((LRU_W, D_MODEL)),
                  _resident((D_MODEL, D_MODEL)),
                  _resident((1, D_MODEL)),
                  _resident((D_MODEL, 2 * D_FF)),
                  _resident((D_FF, D_MODEL)),
                  _resident((1, D_MODEL))],
        out_specs=[pl.BlockSpec((OUT_TILE, D_MODEL), row),
                   pl.BlockSpec((n_tiles * 8, LRU_W), lambda i: (0, 0))],
        out_shape=[jax.ShapeDtypeStruct((n, D_MODEL), F32),
                   jax.ShapeDtypeStruct((n_tiles * 8, LRU_W), F32)],
        scratch_shapes=[pltpu.VMEM((OUT_TILE, D_FF), BF16),
                        pltpu.VMEM((8, LRU_W), F32),
                        pltpu.VMEM((2, OUT_TILE, LRU_W), BF16)],
        compiler_params=_params("arbitrary"),
        name="outproj_ffn",
    )(x, p, oa, p, p, h0, wri, br, bi, lam, wa, wb, wo, nw, wgu, wd, fn)


def _unit_lower_inverse(lows, c):
    ri = lax.broadcasted_iota(jnp.int32, (c, c), 0)
    ci = lax.broadcasted_iota(jnp.int32, (c, c), 1)
    same_block = (ri // INV_BASE) == (ci // INV_BASE)
    eye = (ri == ci).astype(F32)
    ld = [jnp.where(same_block, low, 0.0) for low in lows]
    lo = [low - d for low, d in zip(lows, ld)]
    x = [eye - d for d in ld]
    p = ld
    span = 2
    while span < INV_BASE:
        p = [_mm(t, t) for t in p]
        x = [a + _mm(a, t) for a, t in zip(x, p)]
        span *= 2
    e = [_mm(a, b) for a, b in zip(x, lo)]
    y = [eye - t for t in e]
    q = e
    span = 2
    while span < c // INV_BASE:
        q = [_mm(t, t) for t in q]
        y = [a + _mm(a, t) for a, t in zip(y, q)]
        span *= 2
    return [_mm(a, b) for a, b in zip(y, x)]


def _delta_kernel(geo, qkv_ref, z_ref, ba_ref, s0_ref, alog_ref, dtb_ref, onorm_ref,
                  oa_ref, sout_ref, s_scr):
    c = CHUNK
    i = pl.program_id(0)
    is_prompt = i < geo.prompt_steps
    pos = i % geo.cpp
    first = jnp.logical_or(jnp.logical_not(is_prompt), pos == 0)
    last = jnp.logical_or(jnp.logical_not(is_prompt), pos == geo.cpp - 1)
    null = jnp.where(first, jnp.where(is_prompt, PROMPT_NULL, SAMPLE_NULL), 0)
    valid = _valid_rows(null)

    @pl.when(jnp.logical_and(first, is_prompt))
    def _():
        s_scr[...] = jnp.zeros_like(s_scr)

    @pl.when(jnp.logical_not(is_prompt))
    def _():
        s_scr[...] = s0_ref[...]

    ri = lax.broadcasted_iota(jnp.int32, (c, c), 0)
    ci = lax.broadcasted_iota(jnp.int32, (c, c), 1)
    tri_incl = ri >= ci
    tri_strict = ri > ci

    beta_all, gc_all, gc_t, eg_all, ekd_all, egl_all = [], [], [], [], [], []
    for s in range(2):
        ba = ba_ref[s * c:(s + 1) * c, :]
        beta_all.append(jnp.where(valid, _sigmoid(ba[:, :128]), 0.0))
        g = -jnp.exp(alog_ref[...]) * _softplus(ba[:, 128:] + dtb_ref[...])
        gc = _cumsum_rows(jnp.where(valid, g, 0.0), c)
        g_last = gc[c - 1:c, :]
        gc_all.append(gc)
        gc_t.append(jnp.concatenate([gc, jnp.zeros((128 - c, 128), F32)], axis=0).T)
        eg_all.append(jnp.exp(gc))
        ekd_all.append(jnp.exp(g_last - gc))
        egl_all.append(jnp.exp(g_last))

    probs = [(s, h) for s in range(2) for h in range(HA)]
    rows = lambda s: slice(s * c, (s + 1) * c)
    qb = [qkv_ref[rows(s), h * DK:(h + 1) * DK] for s, h in probs]
    kbf = [qkv_ref[rows(s), QK_W + h * DK:QK_W + (h + 1) * DK] for s, h in probs]
    vb = [qkv_ref[rows(s), 2 * QK_W + h * DV:2 * QK_W + (h + 1) * DV] for s, h in probs]
    q = [t.astype(F32) for t in qb]
    k = [t.astype(F32) for t in kbf]
    beta = [beta_all[s][:, h:h + 1] for s, h in probs]
    eg = [eg_all[s][:, h:h + 1] for s, h in probs]
    n = len(probs)
    kb = [k[p] * beta[p] for p in range(n)]
    sc = [_mm_nt(jnp.concatenate([kb[p].astype(BF16), qb[p]], axis=0), kbf[p]) for p in range(n)]
    decay = [jnp.where(tri_incl, jnp.exp(jnp.where(tri_incl, gc_all[s][:, h:h + 1] - gc_t[s][h:h + 1, :c], 0.0)), 0.0)
             for s, h in probs]
    low = [jnp.where(tri_strict, sc[p][:c] * decay[p], 0.0) for p in range(n)]
    a_intra = [sc[p][c:] * decay[p] for p in range(n)]
    t_inv = _unit_lower_inverse(low, c)
    uw = [_mm(t_inv[p], jnp.concatenate([vb[p].astype(F32) * beta[p], kb[p] * eg[p]], axis=1))
          for p in range(n)]
    s_old = [s_scr[s, h] for s, h in probs]
    wq = [_mm(jnp.concatenate([uw[p][:, DV:], q[p] * eg[p]], axis=0), s_old[p]) for p in range(n)]
    v_new = [uw[p][:, :DV] - wq[p][:c] for p in range(n)]
    o_intra = [_mm(a_intra[p], v_new[p]) for p in range(n)]
    ds = [_mm_tn(k[p] * ekd_all[s][:, h:h + 1], v_new[p]) for p, (s, h) in enumerate(probs)]
    for p, (s, h) in enumerate(probs):
        s_scr[s, h] = s_old[p] * egl_all[s][:, h:h + 1] + ds[p]
        o = wq[p][c:] + o_intra[p]
        zh = z_ref[rows(s), h * DV:(h + 1) * DV].astype(F32)
        o = o * lax.rsqrt(jnp.mean(o * o, axis=-1, keepdims=True) + EPS) * onorm_ref[...] * _silu(zh)
        oa_ref[rows(s), h * DV:(h + 1) * DV] = o.astype(oa_ref.dtype)

    @pl.when(last)
    def _():
        sout_ref[...] = s_scr[...]


def _delta_call(geo, p, ba, s0, alog, dtb, onorm):
    n_pp = geo.n_prompt // 2

    def pair(i):
        return jnp.where(i < geo.prompt_steps, i // geo.cpp, i - geo.prompt_steps + n_pp)

    const2 = lambda i: (0, 0)
    return pl.pallas_call(
        functools.partial(_delta_kernel, geo),
        grid=(geo.steps,),
        in_specs=[pl.BlockSpec((PAIR_ROWS, QKV_W), lambda i: (i, 0)),
                  pl.BlockSpec((PAIR_ROWS, V_W), lambda i: (i, P_Z // V_W)),
                  pl.BlockSpec((PAIR_ROWS, 256), lambda i: (i, 0)),
                  pl.BlockSpec((2, HA, DK, DV), lambda i: (jnp.maximum(i - geo.prompt_steps, 0), 0, 0, 0)),
                  pl.BlockSpec((1, 128), const2),
                  pl.BlockSpec((1, 128), const2),
                  pl.BlockSpec((1, DV), const2)],
        out_specs=[pl.BlockSpec((PAIR_ROWS, V_W), lambda i: (i, 0)),
                   pl.BlockSpec((2, HA, DK, DV), lambda i: (pair(i), 0, 0, 0))],
        out_shape=[jax.ShapeDtypeStruct((geo.rows, V_W), BF16),
                   jax.ShapeDtypeStruct((geo.n_streams, HA, DK, DV), F32)],
        scratch_shapes=[pltpu.VMEM((2, HA, DK, DV), F32)],
        compiler_params=_params("arbitrary"),
        name="delta",
    )(p, p, ba, s0, alog, dtb, onorm)


def _pad_lanes(v, width):
    return jnp.pad(v, ((0, 0), (0, width - v.shape[-1])))


def kernel(x_prompt, x_sample, state_delta_S, state_delta_conv, state_lru_h, state_lru_conv,
           meta_tokens, ffn1_norm, ffn1_w_gu, ffn1_w_down, mix_norm, w_in, delta_conv_w,
           delta_A_log, delta_dt_bias, delta_out_norm, lru_conv_w, lru_conv_b, lru_w_r, lru_b_r,
           lru_w_i, lru_b_i, lru_lambda, w_branch_a, w_branch_b, w_out, ffn2_norm, ffn2_w_gu,
           ffn2_w_down, final_norm):
    n_prompt, seq_len, _ = x_prompt.shape
    n_sample, dec_len, _ = x_sample.shape
    assert dec_len == CHUNK - SAMPLE_NULL and (N_META + seq_len) % CHUNK == N_META
    prompt_rows = PROMPT_NULL + N_META + seq_len
    cpp = prompt_rows // CHUNK
    geo = _Geometry(n_prompt, n_sample, cpp)
    assert geo.rows % ROW_TILE == 0 and (geo.prompt_chunks * CHUNK) % ROW_TILE == 0
    nch_out = OUT_TILE // CHUNK
    dt = x_prompt.dtype

    meta = jnp.broadcast_to(meta_tokens.astype(dt)[None], (n_prompt, N_META, D_MODEL))
    xp = jnp.concatenate([jnp.zeros((n_prompt, PROMPT_NULL, D_MODEL), dt), meta, x_prompt], axis=1)
    xp = xp.reshape(n_prompt // 2, 2, cpp, CHUNK, D_MODEL).transpose(0, 2, 1, 3, 4)
    xs = jnp.concatenate([jnp.zeros((n_sample, SAMPLE_NULL, D_MODEL), dt), x_sample], axis=1)
    x = jnp.concatenate([xp.reshape(-1, D_MODEL), xs.reshape(-1, D_MODEL)], axis=0).astype(F32)

    def cast(w):
        return w.astype(BF16)

    inj = jnp.concatenate([state_delta_conv, state_lru_conv], axis=-1).astype(F32)
    inj = jnp.pad(inj, ((0, 0), (0, 0), (8 - (CONV_W - 1), 0), (0, 0))).reshape(DEPTH, n_sample * 8, CONV_COLS)

    last_ids = geo.last_chunk_ids()
    outs_s, outs_tail, outs_h = [], [], []
    for l in range(DEPTH):
        w_main = cast(jnp.concatenate([w_in[l][:, :OFF_BETA], w_in[l][:, OFF_LX:]], axis=-1))
        w_ba = cast(jnp.concatenate(
            [jnp.pad(w_in[l][:, OFF_BETA:OFF_ALPHA], ((0, 0), (0, 128 - HA))),
             jnp.pad(w_in[l][:, OFF_ALPHA:OFF_LX], ((0, 0), (0, 128 - HA)))], axis=-1))
        w_ri = cast(jnp.concatenate([lru_w_r[l], lru_w_i[l]], axis=-1))
        x = _ffn_call(x, ffn1_norm[l][None].astype(F32), cast(ffn1_w_gu[l]), cast(ffn1_w_down[l]))
        p, ba, tails = _inproj_call(geo, x, mix_norm[l][None].astype(F32), w_main, w_ba,
                                    delta_conv_w[l].astype(F32), lru_conv_w[l].astype(F32),
                                    lru_conv_b[l][None].astype(F32), inj[l])
        oa, s_new = _delta_call(geo, p, ba, state_delta_S[l].astype(F32),
                                _pad_lanes(delta_A_log[l][None].astype(F32), 128),
                                _pad_lanes(delta_dt_bias[l][None].astype(F32), 128),
                                delta_out_norm[l][None].astype(F32))
        h0 = jnp.pad(state_lru_h[l].astype(F32).reshape(-1, nch_out, LRU_W),
                     ((0, 0), (0, 8 - nch_out), (0, 0))).reshape(-1, LRU_W)
        x, hlast = _outproj_call(geo, x, p, oa, h0, w_ri,
                                 lru_b_r[l][None].astype(F32), lru_b_i[l][None].astype(F32),
                                 lru_lambda[l][None].astype(F32), cast(w_branch_a[l]), cast(w_branch_b[l]),
                                 cast(w_out[l]), ffn2_norm[l][None].astype(F32),
                                 cast(ffn2_w_gu[l]), cast(ffn2_w_down[l]),
                                 final_norm[None].astype(F32), final=(l == DEPTH - 1))
        outs_s.append(s_new)
        outs_tail.append(tails[last_ids, 8 - (CONV_W - 1):, :])
        outs_h.append(hlast.reshape(-1, 8, LRU_W)[:, :nch_out].reshape(-1, LRU_W)[last_ids])

    n_prompt_rows = geo.prompt_chunks * CHUNK
    y_prompt = x[:n_prompt_rows].reshape(n_prompt // 2, cpp, 2, CHUNK, D_MODEL).transpose(0, 2, 1, 3, 4)
    y_prompt = y_prompt.reshape(n_prompt, prompt_rows, D_MODEL)[:, CHUNK:]
    y_sample = x[n_prompt_rows:].reshape(n_sample, CHUNK, D_MODEL)[:, SAMPLE_NULL:]
    s_all = jnp.stack(outs_s)
    tail_all = jnp.stack(outs_tail)
    cq_all = tail_all[..., :QKV_W]
    cx_all = tail_all[..., QKV_W:]
    h_all = jnp.stack(outs_h)
    return (y_prompt.astype(dt), y_sample.astype(dt),
            s_all[:, :n_prompt].astype(dt), cq_all[:, :n_prompt].astype(dt),
            h_all[:, :n_prompt].astype(dt), cx_all[:, :n_prompt].astype(dt),
            s_all[:, n_prompt:].astype(state_delta_S.dtype),
            cq_all[:, n_prompt:].astype(state_delta_conv.dtype),
            h_all[:, n_prompt:].astype(state_lru_h.dtype),
            cx_all[:, n_prompt:].astype(state_lru_conv.dtype))
```

```python
import functools

import numpy as np
import jax
import jax.numpy as jnp
from jax import lax
from jax.experimental import pallas as pl
from jax.experimental.pallas import tpu as pltpu

F32 = jnp.float32
BF16 = jnp.bfloat16

D_MODEL = 1024
DEPTH = 4
N_META = 16
HA = 8
DK = 128
DV = 128
QK_W = HA * DK
V_W = HA * DV
QKV_W = 2 * QK_W + V_W
LRU_W = D_MODEL
NB = 8
BW = 128
CONV_W = 4
LRU_C = 8.0
D_FF = 2816
EPS = 1e-6

OFF_Z = QKV_W
OFF_BETA = OFF_Z + V_W
OFF_ALPHA = OFF_BETA + HA
OFF_LX = OFF_ALPHA + HA
OFF_LY = OFF_LX + LRU_W
OFF_GA = OFF_LY + LRU_W
OFF_GB = OFF_GA + D_MODEL
IN_COLS = OFF_GB + D_MODEL

CHUNK = 64
PAIR_ROWS = 2 * CHUNK
PROMPT_NULL = CHUNK - N_META
SAMPLE_NULL = CHUNK // 2
INV_BASE = 16
ROW_TILE = 512
INPROJ_TILE = 256
FF_HALF = D_FF // 2
CONV_COLS = QKV_W + LRU_W
SLAB = 8
N_SLAB = CHUNK // SLAB
TAIL_ROWS = (CONV_W - 1) * SLAB
P_Z, P_X, P_Y, P_G = 3072, 4096, 5120, 6144
P_COLS = 8192
VMEM_LIMIT = 56 * 1024 * 1024


def _rms(x, w):
    return x * lax.rsqrt(jnp.mean(x * x, axis=-1, keepdims=True) + EPS) * w


def _mm(a, b):
    return jnp.dot(a.astype(BF16), b.astype(BF16), preferred_element_type=F32)


def _mm_nt(a, b):
    return lax.dot_general(a.astype(BF16), b.astype(BF16), (((1,), (1,)), ((), ())),
                           preferred_element_type=F32)


def _mm_tn(a, b):
    return lax.dot_general(a.astype(BF16), b.astype(BF16), (((0,), (0,)), ((), ())),
                           preferred_element_type=F32)


def _sigmoid(x):
    return 1.0 / (1.0 + jnp.exp(-x))


def _silu(x):
    return x * _sigmoid(x)


def _softplus(x):
    return jnp.maximum(x, 0.0) + jnp.log1p(jnp.exp(-jnp.abs(x)))


def _gelu_tanh(x):
    return 0.5 * x * (1.0 + jnp.tanh(0.7978845608028654 * (x + 0.044715 * (x * x * x))))


class _Geometry:
    def __init__(self, n_prompt, n_sample, chunks_per_prompt):
        assert n_prompt % 2 == 0 and n_sample % 2 == 0
        self.n_prompt = n_prompt
        self.n_sample = n_sample
        self.cpp = chunks_per_prompt
        self.prompt_steps = (n_prompt // 2) * chunks_per_prompt
        self.steps = self.prompt_steps + n_sample // 2
        self.prompt_chunks = 2 * self.prompt_steps
        self.chunks = 2 * self.steps
        self.rows = self.chunks * CHUNK
        self.n_streams = n_prompt + n_sample

    def chunk_null(self, cid):
        is_prompt = cid < self.prompt_chunks
        first = jnp.logical_or(jnp.logical_not(is_prompt), (cid // 2) % self.cpp == 0)
        return jnp.where(first, jnp.where(is_prompt, PROMPT_NULL, SAMPLE_NULL), 0)

    def last_chunk_ids(self):
        ids = [2 * ((b // 2) * self.cpp + self.cpp - 1) + b % 2 for b in range(self.n_prompt)]
        ids += [self.prompt_chunks + t for t in range(self.n_sample)]
        return np.asarray(ids, np.int32)


def _row_time(rows):
    return N_SLAB * (rows % SLAB) + rows // SLAB


def _valid_rows(null):
    return _row_time(lax.broadcasted_iota(jnp.int32, (CHUNK, 1), 0)) >= null


def _slabs(x):
    return [x[v * SLAB:(v + 1) * SLAB] for v in range(N_SLAB)]


def _shift_run(x, fill):
    return jnp.concatenate([fill, x[:SLAB - 1]], axis=0)


def _scan_time(a, b, h_in):
    bs = _slabs(b)
    if a is None:
        for v in range(1, N_SLAB):
            bs[v] = bs[v] + bs[v - 1]
        run = bs[N_SLAB - 1]
        d = 1
        while d < SLAB:
            run = run + _shift_rows(run, d, 0.0)
            d *= 2
        run = run + h_in
        prev = _shift_run(run, h_in)
        hs = [t + prev for t in bs]
    else:
        as_ = _slabs(a)
        for v in range(1, N_SLAB):
            bs[v] = bs[v] + as_[v] * bs[v - 1]
            as_[v] = as_[v] * as_[v - 1]
        ra, rb = as_[N_SLAB - 1], bs[N_SLAB - 1]
        d = 1
        while d < SLAB:
            rb = rb + ra * _shift_rows(rb, d, 0.0)
            ra = ra * _shift_rows(ra, d, 1.0)
            d *= 2
        run = rb + ra * h_in
        prev = _shift_run(run, h_in)
        hs = [t + u * prev for t, u in zip(bs, as_)]
    return jnp.concatenate(hs, axis=0), run[SLAB - 1:SLAB, :]


def _ffn(x, nw, wgu_ref, wd_ref, h_ref):
    xn = _rms(x, nw).astype(BF16)
    for c in range(2):
        lo = c * FF_HALF
        g = jnp.dot(xn, wgu_ref[:, lo:lo + FF_HALF], preferred_element_type=F32)
        u = jnp.dot(xn, wgu_ref[:, D_FF + lo:D_FF + lo + FF_HALF], preferred_element_type=F32)
        h_ref[:, lo:lo + FF_HALF] = (_silu(g) * u).astype(BF16)
    y = jnp.dot(h_ref[...], wd_ref[...], preferred_element_type=F32)
    return x + 0.5 * y


def _ffn_kernel(x_ref, nw_ref, wgu_ref, wd_ref, o_ref, h_ref):
    o_ref[...] = _ffn(x_ref[...], nw_ref[...], wgu_ref, wd_ref, h_ref)


def _inproj_kernel(geo, x_ref, nw_ref, w_ref, wba_ref, cw_ref, lcw_ref, lcb_ref, inj_ref,
                   p_ref, ba_ref, tails_ref, stage_all, carry_scr):
    i = pl.program_id(0)
    nch = INPROJ_TILE // CHUNK
    xn = _rms(x_ref[...], nw_ref[...]).astype(BF16)

    @pl.when(i == 0)
    def _():
        carry_scr[...] = jnp.zeros_like(carry_scr)

    valid = [_valid_rows(geo.chunk_null(i * nch + j)) for j in range(nch)]
    in_sample = i >= geo.prompt_chunks * CHUNK // INPROJ_TILE
    kinds = (("q", 0, 0), ("k", QK_W, QK_W), ("v", 2 * QK_W, 2 * QK_W), ("x", P_X, QKV_W))
    t0 = TAIL_ROWS

    def project(kidx):
        kind, lo, col = kinds[kidx]
        stage = stage_all.at[kidx]
        pre = jnp.dot(xn, w_ref[:, lo:lo + 1024], preferred_element_type=F32)
        for j in range(nch):
            stage[j, t0:t0 + CHUNK, :] = pre[j * CHUNK:(j + 1) * CHUNK]
            if j >= 2:
                stage[j, 0:t0, :] = pre[(j - 1) * CHUNK - t0:(j - 1) * CHUNK]
            else:
                stage[j, 0:t0, :] = carry_scr[j, :, col:col + 1024]
            stage[j, CHUNK:CHUNK + t0, :] = stage[j, CHUNK:CHUNK + t0, :] + jnp.where(
                in_sample, inj_ref[t0 * j:t0 * (j + 1), col:col + 1024], 0.0)
            tails_ref[j, :, col:col + 1024] = stage[j, CHUNK:CHUNK + t0, :]
        carry_scr[0, :, col:col + 1024] = stage[nch - 2, CHUNK:CHUNK + t0, :]
        carry_scr[1, :, col:col + 1024] = stage[nch - 1, CHUNK:CHUNK + t0, :]

    def convolve(kidx, j):
        kind, lo, col = kinds[kidx]
        stage = stage_all.at[kidx]
        taps = lcw_ref[...] if kind == "x" else cw_ref[:, lo:lo + 1024]
        cur = [stage[j, t0 + v * SLAB:t0 + (v + 1) * SLAB, :] for v in range(N_SLAB)]
        back = []
        for k in range(CONV_W - 1):
            prev_last = stage[j, (k + 1) * SLAB - 1:(k + 1) * SLAB, :]
            back.append(_shift_run(cur[N_SLAB - (CONV_W - 1) + k], prev_last))
        conv = []
        for v in range(N_SLAB):
            acc = cur[v] * taps[CONV_W - 1:CONV_W, :]
            for d in range(1, CONV_W):
                src = cur[v - d] if v >= d else back[CONV_W - 1 - d + v]
                acc = acc + src * taps[CONV_W - 1 - d:CONV_W - d, :]
            conv.append(acc)
        conv = jnp.concatenate(conv, axis=0)
        if kind == "x":
            out = conv + lcb_ref[...]
        else:
            act = _silu(conv)
            if kind != "v":
                scale = DK ** -0.5 if kind == "q" else 1.0
                segs = []
                for h in range(HA):
                    seg = act[:, h * DK:(h + 1) * DK]
                    segs.append(seg * (lax.rsqrt(jnp.sum(seg * seg, axis=-1, keepdims=True) + EPS) * scale))
                act = jnp.concatenate(segs, axis=1)
            out = jnp.where(valid[j], act, 0.0)
        p_ref[j * CHUNK:(j + 1) * CHUNK, lo:lo + 1024] = out.astype(BF16)

    def plain(lo):
        p_ref[:, lo:lo + 1024] = jnp.dot(
            xn, w_ref[:, lo:lo + 1024], preferred_element_type=F32).astype(BF16)

    plain_cols = (P_Z, P_Y, P_G, P_G + 1024)
    project(0)
    for kidx in range(len(kinds)):
        plain(plain_cols[kidx])
        if kidx + 1 < len(kinds):
            project(kidx + 1)
        for j in range(nch):
            convolve(kidx, j)
    ba_ref[...] = jnp.dot(xn, wba_ref[...], preferred_element_type=F32)


def _shift_rows(x, d, fill):
    if d % 8 == 0:
        return jnp.concatenate([jnp.full((d, x.shape[1]), fill, x.dtype), x[:x.shape[0] - d]], axis=0)
    rows = lax.broadcasted_iota(jnp.int32, x.shape, 0)
    return jnp.where(rows >= d, pltpu.roll(x, d, 0), fill)


def _outproj_kernel(geo, final, x_ref, g_ref, oa_ref, xl_ref, h0_ref,
                    wri_ref, br_ref, bi_ref, lam_ref, wa_ref, wb_ref, wo_ref,
                    nw_ref, wgu_ref, wd_ref, fn_ref,
                    o_ref, hlast_ref, h_ref, hc_scr, ob_scr):
    i = pl.program_id(0)
    nch = ROW_TILE // CHUNK

    @pl.when(i == 0)
    def _():
        hc_scr[...] = jnp.zeros_like(hc_scr)

    sp = _softplus(-lam_ref[...])
    nulls = [geo.chunk_null(i * nch + j) for j in range(nch)]
    valid = [_valid_rows(n) for n in nulls]
    is_sample = [(i * nch + j) >= geo.prompt_chunks for j in range(nch)]
    for n in range(NB):
        sl = slice(n * BW, (n + 1) * BW)
        xb = xl_ref[:, sl]
        gates = jnp.dot(xb, wri_ref[n], preferred_element_type=F32)
        r = _sigmoid(gates[:, :BW] + br_ref[:, sl])
        ig = _sigmoid(gates[:, BW:] + bi_ref[:, sl])
        log_a = -LRU_C * r * sp[:, sl]
        a_all = jnp.exp(log_a)
        th = jnp.tanh(log_a)
        mult = jnp.sqrt(-2.0 * th / (1.0 - th))
        b_all = mult * (ig * xb.astype(F32))
        gy = _gelu_tanh(xl_ref[:, LRU_W + n * BW:LRU_W + (n + 1) * BW].astype(F32))
        carry = [hc_scr[0:1, sl], hc_scr[1:2, sl]]
        for j in range(nch):
            rows = slice(j * CHUNK, (j + 1) * CHUNK)
            a = jnp.where(valid[j], a_all[rows], 1.0)
            b = jnp.where(valid[j], b_all[rows], 0.0)
            h_in = jnp.where(nulls[j] > 0, jnp.where(is_sample[j], h0_ref[j:j + 1, sl], 0.0), carry[j % 2])
            hs, carry[j % 2] = _scan_time(a, b, h_in)
            ob_scr[rows, sl] = (hs * gy[rows]).astype(BF16)
            hlast_ref[j:j + 1, sl] = carry[j % 2]
        hc_scr[0:1, sl] = carry[0]
        hc_scr[1:2, sl] = carry[1]

    ga = _sigmoid(g_ref[:, :D_MODEL].astype(F32))
    gb = _sigmoid(g_ref[:, D_MODEL:].astype(F32))
    ma = jnp.dot(oa_ref[...], wa_ref[...], preferred_element_type=F32)
    mb = jnp.dot(ob_scr[...], wb_ref[...], preferred_element_type=F32)
    m = (ga * ma + gb * mb).astype(BF16)
    x = x_ref[...] + jnp.dot(m, wo_ref[...], preferred_element_type=F32)
    x = _ffn(x, nw_ref[...], wgu_ref, wd_ref, h_ref)
    if final:
        x = _rms(x, fn_ref[...])
    o_ref[...] = x


def _resident(shape):
    nd = len(shape)
    return pl.BlockSpec(shape, lambda i: (0,) * nd, pipeline_mode=pl.Buffered(1))


def _params(semantics):
    return pltpu.CompilerParams(dimension_semantics=(semantics,), vmem_limit_bytes=VMEM_LIMIT)


def _ffn_call(x, nw, wgu, wd):
    n = x.shape[0]
    row = lambda i: (i, 0)
    return pl.pallas_call(
        _ffn_kernel,
        grid=(n // ROW_TILE,),
        in_specs=[pl.BlockSpec((ROW_TILE, D_MODEL), row),
                  _resident((1, D_MODEL)),
                  _resident((D_MODEL, 2 * D_FF)),
                  _resident((D_FF, D_MODEL))],
        out_specs=pl.BlockSpec((ROW_TILE, D_MODEL), row),
        out_shape=jax.ShapeDtypeStruct((n, D_MODEL), F32),
        scratch_shapes=[pltpu.VMEM((ROW_TILE, D_FF), BF16)],
        compiler_params=_params("parallel"),
        name="ffn",
    )(x, nw, wgu, wd)


def _inproj_call(geo, x, nw, w, wba, cw, lcw, lcb, inj):
    n = x.shape[0]
    nch = INPROJ_TILE // CHUNK
    sample_tile0 = geo.prompt_chunks * CHUNK // INPROJ_TILE
    row = lambda i: (i, 0)
    return pl.pallas_call(
        functools.partial(_inproj_kernel, geo),
        grid=(n // INPROJ_TILE,),
        in_specs=[pl.BlockSpec((INPROJ_TILE, D_MODEL), row),
                  _resident((1, D_MODEL)),
                  _resident((D_MODEL, P_COLS)),
                  _resident((D_MODEL, 256)),
                  _resident((CONV_W, QKV_W)),
                  _resident((CONV_W, LRU_W)),
                  _resident((1, LRU_W)),
                  pl.BlockSpec((TAIL_ROWS * nch, CONV_COLS), lambda i: (jnp.maximum(i - sample_tile0, 0), 0))],
        out_specs=[pl.BlockSpec((INPROJ_TILE, P_COLS), row),
                   pl.BlockSpec((INPROJ_TILE, 256), row),
                   pl.BlockSpec((nch, TAIL_ROWS, CONV_COLS), lambda i: (i, 0, 0))],
        out_shape=[jax.ShapeDtypeStruct((n, P_COLS), BF16),
                   jax.ShapeDtypeStruct((n, 256), F32),
                   jax.ShapeDtypeStruct((n // CHUNK, TAIL_ROWS, CONV_COLS), F32)],
        scratch_shapes=[pltpu.VMEM((4, nch, CHUNK + TAIL_ROWS, 1024), F32),
                        pltpu.VMEM((2, TAIL_ROWS, CONV_COLS), F32)],
        compiler_params=_params("arbitrary"),
        name="inproj",
    )(x, nw, w, wba, cw, lcw, lcb, inj)


def _outproj_call(geo, x, p, oa, h0, wri, br, bi, lam, wa, wb, wo, nw, wgu, wd, fn, final):
    n = x.shape[0]
    nch = ROW_TILE // CHUNK
    sample_tile0 = geo.prompt_chunks * CHUNK // ROW_TILE
    row = lambda i: (i, 0)
    return pl.pallas_call(
        functools.partial(_outproj_kernel, geo, final),
        grid=(n // ROW_TILE,),
        in_specs=[pl.BlockSpec((ROW_TILE, D_MODEL), row),
                  pl.BlockSpec((ROW_TILE, 2 * D_MODEL), lambda i: (i, P_G // 2048)),
                  pl.BlockSpec((ROW_TILE, V_W), row),
                  pl.BlockSpec((ROW_TILE, 2 * LRU_W), lambda i: (i, P_X // 2048)),
                  pl.BlockSpec((nch, LRU_W), lambda i: (jnp.maximum(i - sample_tile0, 0), 0)),
                  _resident((NB, BW, 2 * BW)),
                  _resident((1, LRU_W)),
                  _resident((1, LRU_W)),
                  _resident((1, LRU_W)),
                  _resident((V_W, D_MODEL)),
                  _resident((LRU_W, D_MODEL)),
                  _resident((D_MODEL, D_MODEL)),
                  _resident((1, D_MODEL)),
                  _resident((D_MODEL, 2 * D_FF)),
                  _resident((D_FF, D_MODEL)),
                  _resident((1, D_MODEL))],
        out_specs=[pl.BlockSpec((ROW_TILE, D_MODEL), row),
                   pl.BlockSpec((nch, LRU_W), row)],
        out_shape=[jax.ShapeDtypeStruct((n, D_MODEL), F32),
                   jax.ShapeDtypeStruct((n // CHUNK, LRU_W), F32)],
        scratch_shapes=[pltpu.VMEM((ROW_TILE, D_FF), BF16),
                        pltpu.VMEM((8, LRU_W), F32),
                        pltpu.VMEM((ROW_TILE, LRU_W), BF16)],
        compiler_params=_params("arbitrary"),
        name="outproj_ffn",
    )(x, p, oa, p, h0, wri, br, bi, lam, wa, wb, wo, nw, wgu, wd, fn)


def _unit_lower_inverse(lows, c):
    ti = _row_time(lax.broadcasted_iota(jnp.int32, (c, c), 0))
    tj = _row_time(lax.broadcasted_iota(jnp.int32, (c, c), 1))
    same_block = (ti // INV_BASE) == (tj // INV_BASE)
    eye = (ti == tj).astype(F32)
    ld = [jnp.where(same_block, low, 0.0) for low in lows]
    lo = [low - d for low, d in zip(lows, ld)]
    x = [eye - d for d in ld]
    p = ld
    span = 2
    while span < INV_BASE:
        p = [_mm(t, t) for t in p]
        x = [a + _mm(a, t) for a, t in zip(x, p)]
        span *= 2
    e = [_mm(a, b) for a, b in zip(x, lo)]
    y = [eye - t for t in e]
    q = e
    span = 2
    while span < c // INV_BASE:
        q = [_mm(t, t) for t in q]
        y = [a + _mm(a, t) for a, t in zip(y, q)]
        span *= 2
    return [_mm(a, b) for a, b in zip(y, x)]


def _delta_kernel(geo, qkv_ref, z_ref, ba_ref, s0_ref, alog_ref, dtb_ref, onorm_ref,
                  oa_ref, sout_ref, s_scr):
    c = CHUNK
    i = pl.program_id(0)
    is_prompt = i < geo.prompt_steps
    pos = i % geo.cpp
    first = jnp.logical_or(jnp.logical_not(is_prompt), pos == 0)
    last = jnp.logical_or(jnp.logical_not(is_prompt), pos == geo.cpp - 1)
    null = jnp.where(first, jnp.where(is_prompt, PROMPT_NULL, SAMPLE_NULL), 0)
    valid = _valid_rows(null)

    @pl.when(jnp.logical_and(first, is_prompt))
    def _():
        s_scr[...] = jnp.zeros_like(s_scr)

    @pl.when(jnp.logical_not(is_prompt))
    def _():
        s_scr[...] = s0_ref[...]

    ti = _row_time(lax.broadcasted_iota(jnp.int32, (c, c), 0))
    tj = _row_time(lax.broadcasted_iota(jnp.int32, (c, c), 1))
    tri_incl = ti >= tj
    tri_strict = ti > tj

    beta_all, gc_all, gc_t, eg_all, ekd_all, egl_all = [], [], [], [], [], []
    for s in range(2):
        ba = ba_ref[s * c:(s + 1) * c, :]
        beta_all.append(jnp.where(valid, _sigmoid(ba[:, :128]), 0.0))
        g = -jnp.exp(alog_ref[...]) * _softplus(ba[:, 128:] + dtb_ref[...])
        gc, g_last = _scan_time(None, jnp.where(valid, g, 0.0), jnp.zeros((1, 128), F32))
        gc_all.append(gc)
        gc_t.append(jnp.concatenate([gc, jnp.zeros((128 - c, 128), F32)], axis=0).T)
        eg_all.append(jnp.exp(gc))
        ekd_all.append(jnp.exp(g_last - gc))
        egl_all.append(jnp.exp(g_last))

    probs = [(s, h) for s in range(2) for h in range(HA)]
    rows = lambda s: slice(s * c, (s + 1) * c)
    qb = [qkv_ref[rows(s), h * DK:(h + 1) * DK] for s, h in probs]
    kbf = [qkv_ref[rows(s), QK_W + h * DK:QK_W + (h + 1) * DK] for s, h in probs]
    vb = [qkv_ref[rows(s), 2 * QK_W + h * DV:2 * QK_W + (h + 1) * DV] for s, h in probs]
    q = [t.astype(F32) for t in qb]
    k = [t.astype(F32) for t in kbf]
    beta = [beta_all[s][:, h:h + 1] for s, h in probs]
    eg = [eg_all[s][:, h:h + 1] for s, h in probs]
    n = len(probs)
    kb = [k[p] * beta[p] for p in range(n)]
    sc = [_mm_nt(jnp.concatenate([kb[p].astype(BF16), qb[p]], axis=0), kbf[p]) for p in range(n)]
    decay = [jnp.where(tri_incl, jnp.exp(jnp.where(tri_incl, gc_all[s][:, h:h + 1] - gc_t[s][h:h + 1, :c], 0.0)), 0.0)
             for s, h in probs]
    low = [jnp.where(tri_strict, sc[p][:c] * decay[p], 0.0) for p in range(n)]
    a_intra = [sc[p][c:] * decay[p] for p in range(n)]
    t_inv = _unit_lower_inverse(low, c)
    uw = [_mm(t_inv[p], jnp.concatenate([vb[p].astype(F32) * beta[p], kb[p] * eg[p]], axis=1))
          for p in range(n)]
    s_old = [s_scr[s, h] for s, h in probs]
    wq = [_mm(jnp.concatenate([uw[p][:, DV:], q[p] * eg[p]], axis=0), s_old[p]) for p in range(n)]
    v_new = [uw[p][:, :DV] - wq[p][:c] for p in range(n)]
    o_intra = [_mm(a_intra[p], v_new[p]) for p in range(n)]
    ds = [_mm_tn(k[p] * ekd_all[s][:, h:h + 1], v_new[p]) for p, (s, h) in enumerate(probs)]
    for p, (s, h) in enumerate(probs):
        s_scr[s, h] = s_old[p] * egl_all[s][:, h:h + 1] + ds[p]
        o = wq[p][c:] + o_intra[p]
        zh = z_ref[rows(s), h * DV:(h + 1) * DV].astype(F32)
        o = o * lax.rsqrt(jnp.mean(o * o, axis=-1, keepdims=True) + EPS) * onorm_ref[...] * _silu(zh)
        oa_ref[rows(s), h * DV:(h + 1) * DV] = o.astype(oa_ref.dtype)

    @pl.when(last)
    def _():
        sout_ref[...] = s_scr[...]


def _delta_call(geo, p, ba, s0, alog, dtb, onorm):
    n_pp = geo.n_prompt // 2

    def pair(i):
        return jnp.where(i < geo.prompt_steps, i // geo.cpp, i - geo.prompt_steps + n_pp)

    const2 = lambda i: (0, 0)
    return pl.pallas_call(
        functools.partial(_delta_kernel, geo),
        grid=(geo.steps,),
        in_specs=[pl.BlockSpec((PAIR_ROWS, QKV_W), lambda i: (i, 0)),
                  pl.BlockSpec((PAIR_ROWS, V_W), lambda i: (i, P_Z // V_W)),
                  pl.BlockSpec((PAIR_ROWS, 256), lambda i: (i, 0)),
                  pl.BlockSpec((2, HA, DK, DV), lambda i: (jnp.maximum(i - geo.prompt_steps, 0), 0, 0, 0)),
                  pl.BlockSpec((1, 128), const2),
                  pl.BlockSpec((1, 128), const2),
                  pl.BlockSpec((1, DV), const2)],
        out_specs=[pl.BlockSpec((PAIR_ROWS, V_W), lambda i: (i, 0)),
                   pl.BlockSpec((2, HA, DK, DV), lambda i: (pair(i), 0, 0, 0))],
        out_shape=[jax.ShapeDtypeStruct((geo.rows, V_W), BF16),
                   jax.ShapeDtypeStruct((geo.n_streams, HA, DK, DV), F32)],
        scratch_shapes=[pltpu.VMEM((2, HA, DK, DV), F32)],
        compiler_params=_params("arbitrary"),
        name="delta",
    )(p, p, ba, s0, alog, dtb, onorm)


def _pad_lanes(v, width):
    return jnp.pad(v, ((0, 0), (0, width - v.shape[-1])))


def kernel(x_prompt, x_sample, state_delta_S, state_delta_conv, state_lru_h, state_lru_conv,
           meta_tokens, ffn1_norm, ffn1_w_gu, ffn1_w_down, mix_norm, w_in, delta_conv_w,
           delta_A_log, delta_dt_bias, delta_out_norm, lru_conv_w, lru_conv_b, lru_w_r, lru_b_r,
           lru_w_i, lru_b_i, lru_lambda, w_branch_a, w_branch_b, w_out, ffn2_norm, ffn2_w_gu,
           ffn2_w_down, final_norm):
    n_prompt, seq_len, _ = x_prompt.shape
    n_sample, dec_len, _ = x_sample.shape
    assert dec_len == CHUNK - SAMPLE_NULL and (N_META + seq_len) % CHUNK == N_META
    prompt_rows = PROMPT_NULL + N_META + seq_len
    cpp = prompt_rows // CHUNK
    geo = _Geometry(n_prompt, n_sample, cpp)
    assert geo.rows % ROW_TILE == 0 and (geo.prompt_chunks * CHUNK) % ROW_TILE == 0
    dt = x_prompt.dtype

    meta = jnp.broadcast_to(meta_tokens.astype(dt)[None], (n_prompt, N_META, D_MODEL))
    xp = jnp.concatenate([jnp.zeros((n_prompt, PROMPT_NULL, D_MODEL), dt), meta, x_prompt], axis=1)
    xp = xp.reshape(n_prompt // 2, 2, cpp, SLAB, N_SLAB, D_MODEL).transpose(0, 2, 1, 4, 3, 5)
    xs = jnp.concatenate([jnp.zeros((n_sample, SAMPLE_NULL, D_MODEL), dt), x_sample], axis=1)
    xs = xs.reshape(n_sample, SLAB, N_SLAB, D_MODEL).transpose(0, 2, 1, 3)
    x = jnp.concatenate([xp.reshape(-1, D_MODEL), xs.reshape(-1, D_MODEL)], axis=0).astype(F32)

    def cast(w):
        return w.astype(BF16)

    inj = jnp.concatenate([state_delta_conv, state_lru_conv], axis=-1).astype(F32)
    null_runs = SAMPLE_NULL // N_SLAB
    inj = jnp.pad(inj[:, :, :, None, :], ((0, 0), (0, 0), (0, 0), (null_runs - 1, SLAB - null_runs), (0, 0)))
    inj = inj.reshape(DEPTH, n_sample * TAIL_ROWS, CONV_COLS)

    last_ids = geo.last_chunk_ids()
    outs_s, outs_tail, outs_h = [], [], []
    for l in range(DEPTH):
        w_main = cast(jnp.concatenate([w_in[l][:, :OFF_BETA], w_in[l][:, OFF_LX:]], axis=-1))
        w_ba = cast(jnp.concatenate(
            [jnp.pad(w_in[l][:, OFF_BETA:OFF_ALPHA], ((0, 0), (0, 128 - HA))),
             jnp.pad(w_in[l][:, OFF_ALPHA:OFF_LX], ((0, 0), (0, 128 - HA)))], axis=-1))
        w_ri = cast(jnp.concatenate([lru_w_r[l], lru_w_i[l]], axis=-1))
        x = _ffn_call(x, ffn1_norm[l][None].astype(F32), cast(ffn1_w_gu[l]), cast(ffn1_w_down[l]))
        p, ba, tails = _inproj_call(geo, x, mix_norm[l][None].astype(F32), w_main, w_ba,
                                    delta_conv_w[l].astype(F32), lru_conv_w[l].astype(F32),
                                    lru_conv_b[l][None].astype(F32), inj[l])
        oa, s_new = _delta_call(geo, p, ba, state_delta_S[l].astype(F32),
                                _pad_lanes(delta_A_log[l][None].astype(F32), 128),
                                _pad_lanes(delta_dt_bias[l][None].astype(F32), 128),
                                delta_out_norm[l][None].astype(F32))
        x, hlast = _outproj_call(geo, x, p, oa, state_lru_h[l].astype(F32), w_ri,
                                 lru_b_r[l][None].astype(F32), lru_b_i[l][None].astype(F32),
                                 lru_lambda[l][None].astype(F32), cast(w_branch_a[l]), cast(w_branch_b[l]),
                                 cast(w_out[l]), ffn2_norm[l][None].astype(F32),
                                 cast(ffn2_w_gu[l]), cast(ffn2_w_down[l]),
                                 final_norm[None].astype(F32), final=(l == DEPTH - 1))
        outs_s.append(s_new)
        outs_tail.append(tails[last_ids, SLAB - 1::SLAB, :])
        outs_h.append(hlast[last_ids])

    n_prompt_rows = geo.prompt_chunks * CHUNK
    y_prompt = x[:n_prompt_rows].reshape(n_prompt // 2, cpp, 2, N_SLAB, SLAB, D_MODEL).transpose(0, 2, 1, 4, 3, 5)
    y_prompt = y_prompt.reshape(n_prompt, prompt_rows, D_MODEL)[:, CHUNK:]
    y_sample = x[n_prompt_rows:].reshape(n_sample, N_SLAB, SLAB, D_MODEL).transpose(0, 2, 1, 3)
    y_sample = y_sample.reshape(n_sample, CHUNK, D_MODEL)[:, SAMPLE_NULL:]
    s_all = jnp.stack(outs_s)
    tail_all = jnp.stack(outs_tail)
    cq_all = tail_all[..., :QKV_W]
    cx_all = tail_all[..., QKV_W:]
    h_all = jnp.stack(outs_h)
    return (y_prompt.astype(dt), y_sample.astype(dt),
            s_all[:, :n_prompt].astype(dt), cq_all[:, :n_prompt].astype(dt),
            h_all[:, :n_prompt].astype(dt), cx_all[:, :n_prompt].astype(dt),
            s_all[:, n_prompt:].astype(state_delta_S.dtype),
            cq_all[:, n_prompt:].astype(state_delta_conv.dtype),
            h_all[:, n_prompt:].astype(state_lru_h.dtype),
            cx_all[:, n_prompt:].astype(state_lru_conv.dtype))
```

```python
import functools

import numpy as np
import jax
import jax.numpy as jnp
from jax import lax
from jax.experimental import pallas as pl
from jax.experimental.pallas import tpu as pltpu

F32 = jnp.float32
BF16 = jnp.bfloat16

D_MODEL = 1024
DEPTH = 4
N_META = 16
HA = 8
DK = 128
DV = 128
QK_W = HA * DK
V_W = HA * DV
QKV_W = 2 * QK_W + V_W
LRU_W = D_MODEL
NB = 8
BW = 128
CONV_W = 4
LRU_C = 8.0
D_FF = 2816
EPS = 1e-6

OFF_Z = QKV_W
OFF_BETA = OFF_Z + V_W
OFF_ALPHA = OFF_BETA + HA
OFF_LX = OFF_ALPHA + HA
OFF_LY = OFF_LX + LRU_W
OFF_GA = OFF_LY + LRU_W
OFF_GB = OFF_GA + D_MODEL
IN_COLS = OFF_GB + D_MODEL

CHUNK = 64
GROUP = 4
GROUP_ROWS = GROUP * CHUNK
PROMPT_NULL = CHUNK - N_META
SAMPLE_NULL = CHUNK // 2
INV_BASE = 16
ROW_TILE = 512
INPROJ_TILE = 256
FF_HALF = D_FF // 2
CONV_COLS = QKV_W + LRU_W
SLAB = 8
N_SLAB = CHUNK // SLAB
TAIL_ROWS = (CONV_W - 1) * SLAB
P_Z, P_X, P_Y, P_G = 3072, 4096, 5120, 6144
P_COLS = 8192
VMEM_LIMIT = 56 * 1024 * 1024


def _rms(x, w):
    return x * lax.rsqrt(jnp.mean(x * x, axis=-1, keepdims=True) + EPS) * w


def _mm(a, b):
    return jnp.dot(a.astype(BF16), b.astype(BF16), preferred_element_type=F32)


def _mm_nt(a, b):
    return lax.dot_general(a.astype(BF16), b.astype(BF16), (((1,), (1,)), ((), ())),
                           preferred_element_type=F32)


def _mm_tn(a, b):
    return lax.dot_general(a.astype(BF16), b.astype(BF16), (((0,), (0,)), ((), ())),
                           preferred_element_type=F32)


def _sigmoid(x):
    return 1.0 / (1.0 + jnp.exp(-x))


def _silu(x):
    return x * _sigmoid(x)


def _softplus(x):
    return jnp.maximum(x, 0.0) + jnp.log1p(jnp.exp(-jnp.abs(x)))


def _gelu_tanh(x):
    return 0.5 * x * (1.0 + jnp.tanh(0.7978845608028654 * (x + 0.044715 * (x * x * x))))


class _Geometry:
    def __init__(self, n_prompt, n_sample, chunks_per_prompt):
        assert n_prompt % GROUP == 0 and n_sample % GROUP == 0
        self.n_prompt = n_prompt
        self.n_sample = n_sample
        self.cpp = chunks_per_prompt
        self.prompt_steps = (n_prompt // GROUP) * chunks_per_prompt
        self.steps = self.prompt_steps + n_sample // GROUP
        self.prompt_chunks = GROUP * self.prompt_steps
        self.chunks = GROUP * self.steps
        self.rows = self.chunks * CHUNK
        self.n_streams = n_prompt + n_sample

    def chunk_null(self, cid):
        is_prompt = cid < self.prompt_chunks
        first = jnp.logical_or(jnp.logical_not(is_prompt), (cid // GROUP) % self.cpp == 0)
        return jnp.where(first, jnp.where(is_prompt, PROMPT_NULL, SAMPLE_NULL), 0)

    def last_chunk_ids(self):
        ids = [GROUP * ((b // GROUP) * self.cpp + self.cpp - 1) + b % GROUP for b in range(self.n_prompt)]
        ids += [self.prompt_chunks + t for t in range(self.n_sample)]
        return np.asarray(ids, np.int32)


def _row_time(rows):
    return N_SLAB * (rows % SLAB) + rows // SLAB


def _valid_rows(null):
    return _row_time(lax.broadcasted_iota(jnp.int32, (CHUNK, 1), 0)) >= null


def _slabs(x):
    return [x[v * SLAB:(v + 1) * SLAB] for v in range(N_SLAB)]


def _shift_run(x, fill):
    return jnp.concatenate([fill, x[:SLAB - 1]], axis=0)


def _scan_time(a, b, h_in):
    bs = _slabs(b)
    if a is None:
        for v in range(1, N_SLAB):
            bs[v] = bs[v] + bs[v - 1]
        run = bs[N_SLAB - 1]
        d = 1
        while d < SLAB:
            run = run + _shift_rows(run, d, 0.0)
            d *= 2
        run = run + h_in
        prev = _shift_run(run, h_in)
        hs = [t + prev for t in bs]
    else:
        as_ = _slabs(a)
        for v in range(1, N_SLAB):
            bs[v] = bs[v] + as_[v] * bs[v - 1]
            as_[v] = as_[v] * as_[v - 1]
        ra, rb = as_[N_SLAB - 1], bs[N_SLAB - 1]
        d = 1
        while d < SLAB:
            rb = rb + ra * _shift_rows(rb, d, 0.0)
            ra = ra * _shift_rows(ra, d, 1.0)
            d *= 2
        run = rb + ra * h_in
        prev = _shift_run(run, h_in)
        hs = [t + u * prev for t, u in zip(bs, as_)]
    return jnp.concatenate(hs, axis=0), run[SLAB - 1:SLAB, :]


def _ffn(x, nw, wgu_ref, wd_ref, h_ref):
    xn = _rms(x, nw).astype(BF16)
    for c in range(2):
        lo = c * FF_HALF
        g = jnp.dot(xn, wgu_ref[:, lo:lo + FF_HALF], preferred_element_type=F32)
        u = jnp.dot(xn, wgu_ref[:, D_FF + lo:D_FF + lo + FF_HALF], preferred_element_type=F32)
        h_ref[:, lo:lo + FF_HALF] = (_silu(g) * u).astype(BF16)
    y = jnp.dot(h_ref[...], wd_ref[...], preferred_element_type=F32)
    return x + 0.5 * y


def _ffn_kernel(x_ref, nw_ref, wgu_ref, wd_ref, o_ref, h_ref):
    o_ref[...] = _ffn(x_ref[...], nw_ref[...], wgu_ref, wd_ref, h_ref)


def _inproj_kernel(geo, x_ref, nw_ref, w_ref, wba_ref, cw_ref, lcw_ref, lcb_ref, inj_ref,
                   p_ref, ba_ref, tails_ref, stage_all, carry_scr):
    i = pl.program_id(0)
    nch = INPROJ_TILE // CHUNK
    xn = _rms(x_ref[...], nw_ref[...]).astype(BF16)

    @pl.when(i == 0)
    def _():
        carry_scr[...] = jnp.zeros_like(carry_scr)

    valid = [_valid_rows(geo.chunk_null(i * nch + j)) for j in range(nch)]
    in_sample = i >= geo.prompt_chunks * CHUNK // INPROJ_TILE
    kinds = (("q", 0, 0), ("k", QK_W, QK_W), ("v", 2 * QK_W, 2 * QK_W), ("x", P_X, QKV_W))
    t0 = TAIL_ROWS

    def project(kidx):
        kind, lo, col = kinds[kidx]
        stage = stage_all.at[kidx]
        pre = jnp.dot(xn, w_ref[:, lo:lo + 1024], preferred_element_type=F32)
        for j in range(nch):
            stage[j, t0:t0 + CHUNK, :] = pre[j * CHUNK:(j + 1) * CHUNK]
            if j >= GROUP:
                stage[j, 0:t0, :] = pre[(j - GROUP + 1) * CHUNK - t0:(j - GROUP + 1) * CHUNK]
            else:
                stage[j, 0:t0, :] = carry_scr[j, :, col:col + 1024]
            stage[j, CHUNK:CHUNK + t0, :] = stage[j, CHUNK:CHUNK + t0, :] + jnp.where(
                in_sample, inj_ref[t0 * j:t0 * (j + 1), col:col + 1024], 0.0)
            tails_ref[j, :, col:col + 1024] = stage[j, CHUNK:CHUNK + t0, :]
        for g in range(GROUP):
            carry_scr[g, :, col:col + 1024] = stage[nch - GROUP + g, CHUNK:CHUNK + t0, :]

    def convolve(kidx, j):
        kind, lo, col = kinds[kidx]
        stage = stage_all.at[kidx]
        taps = lcw_ref[...] if kind == "x" else cw_ref[:, lo:lo + 1024]
        cur = [stage[j, t0 + v * SLAB:t0 + (v + 1) * SLAB, :] for v in range(N_SLAB)]
        back = []
        for k in range(CONV_W - 1):
            prev_last = stage[j, (k + 1) * SLAB - 1:(k + 1) * SLAB, :]
            back.append(_shift_run(cur[N_SLAB - (CONV_W - 1) + k], prev_last))
        conv = []
        for v in range(N_SLAB):
            acc = cur[v] * taps[CONV_W - 1:CONV_W, :]
            for d in range(1, CONV_W):
                src = cur[v - d] if v >= d else back[CONV_W - 1 - d + v]
                acc = acc + src * taps[CONV_W - 1 - d:CONV_W - d, :]
            conv.append(acc)
        conv = jnp.concatenate(conv, axis=0)
        if kind == "x":
            out = conv + lcb_ref[...]
        else:
            act = _silu(conv)
            if kind != "v":
                scale = DK ** -0.5 if kind == "q" else 1.0
                segs = []
                for h in range(HA):
                    seg = act[:, h * DK:(h + 1) * DK]
                    segs.append(seg * (lax.rsqrt(jnp.sum(seg * seg, axis=-1, keepdims=True) + EPS) * scale))
                act = jnp.concatenate(segs, axis=1)
            out = jnp.where(valid[j], act, 0.0)
        p_ref[j * CHUNK:(j + 1) * CHUNK, lo:lo + 1024] = out.astype(BF16)

    def plain(lo):
        p_ref[:, lo:lo + 1024] = jnp.dot(
            xn, w_ref[:, lo:lo + 1024], preferred_element_type=F32).astype(BF16)

    plain_cols = (P_Z, P_Y, P_G, P_G + 1024)
    project(0)
    for kidx in range(len(kinds)):
        plain(plain_cols[kidx])
        if kidx + 1 < len(kinds):
            project(kidx + 1)
        for j in range(nch):
            convolve(kidx, j)
    ba_ref[...] = jnp.dot(xn, wba_ref[...], preferred_element_type=F32)


def _shift_rows(x, d, fill):
    if d % 8 == 0:
        return jnp.concatenate([jnp.full((d, x.shape[1]), fill, x.dtype), x[:x.shape[0] - d]], axis=0)
    rows = lax.broadcasted_iota(jnp.int32, x.shape, 0)
    return jnp.where(rows >= d, pltpu.roll(x, d, 0), fill)


def _outproj_kernel(geo, final, x_ref, g_ref, oa_ref, xl_ref, h0_ref,
                    wri_ref, br_ref, bi_ref, lam_ref, wa_ref, wb_ref, wo_ref,
                    nw_ref, wgu_ref, wd_ref, fn_ref,
                    o_ref, hlast_ref, h_ref, hc_scr, ob_scr):
    i = pl.program_id(0)
    nch = ROW_TILE // CHUNK

    @pl.when(i == 0)
    def _():
        hc_scr[...] = jnp.zeros_like(hc_scr)

    sp = _softplus(-lam_ref[...])
    nulls = [geo.chunk_null(i * nch + j) for j in range(nch)]
    valid = [_valid_rows(n) for n in nulls]
    is_sample = [(i * nch + j) >= geo.prompt_chunks for j in range(nch)]
    for n in range(NB):
        sl = slice(n * BW, (n + 1) * BW)
        xb = xl_ref[:, sl]
        gates = jnp.dot(xb, wri_ref[n], preferred_element_type=F32)
        r = _sigmoid(gates[:, :BW] + br_ref[:, sl])
        ig = _sigmoid(gates[:, BW:] + bi_ref[:, sl])
        log_a = -LRU_C * r * sp[:, sl]
        a_all = jnp.exp(log_a)
        th = jnp.tanh(log_a)
        mult = jnp.sqrt(-2.0 * th / (1.0 - th))
        b_all = mult * (ig * xb.astype(F32))
        gy = _gelu_tanh(xl_ref[:, LRU_W + n * BW:LRU_W + (n + 1) * BW].astype(F32))
        carry = [hc_scr[g:g + 1, sl] for g in range(GROUP)]
        for j in range(nch):
            rows = slice(j * CHUNK, (j + 1) * CHUNK)
            a = jnp.where(valid[j], a_all[rows], 1.0)
            b = jnp.where(valid[j], b_all[rows], 0.0)
            h_in = jnp.where(nulls[j] > 0, jnp.where(is_sample[j], h0_ref[j:j + 1, sl], 0.0), carry[j % GROUP])
            hs, carry[j % GROUP] = _scan_time(a, b, h_in)
            ob_scr[rows, sl] = (hs * gy[rows]).astype(BF16)
            hlast_ref[j:j + 1, sl] = carry[j % GROUP]
        for g in range(GROUP):
            hc_scr[g:g + 1, sl] = carry[g]

    ga = _sigmoid(g_ref[:, :D_MODEL].astype(F32))
    gb = _sigmoid(g_ref[:, D_MODEL:].astype(F32))
    ma = jnp.dot(oa_ref[...], wa_ref[...], preferred_element_type=F32)
    mb = jnp.dot(ob_scr[...], wb_ref[...], preferred_element_type=F32)
    m = (ga * ma + gb * mb).astype(BF16)
    x = x_ref[...] + jnp.dot(m, wo_ref[...], preferred_element_type=F32)
    x = _ffn(x, nw_ref[...], wgu_ref, wd_ref, h_ref)
    if final:
        x = _rms(x, fn_ref[...])
    o_ref[...] = x


def _resident(shape):
    nd = len(shape)
    return pl.BlockSpec(shape, lambda i: (0,) * nd, pipeline_mode=pl.Buffered(1))


def _params(semantics):
    return pltpu.CompilerParams(dimension_semantics=(semantics,), vmem_limit_bytes=VMEM_LIMIT)


def _ffn_call(x, nw, wgu, wd):
    n = x.shape[0]
    row = lambda i: (i, 0)
    return pl.pallas_call(
        _ffn_kernel,
        grid=(n // ROW_TILE,),
        in_specs=[pl.BlockSpec((ROW_TILE, D_MODEL), row),
                  _resident((1, D_MODEL)),
                  _resident((D_MODEL, 2 * D_FF)),
                  _resident((D_FF, D_MODEL))],
        out_specs=pl.BlockSpec((ROW_TILE, D_MODEL), row),
        out_shape=jax.ShapeDtypeStruct((n, D_MODEL), F32),
        scratch_shapes=[pltpu.VMEM((ROW_TILE, D_FF), BF16)],
        compiler_params=_params("parallel"),
        name="ffn",
    )(x, nw, wgu, wd)


def _inproj_call(geo, x, nw, w, wba, cw, lcw, lcb, inj):
    n = x.shape[0]
    nch = INPROJ_TILE // CHUNK
    sample_tile0 = geo.prompt_chunks * CHUNK // INPROJ_TILE
    row = lambda i: (i, 0)
    return pl.pallas_call(
        functools.partial(_inproj_kernel, geo),
        grid=(n // INPROJ_TILE,),
        in_specs=[pl.BlockSpec((INPROJ_TILE, D_MODEL), row),
                  _resident((1, D_MODEL)),
                  _resident((D_MODEL, P_COLS)),
                  _resident((D_MODEL, 256)),
                  _resident((CONV_W, QKV_W)),
                  _resident((CONV_W, LRU_W)),
                  _resident((1, LRU_W)),
                  pl.BlockSpec((TAIL_ROWS * nch, CONV_COLS), lambda i: (jnp.maximum(i - sample_tile0, 0), 0))],
        out_specs=[pl.BlockSpec((INPROJ_TILE, P_COLS), row),
                   pl.BlockSpec((INPROJ_TILE, 256), row),
                   pl.BlockSpec((nch, TAIL_ROWS, CONV_COLS), lambda i: (i, 0, 0))],
        out_shape=[jax.ShapeDtypeStruct((n, P_COLS), BF16),
                   jax.ShapeDtypeStruct((n, 256), F32),
                   jax.ShapeDtypeStruct((n // CHUNK, TAIL_ROWS, CONV_COLS), F32)],
        scratch_shapes=[pltpu.VMEM((4, nch, CHUNK + TAIL_ROWS, 1024), F32),
                        pltpu.VMEM((GROUP, TAIL_ROWS, CONV_COLS), F32)],
        compiler_params=_params("arbitrary"),
        name="inproj",
    )(x, nw, w, wba, cw, lcw, lcb, inj)


def _outproj_call(geo, x, p, oa, h0, wri, br, bi, lam, wa, wb, wo, nw, wgu, wd, fn, final):
    n = x.shape[0]
    nch = ROW_TILE // CHUNK
    sample_tile0 = geo.prompt_chunks * CHUNK // ROW_TILE
    row = lambda i: (i, 0)
    return pl.pallas_call(
        functools.partial(_outproj_kernel, geo, final),
        grid=(n // ROW_TILE,),
        in_specs=[pl.BlockSpec((ROW_TILE, D_MODEL), row),
                  pl.BlockSpec((ROW_TILE, 2 * D_MODEL), lambda i: (i, P_G // 2048)),
                  pl.BlockSpec((ROW_TILE, V_W), row),
                  pl.BlockSpec((ROW_TILE, 2 * LRU_W), lambda i: (i, P_X // 2048)),
                  pl.BlockSpec((nch, LRU_W), lambda i: (jnp.maximum(i - sample_tile0, 0), 0)),
                  _resident((NB, BW, 2 * BW)),
                  _resident((1, LRU_W)),
                  _resident((1, LRU_W)),
                  _resident((1, LRU_W)),
                  _resident((V_W, D_MODEL)),
                  _resident((LRU_W, D_MODEL)),
                  _resident((D_MODEL, D_MODEL)),
                  _resident((1, D_MODEL)),
                  _resident((D_MODEL, 2 * D_FF)),
                  _resident((D_FF, D_MODEL)),
                  _resident((1, D_MODEL))],
        out_specs=[pl.BlockSpec((ROW_TILE, D_MODEL), row),
                   pl.BlockSpec((nch, LRU_W), row)],
        out_shape=[jax.ShapeDtypeStruct((n, D_MODEL), F32),
                   jax.ShapeDtypeStruct((n // CHUNK, LRU_W), F32)],
        scratch_shapes=[pltpu.VMEM((ROW_TILE, D_FF), BF16),
                        pltpu.VMEM((8, LRU_W), F32),
                        pltpu.VMEM((ROW_TILE, LRU_W), BF16)],
        compiler_params=_params("arbitrary"),
        name="outproj_ffn",
    )(x, p, oa, p, h0, wri, br, bi, lam, wa, wb, wo, nw, wgu, wd, fn)


def _unit_lower_inverse(lows, c):
    ti = _row_time(lax.broadcasted_iota(jnp.int32, (c, c), 0))
    tj = _row_time(lax.broadcasted_iota(jnp.int32, (c, c), 1))
    same_block = (ti // INV_BASE) == (tj // INV_BASE)
    eye = (ti == tj).astype(F32)
    ld = [jnp.where(same_block, low, 0.0) for low in lows]
    lo = [low - d for low, d in zip(lows, ld)]
    x = [eye - d for d in ld]
    p = ld
    span = 2
    while span < INV_BASE:
        p = [_mm(t, t) for t in p]
        x = [a + _mm(a, t) for a, t in zip(x, p)]
        span *= 2
    e = [_mm(a, b) for a, b in zip(x, lo)]
    y = [eye - t for t in e]
    q = e
    span = 2
    while span < c // INV_BASE:
        q = [_mm(t, t) for t in q]
        y = [a + _mm(a, t) for a, t in zip(y, q)]
        span *= 2
    return [_mm(a, b) for a, b in zip(y, x)]


def _delta_kernel(geo, qkv_ref, z_ref, ba_ref, s0_ref, alog_ref, dtb_ref, onorm_ref,
                  oa_ref, sout_ref, s_scr):
    c = CHUNK
    i = pl.program_id(0)
    is_prompt = i < geo.prompt_steps
    pos = i % geo.cpp
    first = jnp.logical_or(jnp.logical_not(is_prompt), pos == 0)
    last = jnp.logical_or(jnp.logical_not(is_prompt), pos == geo.cpp - 1)
    null = jnp.where(first, jnp.where(is_prompt, PROMPT_NULL, SAMPLE_NULL), 0)
    valid = _valid_rows(null)

    @pl.when(jnp.logical_and(first, is_prompt))
    def _():
        s_scr[...] = jnp.zeros_like(s_scr)

    @pl.when(jnp.logical_not(is_prompt))
    def _():
        s_scr[...] = s0_ref[...]

    ti = _row_time(lax.broadcasted_iota(jnp.int32, (c, c), 0))
    tj = _row_time(lax.broadcasted_iota(jnp.int32, (c, c), 1))
    tri_incl = ti >= tj
    tri_strict = ti > tj

    beta_all, gc_all, gc_t, eg_all, ekd_all, egl_all = [], [], [], [], [], []
    for s in range(GROUP):
        ba = ba_ref[s * c:(s + 1) * c, :]
        beta_all.append(jnp.where(valid, _sigmoid(ba[:, :128]), 0.0))
        g = -jnp.exp(alog_ref[...]) * _softplus(ba[:, 128:] + dtb_ref[...])
        gc, g_last = _scan_time(None, jnp.where(valid, g, 0.0), jnp.zeros((1, 128), F32))
        gc_all.append(gc)
        gc_t.append(jnp.concatenate([gc, jnp.zeros((128 - c, 128), F32)], axis=0).T)
        eg_all.append(jnp.exp(gc))
        ekd_all.append(jnp.exp(g_last - gc))
        egl_all.append(jnp.exp(g_last))

    probs = [(s, h) for s in range(GROUP) for h in range(HA)]
    rows = lambda s: slice(s * c, (s + 1) * c)
    qb = [qkv_ref[rows(s), h * DK:(h + 1) * DK] for s, h in probs]
    kbf = [qkv_ref[rows(s), QK_W + h * DK:QK_W + (h + 1) * DK] for s, h in probs]
    vb = [qkv_ref[rows(s), 2 * QK_W + h * DV:2 * QK_W + (h + 1) * DV] for s, h in probs]
    q = [t.astype(F32) for t in qb]
    k = [t.astype(F32) for t in kbf]
    beta = [beta_all[s][:, h:h + 1] for s, h in probs]
    eg = [eg_all[s][:, h:h + 1] for s, h in probs]
    n = len(probs)
    kb = [k[p] * beta[p] for p in range(n)]
    sc = [_mm_nt(jnp.concatenate([kb[p].astype(BF16), qb[p]], axis=0), kbf[p]) for p in range(n)]
    decay = [jnp.where(tri_incl, jnp.exp(jnp.where(tri_incl, gc_all[s][:, h:h + 1] - gc_t[s][h:h + 1, :c], 0.0)), 0.0)
             for s, h in probs]
    low = [jnp.where(tri_strict, sc[p][:c] * decay[p], 0.0) for p in range(n)]
    a_intra = [sc[p][c:] * decay[p] for p in range(n)]
    t_inv = _unit_lower_inverse(low, c)
    uw = [_mm(t_inv[p], jnp.concatenate([vb[p].astype(F32) * beta[p], kb[p] * eg[p]], axis=1))
          for p in range(n)]
    s_old = [s_scr[s, h] for s, h in probs]
    wq = [_mm(jnp.concatenate([uw[p][:, DV:], q[p] * eg[p]], axis=0), s_old[p]) for p in range(n)]
    v_new = [uw[p][:, :DV] - wq[p][:c] for p in range(n)]
    o_intra = [_mm(a_intra[p], v_new[p]) for p in range(n)]
    ds = [_mm_tn(k[p] * ekd_all[s][:, h:h + 1], v_new[p]) for p, (s, h) in enumerate(probs)]
    for p, (s, h) in enumerate(probs):
        s_scr[s, h] = s_old[p] * egl_all[s][:, h:h + 1] + ds[p]
        o = wq[p][c:] + o_intra[p]
        zh = z_ref[rows(s), h * DV:(h + 1) * DV].astype(F32)
        o = o * lax.rsqrt(jnp.mean(o * o, axis=-1, keepdims=True) + EPS) * onorm_ref[...] * _silu(zh)
        oa_ref[rows(s), h * DV:(h + 1) * DV] = o.astype(oa_ref.dtype)

    @pl.when(last)
    def _():
        sout_ref[...] = s_scr[...]


def _delta_call(geo, p, ba, s0, alog, dtb, onorm):
    n_pp = geo.n_prompt // GROUP

    def pair(i):
        return jnp.where(i < geo.prompt_steps, i // geo.cpp, i - geo.prompt_steps + n_pp)

    const2 = lambda i: (0, 0)
    return pl.pallas_call(
        functools.partial(_delta_kernel, geo),
        grid=(geo.steps,),
        in_specs=[pl.BlockSpec((GROUP_ROWS, QKV_W), lambda i: (i, 0)),
                  pl.BlockSpec((GROUP_ROWS, V_W), lambda i: (i, P_Z // V_W)),
                  pl.BlockSpec((GROUP_ROWS, 256), lambda i: (i, 0)),
                  pl.BlockSpec((GROUP, HA, DK, DV), lambda i: (jnp.maximum(i - geo.prompt_steps, 0), 0, 0, 0)),
                  pl.BlockSpec((1, 128), const2),
                  pl.BlockSpec((1, 128), const2),
                  pl.BlockSpec((1, DV), const2)],
        out_specs=[pl.BlockSpec((GROUP_ROWS, V_W), lambda i: (i, 0)),
                   pl.BlockSpec((GROUP, HA, DK, DV), lambda i: (pair(i), 0, 0, 0))],
        out_shape=[jax.ShapeDtypeStruct((geo.rows, V_W), BF16),
                   jax.ShapeDtypeStruct((geo.n_streams, HA, DK, DV), F32)],
        scratch_shapes=[pltpu.VMEM((GROUP, HA, DK, DV), F32)],
        compiler_params=_params("arbitrary"),
        name="delta",
    )(p, p, ba, s0, alog, dtb, onorm)


def _pad_lanes(v, width):
    return jnp.pad(v, ((0, 0), (0, width - v.shape[-1])))


def kernel(x_prompt, x_sample, state_delta_S, state_delta_conv, state_lru_h, state_lru_conv,
           meta_tokens, ffn1_norm, ffn1_w_gu, ffn1_w_down, mix_norm, w_in, delta_conv_w,
           delta_A_log, delta_dt_bias, delta_out_norm, lru_conv_w, lru_conv_b, lru_w_r, lru_b_r,
           lru_w_i, lru_b_i, lru_lambda, w_branch_a, w_branch_b, w_out, ffn2_norm, ffn2_w_gu,
           ffn2_w_down, final_norm):
    n_prompt, seq_len, _ = x_prompt.shape
    n_sample, dec_len, _ = x_sample.shape
    assert dec_len == CHUNK - SAMPLE_NULL and (N_META + seq_len) % CHUNK == N_META
    prompt_rows = PROMPT_NULL + N_META + seq_len
    cpp = prompt_rows // CHUNK
    geo = _Geometry(n_prompt, n_sample, cpp)
    assert geo.rows % ROW_TILE == 0 and (geo.prompt_chunks * CHUNK) % ROW_TILE == 0
    assert INPROJ_TILE % GROUP_ROWS == 0 and ROW_TILE % GROUP_ROWS == 0
    dt = x_prompt.dtype

    meta = jnp.broadcast_to(meta_tokens.astype(dt)[None], (n_prompt, N_META, D_MODEL))
    xp = jnp.concatenate([jnp.zeros((n_prompt, PROMPT_NULL, D_MODEL), dt), meta, x_prompt], axis=1)
    xp = xp.reshape(n_prompt // GROUP, GROUP, cpp, SLAB, N_SLAB, D_MODEL).transpose(0, 2, 1, 4, 3, 5)
    xs = jnp.concatenate([jnp.zeros((n_sample, SAMPLE_NULL, D_MODEL), dt), x_sample], axis=1)
    xs = xs.reshape(n_sample, SLAB, N_SLAB, D_MODEL).transpose(0, 2, 1, 3)
    x = jnp.concatenate([xp.reshape(-1, D_MODEL), xs.reshape(-1, D_MODEL)], axis=0).astype(F32)

    def cast(w):
        return w.astype(BF16)

    inj = jnp.concatenate([state_delta_conv, state_lru_conv], axis=-1).astype(F32)
    null_runs = SAMPLE_NULL // N_SLAB
    inj = jnp.pad(inj[:, :, :, None, :], ((0, 0), (0, 0), (0, 0), (null_runs - 1, SLAB - null_runs), (0, 0)))
    inj = inj.reshape(DEPTH, n_sample * TAIL_ROWS, CONV_COLS)

    last_ids = geo.last_chunk_ids()
    outs_s, outs_tail, outs_h = [], [], []
    for l in range(DEPTH):
        w_main = cast(jnp.concatenate([w_in[l][:, :OFF_BETA], w_in[l][:, OFF_LX:]], axis=-1))
        w_ba = cast(jnp.concatenate(
            [jnp.pad(w_in[l][:, OFF_BETA:OFF_ALPHA], ((0, 0), (0, 128 - HA))),
             jnp.pad(w_in[l][:, OFF_ALPHA:OFF_LX], ((0, 0), (0, 128 - HA)))], axis=-1))
        w_ri = cast(jnp.concatenate([lru_w_r[l], lru_w_i[l]], axis=-1))
        x = _ffn_call(x, ffn1_norm[l][None].astype(F32), cast(ffn1_w_gu[l]), cast(ffn1_w_down[l]))
        p, ba, tails = _inproj_call(geo, x, mix_norm[l][None].astype(F32), w_main, w_ba,
                                    delta_conv_w[l].astype(F32), lru_conv_w[l].astype(F32),
                                    lru_conv_b[l][None].astype(F32), inj[l])
        oa, s_new = _delta_call(geo, p, ba, state_delta_S[l].astype(F32),
                                _pad_lanes(delta_A_log[l][None].astype(F32), 128),
                                _pad_lanes(delta_dt_bias[l][None].astype(F32), 128),
                                delta_out_norm[l][None].astype(F32))
        x, hlast = _outproj_call(geo, x, p, oa, state_lru_h[l].astype(F32), w_ri,
                                 lru_b_r[l][None].astype(F32), lru_b_i[l][None].astype(F32),
                                 lru_lambda[l][None].astype(F32), cast(w_branch_a[l]), cast(w_branch_b[l]),
                                 cast(w_out[l]), ffn2_norm[l][None].astype(F32),
                                 cast(ffn2_w_gu[l]), cast(ffn2_w_down[l]),
                                 final_norm[None].astype(F32), final=(l == DEPTH - 1))
        outs_s.append(s_new)
        outs_tail.append(tails[last_ids, SLAB - 1::SLAB, :])
        outs_h.append(hlast[last_ids])

    n_prompt_rows = geo.prompt_chunks * CHUNK
    y_prompt = x[:n_prompt_rows].reshape(n_prompt // GROUP, cpp, GROUP, N_SLAB, SLAB, D_MODEL).transpose(0, 2, 1, 4, 3, 5)
    y_prompt = y_prompt.reshape(n_prompt, prompt_rows, D_MODEL)[:, CHUNK:]
    y_sample = x[n_prompt_rows:].reshape(n_sample, N_SLAB, SLAB, D_MODEL).transpose(0, 2, 1, 3)
    y_sample = y_sample.reshape(n_sample, CHUNK, D_MODEL)[:, SAMPLE_NULL:]
    s_all = jnp.stack(outs_s)
    tail_all = jnp.stack(outs_tail)
    cq_all = tail_all[..., :QKV_W]
    cx_all = tail_all[..., QKV_W:]
    h_all = jnp.stack(outs_h)
    return (y_prompt.astype(dt), y_sample.astype(dt),
            s_all[:, :n_prompt].astype(dt), cq_all[:, :n_prompt].astype(dt),
            h_all[:, :n_prompt].astype(dt), cx_all[:, :n_prompt].astype(dt),
            s_all[:, n_prompt:].astype(state_delta_S.dtype),
            cq_all[:, n_prompt:].astype(state_delta_conv.dtype),
            h_all[:, n_prompt:].astype(state_lru_h.dtype),
            cx_all[:, n_prompt:].astype(state_lru_conv.dtype))
```

```python
import functools

import numpy as np
import jax
import jax.numpy as jnp
from jax import lax
from jax.experimental import pallas as pl
from jax.experimental.pallas import tpu as pltpu

F32 = jnp.float32
BF16 = jnp.bfloat16

D_MODEL = 1024
DEPTH = 4
N_META = 16
HA = 8
DK = 128
DV = 128
QK_W = HA * DK
V_W = HA * DV
QKV_W = 2 * QK_W + V_W
LRU_W = D_MODEL
NB = 8
BW = 128
CONV_W = 4
LRU_C = 8.0
D_FF = 2816
EPS = 1e-6

OFF_Z = QKV_W
OFF_BETA = OFF_Z + V_W
OFF_ALPHA = OFF_BETA + HA
OFF_LX = OFF_ALPHA + HA
OFF_LY = OFF_LX + LRU_W
OFF_GA = OFF_LY + LRU_W
OFF_GB = OFF_GA + D_MODEL
IN_COLS = OFF_GB + D_MODEL

CHUNK = 64
GROUP = 4
GROUP_ROWS = GROUP * CHUNK
PROMPT_NULL = CHUNK - N_META
SAMPLE_NULL = CHUNK // 2
INV_BASE = 16
ROW_TILE = 512
INPROJ_TILE = 256
FF_HALF = D_FF // 2
CONV_COLS = QKV_W + LRU_W
SLAB = 8
N_SLAB = CHUNK // SLAB
TAIL_ROWS = (CONV_W - 1) * SLAB
P_Z, P_X, P_Y, P_G = 3072, 4096, 5120, 6144
P_COLS = 8192
VMEM_LIMIT = 56 * 1024 * 1024


def _rms(x, w):
    return x * lax.rsqrt(jnp.mean(x * x, axis=-1, keepdims=True) + EPS) * w


def _mm(a, b):
    return jnp.dot(a.astype(BF16), b.astype(BF16), preferred_element_type=F32)


def _mm_nt(a, b):
    return lax.dot_general(a.astype(BF16), b.astype(BF16), (((1,), (1,)), ((), ())),
                           preferred_element_type=F32)


def _mm_tn(a, b):
    return lax.dot_general(a.astype(BF16), b.astype(BF16), (((0,), (0,)), ((), ())),
                           preferred_element_type=F32)


def _sigmoid(x):
    return 0.5 * (1.0 + jnp.tanh(0.5 * x))


def _silu(x):
    return x * _sigmoid(x)


def _softplus(x):
    return jnp.maximum(x, 0.0) + jnp.log1p(jnp.exp(-jnp.abs(x)))


def _gelu_tanh(x):
    return 0.5 * x * (1.0 + jnp.tanh(0.7978845608028654 * (x + 0.044715 * (x * x * x))))


class _Geometry:
    def __init__(self, n_prompt, n_sample, chunks_per_prompt):
        assert n_prompt % GROUP == 0 and n_sample % GROUP == 0
        self.n_prompt = n_prompt
        self.n_sample = n_sample
        self.cpp = chunks_per_prompt
        self.prompt_steps = (n_prompt // GROUP) * chunks_per_prompt
        self.steps = self.prompt_steps + n_sample // GROUP
        self.prompt_chunks = GROUP * self.prompt_steps
        self.chunks = GROUP * self.steps
        self.rows = self.chunks * CHUNK
        self.n_streams = n_prompt + n_sample

    def chunk_null(self, cid):
        is_prompt = cid < self.prompt_chunks
        first = jnp.logical_or(jnp.logical_not(is_prompt), (cid // GROUP) % self.cpp == 0)
        return jnp.where(first, jnp.where(is_prompt, PROMPT_NULL, SAMPLE_NULL), 0)

    def last_chunk_ids(self):
        ids = [GROUP * ((b // GROUP) * self.cpp + self.cpp - 1) + b % GROUP for b in range(self.n_prompt)]
        ids += [self.prompt_chunks + t for t in range(self.n_sample)]
        return np.asarray(ids, np.int32)


def _row_time(rows):
    return N_SLAB * (rows % SLAB) + rows // SLAB


def _valid_rows(null):
    return _row_time(lax.broadcasted_iota(jnp.int32, (CHUNK, 1), 0)) >= null


def _slabs(x):
    return [x[v * SLAB:(v + 1) * SLAB] for v in range(N_SLAB)]


def _shift_run(x, fill):
    return jnp.concatenate([fill, x[:SLAB - 1]], axis=0)


def _scan_time(a, b, h_in):
    bs = _slabs(b)
    if a is None:
        for v in range(1, N_SLAB):
            bs[v] = bs[v] + bs[v - 1]
        run = bs[N_SLAB - 1]
        d = 1
        while d < SLAB:
            run = run + _shift_rows(run, d, 0.0)
            d *= 2
        run = run + h_in
        prev = _shift_run(run, h_in)
        hs = [t + prev for t in bs]
    else:
        as_ = _slabs(a)
        for v in range(1, N_SLAB):
            bs[v] = bs[v] + as_[v] * bs[v - 1]
            as_[v] = as_[v] * as_[v - 1]
        ra, rb = as_[N_SLAB - 1], bs[N_SLAB - 1]
        d = 1
        while d < SLAB:
            rb = rb + ra * _shift_rows(rb, d, 0.0)
            ra = ra * _shift_rows(ra, d, 1.0)
            d *= 2
        run = rb + ra * h_in
        prev = _shift_run(run, h_in)
        hs = [t + u * prev for t, u in zip(bs, as_)]
    return jnp.concatenate(hs, axis=0), run[SLAB - 1:SLAB, :]


def _ffn(x, nw, wgu_ref, wd_ref, h_ref):
    xn = _rms(x, nw).astype(BF16)
    for c in range(2):
        lo = c * FF_HALF
        g = jnp.dot(xn, wgu_ref[:, lo:lo + FF_HALF], preferred_element_type=F32)
        u = jnp.dot(xn, wgu_ref[:, D_FF + lo:D_FF + lo + FF_HALF], preferred_element_type=F32)
        h_ref[:, lo:lo + FF_HALF] = (_silu(g) * u).astype(BF16)
    y = jnp.dot(h_ref[...], wd_ref[...], preferred_element_type=F32)
    return x + 0.5 * y


def _ffn_kernel(x_ref, nw_ref, wgu_ref, wd_ref, o_ref, h_ref):
    o_ref[...] = _ffn(x_ref[...], nw_ref[...], wgu_ref, wd_ref, h_ref)


def _inproj_kernel(geo, x_ref, nw_ref, w_ref, wba_ref, cw_ref, lcw_ref, lcb_ref, inj_ref,
                   p_ref, ba_ref, tails_ref, stage_all, carry_scr):
    i = pl.program_id(0)
    nch = INPROJ_TILE // CHUNK
    xn = _rms(x_ref[...], nw_ref[...]).astype(BF16)

    @pl.when(i == 0)
    def _():
        carry_scr[...] = jnp.zeros_like(carry_scr)

    valid = [_valid_rows(geo.chunk_null(i * nch + j)) for j in range(nch)]
    in_sample = i >= geo.prompt_chunks * CHUNK // INPROJ_TILE
    kinds = (("q", 0, 0), ("k", QK_W, QK_W), ("v", 2 * QK_W, 2 * QK_W), ("x", P_X, QKV_W))
    t0 = TAIL_ROWS

    def project(kidx):
        kind, lo, col = kinds[kidx]
        stage = stage_all.at[kidx]
        pre = jnp.dot(xn, w_ref[:, lo:lo + 1024], preferred_element_type=F32)
        for j in range(nch):
            stage[j, t0:t0 + CHUNK, :] = pre[j * CHUNK:(j + 1) * CHUNK]
            if j >= GROUP:
                stage[j, 0:t0, :] = pre[(j - GROUP + 1) * CHUNK - t0:(j - GROUP + 1) * CHUNK]
            else:
                stage[j, 0:t0, :] = carry_scr[j, :, col:col + 1024]
            stage[j, CHUNK:CHUNK + t0, :] = stage[j, CHUNK:CHUNK + t0, :] + jnp.where(
                in_sample, inj_ref[t0 * j:t0 * (j + 1), col:col + 1024], 0.0)
            tails_ref[j, :, col:col + 1024] = stage[j, CHUNK:CHUNK + t0, :]
        for g in range(GROUP):
            carry_scr[g, :, col:col + 1024] = stage[nch - GROUP + g, CHUNK:CHUNK + t0, :]

    def convolve(kidx, j):
        kind, lo, col = kinds[kidx]
        stage = stage_all.at[kidx]
        taps = lcw_ref[...] if kind == "x" else cw_ref[:, lo:lo + 1024]
        cur = [stage[j, t0 + v * SLAB:t0 + (v + 1) * SLAB, :] for v in range(N_SLAB)]
        back = []
        for k in range(CONV_W - 1):
            prev_last = stage[j, (k + 1) * SLAB - 1:(k + 1) * SLAB, :]
            back.append(_shift_run(cur[N_SLAB - (CONV_W - 1) + k], prev_last))
        conv = []
        for v in range(N_SLAB):
            acc = cur[v] * taps[CONV_W - 1:CONV_W, :]
            for d in range(1, CONV_W):
                src = cur[v - d] if v >= d else back[CONV_W - 1 - d + v]
                acc = acc + src * taps[CONV_W - 1 - d:CONV_W - d, :]
            conv.append(acc)
        conv = jnp.concatenate(conv, axis=0)
        if kind == "x":
            out = conv + lcb_ref[...]
        else:
            act = _silu(conv)
            if kind != "v":
                scale = DK ** -0.5 if kind == "q" else 1.0
                segs = []
                for h in range(HA):
                    seg = act[:, h * DK:(h + 1) * DK]
                    segs.append(seg * (lax.rsqrt(jnp.sum(seg * seg, axis=-1, keepdims=True) + EPS) * scale))
                act = jnp.concatenate(segs, axis=1)
            out = jnp.where(valid[j], act, 0.0)
        p_ref[j * CHUNK:(j + 1) * CHUNK, lo:lo + 1024] = out.astype(BF16)

    def plain(lo):
        p_ref[:, lo:lo + 1024] = jnp.dot(
            xn, w_ref[:, lo:lo + 1024], preferred_element_type=F32).astype(BF16)

    plain_cols = (P_Z, P_Y, P_G, P_G + 1024)
    project(0)
    for kidx in range(len(kinds)):
        plain(plain_cols[kidx])
        if kidx + 1 < len(kinds):
            project(kidx + 1)
        for j in range(nch):
            convolve(kidx, j)
    ba_ref[...] = jnp.dot(xn, wba_ref[...], preferred_element_type=F32)


def _shift_rows(x, d, fill):
    if d % 8 == 0:
        return jnp.concatenate([jnp.full((d, x.shape[1]), fill, x.dtype), x[:x.shape[0] - d]], axis=0)
    rows = lax.broadcasted_iota(jnp.int32, x.shape, 0)
    return jnp.where(rows >= d, pltpu.roll(x, d, 0), fill)


def _outproj_kernel(geo, final, x_ref, g_ref, oa_ref, xl_ref, h0_ref,
                    wri_ref, br_ref, bi_ref, lam_ref, wa_ref, wb_ref, wo_ref,
                    nw_ref, wgu_ref, wd_ref, fn_ref,
                    o_ref, hlast_ref, h_ref, hc_scr, ob_scr):
    i = pl.program_id(0)
    nch = ROW_TILE // CHUNK

    @pl.when(i == 0)
    def _():
        hc_scr[...] = jnp.zeros_like(hc_scr)

    sp = _softplus(-lam_ref[...])
    nulls = [geo.chunk_null(i * nch + j) for j in range(nch)]
    valid = [_valid_rows(n) for n in nulls]
    is_sample = [(i * nch + j) >= geo.prompt_chunks for j in range(nch)]
    for n in range(NB):
        sl = slice(n * BW, (n + 1) * BW)
        xb = xl_ref[:, sl]
        gates = jnp.dot(xb, wri_ref[n], preferred_element_type=F32)
        r = _sigmoid(gates[:, :BW] + br_ref[:, sl])
        ig = _sigmoid(gates[:, BW:] + bi_ref[:, sl])
        log_a = -LRU_C * r * sp[:, sl]
        a_all = jnp.exp(log_a)
        th = jnp.tanh(log_a)
        mult = jnp.sqrt(-2.0 * th / (1.0 - th))
        b_all = mult * (ig * xb.astype(F32))
        gy = _gelu_tanh(xl_ref[:, LRU_W + n * BW:LRU_W + (n + 1) * BW].astype(F32))
        carry = [hc_scr[g:g + 1, sl] for g in range(GROUP)]
        for j in range(nch):
            rows = slice(j * CHUNK, (j + 1) * CHUNK)
            a = jnp.where(valid[j], a_all[rows], 1.0)
            b = jnp.where(valid[j], b_all[rows], 0.0)
            h_in = jnp.where(nulls[j] > 0, jnp.where(is_sample[j], h0_ref[j:j + 1, sl], 0.0), carry[j % GROUP])
            hs, carry[j % GROUP] = _scan_time(a, b, h_in)
            ob_scr[rows, sl] = (hs * gy[rows]).astype(BF16)
            hlast_ref[j:j + 1, sl] = carry[j % GROUP]
        for g in range(GROUP):
            hc_scr[g:g + 1, sl] = carry[g]

    ga = _sigmoid(g_ref[:, :D_MODEL].astype(F32))
    gb = _sigmoid(g_ref[:, D_MODEL:].astype(F32))
    ma = jnp.dot(oa_ref[...], wa_ref[...], preferred_element_type=F32)
    mb = jnp.dot(ob_scr[...], wb_ref[...], preferred_element_type=F32)
    m = (ga * ma + gb * mb).astype(BF16)
    x = x_ref[...] + jnp.dot(m, wo_ref[...], preferred_element_type=F32)
    x = _ffn(x, nw_ref[...], wgu_ref, wd_ref, h_ref)
    if final:
        x = _rms(x, fn_ref[...])
    o_ref[...] = x


def _resident(shape):
    nd = len(shape)
    return pl.BlockSpec(shape, lambda i: (0,) * nd, pipeline_mode=pl.Buffered(1))


def _params(semantics):
    return pltpu.CompilerParams(dimension_semantics=(semantics,), vmem_limit_bytes=VMEM_LIMIT)


def _ffn_call(x, nw, wgu, wd):
    n = x.shape[0]
    row = lambda i: (i, 0)
    return pl.pallas_call(
        _ffn_kernel,
        grid=(n // ROW_TILE,),
        in_specs=[pl.BlockSpec((ROW_TILE, D_MODEL), row),
                  _resident((1, D_MODEL)),
                  _resident((D_MODEL, 2 * D_FF)),
                  _resident((D_FF, D_MODEL))],
        out_specs=pl.BlockSpec((ROW_TILE, D_MODEL), row),
        out_shape=jax.ShapeDtypeStruct((n, D_MODEL), F32),
        scratch_shapes=[pltpu.VMEM((ROW_TILE, D_FF), BF16)],
        compiler_params=_params("parallel"),
        name="ffn",
    )(x, nw, wgu, wd)


def _inproj_call(geo, x, nw, w, wba, cw, lcw, lcb, inj):
    n = x.shape[0]
    nch = INPROJ_TILE // CHUNK
    sample_tile0 = geo.prompt_chunks * CHUNK // INPROJ_TILE
    row = lambda i: (i, 0)
    return pl.pallas_call(
        functools.partial(_inproj_kernel, geo),
        grid=(n // INPROJ_TILE,),
        in_specs=[pl.BlockSpec((INPROJ_TILE, D_MODEL), row),
                  _resident((1, D_MODEL)),
                  _resident((D_MODEL, P_COLS)),
                  _resident((D_MODEL, 256)),
                  _resident((CONV_W, QKV_W)),
                  _resident((CONV_W, LRU_W)),
                  _resident((1, LRU_W)),
                  pl.BlockSpec((TAIL_ROWS * nch, CONV_COLS), lambda i: (jnp.maximum(i - sample_tile0, 0), 0))],
        out_specs=[pl.BlockSpec((INPROJ_TILE, P_COLS), row),
                   pl.BlockSpec((INPROJ_TILE, 256), row),
                   pl.BlockSpec((nch, TAIL_ROWS, CONV_COLS), lambda i: (i, 0, 0))],
        out_shape=[jax.ShapeDtypeStruct((n, P_COLS), BF16),
                   jax.ShapeDtypeStruct((n, 256), F32),
                   jax.ShapeDtypeStruct((n // CHUNK, TAIL_ROWS, CONV_COLS), F32)],
        scratch_shapes=[pltpu.VMEM((4, nch, CHUNK + TAIL_ROWS, 1024), F32),
                        pltpu.VMEM((GROUP, TAIL_ROWS, CONV_COLS), F32)],
        compiler_params=_params("arbitrary"),
        name="inproj",
    )(x, nw, w, wba, cw, lcw, lcb, inj)


def _outproj_call(geo, x, p, oa, h0, wri, br, bi, lam, wa, wb, wo, nw, wgu, wd, fn, final):
    n = x.shape[0]
    nch = ROW_TILE // CHUNK
    sample_tile0 = geo.prompt_chunks * CHUNK // ROW_TILE
    row = lambda i: (i, 0)
    return pl.pallas_call(
        functools.partial(_outproj_kernel, geo, final),
        grid=(n // ROW_TILE,),
        in_specs=[pl.BlockSpec((ROW_TILE, D_MODEL), row),
                  pl.BlockSpec((ROW_TILE, 2 * D_MODEL), lambda i: (i, P_G // 2048)),
                  pl.BlockSpec((ROW_TILE, V_W), row),
                  pl.BlockSpec((ROW_TILE, 2 * LRU_W), lambda i: (i, P_X // 2048)),
                  pl.BlockSpec((nch, LRU_W), lambda i: (jnp.maximum(i - sample_tile0, 0), 0)),
                  _resident((NB, BW, 2 * BW)),
                  _resident((1, LRU_W)),
                  _resident((1, LRU_W)),
                  _resident((1, LRU_W)),
                  _resident((V_W, D_MODEL)),
                  _resident((LRU_W, D_MODEL)),
                  _resident((D_MODEL, D_MODEL)),
                  _resident((1, D_MODEL)),
                  _resident((D_MODEL, 2 * D_FF)),
                  _resident((D_FF, D_MODEL)),
                  _resident((1, D_MODEL))],
        out_specs=[pl.BlockSpec((ROW_TILE, D_MODEL), row),
                   pl.BlockSpec((nch, LRU_W), row)],
        out_shape=[jax.ShapeDtypeStruct((n, D_MODEL), F32),
                   jax.ShapeDtypeStruct((n // CHUNK, LRU_W), F32)],
        scratch_shapes=[pltpu.VMEM((ROW_TILE, D_FF), BF16),
                        pltpu.VMEM((8, LRU_W), F32),
                        pltpu.VMEM((ROW_TILE, LRU_W), BF16)],
        compiler_params=_params("arbitrary"),
        name="outproj_ffn",
    )(x, p, oa, p, h0, wri, br, bi, lam, wa, wb, wo, nw, wgu, wd, fn)


def _unit_lower_inverse(lows, c):
    ti = _row_time(lax.broadcasted_iota(jnp.int32, (c, c), 0))
    tj = _row_time(lax.broadcasted_iota(jnp.int32, (c, c), 1))
    same_block = (ti // INV_BASE) == (tj // INV_BASE)
    eye = (ti == tj).astype(F32)
    ld = [jnp.where(same_block, low, 0.0) for low in lows]
    lo = [low - d for low, d in zip(lows, ld)]
    x = [eye - d for d in ld]
    p = ld
    span = 2
    while span < INV_BASE:
        p = [_mm(t, t) for t in p]
        x = [a + _mm(a, t) for a, t in zip(x, p)]
        span *= 2
    e = [_mm(a, b) for a, b in zip(x, lo)]
    y = [eye - t for t in e]
    q = e
    span = 2
    while span < c // INV_BASE:
        q = [_mm(t, t) for t in q]
        y = [a + _mm(a, t) for a, t in zip(y, q)]
        span *= 2
    return [_mm(a, b) for a, b in zip(y, x)]


def _delta_kernel(geo, qkv_ref, z_ref, ba_ref, s0_ref, alog_ref, dtb_ref, onorm_ref,
                  oa_ref, sout_ref, s_scr):
    c = CHUNK
    i = pl.program_id(0)
    is_prompt = i < geo.prompt_steps
    pos = i % geo.cpp
    first = jnp.logical_or(jnp.logical_not(is_prompt), pos == 0)
    last = jnp.logical_or(jnp.logical_not(is_prompt), pos == geo.cpp - 1)
    null = jnp.where(first, jnp.where(is_prompt, PROMPT_NULL, SAMPLE_NULL), 0)
    valid = _valid_rows(null)

    @pl.when(jnp.logical_and(first, is_prompt))
    def _():
        s_scr[...] = jnp.zeros_like(s_scr)

    @pl.when(jnp.logical_not(is_prompt))
    def _():
        s_scr[...] = s0_ref[...]

    ti = _row_time(lax.broadcasted_iota(jnp.int32, (c, c), 0))
    tj = _row_time(lax.broadcasted_iota(jnp.int32, (c, c), 1))
    tri_incl = ti >= tj
    tri_strict = ti > tj

    beta_all, gc_all, gc_t, eg_all, ekd_all, egl_all = [], [], [], [], [], []
    for s in range(GROUP):
        ba = ba_ref[s * c:(s + 1) * c, :]
        beta_all.append(jnp.where(valid, _sigmoid(ba[:, :128]), 0.0))
        g = -jnp.exp(alog_ref[...]) * _softplus(ba[:, 128:] + dtb_ref[...])
        gc, g_last = _scan_time(None, jnp.where(valid, g, 0.0), jnp.zeros((1, 128), F32))
        gc_all.append(gc)
        gc_t.append(jnp.concatenate([gc, jnp.zeros((128 - c, 128), F32)], axis=0).T)
        eg_all.append(jnp.exp(gc))
        ekd_all.append(jnp.exp(g_last - gc))
        egl_all.append(jnp.exp(g_last))

    probs = [(s, h) for s in range(GROUP) for h in range(HA)]
    rows = lambda s: slice(s * c, (s + 1) * c)
    qb = [qkv_ref[rows(s), h * DK:(h + 1) * DK] for s, h in probs]
    kbf = [qkv_ref[rows(s), QK_W + h * DK:QK_W + (h + 1) * DK] for s, h in probs]
    vb = [qkv_ref[rows(s), 2 * QK_W + h * DV:2 * QK_W + (h + 1) * DV] for s, h in probs]
    q = [t.astype(F32) for t in qb]
    k = [t.astype(F32) for t in kbf]
    beta = [beta_all[s][:, h:h + 1] for s, h in probs]
    eg = [eg_all[s][:, h:h + 1] for s, h in probs]
    n = len(probs)
    kb = [k[p] * beta[p] for p in range(n)]
    sc = [_mm_nt(jnp.concatenate([kb[p].astype(BF16), qb[p]], axis=0), kbf[p]) for p in range(n)]
    decay = [jnp.where(tri_incl, jnp.exp(jnp.where(tri_incl, gc_all[s][:, h:h + 1] - gc_t[s][h:h + 1, :c], 0.0)), 0.0)
             for s, h in probs]
    low = [jnp.where(tri_strict, sc[p][:c] * decay[p], 0.0) for p in range(n)]
    a_intra = [sc[p][c:] * decay[p] for p in range(n)]
    t_inv = _unit_lower_inverse(low, c)
    uw = [_mm(t_inv[p], jnp.concatenate([vb[p].astype(F32) * beta[p], kb[p] * eg[p]], axis=1))
          for p in range(n)]
    s_old = [s_scr[s, h] for s, h in probs]
    wq = [_mm(jnp.concatenate([uw[p][:, DV:], q[p] * eg[p]], axis=0), s_old[p]) for p in range(n)]
    v_new = [uw[p][:, :DV] - wq[p][:c] for p in range(n)]
    o_intra = [_mm(a_intra[p], v_new[p]) for p in range(n)]
    ds = [_mm_tn(k[p] * ekd_all[s][:, h:h + 1], v_new[p]) for p, (s, h) in enumerate(probs)]
    for p, (s, h) in enumerate(probs):
        s_scr[s, h] = s_old[p] * egl_all[s][:, h:h + 1] + ds[p]
        o = wq[p][c:] + o_intra[p]
        zh = z_ref[rows(s), h * DV:(h + 1) * DV].astype(F32)
        o = o * lax.rsqrt(jnp.mean(o * o, axis=-1, keepdims=True) + EPS) * onorm_ref[...] * _silu(zh)
        oa_ref[rows(s), h * DV:(h + 1) * DV] = o.astype(oa_ref.dtype)

    @pl.when(last)
    def _():
        sout_ref[...] = s_scr[...]


def _delta_call(geo, p, ba, s0, alog, dtb, onorm):
    n_pp = geo.n_prompt // GROUP

    def pair(i):
        return jnp.where(i < geo.prompt_steps, i // geo.cpp, i - geo.prompt_steps + n_pp)

    const2 = lambda i: (0, 0)
    return pl.pallas_call(
        functools.partial(_delta_kernel, geo),
        grid=(geo.steps,),
        in_specs=[pl.BlockSpec((GROUP_ROWS, QKV_W), lambda i: (i, 0)),
                  pl.BlockSpec((GROUP_ROWS, V_W), lambda i: (i, P_Z // V_W)),
                  pl.BlockSpec((GROUP_ROWS, 256), lambda i: (i, 0)),
                  pl.BlockSpec((GROUP, HA, DK, DV), lambda i: (jnp.maximum(i - geo.prompt_steps, 0), 0, 0, 0)),
                  pl.BlockSpec((1, 128), const2),
                  pl.BlockSpec((1, 128), const2),
                  pl.BlockSpec((1, DV), const2)],
        out_specs=[pl.BlockSpec((GROUP_ROWS, V_W), lambda i: (i, 0)),
                   pl.BlockSpec((GROUP, HA, DK, DV), lambda i: (pair(i), 0, 0, 0))],
        out_shape=[jax.ShapeDtypeStruct((geo.rows, V_W), BF16),
                   jax.ShapeDtypeStruct((geo.n_streams, HA, DK, DV), F32)],
        scratch_shapes=[pltpu.VMEM((GROUP, HA, DK, DV), F32)],
        compiler_params=_params("arbitrary"),
        name="delta",
    )(p, p, ba, s0, alog, dtb, onorm)


def _pad_lanes(v, width):
    return jnp.pad(v, ((0, 0), (0, width - v.shape[-1])))


def kernel(x_prompt, x_sample, state_delta_S, state_delta_conv, state_lru_h, state_lru_conv,
           meta_tokens, ffn1_norm, ffn1_w_gu, ffn1_w_down, mix_norm, w_in, delta_conv_w,
           delta_A_log, delta_dt_bias, delta_out_norm, lru_conv_w, lru_conv_b, lru_w_r, lru_b_r,
           lru_w_i, lru_b_i, lru_lambda, w_branch_a, w_branch_b, w_out, ffn2_norm, ffn2_w_gu,
           ffn2_w_down, final_norm):
    n_prompt, seq_len, _ = x_prompt.shape
    n_sample, dec_len, _ = x_sample.shape
    assert dec_len == CHUNK - SAMPLE_NULL and (N_META + seq_len) % CHUNK == N_META
    prompt_rows = PROMPT_NULL + N_META + seq_len
    cpp = prompt_rows // CHUNK
    geo = _Geometry(n_prompt, n_sample, cpp)
    assert geo.rows % ROW_TILE == 0 and (geo.prompt_chunks * CHUNK) % ROW_TILE == 0
    assert INPROJ_TILE % GROUP_ROWS == 0 and ROW_TILE % GROUP_ROWS == 0
    dt = x_prompt.dtype

    n_groups = n_prompt // GROUP
    first = jnp.concatenate([jnp.zeros((PROMPT_NULL, D_MODEL), dt), meta_tokens.astype(dt)], axis=0)
    first = first.reshape(SLAB, N_SLAB, D_MODEL).transpose(1, 0, 2)
    first = jnp.broadcast_to(first[None, None, None], (n_groups, 1, GROUP, N_SLAB, SLAB, D_MODEL))
    body = x_prompt.reshape(n_groups, GROUP, cpp - 1, SLAB, N_SLAB, D_MODEL).transpose(0, 2, 1, 4, 3, 5)
    xp = jnp.concatenate([first, body], axis=1)
    xs = jnp.concatenate([jnp.zeros((n_sample, SAMPLE_NULL, D_MODEL), dt), x_sample], axis=1)
    xs = xs.reshape(n_sample, SLAB, N_SLAB, D_MODEL).transpose(0, 2, 1, 3)
    x = jnp.concatenate([xp.reshape(-1, D_MODEL), xs.reshape(-1, D_MODEL)], axis=0).astype(F32)

    def cast(w):
        return w.astype(BF16)

    inj = jnp.concatenate([state_delta_conv, state_lru_conv], axis=-1).astype(F32)
    null_runs = SAMPLE_NULL // N_SLAB
    inj = jnp.pad(inj[:, :, :, None, :], ((0, 0), (0, 0), (0, 0), (null_runs - 1, SLAB - null_runs), (0, 0)))
    inj = inj.reshape(DEPTH, n_sample * TAIL_ROWS, CONV_COLS)

    last_ids = geo.last_chunk_ids()
    outs_s, outs_tail, outs_h = [], [], []
    for l in range(DEPTH):
        w_main = cast(jnp.concatenate([w_in[l][:, :OFF_BETA], w_in[l][:, OFF_LX:]], axis=-1))
        w_ba = cast(jnp.concatenate(
            [jnp.pad(w_in[l][:, OFF_BETA:OFF_ALPHA], ((0, 0), (0, 128 - HA))),
             jnp.pad(w_in[l][:, OFF_ALPHA:OFF_LX], ((0, 0), (0, 128 - HA)))], axis=-1))
        w_ri = cast(jnp.concatenate([lru_w_r[l], lru_w_i[l]], axis=-1))
        x = _ffn_call(x, ffn1_norm[l][None].astype(F32), cast(ffn1_w_gu[l]), cast(ffn1_w_down[l]))
        p, ba, tails = _inproj_call(geo, x, mix_norm[l][None].astype(F32), w_main, w_ba,
                                    delta_conv_w[l].astype(F32), lru_conv_w[l].astype(F32),
                                    lru_conv_b[l][None].astype(F32), inj[l])
        oa, s_new = _delta_call(geo, p, ba, state_delta_S[l].astype(F32),
                                _pad_lanes(delta_A_log[l][None].astype(F32), 128),
                                _pad_lanes(delta_dt_bias[l][None].astype(F32), 128),
                                delta_out_norm[l][None].astype(F32))
        x, hlast = _outproj_call(geo, x, p, oa, state_lru_h[l].astype(F32), w_ri,
                                 lru_b_r[l][None].astype(F32), lru_b_i[l][None].astype(F32),
                                 lru_lambda[l][None].astype(F32), cast(w_branch_a[l]), cast(w_branch_b[l]),
                                 cast(w_out[l]), ffn2_norm[l][None].astype(F32),
                                 cast(ffn2_w_gu[l]), cast(ffn2_w_down[l]),
                                 final_norm[None].astype(F32), final=(l == DEPTH - 1))
        outs_s.append(s_new)
        outs_tail.append(tails[last_ids, SLAB - 1::SLAB, :])
        outs_h.append(hlast[last_ids])

    n_prompt_rows = geo.prompt_chunks * CHUNK
    y_prompt = x[:n_prompt_rows].reshape(n_groups, cpp, GROUP, N_SLAB, SLAB, D_MODEL)[:, 1:]
    y_prompt = y_prompt.transpose(0, 2, 1, 4, 3, 5).reshape(n_prompt, seq_len, D_MODEL)
    y_sample = x[n_prompt_rows:].reshape(n_sample, N_SLAB, SLAB, D_MODEL).transpose(0, 2, 1, 3)
    y_sample = y_sample.reshape(n_sample, CHUNK, D_MODEL)[:, SAMPLE_NULL:]
    s_all = jnp.stack(outs_s)
    tail_all = jnp.stack(outs_tail)
    cq_all = tail_all[..., :QKV_W]
    cx_all = tail_all[..., QKV_W:]
    h_all = jnp.stack(outs_h)
    return (y_prompt.astype(dt), y_sample.astype(dt),
            s_all[:, :n_prompt].astype(dt), cq_all[:, :n_prompt].astype(dt),
            h_all[:, :n_prompt].astype(dt), cx_all[:, :n_prompt].astype(dt),
            s_all[:, n_prompt:].astype(state_delta_S.dtype),
            cq_all[:, n_prompt:].astype(state_delta_conv.dtype),
            h_all[:, n_prompt:].astype(state_lru_h.dtype),
            cx_all[:, n_prompt:].astype(state_lru_conv.dtype))
```

```python
import functools

import numpy as np
import jax
import jax.numpy as jnp
from jax import lax
from jax.experimental import pallas as pl
from jax.experimental.pallas import tpu as pltpu

F32 = jnp.float32
BF16 = jnp.bfloat16

D_MODEL = 1024
DEPTH = 4
N_META = 16
HA = 8
DK = 128
DV = 128
QK_W = HA * DK
V_W = HA * DV
QKV_W = 2 * QK_W + V_W
LRU_W = D_MODEL
NB = 8
BW = 128
CONV_W = 4
LRU_C = 8.0
D_FF = 2816
EPS = 1e-6

OFF_Z = QKV_W
OFF_BETA = OFF_Z + V_W
OFF_ALPHA = OFF_BETA + HA
OFF_LX = OFF_ALPHA + HA
OFF_LY = OFF_LX + LRU_W
OFF_GA = OFF_LY + LRU_W
OFF_GB = OFF_GA + D_MODEL
IN_COLS = OFF_GB + D_MODEL

CHUNK = 64
GROUP = 4
GROUP_ROWS = GROUP * CHUNK
PROMPT_NULL = CHUNK - N_META
SAMPLE_NULL = CHUNK // 2
INV_BASE = 16
ROW_TILE = 512
INPROJ_TILE = 256
FF_HALF = D_FF // 2
CONV_COLS = QKV_W + LRU_W
SLAB = 8
N_SLAB = CHUNK // SLAB
TAIL_ROWS = (CONV_W - 1) * SLAB
P_Z, P_X, P_Y, P_G = 3072, 4096, 5120, 6144
P_COLS = 8192
VMEM_LIMIT = 56 * 1024 * 1024


def _rms(x, w):
    return x * lax.rsqrt(jnp.mean(x * x, axis=-1, keepdims=True) + EPS) * w


def _mm(a, b):
    return jnp.dot(a.astype(BF16), b.astype(BF16), preferred_element_type=F32)


def _mm_nt(a, b):
    return lax.dot_general(a.astype(BF16), b.astype(BF16), (((1,), (1,)), ((), ())),
                           preferred_element_type=F32)


def _mm_tn(a, b):
    return lax.dot_general(a.astype(BF16), b.astype(BF16), (((0,), (0,)), ((), ())),
                           preferred_element_type=F32)


def _sigmoid(x):
    return 0.5 * (1.0 + jnp.tanh(0.5 * x))


def _silu(x):
    return x * _sigmoid(x)


def _softplus(x):
    return jnp.maximum(x, 0.0) + jnp.log1p(jnp.exp(-jnp.abs(x)))


def _gelu_tanh(x):
    return 0.5 * x * (1.0 + jnp.tanh(0.7978845608028654 * (x + 0.044715 * (x * x * x))))


class _Geometry:
    def __init__(self, n_prompt, n_sample, chunks_per_prompt):
        assert n_prompt % GROUP == 0 and n_sample % GROUP == 0
        self.n_prompt = n_prompt
        self.n_sample = n_sample
        self.cpp = chunks_per_prompt
        self.prompt_steps = (n_prompt // GROUP) * chunks_per_prompt
        self.steps = self.prompt_steps + n_sample // GROUP
        self.prompt_chunks = GROUP * self.prompt_steps
        self.chunks = GROUP * self.steps
        self.rows = self.chunks * CHUNK
        self.n_streams = n_prompt + n_sample

    def chunk_null(self, cid):
        is_prompt = cid < self.prompt_chunks
        first = jnp.logical_or(jnp.logical_not(is_prompt), (cid // GROUP) % self.cpp == 0)
        return jnp.where(first, jnp.where(is_prompt, PROMPT_NULL, SAMPLE_NULL), 0)

    def last_chunk_ids(self):
        ids = [GROUP * ((b // GROUP) * self.cpp + self.cpp - 1) + b % GROUP for b in range(self.n_prompt)]
        ids += [self.prompt_chunks + t for t in range(self.n_sample)]
        return np.asarray(ids, np.int32)


def _row_time(rows):
    return N_SLAB * (rows % SLAB) + rows // SLAB


def _valid_rows(null):
    return _row_time(lax.broadcasted_iota(jnp.int32, (CHUNK, 1), 0)) >= null


def _slabs(x):
    return [x[v * SLAB:(v + 1) * SLAB] for v in range(N_SLAB)]


def _shift_run(x, fill):
    return jnp.concatenate([fill, x[:SLAB - 1]], axis=0)


def _scan_time(a, b, h_in):
    bs = _slabs(b)
    if a is None:
        for v in range(1, N_SLAB):
            bs[v] = bs[v] + bs[v - 1]
        run = bs[N_SLAB - 1]
        d = 1
        while d < SLAB:
            run = run + _shift_rows(run, d, 0.0)
            d *= 2
        run = run + h_in
        prev = _shift_run(run, h_in)
        hs = [t + prev for t in bs]
    else:
        as_ = _slabs(a)
        for v in range(1, N_SLAB):
            bs[v] = bs[v] + as_[v] * bs[v - 1]
            as_[v] = as_[v] * as_[v - 1]
        ra, rb = as_[N_SLAB - 1], bs[N_SLAB - 1]
        d = 1
        while d < SLAB:
            rb = rb + ra * _shift_rows(rb, d, 0.0)
            ra = ra * _shift_rows(ra, d, 1.0)
            d *= 2
        run = rb + ra * h_in
        prev = _shift_run(run, h_in)
        hs = [t + u * prev for t, u in zip(bs, as_)]
    return jnp.concatenate(hs, axis=0), run[SLAB - 1:SLAB, :]


def _ffn(x, nw, wgu_ref, wd_ref, h_ref):
    xn = _rms(x, nw).astype(BF16)
    for c in range(2):
        lo = c * FF_HALF
        g = jnp.dot(xn, wgu_ref[:, lo:lo + FF_HALF], preferred_element_type=F32)
        u = jnp.dot(xn, wgu_ref[:, D_FF + lo:D_FF + lo + FF_HALF], preferred_element_type=F32)
        h_ref[:, lo:lo + FF_HALF] = (_silu(g) * u).astype(BF16)
    y = jnp.dot(h_ref[...], wd_ref[...], preferred_element_type=F32)
    return x + 0.5 * y


def _ffn_kernel(x_ref, nw_ref, wgu_ref, wd_ref, o_ref, h_ref):
    o_ref[...] = _ffn(x_ref[...], nw_ref[...], wgu_ref, wd_ref, h_ref)


def _inproj_kernel(geo, x_ref, nw_ref, w_ref, wba_ref, cw_ref, lcw_ref, lcb_ref, inj_ref,
                   p_ref, ba_ref, tails_ref, stage_all, carry_scr):
    i = pl.program_id(0)
    nch = INPROJ_TILE // CHUNK
    xn = _rms(x_ref[...], nw_ref[...]).astype(BF16)

    @pl.when(i == 0)
    def _():
        carry_scr[...] = jnp.zeros_like(carry_scr)

    valid = [_valid_rows(geo.chunk_null(i * nch + j)) for j in range(nch)]
    in_sample = i >= geo.prompt_chunks * CHUNK // INPROJ_TILE
    kinds = (("q", 0, 0), ("k", QK_W, QK_W), ("v", 2 * QK_W, 2 * QK_W), ("x", P_X, QKV_W))
    t0 = TAIL_ROWS

    def project(kidx):
        kind, lo, col = kinds[kidx]
        stage = stage_all.at[kidx]
        pre = jnp.dot(xn, w_ref[:, lo:lo + 1024], preferred_element_type=F32)
        for j in range(nch):
            stage[j, t0:t0 + CHUNK, :] = pre[j * CHUNK:(j + 1) * CHUNK]
            if j >= GROUP:
                stage[j, 0:t0, :] = pre[(j - GROUP + 1) * CHUNK - t0:(j - GROUP + 1) * CHUNK]
            else:
                stage[j, 0:t0, :] = carry_scr[j, :, col:col + 1024]
            stage[j, CHUNK:CHUNK + t0, :] = stage[j, CHUNK:CHUNK + t0, :] + jnp.where(
                in_sample, inj_ref[t0 * j:t0 * (j + 1), col:col + 1024], 0.0)
            tails_ref[j, :, col:col + 1024] = stage[j, CHUNK:CHUNK + t0, :]
        for g in range(GROUP):
            carry_scr[g, :, col:col + 1024] = stage[nch - GROUP + g, CHUNK:CHUNK + t0, :]

    def convolve(kidx, j):
        kind, lo, col = kinds[kidx]
        stage = stage_all.at[kidx]
        taps = lcw_ref[...] if kind == "x" else cw_ref[:, lo:lo + 1024]
        cur = [stage[j, t0 + v * SLAB:t0 + (v + 1) * SLAB, :] for v in range(N_SLAB)]
        back = []
        for k in range(CONV_W - 1):
            prev_last = stage[j, (k + 1) * SLAB - 1:(k + 1) * SLAB, :]
            back.append(_shift_run(cur[N_SLAB - (CONV_W - 1) + k], prev_last))
        conv = []
        for v in range(N_SLAB):
            acc = cur[v] * taps[CONV_W - 1:CONV_W, :]
            for d in range(1, CONV_W):
                src = cur[v - d] if v >= d else back[CONV_W - 1 - d + v]
                acc = acc + src * taps[CONV_W - 1 - d:CONV_W - d, :]
            conv.append(acc)
        conv = jnp.concatenate(conv, axis=0)
        if kind == "x":
            out = conv + lcb_ref[...]
        else:
            act = _silu(conv)
            if kind != "v":
                scale = DK ** -0.5 if kind == "q" else 1.0
                segs = []
                for h in range(HA):
                    seg = act[:, h * DK:(h + 1) * DK]
                    segs.append(seg * (lax.rsqrt(jnp.sum(seg * seg, axis=-1, keepdims=True) + EPS) * scale))
                act = jnp.concatenate(segs, axis=1)
            out = jnp.where(valid[j], act, 0.0)
        p_ref[j * CHUNK:(j + 1) * CHUNK, lo:lo + 1024] = out.astype(BF16)

    def plain(lo):
        p_ref[:, lo:lo + 1024] = jnp.dot(
            xn, w_ref[:, lo:lo + 1024], preferred_element_type=F32).astype(BF16)

    plain_cols = (P_Z, P_Y, P_G, P_G + 1024)
    project(0)
    for kidx in range(len(kinds)):
        plain(plain_cols[kidx])
        if kidx + 1 < len(kinds):
            project(kidx + 1)
        for j in range(nch):
            convolve(kidx, j)
    ba_ref[...] = jnp.dot(xn, wba_ref[...], preferred_element_type=F32)


def _shift_rows(x, d, fill):
    if d % 8 == 0:
        return jnp.concatenate([jnp.full((d, x.shape[1]), fill, x.dtype), x[:x.shape[0] - d]], axis=0)
    rows = lax.broadcasted_iota(jnp.int32, x.shape, 0)
    return jnp.where(rows >= d, pltpu.roll(x, d, 0), fill)


def _outproj_kernel(geo, final, x_ref, g_ref, oa_ref, xl_ref, h0_ref,
                    wri_ref, br_ref, bi_ref, lam_ref, wa_ref, wb_ref, wo_ref,
                    nw_ref, wgu_ref, wd_ref, fn_ref,
                    o_ref, hlast_ref, h_ref, hc_scr, ob_scr):
    i = pl.program_id(0)
    nch = ROW_TILE // CHUNK

    @pl.when(i == 0)
    def _():
        hc_scr[...] = jnp.zeros_like(hc_scr)

    sp = _softplus(-lam_ref[...])
    nulls = [geo.chunk_null(i * nch + j) for j in range(nch)]
    valid = [_valid_rows(n) for n in nulls]
    is_sample = [(i * nch + j) >= geo.prompt_chunks for j in range(nch)]
    for n in range(NB):
        sl = slice(n * BW, (n + 1) * BW)
        xb = xl_ref[:, sl]
        gates = jnp.dot(xb, wri_ref[n], preferred_element_type=F32)
        r = _sigmoid(gates[:, :BW] + br_ref[:, sl])
        ig = _sigmoid(gates[:, BW:] + bi_ref[:, sl])
        log_a = -LRU_C * r * sp[:, sl]
        a_all = jnp.exp(log_a)
        th = jnp.tanh(log_a)
        mult = jnp.sqrt(-2.0 * th / (1.0 - th))
        b_all = mult * (ig * xb.astype(F32))
        gy = _gelu_tanh(xl_ref[:, LRU_W + n * BW:LRU_W + (n + 1) * BW].astype(F32))
        carry = [hc_scr[g:g + 1, sl] for g in range(GROUP)]
        for j in range(nch):
            rows = slice(j * CHUNK, (j + 1) * CHUNK)
            a = jnp.where(valid[j], a_all[rows], 1.0)
            b = jnp.where(valid[j], b_all[rows], 0.0)
            h_in = jnp.where(nulls[j] > 0, jnp.where(is_sample[j], h0_ref[j:j + 1, sl], 0.0), carry[j % GROUP])
            hs, carry[j % GROUP] = _scan_time(a, b, h_in)
            ob_scr[rows, sl] = (hs * gy[rows]).astype(BF16)
            hlast_ref[j:j + 1, sl] = carry[j % GROUP]
        for g in range(GROUP):
            hc_scr[g:g + 1, sl] = carry[g]

    ga = _sigmoid(g_ref[:, :D_MODEL].astype(F32))
    gb = _sigmoid(g_ref[:, D_MODEL:].astype(F32))
    ma = jnp.dot(oa_ref[...], wa_ref[...], preferred_element_type=F32)
    mb = jnp.dot(ob_scr[...], wb_ref[...], preferred_element_type=F32)
    m = (ga * ma + gb * mb).astype(BF16)
    x = x_ref[...] + jnp.dot(m, wo_ref[...], preferred_element_type=F32)
    x = _ffn(x, nw_ref[...], wgu_ref, wd_ref, h_ref)
    if final:
        x = _rms(x, fn_ref[...])
    o_ref[...] = x


def _resident(shape, layer):
    nd = len(shape)
    return pl.BlockSpec((None,) + shape, lambda i: (layer,) + (0,) * nd, pipeline_mode=pl.Buffered(1))


def _params(semantics):
    return pltpu.CompilerParams(dimension_semantics=(semantics,), vmem_limit_bytes=VMEM_LIMIT)


def _ffn_call(layer, x, nw, wgu, wd):
    n = x.shape[0]
    row = lambda i: (i, 0)
    return pl.pallas_call(
        _ffn_kernel,
        grid=(n // ROW_TILE,),
        in_specs=[pl.BlockSpec((ROW_TILE, D_MODEL), row),
                  _resident((1, D_MODEL), layer),
                  _resident((D_MODEL, 2 * D_FF), layer),
                  _resident((D_FF, D_MODEL), layer)],
        out_specs=pl.BlockSpec((ROW_TILE, D_MODEL), row),
        out_shape=jax.ShapeDtypeStruct((n, D_MODEL), F32),
        scratch_shapes=[pltpu.VMEM((ROW_TILE, D_FF), BF16)],
        compiler_params=_params("parallel"),
        name="ffn",
    )(x, nw, wgu, wd)


def _inproj_call(geo, layer, x, nw, w, wba, cw, lcw, lcb, inj):
    n = x.shape[0]
    nch = INPROJ_TILE // CHUNK
    sample_tile0 = geo.prompt_chunks * CHUNK // INPROJ_TILE
    row = lambda i: (i, 0)
    return pl.pallas_call(
        functools.partial(_inproj_kernel, geo),
        grid=(n // INPROJ_TILE,),
        in_specs=[pl.BlockSpec((INPROJ_TILE, D_MODEL), row),
                  _resident((1, D_MODEL), layer),
                  _resident((D_MODEL, P_COLS), layer),
                  _resident((D_MODEL, 256), layer),
                  _resident((CONV_W, QKV_W), layer),
                  _resident((CONV_W, LRU_W), layer),
                  _resident((1, LRU_W), layer),
                  pl.BlockSpec((None, TAIL_ROWS * nch, CONV_COLS),
                               lambda i: (layer, jnp.maximum(i - sample_tile0, 0), 0))],
        out_specs=[pl.BlockSpec((INPROJ_TILE, P_COLS), row),
                   pl.BlockSpec((INPROJ_TILE, 256), row),
                   pl.BlockSpec((nch, TAIL_ROWS, CONV_COLS), lambda i: (i, 0, 0))],
        out_shape=[jax.ShapeDtypeStruct((n, P_COLS), BF16),
                   jax.ShapeDtypeStruct((n, 256), F32),
                   jax.ShapeDtypeStruct((n // CHUNK, TAIL_ROWS, CONV_COLS), F32)],
        scratch_shapes=[pltpu.VMEM((4, nch, CHUNK + TAIL_ROWS, 1024), F32),
                        pltpu.VMEM((GROUP, TAIL_ROWS, CONV_COLS), F32)],
        compiler_params=_params("arbitrary"),
        name="inproj",
    )(x, nw, w, wba, cw, lcw, lcb, inj)


def _outproj_call(geo, layer, x, p, oa, h0, wri, br, bi, lam, wa, wb, wo, nw, wgu, wd, fn, final):
    n = x.shape[0]
    nch = ROW_TILE // CHUNK
    sample_tile0 = geo.prompt_chunks * CHUNK // ROW_TILE
    row = lambda i: (i, 0)
    return pl.pallas_call(
        functools.partial(_outproj_kernel, geo, final),
        grid=(n // ROW_TILE,),
        in_specs=[pl.BlockSpec((ROW_TILE, D_MODEL), row),
                  pl.BlockSpec((ROW_TILE, 2 * D_MODEL), lambda i: (i, P_G // 2048)),
                  pl.BlockSpec((ROW_TILE, V_W), row),
                  pl.BlockSpec((ROW_TILE, 2 * LRU_W), lambda i: (i, P_X // 2048)),
                  pl.BlockSpec((None, nch, LRU_W), lambda i: (layer, jnp.maximum(i - sample_tile0, 0), 0)),
                  _resident((NB, BW, 2 * BW), layer),
                  _resident((1, LRU_W), layer),
                  _resident((1, LRU_W), layer),
                  _resident((1, LRU_W), layer),
                  _resident((V_W, D_MODEL), layer),
                  _resident((LRU_W, D_MODEL), layer),
                  _resident((D_MODEL, D_MODEL), layer),
                  _resident((1, D_MODEL), layer),
                  _resident((D_MODEL, 2 * D_FF), layer),
                  _resident((D_FF, D_MODEL), layer),
                  pl.BlockSpec((1, D_MODEL), lambda i: (0, 0), pipeline_mode=pl.Buffered(1))],
        out_specs=[pl.BlockSpec((ROW_TILE, D_MODEL), row),
                   pl.BlockSpec((nch, LRU_W), row)],
        out_shape=[jax.ShapeDtypeStruct((n, D_MODEL), F32),
                   jax.ShapeDtypeStruct((n // CHUNK, LRU_W), F32)],
        scratch_shapes=[pltpu.VMEM((ROW_TILE, D_FF), BF16),
                        pltpu.VMEM((8, LRU_W), F32),
                        pltpu.VMEM((ROW_TILE, LRU_W), BF16)],
        compiler_params=_params("arbitrary"),
        name="outproj_ffn",
    )(x, p, oa, p, h0, wri, br, bi, lam, wa, wb, wo, nw, wgu, wd, fn)


def _unit_lower_inverse(lows, c):
    ti = _row_time(lax.broadcasted_iota(jnp.int32, (c, c), 0))
    tj = _row_time(lax.broadcasted_iota(jnp.int32, (c, c), 1))
    same_block = (ti // INV_BASE) == (tj // INV_BASE)
    eye = (ti == tj).astype(F32)
    ld = [jnp.where(same_block, low, 0.0) for low in lows]
    lo = [low - d for low, d in zip(lows, ld)]
    x = [eye - d for d in ld]
    p = ld
    span = 2
    while span < INV_BASE:
        p = [_mm(t, t) for t in p]
        x = [a + _mm(a, t) for a, t in zip(x, p)]
        span *= 2
    e = [_mm(a, b) for a, b in zip(x, lo)]
    y = [eye - t for t in e]
    q = e
    span = 2
    while span < c // INV_BASE:
        q = [_mm(t, t) for t in q]
        y = [a + _mm(a, t) for a, t in zip(y, q)]
        span *= 2
    return [_mm(a, b) for a, b in zip(y, x)]


def _delta_kernel(geo, qkv_ref, z_ref, ba_ref, s0_ref, alog_ref, dtb_ref, onorm_ref,
                  oa_ref, sout_ref, s_scr):
    c = CHUNK
    i = pl.program_id(0)
    is_prompt = i < geo.prompt_steps
    pos = i % geo.cpp
    first = jnp.logical_or(jnp.logical_not(is_prompt), pos == 0)
    last = jnp.logical_or(jnp.logical_not(is_prompt), pos == geo.cpp - 1)
    null = jnp.where(first, jnp.where(is_prompt, PROMPT_NULL, SAMPLE_NULL), 0)
    valid = _valid_rows(null)

    @pl.when(jnp.logical_and(first, is_prompt))
    def _():
        s_scr[...] = jnp.zeros_like(s_scr)

    @pl.when(jnp.logical_not(is_prompt))
    def _():
        s_scr[...] = s0_ref[...]

    ti = _row_time(lax.broadcasted_iota(jnp.int32, (c, c), 0))
    tj = _row_time(lax.broadcasted_iota(jnp.int32, (c, c), 1))
    tri_incl = ti >= tj
    tri_strict = ti > tj

    beta_all, gc_all, gc_t, eg_all, ekd_all, egl_all = [], [], [], [], [], []
    for s in range(GROUP):
        ba = ba_ref[s * c:(s + 1) * c, :]
        beta_all.append(jnp.where(valid, _sigmoid(ba[:, :128]), 0.0))
        g = -jnp.exp(alog_ref[...]) * _softplus(ba[:, 128:] + dtb_ref[...])
        gc, g_last = _scan_time(None, jnp.where(valid, g, 0.0), jnp.zeros((1, 128), F32))
        gc_all.append(gc)
        gc_t.append(jnp.concatenate([gc, jnp.zeros((128 - c, 128), F32)], axis=0).T)
        eg_all.append(jnp.exp(gc))
        ekd_all.append(jnp.exp(g_last - gc))
        egl_all.append(jnp.exp(g_last))

    probs = [(s, h) for s in range(GROUP) for h in range(HA)]
    rows = lambda s: slice(s * c, (s + 1) * c)
    qb = [qkv_ref[rows(s), h * DK:(h + 1) * DK] for s, h in probs]
    kbf = [qkv_ref[rows(s), QK_W + h * DK:QK_W + (h + 1) * DK] for s, h in probs]
    vb = [qkv_ref[rows(s), 2 * QK_W + h * DV:2 * QK_W + (h + 1) * DV] for s, h in probs]
    q = [t.astype(F32) for t in qb]
    k = [t.astype(F32) for t in kbf]
    beta = [beta_all[s][:, h:h + 1] for s, h in probs]
    eg = [eg_all[s][:, h:h + 1] for s, h in probs]
    n = len(probs)
    kb = [k[p] * beta[p] for p in range(n)]
    sc = [_mm_nt(jnp.concatenate([kb[p].astype(BF16), qb[p]], axis=0), kbf[p]) for p in range(n)]
    decay = [jnp.where(tri_incl, jnp.exp(jnp.where(tri_incl, gc_all[s][:, h:h + 1] - gc_t[s][h:h + 1, :c], 0.0)), 0.0)
             for s, h in probs]
    low = [jnp.where(tri_strict, sc[p][:c] * decay[p], 0.0) for p in range(n)]
    a_intra = [sc[p][c:] * decay[p] for p in range(n)]
    t_inv = _unit_lower_inverse(low, c)
    uw = [_mm(t_inv[p], jnp.concatenate([vb[p].astype(F32) * beta[p], kb[p] * eg[p]], axis=1))
          for p in range(n)]
    s_old = [s_scr[s, h] for s, h in probs]
    wq = [_mm(jnp.concatenate([uw[p][:, DV:], q[p] * eg[p]], axis=0), s_old[p]) for p in range(n)]
    v_new = [uw[p][:, :DV] - wq[p][:c] for p in range(n)]
    o_intra = [_mm(a_intra[p], v_new[p]) for p in range(n)]
    ds = [_mm_tn(k[p] * ekd_all[s][:, h:h + 1], v_new[p]) for p, (s, h) in enumerate(probs)]
    for p, (s, h) in enumerate(probs):
        s_scr[s, h] = s_old[p] * egl_all[s][:, h:h + 1] + ds[p]
        o = wq[p][c:] + o_intra[p]
        zh = z_ref[rows(s), h * DV:(h + 1) * DV].astype(F32)
        o = o * lax.rsqrt(jnp.mean(o * o, axis=-1, keepdims=True) + EPS) * onorm_ref[...] * _silu(zh)
        oa_ref[rows(s), h * DV:(h + 1) * DV] = o.astype(oa_ref.dtype)

    @pl.when(last)
    def _():
        sout_ref[...] = s_scr[...]


def _delta_call(geo, layer, p, ba, s0, alog, dtb, onorm):
    n_pp = geo.n_prompt // GROUP

    def pair(i):
        return jnp.where(i < geo.prompt_steps, i // geo.cpp, i - geo.prompt_steps + n_pp)

    vec = lambda width: pl.BlockSpec((None, 1, width), lambda i: (layer, 0, 0))
    return pl.pallas_call(
        functools.partial(_delta_kernel, geo),
        grid=(geo.steps,),
        in_specs=[pl.BlockSpec((GROUP_ROWS, QKV_W), lambda i: (i, 0)),
                  pl.BlockSpec((GROUP_ROWS, V_W), lambda i: (i, P_Z // V_W)),
                  pl.BlockSpec((GROUP_ROWS, 256), lambda i: (i, 0)),
                  pl.BlockSpec((None, GROUP, HA, DK, DV),
                               lambda i: (layer, jnp.maximum(i - geo.prompt_steps, 0), 0, 0, 0)),
                  vec(128), vec(128), vec(DV)],
        out_specs=[pl.BlockSpec((GROUP_ROWS, V_W), lambda i: (i, 0)),
                   pl.BlockSpec((GROUP, HA, DK, DV), lambda i: (pair(i), 0, 0, 0))],
        out_shape=[jax.ShapeDtypeStruct((geo.rows, V_W), BF16),
                   jax.ShapeDtypeStruct((geo.n_streams, HA, DK, DV), F32)],
        scratch_shapes=[pltpu.VMEM((GROUP, HA, DK, DV), F32)],
        compiler_params=_params("arbitrary"),
        name="delta",
    )(p, p, ba, s0, alog, dtb, onorm)


def _pad_lanes(v, width):
    return jnp.pad(v, ((0, 0), (0, width - v.shape[-1])))


def kernel(x_prompt, x_sample, state_delta_S, state_delta_conv, state_lru_h, state_lru_conv,
           meta_tokens, ffn1_norm, ffn1_w_gu, ffn1_w_down, mix_norm, w_in, delta_conv_w,
           delta_A_log, delta_dt_bias, delta_out_norm, lru_conv_w, lru_conv_b, lru_w_r, lru_b_r,
           lru_w_i, lru_b_i, lru_lambda, w_branch_a, w_branch_b, w_out, ffn2_norm, ffn2_w_gu,
           ffn2_w_down, final_norm):
    n_prompt, seq_len, _ = x_prompt.shape
    n_sample, dec_len, _ = x_sample.shape
    assert dec_len == CHUNK - SAMPLE_NULL and (N_META + seq_len) % CHUNK == N_META
    prompt_rows = PROMPT_NULL + N_META + seq_len
    cpp = prompt_rows // CHUNK
    geo = _Geometry(n_prompt, n_sample, cpp)
    assert geo.rows % ROW_TILE == 0 and (geo.prompt_chunks * CHUNK) % ROW_TILE == 0
    assert INPROJ_TILE % GROUP_ROWS == 0 and ROW_TILE % GROUP_ROWS == 0
    dt = x_prompt.dtype

    n_groups = n_prompt // GROUP
    first = jnp.concatenate([jnp.zeros((PROMPT_NULL, D_MODEL), dt), meta_tokens.astype(dt)], axis=0)
    first = first.reshape(SLAB, N_SLAB, D_MODEL).transpose(1, 0, 2)
    first = jnp.broadcast_to(first[None, None, None], (n_groups, 1, GROUP, N_SLAB, SLAB, D_MODEL))
    body = x_prompt.reshape(n_groups, GROUP, cpp - 1, SLAB, N_SLAB, D_MODEL).transpose(0, 2, 1, 4, 3, 5)
    xp = jnp.concatenate([first, body], axis=1)
    xs = jnp.concatenate([jnp.zeros((n_sample, SAMPLE_NULL, D_MODEL), dt), x_sample], axis=1)
    xs = xs.reshape(n_sample, SLAB, N_SLAB, D_MODEL).transpose(0, 2, 1, 3)
    x = jnp.concatenate([xp.reshape(-1, D_MODEL), xs.reshape(-1, D_MODEL)], axis=0).astype(F32)

    cast = lambda w: w.astype(BF16)
    vec = lambda v: v.astype(F32)[:, None, :]
    wgu1, wd1, wgu2, wd2 = cast(ffn1_w_gu), cast(ffn1_w_down), cast(ffn2_w_gu), cast(ffn2_w_down)
    w_main = cast(jnp.concatenate([w_in[:, :, :OFF_BETA], w_in[:, :, OFF_LX:]], axis=-1))
    pad_heads = lambda w: jnp.pad(w, ((0, 0), (0, 0), (0, 128 - HA)))
    w_ba = cast(jnp.concatenate([pad_heads(w_in[:, :, OFF_BETA:OFF_ALPHA]),
                                 pad_heads(w_in[:, :, OFF_ALPHA:OFF_LX])], axis=-1))
    w_ri = cast(jnp.concatenate([lru_w_r, lru_w_i], axis=-1))
    wa, wb, wo = cast(w_branch_a), cast(w_branch_b), cast(w_out)
    norm1, norm_mix, norm2 = vec(ffn1_norm), vec(mix_norm), vec(ffn2_norm)
    alog, dtb, onorm = vec(_pad_lanes(delta_A_log, 128)), vec(_pad_lanes(delta_dt_bias, 128)), vec(delta_out_norm)
    lcb, b_r, b_i, lam = vec(lru_conv_b), vec(lru_b_r), vec(lru_b_i), vec(lru_lambda)
    cw, lcw = delta_conv_w.astype(F32), lru_conv_w.astype(F32)
    s0, h0 = state_delta_S.astype(F32), state_lru_h.astype(F32)

    inj = jnp.concatenate([state_delta_conv, state_lru_conv], axis=-1).astype(F32)
    null_runs = SAMPLE_NULL // N_SLAB
    inj = jnp.pad(inj[:, :, :, None, :], ((0, 0), (0, 0), (0, 0), (null_runs - 1, SLAB - null_runs), (0, 0)))
    inj = inj.reshape(DEPTH, n_sample * TAIL_ROWS, CONV_COLS)

    last_ids = geo.last_chunk_ids()
    outs_s, outs_tail, outs_h = [], [], []
    for l in range(DEPTH):
        x = _ffn_call(l, x, norm1, wgu1, wd1)
        p, ba, tails = _inproj_call(geo, l, x, norm_mix, w_main, w_ba, cw, lcw, lcb, inj)
        oa, s_new = _delta_call(geo, l, p, ba, s0, alog, dtb, onorm)
        x, hlast = _outproj_call(geo, l, x, p, oa, h0, w_ri, b_r, b_i, lam, wa, wb, wo, norm2, wgu2, wd2,
                                 final_norm[None].astype(F32), final=(l == DEPTH - 1))
        outs_s.append(s_new)
        outs_tail.append(tails[last_ids, SLAB - 1::SLAB, :])
        outs_h.append(hlast[last_ids])

    n_prompt_rows = geo.prompt_chunks * CHUNK
    y_prompt = x[:n_prompt_rows].reshape(n_groups, cpp, GROUP, N_SLAB, SLAB, D_MODEL)[:, 1:]
    y_prompt = y_prompt.transpose(0, 2, 1, 4, 3, 5).reshape(n_prompt, seq_len, D_MODEL)
    y_sample = x[n_prompt_rows:].reshape(n_sample, N_SLAB, SLAB, D_MODEL).transpose(0, 2, 1, 3)
    y_sample = y_sample.reshape(n_sample, CHUNK, D_MODEL)[:, SAMPLE_NULL:]
    s_all = jnp.stack(outs_s)
    tail_all = jnp.stack(outs_tail)
    cq_all = tail_all[..., :QKV_W]
    cx_all = tail_all[..., QKV_W:]
    h_all = jnp.stack(outs_h)
    return (y_prompt.astype(dt), y_sample.astype(dt),
            s_all[:, :n_prompt].astype(dt), cq_all[:, :n_prompt].astype(dt),
            h_all[:, :n_prompt].astype(dt), cx_all[:, :n_prompt].astype(dt),
            s_all[:, n_prompt:].astype(state_delta_S.dtype),
            cq_all[:, n_prompt:].astype(state_delta_conv.dtype),
            h_all[:, n_prompt:].astype(state_lru_h.dtype),
            cx_all[:, n_prompt:].astype(state_lru_conv.dtype))
```

```python
import functools

import numpy as np
import jax
import jax.numpy as jnp
from jax import lax
from jax.experimental import pallas as pl
from jax.experimental.pallas import tpu as pltpu

F32 = jnp.float32
BF16 = jnp.bfloat16

D_MODEL = 1024
DEPTH = 4
N_META = 16
HA = 8
DK = 128
DV = 128
QK_W = HA * DK
V_W = HA * DV
QKV_W = 2 * QK_W + V_W
LRU_W = D_MODEL
NB = 8
BW = 128
CONV_W = 4
LRU_C = 8.0
D_FF = 2816
EPS = 1e-6

OFF_Z = QKV_W
OFF_BETA = OFF_Z + V_W
OFF_ALPHA = OFF_BETA + HA
OFF_LX = OFF_ALPHA + HA
OFF_LY = OFF_LX + LRU_W
OFF_GA = OFF_LY + LRU_W
OFF_GB = OFF_GA + D_MODEL
IN_COLS = OFF_GB + D_MODEL

CHUNK = 64
GROUP = 4
GROUP_ROWS = GROUP * CHUNK
PROMPT_NULL = CHUNK - N_META
SAMPLE_NULL = CHUNK // 2
INV_BASE = 16
ROW_TILE = 512
INPROJ_TILE = 256
FF_HALF = D_FF // 2
CONV_COLS = QKV_W + LRU_W
SLAB = 8
N_SLAB = CHUNK // SLAB
TAIL_ROWS = (CONV_W - 1) * SLAB
P_Z, P_X, P_Y, P_G = 3072, 4096, 5120, 6144
P_COLS = 8192
VMEM_LIMIT = 56 * 1024 * 1024


def _rms(x, w):
    return x * lax.rsqrt(jnp.mean(x * x, axis=-1, keepdims=True) + EPS) * w


def _mm(a, b):
    return jnp.dot(a.astype(BF16), b.astype(BF16), preferred_element_type=F32)


def _mm_nt(a, b):
    return lax.dot_general(a.astype(BF16), b.astype(BF16), (((1,), (1,)), ((), ())),
                           preferred_element_type=F32)


def _mm_tn(a, b):
    return lax.dot_general(a.astype(BF16), b.astype(BF16), (((0,), (0,)), ((), ())),
                           preferred_element_type=F32)


def _sigmoid(x):
    return 0.5 * (1.0 + jnp.tanh(0.5 * x))


def _silu(x):
    return x * _sigmoid(x)


def _softplus(x):
    return jnp.maximum(x, 0.0) + jnp.log1p(jnp.exp(-jnp.abs(x)))


def _gelu_tanh(x):
    return 0.5 * x * (1.0 + jnp.tanh(0.7978845608028654 * (x + 0.044715 * (x * x * x))))


class _Geometry:
    def __init__(self, n_prompt, n_sample, chunks_per_prompt):
        assert n_prompt % GROUP == 0 and n_sample % GROUP == 0
        self.n_prompt = n_prompt
        self.n_sample = n_sample
        self.cpp = chunks_per_prompt
        self.prompt_steps = (n_prompt // GROUP) * chunks_per_prompt
        self.steps = self.prompt_steps + n_sample // GROUP
        self.prompt_chunks = GROUP * self.prompt_steps
        self.chunks = GROUP * self.steps
        self.rows = self.chunks * CHUNK
        self.n_streams = n_prompt + n_sample

    def chunk_null(self, cid):
        is_prompt = cid < self.prompt_chunks
        first = jnp.logical_or(jnp.logical_not(is_prompt), (cid // GROUP) % self.cpp == 0)
        return jnp.where(first, jnp.where(is_prompt, PROMPT_NULL, SAMPLE_NULL), 0)

    def last_chunk_ids(self):
        ids = [GROUP * ((b // GROUP) * self.cpp + self.cpp - 1) + b % GROUP for b in range(self.n_prompt)]
        ids += [self.prompt_chunks + t for t in range(self.n_sample)]
        return np.asarray(ids, np.int32)


def _row_time(rows):
    return N_SLAB * (rows % SLAB) + rows // SLAB


def _valid_rows(null):
    return _row_time(lax.broadcasted_iota(jnp.int32, (CHUNK, 1), 0)) >= null


def _slabs(x):
    return [x[v * SLAB:(v + 1) * SLAB] for v in range(N_SLAB)]


def _shift_run(x, fill):
    return jnp.concatenate([fill, x[:SLAB - 1]], axis=0)


def _scan_time(a, b, h_in):
    bs = _slabs(b)
    if a is None:
        for v in range(1, N_SLAB):
            bs[v] = bs[v] + bs[v - 1]
        run = bs[N_SLAB - 1]
        d = 1
        while d < SLAB:
            run = run + _shift_rows(run, d, 0.0)
            d *= 2
        run = run + h_in
        prev = _shift_run(run, h_in)
        hs = [t + prev for t in bs]
    else:
        as_ = _slabs(a)
        for v in range(1, N_SLAB):
            bs[v] = bs[v] + as_[v] * bs[v - 1]
            as_[v] = as_[v] * as_[v - 1]
        ra, rb = as_[N_SLAB - 1], bs[N_SLAB - 1]
        d = 1
        while d < SLAB:
            rb = rb + ra * _shift_rows(rb, d, 0.0)
            ra = ra * _shift_rows(ra, d, 1.0)
            d *= 2
        run = rb + ra * h_in
        prev = _shift_run(run, h_in)
        hs = [t + u * prev for t, u in zip(bs, as_)]
    return jnp.concatenate(hs, axis=0), run[SLAB - 1:SLAB, :]


def _ffn(x, nw, wgu_ref, wd_ref, h_ref):
    xn = _rms(x, nw).astype(BF16)
    for c in range(2):
        lo = c * FF_HALF
        g = jnp.dot(xn, wgu_ref[:, lo:lo + FF_HALF], preferred_element_type=F32)
        u = jnp.dot(xn, wgu_ref[:, D_FF + lo:D_FF + lo + FF_HALF], preferred_element_type=F32)
        h_ref[:, lo:lo + FF_HALF] = (_silu(g) * u).astype(BF16)
    y = jnp.dot(h_ref[...], wd_ref[...], preferred_element_type=F32)
    return x + 0.5 * y


def _ffn_kernel(x_ref, nw_ref, wgu_ref, wd_ref, o_ref, h_ref):
    o_ref[...] = _ffn(x_ref[...], nw_ref[...], wgu_ref, wd_ref, h_ref)


def _inproj_kernel(geo, x_ref, nw_ref, w_ref, wba_ref, cw_ref, lcw_ref, lcb_ref, inj_ref,
                   p_ref, ba_ref, tails_ref, stage_all, carry_scr):
    i = pl.program_id(0)
    nch = INPROJ_TILE // CHUNK
    xn = _rms(x_ref[...], nw_ref[...]).astype(BF16)

    @pl.when(i == 0)
    def _():
        carry_scr[...] = jnp.zeros_like(carry_scr)

    valid = [_valid_rows(geo.chunk_null(i * nch + j)) for j in range(nch)]
    in_sample = i >= geo.prompt_chunks * CHUNK // INPROJ_TILE
    kinds = (("q", 0, 0), ("k", QK_W, QK_W), ("v", 2 * QK_W, 2 * QK_W), ("x", P_X, QKV_W))
    t0 = TAIL_ROWS

    def project(kidx):
        kind, lo, col = kinds[kidx]
        stage = stage_all.at[kidx]
        pre = jnp.dot(xn, w_ref[:, lo:lo + 1024], preferred_element_type=F32)
        for j in range(nch):
            stage[j, t0:t0 + CHUNK, :] = pre[j * CHUNK:(j + 1) * CHUNK]
            if j >= GROUP:
                stage[j, 0:t0, :] = pre[(j - GROUP + 1) * CHUNK - t0:(j - GROUP + 1) * CHUNK]
            else:
                stage[j, 0:t0, :] = carry_scr[j, :, col:col + 1024]
            stage[j, CHUNK:CHUNK + t0, :] = stage[j, CHUNK:CHUNK + t0, :] + jnp.where(
                in_sample, inj_ref[t0 * j:t0 * (j + 1), col:col + 1024], 0.0)
            tails_ref[j, :, col:col + 1024] = stage[j, CHUNK:CHUNK + t0, :]
        for g in range(GROUP):
            carry_scr[g, :, col:col + 1024] = stage[nch - GROUP + g, CHUNK:CHUNK + t0, :]

    def convolve(kidx, j):
        kind, lo, col = kinds[kidx]
        stage = stage_all.at[kidx]
        taps = lcw_ref[...] if kind == "x" else cw_ref[:, lo:lo + 1024]
        cur = [stage[j, t0 + v * SLAB:t0 + (v + 1) * SLAB, :] for v in range(N_SLAB)]
        back = []
        for k in range(CONV_W - 1):
            prev_last = stage[j, (k + 1) * SLAB - 1:(k + 1) * SLAB, :]
            back.append(_shift_run(cur[N_SLAB - (CONV_W - 1) + k], prev_last))
        conv = []
        for v in range(N_SLAB):
            acc = cur[v] * taps[CONV_W - 1:CONV_W, :]
            for d in range(1, CONV_W):
                src = cur[v - d] if v >= d else back[CONV_W - 1 - d + v]
                acc = acc + src * taps[CONV_W - 1 - d:CONV_W - d, :]
            conv.append(acc)
        conv = jnp.concatenate(conv, axis=0)
        if kind == "x":
            out = conv + lcb_ref[...]
        else:
            act = _silu(conv)
            if kind != "v":
                scale = DK ** -0.5 if kind == "q" else 1.0
                segs = []
                for h in range(HA):
                    seg = act[:, h * DK:(h + 1) * DK]
                    segs.append(seg * (lax.rsqrt(jnp.sum(seg * seg, axis=-1, keepdims=True) + EPS) * scale))
                act = jnp.concatenate(segs, axis=1)
            out = jnp.where(valid[j], act, 0.0)
        p_ref[j * CHUNK:(j + 1) * CHUNK, lo:lo + 1024] = out.astype(BF16)

    def plain(lo):
        p_ref[:, lo:lo + 1024] = jnp.dot(
            xn, w_ref[:, lo:lo + 1024], preferred_element_type=F32).astype(BF16)

    plain_cols = (P_Z, P_Y, P_G, P_G + 1024)
    project(0)
    for kidx in range(len(kinds)):
        plain(plain_cols[kidx])
        if kidx + 1 < len(kinds):
            project(kidx + 1)
        for j in range(nch):
            convolve(kidx, j)
    ba_ref[...] = jnp.dot(xn, wba_ref[...], preferred_element_type=F32)


def _shift_rows(x, d, fill):
    if d % 8 == 0:
        return jnp.concatenate([jnp.full((d, x.shape[1]), fill, x.dtype), x[:x.shape[0] - d]], axis=0)
    rows = lax.broadcasted_iota(jnp.int32, x.shape, 0)
    return jnp.where(rows >= d, pltpu.roll(x, d, 0), fill)


def _outproj_kernel(geo, final, x_ref, g_ref, oa_ref, xl_ref, h0_ref,
                    wri_ref, br_ref, bi_ref, lam_ref, wa_ref, wb_ref, wo_ref,
                    nw_ref, wgu_ref, wd_ref, fn_ref,
                    o_ref, hlast_ref, h_ref, hc_scr, ob_scr):
    i = pl.program_id(0)
    nch = ROW_TILE // CHUNK

    @pl.when(i == 0)
    def _():
        hc_scr[...] = jnp.zeros_like(hc_scr)

    sp = _softplus(-lam_ref[...])
    nulls = [geo.chunk_null(i * nch + j) for j in range(nch)]
    valid = [_valid_rows(n) for n in nulls]
    is_sample = [(i * nch + j) >= geo.prompt_chunks for j in range(nch)]
    for n in range(NB):
        sl = slice(n * BW, (n + 1) * BW)
        xb = xl_ref[:, sl]
        gates = jnp.dot(xb, wri_ref[n], preferred_element_type=F32)
        r = _sigmoid(gates[:, :BW] + br_ref[:, sl])
        ig = _sigmoid(gates[:, BW:] + bi_ref[:, sl])
        log_a = -LRU_C * r * sp[:, sl]
        a_all = jnp.exp(log_a)
        th = jnp.tanh(log_a)
        mult = jnp.sqrt(-2.0 * th / (1.0 - th))
        b_all = mult * (ig * xb.astype(F32))
        gy = _gelu_tanh(xl_ref[:, LRU_W + n * BW:LRU_W + (n + 1) * BW].astype(F32))
        carry = [hc_scr[g:g + 1, sl] for g in range(GROUP)]
        for j in range(nch):
            rows = slice(j * CHUNK, (j + 1) * CHUNK)
            a = jnp.where(valid[j], a_all[rows], 1.0)
            b = jnp.where(valid[j], b_all[rows], 0.0)
            h_in = jnp.where(nulls[j] > 0, jnp.where(is_sample[j], h0_ref[j:j + 1, sl], 0.0), carry[j % GROUP])
            hs, carry[j % GROUP] = _scan_time(a, b, h_in)
            ob_scr[rows, sl] = (hs * gy[rows]).astype(BF16)
            hlast_ref[j:j + 1, sl] = carry[j % GROUP]
        for g in range(GROUP):
            hc_scr[g:g + 1, sl] = carry[g]

    ga = _sigmoid(g_ref[:, :D_MODEL].astype(F32))
    gb = _sigmoid(g_ref[:, D_MODEL:].astype(F32))
    ma = jnp.dot(oa_ref[...], wa_ref[...], preferred_element_type=F32)
    mb = jnp.dot(ob_scr[...], wb_ref[...], preferred_element_type=F32)
    m = (ga * ma + gb * mb).astype(BF16)
    x = x_ref[...] + jnp.dot(m, wo_ref[...], preferred_element_type=F32)
    x = _ffn(x, nw_ref[...], wgu_ref, wd_ref, h_ref)
    if final:
        x = _rms(x, fn_ref[...])
    o_ref[...] = x


def _resident(shape, layer):
    nd = len(shape)
    return pl.BlockSpec((None,) + shape, lambda i: (layer,) + (0,) * nd, pipeline_mode=pl.Buffered(1))


def _params(semantics):
    return pltpu.CompilerParams(dimension_semantics=(semantics,), vmem_limit_bytes=VMEM_LIMIT)


def _ffn_call(layer, x, nw, wgu, wd):
    n = x.shape[0]
    row = lambda i: (i, 0)
    return pl.pallas_call(
        _ffn_kernel,
        grid=(n // ROW_TILE,),
        in_specs=[pl.BlockSpec((ROW_TILE, D_MODEL), row),
                  _resident((1, D_MODEL), layer),
                  _resident((D_MODEL, 2 * D_FF), layer),
                  _resident((D_FF, D_MODEL), layer)],
        out_specs=pl.BlockSpec((ROW_TILE, D_MODEL), row),
        out_shape=jax.ShapeDtypeStruct((n, D_MODEL), F32),
        scratch_shapes=[pltpu.VMEM((ROW_TILE, D_FF), BF16)],
        compiler_params=_params("parallel"),
        name="ffn",
    )(x, nw, wgu, wd)


def _inproj_call(geo, layer, x, nw, w, wba, cw, lcw, lcb, inj):
    n = x.shape[0]
    nch = INPROJ_TILE // CHUNK
    sample_tile0 = geo.prompt_chunks * CHUNK // INPROJ_TILE
    row = lambda i: (i, 0)
    assert INPROJ_TILE == GROUP_ROWS

    def group(i):
        return jnp.where(i < geo.prompt_steps, i // geo.cpp, i - geo.prompt_steps + geo.n_prompt // GROUP)

    return pl.pallas_call(
        functools.partial(_inproj_kernel, geo),
        grid=(n // INPROJ_TILE,),
        in_specs=[pl.BlockSpec((INPROJ_TILE, D_MODEL), row),
                  _resident((1, D_MODEL), layer),
                  _resident((D_MODEL, P_COLS), layer),
                  _resident((D_MODEL, 256), layer),
                  _resident((CONV_W, QKV_W), layer),
                  _resident((CONV_W, LRU_W), layer),
                  _resident((1, LRU_W), layer),
                  pl.BlockSpec((None, TAIL_ROWS * nch, CONV_COLS),
                               lambda i: (layer, jnp.maximum(i - sample_tile0, 0), 0))],
        out_specs=[pl.BlockSpec((INPROJ_TILE, P_COLS), row),
                   pl.BlockSpec((INPROJ_TILE, 256), row),
                   pl.BlockSpec((nch, TAIL_ROWS, CONV_COLS), lambda i: (group(i), 0, 0))],
        out_shape=[jax.ShapeDtypeStruct((n, P_COLS), BF16),
                   jax.ShapeDtypeStruct((n, 256), F32),
                   jax.ShapeDtypeStruct((geo.n_streams, TAIL_ROWS, CONV_COLS), F32)],
        scratch_shapes=[pltpu.VMEM((4, nch, CHUNK + TAIL_ROWS, 1024), F32),
                        pltpu.VMEM((GROUP, TAIL_ROWS, CONV_COLS), F32)],
        compiler_params=_params("arbitrary"),
        name="inproj",
    )(x, nw, w, wba, cw, lcw, lcb, inj)


def _outproj_call(geo, layer, x, p, oa, h0, wri, br, bi, lam, wa, wb, wo, nw, wgu, wd, fn, final):
    n = x.shape[0]
    nch = ROW_TILE // CHUNK
    sample_tile0 = geo.prompt_chunks * CHUNK // ROW_TILE
    row = lambda i: (i, 0)
    return pl.pallas_call(
        functools.partial(_outproj_kernel, geo, final),
        grid=(n // ROW_TILE,),
        in_specs=[pl.BlockSpec((ROW_TILE, D_MODEL), row),
                  pl.BlockSpec((ROW_TILE, 2 * D_MODEL), lambda i: (i, P_G // 2048)),
                  pl.BlockSpec((ROW_TILE, V_W), row),
                  pl.BlockSpec((ROW_TILE, 2 * LRU_W), lambda i: (i, P_X // 2048)),
                  pl.BlockSpec((None, nch, LRU_W), lambda i: (layer, jnp.maximum(i - sample_tile0, 0), 0)),
                  _resident((NB, BW, 2 * BW), layer),
                  _resident((1, LRU_W), layer),
                  _resident((1, LRU_W), layer),
                  _resident((1, LRU_W), layer),
                  _resident((V_W, D_MODEL), layer),
                  _resident((LRU_W, D_MODEL), layer),
                  _resident((D_MODEL, D_MODEL), layer),
                  _resident((1, D_MODEL), layer),
                  _resident((D_MODEL, 2 * D_FF), layer),
                  _resident((D_FF, D_MODEL), layer),
                  pl.BlockSpec((1, D_MODEL), lambda i: (0, 0), pipeline_mode=pl.Buffered(1))],
        out_specs=[pl.BlockSpec((ROW_TILE, D_MODEL), row),
                   pl.BlockSpec((nch, LRU_W), row)],
        out_shape=[jax.ShapeDtypeStruct((n, D_MODEL), F32),
                   jax.ShapeDtypeStruct((n // CHUNK, LRU_W), F32)],
        scratch_shapes=[pltpu.VMEM((ROW_TILE, D_FF), BF16),
                        pltpu.VMEM((8, LRU_W), F32),
                        pltpu.VMEM((ROW_TILE, LRU_W), BF16)],
        compiler_params=_params("arbitrary"),
        name="outproj_ffn",
    )(x, p, oa, p, h0, wri, br, bi, lam, wa, wb, wo, nw, wgu, wd, fn)


def _unit_lower_inverse(lows, c):
    ti = _row_time(lax.broadcasted_iota(jnp.int32, (c, c), 0))
    tj = _row_time(lax.broadcasted_iota(jnp.int32, (c, c), 1))
    same_block = (ti // INV_BASE) == (tj // INV_BASE)
    eye = (ti == tj).astype(F32)
    ld = [jnp.where(same_block, low, 0.0) for low in lows]
    lo = [low - d for low, d in zip(lows, ld)]
    x = [eye - d for d in ld]
    p = ld
    span = 2
    while span < INV_BASE:
        p = [_mm(t, t) for t in p]
        x = [a + _mm(a, t) for a, t in zip(x, p)]
        span *= 2
    e = [_mm(a, b) for a, b in zip(x, lo)]
    y = [eye - t for t in e]
    q = e
    span = 2
    while span < c // INV_BASE:
        q = [_mm(t, t) for t in q]
        y = [a + _mm(a, t) for a, t in zip(y, q)]
        span *= 2
    return [_mm(a, b) for a, b in zip(y, x)]


def _delta_kernel(geo, qkv_ref, z_ref, ba_ref, s0_ref, alog_ref, dtb_ref, onorm_ref,
                  oa_ref, sout_ref, s_scr):
    c = CHUNK
    i = pl.program_id(0)
    is_prompt = i < geo.prompt_steps
    pos = i % geo.cpp
    first = jnp.logical_or(jnp.logical_not(is_prompt), pos == 0)
    last = jnp.logical_or(jnp.logical_not(is_prompt), pos == geo.cpp - 1)
    null = jnp.where(first, jnp.where(is_prompt, PROMPT_NULL, SAMPLE_NULL), 0)
    valid = _valid_rows(null)

    @pl.when(jnp.logical_and(first, is_prompt))
    def _():
        s_scr[...] = jnp.zeros_like(s_scr)

    @pl.when(jnp.logical_not(is_prompt))
    def _():
        s_scr[...] = s0_ref[...]

    ti = _row_time(lax.broadcasted_iota(jnp.int32, (c, c), 0))
    tj = _row_time(lax.broadcasted_iota(jnp.int32, (c, c), 1))
    tri_incl = ti >= tj
    tri_strict = ti > tj

    beta_all, gc_all, gc_t, eg_all, ekd_all, egl_all = [], [], [], [], [], []
    for s in range(GROUP):
        ba = ba_ref[s * c:(s + 1) * c, :]
        beta_all.append(jnp.where(valid, _sigmoid(ba[:, :128]), 0.0))
        g = -jnp.exp(alog_ref[...]) * _softplus(ba[:, 128:] + dtb_ref[...])
        gc, g_last = _scan_time(None, jnp.where(valid, g, 0.0), jnp.zeros((1, 128), F32))
        gc_all.append(gc)
        gc_t.append(jnp.concatenate([gc, jnp.zeros((128 - c, 128), F32)], axis=0).T)
        eg_all.append(jnp.exp(gc))
        ekd_all.append(jnp.exp(g_last - gc))
        egl_all.append(jnp.exp(g_last))

    probs = [(s, h) for s in range(GROUP) for h in range(HA)]
    rows = lambda s: slice(s * c, (s + 1) * c)
    qb = [qkv_ref[rows(s), h * DK:(h + 1) * DK] for s, h in probs]
    kbf = [qkv_ref[rows(s), QK_W + h * DK:QK_W + (h + 1) * DK] for s, h in probs]
    vb = [qkv_ref[rows(s), 2 * QK_W + h * DV:2 * QK_W + (h + 1) * DV] for s, h in probs]
    q = [t.astype(F32) for t in qb]
    k = [t.astype(F32) for t in kbf]
    beta = [beta_all[s][:, h:h + 1] for s, h in probs]
    eg = [eg_all[s][:, h:h + 1] for s, h in probs]
    n = len(probs)
    kb = [k[p] * beta[p] for p in range(n)]
    sc = [_mm_nt(jnp.concatenate([kb[p].astype(BF16), qb[p]], axis=0), kbf[p]) for p in range(n)]
    decay = [jnp.where(tri_incl, jnp.exp(jnp.where(tri_incl, gc_all[s][:, h:h + 1] - gc_t[s][h:h + 1, :c], 0.0)), 0.0)
             for s, h in probs]
    low = [jnp.where(tri_strict, sc[p][:c] * decay[p], 0.0) for p in range(n)]
    a_intra = [sc[p][c:] * decay[p] for p in range(n)]
    t_inv = _unit_lower_inverse(low, c)
    uw = [_mm(t_inv[p], jnp.concatenate([vb[p].astype(F32) * beta[p], kb[p] * eg[p]], axis=1))
          for p in range(n)]
    s_old = [s_scr[s, h] for s, h in probs]
    wq = [_mm(jnp.concatenate([uw[p][:, DV:], q[p] * eg[p]], axis=0), s_old[p]) for p in range(n)]
    v_new = [uw[p][:, :DV] - wq[p][:c] for p in range(n)]
    o_intra = [_mm(a_intra[p], v_new[p]) for p in range(n)]
    ds = [_mm_tn(k[p] * ekd_all[s][:, h:h + 1], v_new[p]) for p, (s, h) in enumerate(probs)]
    for p, (s, h) in enumerate(probs):
        s_scr[s, h] = s_old[p] * egl_all[s][:, h:h + 1] + ds[p]
        o = wq[p][c:] + o_intra[p]
        zh = z_ref[rows(s), h * DV:(h + 1) * DV].astype(F32)
        o = o * lax.rsqrt(jnp.mean(o * o, axis=-1, keepdims=True) + EPS) * onorm_ref[...] * _silu(zh)
        oa_ref[rows(s), h * DV:(h + 1) * DV] = o.astype(oa_ref.dtype)

    @pl.when(last)
    def _():
        sout_ref[...] = s_scr[...]


def _delta_call(geo, layer, p, ba, s0, alog, dtb, onorm):
    n_pp = geo.n_prompt // GROUP

    def pair(i):
        return jnp.where(i < geo.prompt_steps, i // geo.cpp, i - geo.prompt_steps + n_pp)

    vec = lambda width: pl.BlockSpec((None, 1, width), lambda i: (layer, 0, 0))
    return pl.pallas_call(
        functools.partial(_delta_kernel, geo),
        grid=(geo.steps,),
        in_specs=[pl.BlockSpec((GROUP_ROWS, QKV_W), lambda i: (i, 0)),
                  pl.BlockSpec((GROUP_ROWS, V_W), lambda i: (i, P_Z // V_W)),
                  pl.BlockSpec((GROUP_ROWS, 256), lambda i: (i, 0)),
                  pl.BlockSpec((None, GROUP, HA, DK, DV),
                               lambda i: (layer, jnp.maximum(i - geo.prompt_steps, 0), 0, 0, 0)),
                  vec(128), vec(128), vec(DV)],
        out_specs=[pl.BlockSpec((GROUP_ROWS, V_W), lambda i: (i, 0)),
                   pl.BlockSpec((GROUP, HA, DK, DV), lambda i: (pair(i), 0, 0, 0))],
        out_shape=[jax.ShapeDtypeStruct((geo.rows, V_W), BF16),
                   jax.ShapeDtypeStruct((geo.n_streams, HA, DK, DV), F32)],
        scratch_shapes=[pltpu.VMEM((GROUP, HA, DK, DV), F32)],
        compiler_params=_params("arbitrary"),
        name="delta",
    )(p, p, ba, s0, alog, dtb, onorm)


def _pad_lanes(v, width):
    return jnp.pad(v, ((0, 0), (0, width - v.shape[-1])))


def kernel(x_prompt, x_sample, state_delta_S, state_delta_conv, state_lru_h, state_lru_conv,
           meta_tokens, ffn1_norm, ffn1_w_gu, ffn1_w_down, mix_norm, w_in, delta_conv_w,
           delta_A_log, delta_dt_bias, delta_out_norm, lru_conv_w, lru_conv_b, lru_w_r, lru_b_r,
           lru_w_i, lru_b_i, lru_lambda, w_branch_a, w_branch_b, w_out, ffn2_norm, ffn2_w_gu,
           ffn2_w_down, final_norm):
    n_prompt, seq_len, _ = x_prompt.shape
    n_sample, dec_len, _ = x_sample.shape
    assert dec_len == CHUNK - SAMPLE_NULL and (N_META + seq_len) % CHUNK == N_META
    prompt_rows = PROMPT_NULL + N_META + seq_len
    cpp = prompt_rows // CHUNK
    geo = _Geometry(n_prompt, n_sample, cpp)
    assert geo.rows % ROW_TILE == 0 and (geo.prompt_chunks * CHUNK) % ROW_TILE == 0
    assert INPROJ_TILE % GROUP_ROWS == 0 and ROW_TILE % GROUP_ROWS == 0
    dt = x_prompt.dtype

    n_groups = n_prompt // GROUP
    first = jnp.concatenate([jnp.zeros((PROMPT_NULL, D_MODEL), dt), meta_tokens.astype(dt)], axis=0)
    first = first.reshape(SLAB, N_SLAB, D_MODEL).transpose(1, 0, 2)
    first = jnp.broadcast_to(first[None, None, None], (n_groups, 1, GROUP, N_SLAB, SLAB, D_MODEL))
    body = x_prompt.reshape(n_groups, GROUP, cpp - 1, SLAB, N_SLAB, D_MODEL).transpose(0, 2, 1, 4, 3, 5)
    xp = jnp.concatenate([first, body], axis=1)
    xs = jnp.concatenate([jnp.zeros((n_sample, SAMPLE_NULL, D_MODEL), dt), x_sample], axis=1)
    xs = xs.reshape(n_sample, SLAB, N_SLAB, D_MODEL).transpose(0, 2, 1, 3)
    x = jnp.concatenate([xp.reshape(-1, D_MODEL), xs.reshape(-1, D_MODEL)], axis=0).astype(F32)

    cast = lambda w: w.astype(BF16)
    vec = lambda v: v.astype(F32)[:, None, :]
    wgu1, wd1, wgu2, wd2 = cast(ffn1_w_gu), cast(ffn1_w_down), cast(ffn2_w_gu), cast(ffn2_w_down)
    w_main = cast(jnp.concatenate([w_in[:, :, :OFF_BETA], w_in[:, :, OFF_LX:]], axis=-1))
    pad_heads = lambda w: jnp.pad(w, ((0, 0), (0, 0), (0, 128 - HA)))
    w_ba = cast(jnp.concatenate([pad_heads(w_in[:, :, OFF_BETA:OFF_ALPHA]),
                                 pad_heads(w_in[:, :, OFF_ALPHA:OFF_LX])], axis=-1))
    w_ri = cast(jnp.concatenate([lru_w_r, lru_w_i], axis=-1))
    wa, wb, wo = cast(w_branch_a), cast(w_branch_b), cast(w_out)
    norm1, norm_mix, norm2 = vec(ffn1_norm), vec(mix_norm), vec(ffn2_norm)
    alog, dtb, onorm = vec(_pad_lanes(delta_A_log, 128)), vec(_pad_lanes(delta_dt_bias, 128)), vec(delta_out_norm)
    lcb, b_r, b_i, lam = vec(lru_conv_b), vec(lru_b_r), vec(lru_b_i), vec(lru_lambda)
    cw, lcw = delta_conv_w.astype(F32), lru_conv_w.astype(F32)
    s0, h0 = state_delta_S.astype(F32), state_lru_h.astype(F32)

    inj = jnp.concatenate([state_delta_conv, state_lru_conv], axis=-1).astype(F32)
    null_runs = SAMPLE_NULL // N_SLAB
    inj = jnp.pad(inj[:, :, :, None, :], ((0, 0), (0, 0), (0, 0), (null_runs - 1, SLAB - null_runs), (0, 0)))
    inj = inj.reshape(DEPTH, n_sample * TAIL_ROWS, CONV_COLS)

    last_ids = geo.last_chunk_ids()
    outs_s, outs_tail, outs_h = [], [], []
    for l in range(DEPTH):
        x = _ffn_call(l, x, norm1, wgu1, wd1)
        p, ba, tails = _inproj_call(geo, l, x, norm_mix, w_main, w_ba, cw, lcw, lcb, inj)
        oa, s_new = _delta_call(geo, l, p, ba, s0, alog, dtb, onorm)
        x, hlast = _outproj_call(geo, l, x, p, oa, h0, w_ri, b_r, b_i, lam, wa, wb, wo, norm2, wgu2, wd2,
                                 final_norm[None].astype(F32), final=(l == DEPTH - 1))
        outs_s.append(s_new)
        outs_tail.append(tails[:, SLAB - 1::SLAB, :])
        outs_h.append(hlast[last_ids])

    n_prompt_rows = geo.prompt_chunks * CHUNK
    y_prompt = x[:n_prompt_rows].reshape(n_groups, cpp, GROUP, N_SLAB, SLAB, D_MODEL)[:, 1:]
    y_prompt = y_prompt.transpose(0, 2, 1, 4, 3, 5).reshape(n_prompt, seq_len, D_MODEL)
    y_sample = x[n_prompt_rows:].reshape(n_sample, N_SLAB, SLAB, D_MODEL).transpose(0, 2, 1, 3)
    y_sample = y_sample.reshape(n_sample, CHUNK, D_MODEL)[:, SAMPLE_NULL:]
    s_all = jnp.stack(outs_s)
    tail_all = jnp.stack(outs_tail)
    cq_all = tail_all[..., :QKV_W]
    cx_all = tail_all[..., QKV_W:]
    h_all = jnp.stack(outs_h)
    return (y_prompt.astype(dt), y_sample.astype(dt),
            s_all[:, :n_prompt].astype(dt), cq_all[:, :n_prompt].astype(dt),
            h_all[:, :n_prompt].astype(dt), cx_all[:, :n_prompt].astype(dt),
            s_all[:, n_prompt:].astype(state_delta_S.dtype),
            cq_all[:, n_prompt:].astype(state_delta_conv.dtype),
            h_all[:, n_prompt:].astype(state_lru_h.dtype),
            cx_all[:, n_prompt:].astype(state_lru_conv.dtype))
```

```python
import functools

import numpy as np
import jax
import jax.numpy as jnp
from jax import lax
from jax.experimental import pallas as pl
from jax.experimental.pallas import tpu as pltpu

F32 = jnp.float32
BF16 = jnp.bfloat16

D_MODEL = 1024
DEPTH = 4
N_META = 16
HA = 8
DK = 128
DV = 128
QK_W = HA * DK
V_W = HA * DV
QKV_W = 2 * QK_W + V_W
LRU_W = D_MODEL
NB = 8
BW = 128
CONV_W = 4
LRU_C = 8.0
D_FF = 2816
EPS = 1e-6

OFF_Z = QKV_W
OFF_BETA = OFF_Z + V_W
OFF_ALPHA = OFF_BETA + HA
OFF_LX = OFF_ALPHA + HA
OFF_LY = OFF_LX + LRU_W
OFF_GA = OFF_LY + LRU_W
OFF_GB = OFF_GA + D_MODEL
IN_COLS = OFF_GB + D_MODEL

CHUNK = 64
GROUP = 4
GROUP_ROWS = GROUP * CHUNK
PROMPT_NULL = CHUNK - N_META
SAMPLE_NULL = CHUNK // 2
INV_BASE = 16
ROW_TILE = 512
INPROJ_TILE = 256
FF_HALF = D_FF // 2
CONV_COLS = QKV_W + LRU_W
SLAB = 8
N_SLAB = CHUNK // SLAB
TAIL_ROWS = (CONV_W - 1) * SLAB
P_Z, P_X, P_Y, P_G = 3072, 4096, 5120, 6144
P_COLS = 8192
VMEM_LIMIT = 56 * 1024 * 1024


def _rms(x, w):
    return x * lax.rsqrt(jnp.mean(x * x, axis=-1, keepdims=True) + EPS) * w


def _mm(a, b):
    return jnp.dot(a.astype(BF16), b.astype(BF16), preferred_element_type=F32)


def _mm_nt(a, b):
    return lax.dot_general(a.astype(BF16), b.astype(BF16), (((1,), (1,)), ((), ())),
                           preferred_element_type=F32)


def _mm_tn(a, b):
    return lax.dot_general(a.astype(BF16), b.astype(BF16), (((0,), (0,)), ((), ())),
                           preferred_element_type=F32)


def _sigmoid(x):
    return 0.5 * (1.0 + jnp.tanh(0.5 * x))


def _silu(x):
    return x * _sigmoid(x)


def _softplus(x):
    return jnp.maximum(x, 0.0) + jnp.log1p(jnp.exp(-jnp.abs(x)))


def _gelu_tanh(x):
    return 0.5 * x * (1.0 + jnp.tanh(0.7978845608028654 * (x + 0.044715 * (x * x * x))))


class _Geometry:
    def __init__(self, n_prompt, n_sample, chunks_per_prompt):
        assert n_prompt % GROUP == 0 and n_sample % GROUP == 0
        self.n_prompt = n_prompt
        self.n_sample = n_sample
        self.cpp = chunks_per_prompt
        self.prompt_steps = (n_prompt // GROUP) * chunks_per_prompt
        self.steps = self.prompt_steps + n_sample // GROUP
        self.prompt_chunks = GROUP * self.prompt_steps
        self.chunks = GROUP * self.steps
        self.rows = self.chunks * CHUNK
        self.n_streams = n_prompt + n_sample

    def chunk_null(self, cid):
        is_prompt = cid < self.prompt_chunks
        first = jnp.logical_or(jnp.logical_not(is_prompt), (cid // GROUP) % self.cpp == 0)
        return jnp.where(first, jnp.where(is_prompt, PROMPT_NULL, SAMPLE_NULL), 0)

    def last_chunk_ids(self):
        ids = [GROUP * ((b // GROUP) * self.cpp + self.cpp - 1) + b % GROUP for b in range(self.n_prompt)]
        ids += [self.prompt_chunks + t for t in range(self.n_sample)]
        return np.asarray(ids, np.int32)


def _row_time(rows):
    return N_SLAB * (rows % SLAB) + rows // SLAB


def _valid_rows(null):
    return _row_time(lax.broadcasted_iota(jnp.int32, (CHUNK, 1), 0)) >= null


def _slabs(x):
    return [x[v * SLAB:(v + 1) * SLAB] for v in range(N_SLAB)]


def _shift_run(x, fill):
    return jnp.concatenate([fill, x[:SLAB - 1]], axis=0)


def _scan_time(a, b, h_in):
    bs = _slabs(b)
    if a is None:
        for v in range(1, N_SLAB):
            bs[v] = bs[v] + bs[v - 1]
        run = bs[N_SLAB - 1]
        d = 1
        while d < SLAB:
            run = run + _shift_rows(run, d, 0.0)
            d *= 2
        run = run + h_in
        prev = _shift_run(run, h_in)
        hs = [t + prev for t in bs]
    else:
        as_ = _slabs(a)
        for v in range(1, N_SLAB):
            bs[v] = bs[v] + as_[v] * bs[v - 1]
            as_[v] = as_[v] * as_[v - 1]
        ra, rb = as_[N_SLAB - 1], bs[N_SLAB - 1]
        d = 1
        while d < SLAB:
            rb = rb + ra * _shift_rows(rb, d, 0.0)
            ra = ra * _shift_rows(ra, d, 1.0)
            d *= 2
        run = rb + ra * h_in
        prev = _shift_run(run, h_in)
        hs = [t + u * prev for t, u in zip(bs, as_)]
    return jnp.concatenate(hs, axis=0), run[SLAB - 1:SLAB, :]


def _ffn(x, nw, wgu_ref, wd_ref, h_ref):
    xn = _rms(x, nw).astype(BF16)
    for c in range(2):
        lo = c * FF_HALF
        g = jnp.dot(xn, wgu_ref[:, lo:lo + FF_HALF], preferred_element_type=F32)
        u = jnp.dot(xn, wgu_ref[:, D_FF + lo:D_FF + lo + FF_HALF], preferred_element_type=F32)
        h_ref[:, lo:lo + FF_HALF] = (_silu(g) * u).astype(BF16)
    y = jnp.dot(h_ref[...], wd_ref[...], preferred_element_type=F32)
    return x + 0.5 * y


def _ffn_kernel(x_ref, nw_ref, wgu_ref, wd_ref, o_ref, h_ref):
    o_ref[...] = _ffn(x_ref[...], nw_ref[...], wgu_ref, wd_ref, h_ref)


def _inproj_kernel(geo, x_ref, nw_ref, w_ref, wba_ref, cw_ref, lcw_ref, lcb_ref, inj_ref,
                   p_ref, ba_ref, tails_ref, stage_all, carry_scr):
    i = pl.program_id(0)
    nch = INPROJ_TILE // CHUNK
    xn = _rms(x_ref[...], nw_ref[...]).astype(BF16)

    @pl.when(i == 0)
    def _():
        carry_scr[...] = jnp.zeros_like(carry_scr)

    valid = [_valid_rows(geo.chunk_null(i * nch + j)) for j in range(nch)]
    in_sample = i >= geo.prompt_chunks * CHUNK // INPROJ_TILE
    kinds = (("q", 0, 0), ("k", QK_W, QK_W), ("v", 2 * QK_W, 2 * QK_W), ("x", P_X, QKV_W))
    t0 = TAIL_ROWS

    def project(kidx):
        kind, lo, col = kinds[kidx]
        stage = stage_all.at[kidx]
        pre = jnp.dot(xn, w_ref[:, lo:lo + 1024], preferred_element_type=F32)
        for j in range(nch):
            stage[j, t0:t0 + CHUNK, :] = pre[j * CHUNK:(j + 1) * CHUNK]
            if j >= GROUP:
                stage[j, 0:t0, :] = pre[(j - GROUP + 1) * CHUNK - t0:(j - GROUP + 1) * CHUNK]
            else:
                stage[j, 0:t0, :] = carry_scr[j, :, col:col + 1024]
            stage[j, CHUNK:CHUNK + t0, :] = stage[j, CHUNK:CHUNK + t0, :] + jnp.where(
                in_sample, inj_ref[t0 * j:t0 * (j + 1), col:col + 1024], 0.0)
            tails_ref[j, :, col:col + 1024] = stage[j, CHUNK:CHUNK + t0, :]
        for g in range(GROUP):
            carry_scr[g, :, col:col + 1024] = stage[nch - GROUP + g, CHUNK:CHUNK + t0, :]

    def convolve(kidx, j):
        kind, lo, col = kinds[kidx]
        stage = stage_all.at[kidx]
        taps = lcw_ref[...] if kind == "x" else cw_ref[:, lo:lo + 1024]
        cur = [stage[j, t0 + v * SLAB:t0 + (v + 1) * SLAB, :] for v in range(N_SLAB)]
        back = []
        for k in range(CONV_W - 1):
            prev_last = stage[j, (k + 1) * SLAB - 1:(k + 1) * SLAB, :]
            back.append(_shift_run(cur[N_SLAB - (CONV_W - 1) + k], prev_last))
        conv = []
        for v in range(N_SLAB):
            acc = cur[v] * taps[CONV_W - 1:CONV_W, :]
            for d in range(1, CONV_W):
                src = cur[v - d] if v >= d else back[CONV_W - 1 - d + v]
                acc = acc + src * taps[CONV_W - 1 - d:CONV_W - d, :]
            conv.append(acc)
        conv = jnp.concatenate(conv, axis=0)
        if kind == "x":
            out = conv + lcb_ref[...]
        else:
            act = _silu(conv)
            if kind != "v":
                scale = DK ** -0.5 if kind == "q" else 1.0
                segs = []
                for h in range(HA):
                    seg = act[:, h * DK:(h + 1) * DK]
                    segs.append(seg * (lax.rsqrt(jnp.sum(seg * seg, axis=-1, keepdims=True) + EPS) * scale))
                act = jnp.concatenate(segs, axis=1)
            out = jnp.where(valid[j], act, 0.0)
        p_ref[j * CHUNK:(j + 1) * CHUNK, lo:lo + 1024] = out.astype(BF16)

    def plain(lo):
        p_ref[:, lo:lo + 1024] = jnp.dot(
            xn, w_ref[:, lo:lo + 1024], preferred_element_type=F32).astype(BF16)

    plain_cols = (P_Z, P_Y, P_G, P_G + 1024)
    project(0)
    for kidx in range(len(kinds)):
        plain(plain_cols[kidx])
        if kidx + 1 < len(kinds):
            project(kidx + 1)
        for j in range(nch):
            convolve(kidx, j)
    ba_ref[...] = jnp.dot(xn, wba_ref[...], preferred_element_type=F32)


def _shift_rows(x, d, fill):
    if d % 8 == 0:
        return jnp.concatenate([jnp.full((d, x.shape[1]), fill, x.dtype), x[:x.shape[0] - d]], axis=0)
    rows = lax.broadcasted_iota(jnp.int32, x.shape, 0)
    return jnp.where(rows >= d, pltpu.roll(x, d, 0), fill)


def _outproj_kernel(geo, final, x_ref, g_ref, oa_ref, xl_ref, h0_ref,
                    wri_ref, br_ref, bi_ref, lam_ref, wa_ref, wb_ref, wo_ref,
                    nw_ref, wgu_ref, wd_ref, fn_ref,
                    o_ref, hlast_ref, h_ref, hc_scr, ob_scr):
    i = pl.program_id(0)
    nch = ROW_TILE // CHUNK

    @pl.when(i == 0)
    def _():
        hc_scr[...] = jnp.zeros_like(hc_scr)

    sp = _softplus(-lam_ref[...])
    nulls = [geo.chunk_null(i * nch + j) for j in range(nch)]
    valid = [_valid_rows(n) for n in nulls]
    is_sample = [(i * nch + j) >= geo.prompt_chunks for j in range(nch)]
    for n in range(NB):
        sl = slice(n * BW, (n + 1) * BW)
        xb = xl_ref[:, sl]
        gates = jnp.dot(xb, wri_ref[n], preferred_element_type=F32)
        r = _sigmoid(gates[:, :BW] + br_ref[:, sl])
        ig = _sigmoid(gates[:, BW:] + bi_ref[:, sl])
        log_a = -LRU_C * r * sp[:, sl]
        a_all = jnp.exp(log_a)
        th = jnp.tanh(log_a)
        mult = jnp.sqrt(-2.0 * th / (1.0 - th))
        b_all = mult * (ig * xb.astype(F32))
        gy = _gelu_tanh(xl_ref[:, LRU_W + n * BW:LRU_W + (n + 1) * BW].astype(F32))
        carry = [hc_scr[g:g + 1, sl] for g in range(GROUP)]
        for j in range(nch):
            rows = slice(j * CHUNK, (j + 1) * CHUNK)
            a = jnp.where(valid[j], a_all[rows], 1.0)
            b = jnp.where(valid[j], b_all[rows], 0.0)
            h_in = jnp.where(nulls[j] > 0, jnp.where(is_sample[j], h0_ref[j:j + 1, sl], 0.0), carry[j % GROUP])
            hs, carry[j % GROUP] = _scan_time(a, b, h_in)
            ob_scr[rows, sl] = (hs * gy[rows]).astype(BF16)
            hlast_ref[j:j + 1, sl] = carry[j % GROUP]
        for g in range(GROUP):
            hc_scr[g:g + 1, sl] = carry[g]

    ga = _sigmoid(g_ref[:, :D_MODEL].astype(F32))
    gb = _sigmoid(g_ref[:, D_MODEL:].astype(F32))
    ma = jnp.dot(oa_ref[...], wa_ref[...], preferred_element_type=F32)
    mb = jnp.dot(ob_scr[...], wb_ref[...], preferred_element_type=F32)
    m = (ga * ma + gb * mb).astype(BF16)
    x = x_ref[...] + jnp.dot(m, wo_ref[...], preferred_element_type=F32)
    x = _ffn(x, nw_ref[...], wgu_ref, wd_ref, h_ref)
    if final:
        x = _rms(x, fn_ref[...])
    o_ref[...] = x


def _resident(shape, layer):
    nd = len(shape)
    return pl.BlockSpec((None,) + shape, lambda i: (layer,) + (0,) * nd, pipeline_mode=pl.Buffered(1))


def _params(semantics):
    return pltpu.CompilerParams(dimension_semantics=(semantics,), vmem_limit_bytes=VMEM_LIMIT)


def _ffn_call(layer, x, nw, wgu, wd):
    n = x.shape[0]
    row = lambda i: (i, 0)
    return pl.pallas_call(
        _ffn_kernel,
        grid=(n // ROW_TILE,),
        in_specs=[pl.BlockSpec((ROW_TILE, D_MODEL), row),
                  _resident((1, D_MODEL), layer),
                  _resident((D_MODEL, 2 * D_FF), layer),
                  _resident((D_FF, D_MODEL), layer)],
        out_specs=pl.BlockSpec((ROW_TILE, D_MODEL), row),
        out_shape=jax.ShapeDtypeStruct((n, D_MODEL), F32),
        scratch_shapes=[pltpu.VMEM((ROW_TILE, D_FF), BF16)],
        compiler_params=_params("parallel"),
        name="ffn",
    )(x, nw, wgu, wd)


def _inproj_call(geo, layer, x, nw, w, wba, cw, lcw, lcb, inj):
    n = x.shape[0]
    nch = INPROJ_TILE // CHUNK
    sample_tile0 = geo.prompt_chunks * CHUNK // INPROJ_TILE
    row = lambda i: (i, 0)
    return pl.pallas_call(
        functools.partial(_inproj_kernel, geo),
        grid=(n // INPROJ_TILE,),
        in_specs=[pl.BlockSpec((INPROJ_TILE, D_MODEL), row),
                  _resident((1, D_MODEL), layer),
                  _resident((D_MODEL, P_COLS), layer),
                  _resident((D_MODEL, 256), layer),
                  _resident((CONV_W, QKV_W), layer),
                  _resident((CONV_W, LRU_W), layer),
                  _resident((1, LRU_W), layer),
                  pl.BlockSpec((None, TAIL_ROWS * nch, CONV_COLS),
                               lambda i: (layer, jnp.maximum(i - sample_tile0, 0), 0))],
        out_specs=[pl.BlockSpec((INPROJ_TILE, P_COLS), row),
                   pl.BlockSpec((INPROJ_TILE, 256), row),
                   pl.BlockSpec((nch, TAIL_ROWS, CONV_COLS), lambda i: (i, 0, 0))],
        out_shape=[jax.ShapeDtypeStruct((n, P_COLS), BF16),
                   jax.ShapeDtypeStruct((n, 256), F32),
                   jax.ShapeDtypeStruct((n // CHUNK, TAIL_ROWS, CONV_COLS), F32)],
        scratch_shapes=[pltpu.VMEM((4, nch, CHUNK + TAIL_ROWS, 1024), F32),
                        pltpu.VMEM((GROUP, TAIL_ROWS, CONV_COLS), F32)],
        compiler_params=_params("arbitrary"),
        name="inproj",
    )(x, nw, w, wba, cw, lcw, lcb, inj)


def _outproj_call(geo, layer, x, p, oa, h0, wri, br, bi, lam, wa, wb, wo, nw, wgu, wd, fn, final):
    n = x.shape[0]
    nch = ROW_TILE // CHUNK
    sample_tile0 = geo.prompt_chunks * CHUNK // ROW_TILE
    row = lambda i: (i, 0)
    return pl.pallas_call(
        functools.partial(_outproj_kernel, geo, final),
        grid=(n // ROW_TILE,),
        in_specs=[pl.BlockSpec((ROW_TILE, D_MODEL), row),
                  pl.BlockSpec((ROW_TILE, 2 * D_MODEL), lambda i: (i, P_G // 2048)),
                  pl.BlockSpec((ROW_TILE, V_W), row),
                  pl.BlockSpec((ROW_TILE, 2 * LRU_W), lambda i: (i, P_X // 2048)),
                  pl.BlockSpec((None, nch, LRU_W), lambda i: (layer, jnp.maximum(i - sample_tile0, 0), 0)),
                  _resident((NB, BW, 2 * BW), layer),
                  _resident((1, LRU_W), layer),
                  _resident((1, LRU_W), layer),
                  _resident((1, LRU_W), layer),
                  _resident((V_W, D_MODEL), layer),
                  _resident((LRU_W, D_MODEL), layer),
                  _resident((D_MODEL, D_MODEL), layer),
                  _resident((1, D_MODEL), layer),
                  _resident((D_MODEL, 2 * D_FF), layer),
                  _resident((D_FF, D_MODEL), layer),
                  pl.BlockSpec((1, D_MODEL), lambda i: (0, 0), pipeline_mode=pl.Buffered(1))],
        out_specs=[pl.BlockSpec((ROW_TILE, D_MODEL), row),
                   pl.BlockSpec((nch, LRU_W), row)],
        out_shape=[jax.ShapeDtypeStruct((n, D_MODEL), F32),
                   jax.ShapeDtypeStruct((n // CHUNK, LRU_W), F32)],
        scratch_shapes=[pltpu.VMEM((ROW_TILE, D_FF), BF16),
                        pltpu.VMEM((8, LRU_W), F32),
                        pltpu.VMEM((ROW_TILE, LRU_W), BF16)],
        compiler_params=_params("arbitrary"),
        name="outproj_ffn",
    )(x, p, oa, p, h0, wri, br, bi, lam, wa, wb, wo, nw, wgu, wd, fn)


def _unit_lower_inverse(lows, c):
    ti = _row_time(lax.broadcasted_iota(jnp.int32, (c, c), 0))
    tj = _row_time(lax.broadcasted_iota(jnp.int32, (c, c), 1))
    same_block = (ti // INV_BASE) == (tj // INV_BASE)
    eye = (ti == tj).astype(F32)
    ld = [jnp.where(same_block, low, 0.0) for low in lows]
    lo = [low - d for low, d in zip(lows, ld)]
    x = [eye - d for d in ld]
    p = ld
    span = 2
    while span < INV_BASE:
        p = [_mm(t, t) for t in p]
        x = [a + _mm(a, t) for a, t in zip(x, p)]
        span *= 2
    e = [_mm(a, b) for a, b in zip(x, lo)]
    y = [eye - t for t in e]
    q = e
    span = 2
    while span < c // INV_BASE:
        q = [_mm(t, t) for t in q]
        y = [a + _mm(a, t) for a, t in zip(y, q)]
        span *= 2
    return [_mm(a, b) for a, b in zip(y, x)]


def _delta_kernel(geo, qkv_ref, z_ref, ba_ref, s0_ref, alog_ref, dtb_ref, onorm_ref,
                  oa_ref, sout_ref, s_scr):
    c = CHUNK
    i = pl.program_id(0)
    is_prompt = i < geo.prompt_steps
    pos = i % geo.cpp
    first = jnp.logical_or(jnp.logical_not(is_prompt), pos == 0)
    last = jnp.logical_or(jnp.logical_not(is_prompt), pos == geo.cpp - 1)
    null = jnp.where(first, jnp.where(is_prompt, PROMPT_NULL, SAMPLE_NULL), 0)
    valid = _valid_rows(null)

    @pl.when(jnp.logical_and(first, is_prompt))
    def _():
        s_scr[...] = jnp.zeros_like(s_scr)

    @pl.when(jnp.logical_not(is_prompt))
    def _():
        s_scr[...] = s0_ref[...]

    ti = _row_time(lax.broadcasted_iota(jnp.int32, (c, c), 0))
    tj = _row_time(lax.broadcasted_iota(jnp.int32, (c, c), 1))
    tri_incl = ti >= tj
    tri_strict = ti > tj

    beta_all, gc_all, gc_t, eg_all, ekd_all, egl_all = [], [], [], [], [], []
    for s in range(GROUP):
        ba = ba_ref[s * c:(s + 1) * c, :]
        beta_all.append(jnp.where(valid, _sigmoid(ba[:, :128]), 0.0))
        g = -jnp.exp(alog_ref[...]) * _softplus(ba[:, 128:] + dtb_ref[...])
        gc, g_last = _scan_time(None, jnp.where(valid, g, 0.0), jnp.zeros((1, 128), F32))
        gc_all.append(gc)
        gc_t.append(jnp.concatenate([gc, jnp.zeros((128 - c, 128), F32)], axis=0).T)
        eg_all.append(jnp.exp(gc))
        ekd_all.append(jnp.exp(g_last - gc))
        egl_all.append(jnp.exp(g_last))

    probs = [(s, h) for s in range(GROUP) for h in range(HA)]
    rows = lambda s: slice(s * c, (s + 1) * c)
    qb = [qkv_ref[rows(s), h * DK:(h + 1) * DK] for s, h in probs]
    kbf = [qkv_ref[rows(s), QK_W + h * DK:QK_W + (h + 1) * DK] for s, h in probs]
    vb = [qkv_ref[rows(s), 2 * QK_W + h * DV:2 * QK_W + (h + 1) * DV] for s, h in probs]
    q = [t.astype(F32) for t in qb]
    k = [t.astype(F32) for t in kbf]
    beta = [beta_all[s][:, h:h + 1] for s, h in probs]
    eg = [eg_all[s][:, h:h + 1] for s, h in probs]
    n = len(probs)
    kb = [k[p] * beta[p] for p in range(n)]
    sc = [_mm_nt(jnp.concatenate([kb[p].astype(BF16), qb[p]], axis=0), kbf[p]) for p in range(n)]
    decay = [jnp.where(tri_incl, jnp.exp(jnp.where(tri_incl, gc_all[s][:, h:h + 1] - gc_t[s][h:h + 1, :c], 0.0)), 0.0)
             for s, h in probs]
    low = [jnp.where(tri_strict, sc[p][:c] * decay[p], 0.0) for p in range(n)]
    a_intra = [sc[p][c:] * decay[p] for p in range(n)]
    t_inv = _unit_lower_inverse(low, c)
    uw = [_mm(t_inv[p], jnp.concatenate([vb[p].astype(F32) * beta[p], kb[p] * eg[p]], axis=1))
          for p in range(n)]
    s_old = [s_scr[s, h] for s, h in probs]
    wq = [_mm(jnp.concatenate([uw[p][:, DV:], q[p] * eg[p]], axis=0), s_old[p]) for p in range(n)]
    v_new = [uw[p][:, :DV] - wq[p][:c] for p in range(n)]
    o_intra = [_mm(a_intra[p], v_new[p]) for p in range(n)]
    ds = [_mm_tn(k[p] * ekd_all[s][:, h:h + 1], v_new[p]) for p, (s, h) in enumerate(probs)]
    for p, (s, h) in enumerate(probs):
        s_scr[s, h] = s_old[p] * egl_all[s][:, h:h + 1] + ds[p]
        o = wq[p][c:] + o_intra[p]
        zh = z_ref[rows(s), h * DV:(h + 1) * DV].astype(F32)
        o = o * lax.rsqrt(jnp.mean(o * o, axis=-1, keepdims=True) + EPS) * onorm_ref[...] * _silu(zh)
        oa_ref[rows(s), h * DV:(h + 1) * DV] = o.astype(oa_ref.dtype)

    @pl.when(last)
    def _():
        sout_ref[...] = s_scr[...]


def _delta_call(geo, layer, p, ba, s0, alog, dtb, onorm):
    n_pp = geo.n_prompt // GROUP

    def pair(i):
        return jnp.where(i < geo.prompt_steps, i // geo.cpp, i - geo.prompt_steps + n_pp)

    vec = lambda width: pl.BlockSpec((None, 1, width), lambda i: (layer, 0, 0))
    return pl.pallas_call(
        functools.partial(_delta_kernel, geo),
        grid=(geo.steps,),
        in_specs=[pl.BlockSpec((GROUP_ROWS, QKV_W), lambda i: (i, 0)),
                  pl.BlockSpec((GROUP_ROWS, V_W), lambda i: (i, P_Z // V_W)),
                  pl.BlockSpec((GROUP_ROWS, 256), lambda i: (i, 0)),
                  pl.BlockSpec((None, GROUP, HA, DK, DV),
                               lambda i: (layer, jnp.maximum(i - geo.prompt_steps, 0), 0, 0, 0)),
                  vec(128), vec(128), vec(DV)],
        out_specs=[pl.BlockSpec((GROUP_ROWS, V_W), lambda i: (i, 0)),
                   pl.BlockSpec((GROUP, HA, DK, DV), lambda i: (pair(i), 0, 0, 0))],
        out_shape=[jax.ShapeDtypeStruct((geo.rows, V_W), BF16),
                   jax.ShapeDtypeStruct((geo.n_streams, HA, DK, DV), F32)],
        scratch_shapes=[pltpu.VMEM((GROUP, HA, DK, DV), F32)],
        compiler_params=_params("arbitrary"),
        name="delta",
    )(p, p, ba, s0, alog, dtb, onorm)


def _permute_chunk(x):
    return jnp.swapaxes(x.reshape(SLAB, N_SLAB, x.shape[-1]), 0, 1).reshape(CHUNK, x.shape[-1])


def _pack_kernel(geo, xp_ref, xs_ref, meta_ref, o_ref):
    i = pl.program_id(0)
    is_prompt = i < geo.prompt_steps
    pos = i % geo.cpp
    width = o_ref.shape[1]

    @pl.when(jnp.logical_and(is_prompt, pos > 0))
    def _():
        for s in range(GROUP):
            o_ref[s * CHUNK:(s + 1) * CHUNK, :] = _permute_chunk(xp_ref[s, 0].astype(o_ref.dtype))

    @pl.when(jnp.logical_and(is_prompt, pos == 0))
    def _():
        first = _permute_chunk(jnp.concatenate(
            [jnp.zeros((PROMPT_NULL, width), o_ref.dtype), meta_ref[...].astype(o_ref.dtype)], axis=0))
        for s in range(GROUP):
            o_ref[s * CHUNK:(s + 1) * CHUNK, :] = first

    @pl.when(jnp.logical_not(is_prompt))
    def _():
        for s in range(GROUP):
            o_ref[s * CHUNK:(s + 1) * CHUNK, :] = _permute_chunk(jnp.concatenate(
                [jnp.zeros((SAMPLE_NULL, width), o_ref.dtype), xs_ref[s].astype(o_ref.dtype)], axis=0))


def _unpack_kernel(geo, x_ref, yp_ref, ys_ref):
    i = pl.program_id(0)
    is_prompt = i < geo.prompt_steps
    pos = i % geo.cpp

    @pl.when(jnp.logical_and(is_prompt, pos > 0))
    def _():
        for s in range(GROUP):
            yp_ref[s, 0] = _permute_chunk(x_ref[s * CHUNK:(s + 1) * CHUNK, :]).astype(yp_ref.dtype)

    @pl.when(jnp.logical_not(is_prompt))
    def _():
        for s in range(GROUP):
            ys_ref[s] = _permute_chunk(x_ref[s * CHUNK:(s + 1) * CHUNK, :])[SAMPLE_NULL:].astype(ys_ref.dtype)


def _layout_specs(geo):
    def prompt_idx(i):
        step = jnp.minimum(i, geo.prompt_steps - 1)
        return (step // geo.cpp, jnp.maximum(step % geo.cpp - 1, 0), 0, 0)

    def sample_idx(i):
        return (jnp.maximum(i - geo.prompt_steps, 0), 0, 0)

    return (pl.BlockSpec((GROUP, 1, CHUNK, D_MODEL), prompt_idx),
            pl.BlockSpec((GROUP, CHUNK - SAMPLE_NULL, D_MODEL), sample_idx))


def _pack_call(geo, x_prompt, x_sample, meta):
    prompt_spec, sample_spec = _layout_specs(geo)
    xp = x_prompt.reshape(geo.n_prompt, geo.cpp - 1, CHUNK, D_MODEL)
    return pl.pallas_call(
        functools.partial(_pack_kernel, geo),
        grid=(geo.steps,),
        in_specs=[prompt_spec, sample_spec, pl.BlockSpec((N_META, D_MODEL), lambda i: (0, 0))],
        out_specs=pl.BlockSpec((GROUP_ROWS, D_MODEL), lambda i: (i, 0)),
        out_shape=jax.ShapeDtypeStruct((geo.rows, D_MODEL), F32),
        compiler_params=_params("parallel"),
        name="pack",
    )(xp, x_sample, meta)


def _unpack_call(geo, x, dtype):
    prompt_spec, sample_spec = _layout_specs(geo)
    yp, ys = pl.pallas_call(
        functools.partial(_unpack_kernel, geo),
        grid=(geo.steps,),
        in_specs=[pl.BlockSpec((GROUP_ROWS, D_MODEL), lambda i: (i, 0))],
        out_specs=[prompt_spec, sample_spec],
        out_shape=[jax.ShapeDtypeStruct((geo.n_prompt, geo.cpp - 1, CHUNK, D_MODEL), dtype),
                   jax.ShapeDtypeStruct((geo.n_sample, CHUNK - SAMPLE_NULL, D_MODEL), dtype)],
        compiler_params=_params("arbitrary"),
        name="unpack",
    )(x)
    return yp.reshape(geo.n_prompt, (geo.cpp - 1) * CHUNK, D_MODEL), ys


def _pad_lanes(v, width):
    return jnp.pad(v, ((0, 0), (0, width - v.shape[-1])))


def kernel(x_prompt, x_sample, state_delta_S, state_delta_conv, state_lru_h, state_lru_conv,
           meta_tokens, ffn1_norm, ffn1_w_gu, ffn1_w_down, mix_norm, w_in, delta_conv_w,
           delta_A_log, delta_dt_bias, delta_out_norm, lru_conv_w, lru_conv_b, lru_w_r, lru_b_r,
           lru_w_i, lru_b_i, lru_lambda, w_branch_a, w_branch_b, w_out, ffn2_norm, ffn2_w_gu,
           ffn2_w_down, final_norm):
    n_prompt, seq_len, _ = x_prompt.shape
    n_sample, dec_len, _ = x_sample.shape
    assert dec_len == CHUNK - SAMPLE_NULL and (N_META + seq_len) % CHUNK == N_META
    prompt_rows = PROMPT_NULL + N_META + seq_len
    cpp = prompt_rows // CHUNK
    geo = _Geometry(n_prompt, n_sample, cpp)
    assert geo.rows % ROW_TILE == 0 and (geo.prompt_chunks * CHUNK) % ROW_TILE == 0
    assert INPROJ_TILE % GROUP_ROWS == 0 and ROW_TILE % GROUP_ROWS == 0
    dt = x_prompt.dtype

    x = _pack_call(geo, x_prompt, x_sample, meta_tokens)

    cast = lambda w: w.astype(BF16)
    vec = lambda v: v.astype(F32)[:, None, :]
    wgu1, wd1, wgu2, wd2 = cast(ffn1_w_gu), cast(ffn1_w_down), cast(ffn2_w_gu), cast(ffn2_w_down)
    w_main = cast(jnp.concatenate([w_in[:, :, :OFF_BETA], w_in[:, :, OFF_LX:]], axis=-1))
    pad_heads = lambda w: jnp.pad(w, ((0, 0), (0, 0), (0, 128 - HA)))
    w_ba = cast(jnp.concatenate([pad_heads(w_in[:, :, OFF_BETA:OFF_ALPHA]),
                                 pad_heads(w_in[:, :, OFF_ALPHA:OFF_LX])], axis=-1))
    w_ri = cast(jnp.concatenate([lru_w_r, lru_w_i], axis=-1))
    wa, wb, wo = cast(w_branch_a), cast(w_branch_b), cast(w_out)
    norm1, norm_mix, norm2 = vec(ffn1_norm), vec(mix_norm), vec(ffn2_norm)
    alog, dtb, onorm = vec(_pad_lanes(delta_A_log, 128)), vec(_pad_lanes(delta_dt_bias, 128)), vec(delta_out_norm)
    lcb, b_r, b_i, lam = vec(lru_conv_b), vec(lru_b_r), vec(lru_b_i), vec(lru_lambda)
    cw, lcw = delta_conv_w.astype(F32), lru_conv_w.astype(F32)
    s0, h0 = state_delta_S.astype(F32), state_lru_h.astype(F32)

    inj = jnp.concatenate([state_delta_conv, state_lru_conv], axis=-1).astype(F32)
    null_runs = SAMPLE_NULL // N_SLAB
    inj = jnp.pad(inj[:, :, :, None, :], ((0, 0), (0, 0), (0, 0), (null_runs - 1, SLAB - null_runs), (0, 0)))
    inj = inj.reshape(DEPTH, n_sample * TAIL_ROWS, CONV_COLS)

    last_ids = geo.last_chunk_ids()
    outs_s, outs_tail, outs_h = [], [], []
    for l in range(DEPTH):
        x = _ffn_call(l, x, norm1, wgu1, wd1)
        p, ba, tails = _inproj_call(geo, l, x, norm_mix, w_main, w_ba, cw, lcw, lcb, inj)
        oa, s_new = _delta_call(geo, l, p, ba, s0, alog, dtb, onorm)
        x, hlast = _outproj_call(geo, l, x, p, oa, h0, w_ri, b_r, b_i, lam, wa, wb, wo, norm2, wgu2, wd2,
                                 final_norm[None].astype(F32), final=(l == DEPTH - 1))
        outs_s.append(s_new)
        outs_tail.append(tails[last_ids, SLAB - 1::SLAB, :])
        outs_h.append(hlast[last_ids])

    n_prompt_rows = geo.prompt_chunks * CHUNK
    y_prompt, y_sample = _unpack_call(geo, x, dt)
    s_all = jnp.stack(outs_s)
    tail_all = jnp.stack(outs_tail)
    cq_all = tail_all[..., :QKV_W]
    cx_all = tail_all[..., QKV_W:]
    h_all = jnp.stack(outs_h)
    return (y_prompt.astype(dt), y_sample.astype(dt),
            s_all[:, :n_prompt].astype(dt), cq_all[:, :n_prompt].astype(dt),
            h_all[:, :n_prompt].astype(dt), cx_all[:, :n_prompt].astype(dt),
            s_all[:, n_prompt:].astype(state_delta_S.dtype),
            cq_all[:, n_prompt:].astype(state_delta_conv.dtype),
            h_all[:, n_prompt:].astype(state_lru_h.dtype),
            cx_all[:, n_prompt:].astype(state_lru_conv.dtype))
```

```python
import functools

import numpy as np
import jax
import jax.numpy as jnp
from jax import lax
from jax.experimental import pallas as pl
from jax.experimental.pallas import tpu as pltpu

F32 = jnp.float32
BF16 = jnp.bfloat16

D_MODEL = 1024
DEPTH = 4
N_META = 16
HA = 8
DK = 128
DV = 128
QK_W = HA * DK
V_W = HA * DV
QKV_W = 2 * QK_W + V_W
LRU_W = D_MODEL
NB = 8
BW = 128
CONV_W = 4
LRU_C = 8.0
D_FF = 2816
EPS = 1e-6

OFF_Z = QKV_W
OFF_BETA = OFF_Z + V_W
OFF_ALPHA = OFF_BETA + HA
OFF_LX = OFF_ALPHA + HA
OFF_LY = OFF_LX + LRU_W
OFF_GA = OFF_LY + LRU_W
OFF_GB = OFF_GA + D_MODEL
IN_COLS = OFF_GB + D_MODEL

CHUNK = 64
GROUP = 4
GROUP_ROWS = GROUP * CHUNK
PROMPT_NULL = CHUNK - N_META
SAMPLE_NULL = CHUNK // 2
INV_BASE = 16
ROW_TILE = 512
INPROJ_TILE = 256
FF_HALF = D_FF // 2
CONV_COLS = QKV_W + LRU_W
SLAB = 8
N_SLAB = CHUNK // SLAB
TAIL_ROWS = (CONV_W - 1) * SLAB
P_Z, P_X, P_Y, P_G = 3072, 4096, 5120, 6144
P_COLS = 8192
VMEM_LIMIT = 56 * 1024 * 1024


def _rms(x, w):
    return x * lax.rsqrt(jnp.mean(x * x, axis=-1, keepdims=True) + EPS) * w


def _mm(a, b):
    return jnp.dot(a.astype(BF16), b.astype(BF16), preferred_element_type=F32)


def _mm_nt(a, b):
    return lax.dot_general(a.astype(BF16), b.astype(BF16), (((1,), (1,)), ((), ())),
                           preferred_element_type=F32)


def _mm_tn(a, b):
    return lax.dot_general(a.astype(BF16), b.astype(BF16), (((0,), (0,)), ((), ())),
                           preferred_element_type=F32)


def _sigmoid(x):
    return 0.5 * (1.0 + jnp.tanh(0.5 * x))


def _silu(x):
    return x * _sigmoid(x)


def _softplus(x):
    return jnp.maximum(x, 0.0) + jnp.log1p(jnp.exp(-jnp.abs(x)))


def _gelu_tanh(x):
    return 0.5 * x * (1.0 + jnp.tanh(0.7978845608028654 * (x + 0.044715 * (x * x * x))))


class _Geometry:
    def __init__(self, n_prompt, n_sample, chunks_per_prompt):
        assert n_prompt % GROUP == 0 and n_sample % GROUP == 0
        self.n_prompt = n_prompt
        self.n_sample = n_sample
        self.cpp = chunks_per_prompt
        self.prompt_steps = (n_prompt // GROUP) * chunks_per_prompt
        self.steps = self.prompt_steps + n_sample // GROUP
        self.prompt_chunks = GROUP * self.prompt_steps
        self.chunks = GROUP * self.steps
        self.rows = self.chunks * CHUNK
        self.n_streams = n_prompt + n_sample

    def chunk_null(self, cid):
        is_prompt = cid < self.prompt_chunks
        first = jnp.logical_or(jnp.logical_not(is_prompt), (cid // GROUP) % self.cpp == 0)
        return jnp.where(first, jnp.where(is_prompt, PROMPT_NULL, SAMPLE_NULL), 0)

    def last_chunk_ids(self):
        ids = [GROUP * ((b // GROUP) * self.cpp + self.cpp - 1) + b % GROUP for b in range(self.n_prompt)]
        ids += [self.prompt_chunks + t for t in range(self.n_sample)]
        return np.asarray(ids, np.int32)


def _row_time(rows):
    return N_SLAB * (rows % SLAB) + rows // SLAB


def _valid_rows(null):
    return _row_time(lax.broadcasted_iota(jnp.int32, (CHUNK, 1), 0)) >= null


def _slabs(x):
    return [x[v * SLAB:(v + 1) * SLAB] for v in range(N_SLAB)]


def _shift_run(x, fill):
    return jnp.concatenate([fill, x[:SLAB - 1]], axis=0)


def _scan_time(a, b, h_in):
    bs = _slabs(b)
    if a is None:
        for v in range(1, N_SLAB):
            bs[v] = bs[v] + bs[v - 1]
        run = bs[N_SLAB - 1]
        d = 1
        while d < SLAB:
            run = run + _shift_rows(run, d, 0.0)
            d *= 2
        run = run + h_in
        prev = _shift_run(run, h_in)
        hs = [t + prev for t in bs]
    else:
        as_ = _slabs(a)
        for v in range(1, N_SLAB):
            bs[v] = bs[v] + as_[v] * bs[v - 1]
            as_[v] = as_[v] * as_[v - 1]
        ra, rb = as_[N_SLAB - 1], bs[N_SLAB - 1]
        d = 1
        while d < SLAB:
            rb = rb + ra * _shift_rows(rb, d, 0.0)
            ra = ra * _shift_rows(ra, d, 1.0)
            d *= 2
        run = rb + ra * h_in
        prev = _shift_run(run, h_in)
        hs = [t + u * prev for t, u in zip(bs, as_)]
    return jnp.concatenate(hs, axis=0), run[SLAB - 1:SLAB, :]


def _ffn(x, nw, wgu_ref, wd_ref, h_ref):
    xn = _rms(x, nw).astype(BF16)
    for c in range(2):
        lo = c * FF_HALF
        g = jnp.dot(xn, wgu_ref[:, lo:lo + FF_HALF], preferred_element_type=F32)
        u = jnp.dot(xn, wgu_ref[:, D_FF + lo:D_FF + lo + FF_HALF], preferred_element_type=F32)
        h_ref[:, lo:lo + FF_HALF] = (_silu(g) * u).astype(BF16)
    y = jnp.dot(h_ref[...], wd_ref[...], preferred_element_type=F32)
    return x + 0.5 * y


def _ffn_kernel(x_ref, nw_ref, wgu_ref, wd_ref, o_ref, h_ref):
    o_ref[...] = _ffn(x_ref[...], nw_ref[...], wgu_ref, wd_ref, h_ref)


def _inproj_kernel(geo, x_ref, nw_ref, w_ref, wba_ref, cw_ref, lcw_ref, lcb_ref, inj_ref,
                   p_ref, ba_ref, tails_ref, stage_all, carry_scr):
    i = pl.program_id(0)
    nch = INPROJ_TILE // CHUNK
    xn = _rms(x_ref[...], nw_ref[...]).astype(BF16)

    @pl.when(i == 0)
    def _():
        carry_scr[...] = jnp.zeros_like(carry_scr)

    valid = [_valid_rows(geo.chunk_null(i * nch + j)) for j in range(nch)]
    in_sample = i >= geo.prompt_chunks * CHUNK // INPROJ_TILE
    kinds = (("q", 0, 0), ("k", QK_W, QK_W), ("v", 2 * QK_W, 2 * QK_W), ("x", P_X, QKV_W))
    t0 = TAIL_ROWS

    def project(kidx):
        kind, lo, col = kinds[kidx]
        stage = stage_all.at[kidx]
        pre = jnp.dot(xn, w_ref[:, lo:lo + 1024], preferred_element_type=F32)
        for j in range(nch):
            stage[j, t0:t0 + CHUNK, :] = pre[j * CHUNK:(j + 1) * CHUNK]
            if j >= GROUP:
                stage[j, 0:t0, :] = pre[(j - GROUP + 1) * CHUNK - t0:(j - GROUP + 1) * CHUNK]
            else:
                stage[j, 0:t0, :] = carry_scr[j, :, col:col + 1024]
            stage[j, CHUNK:CHUNK + t0, :] = stage[j, CHUNK:CHUNK + t0, :] + jnp.where(
                in_sample, inj_ref[t0 * j:t0 * (j + 1), col:col + 1024], 0.0)
            tails_ref[j, :, col:col + 1024] = stage[j, CHUNK:CHUNK + t0, :]
        for g in range(GROUP):
            carry_scr[g, :, col:col + 1024] = stage[nch - GROUP + g, CHUNK:CHUNK + t0, :]

    def convolve(kidx, j):
        kind, lo, col = kinds[kidx]
        stage = stage_all.at[kidx]
        taps = lcw_ref[...] if kind == "x" else cw_ref[:, lo:lo + 1024]
        cur = [stage[j, t0 + v * SLAB:t0 + (v + 1) * SLAB, :] for v in range(N_SLAB)]
        back = []
        for k in range(CONV_W - 1):
            prev_last = stage[j, (k + 1) * SLAB - 1:(k + 1) * SLAB, :]
            back.append(_shift_run(cur[N_SLAB - (CONV_W - 1) + k], prev_last))
        conv = []
        for v in range(N_SLAB):
            acc = cur[v] * taps[CONV_W - 1:CONV_W, :]
            for d in range(1, CONV_W):
                src = cur[v - d] if v >= d else back[CONV_W - 1 - d + v]
                acc = acc + src * taps[CONV_W - 1 - d:CONV_W - d, :]
            conv.append(acc)
        conv = jnp.concatenate(conv, axis=0)
        if kind == "x":
            out = conv + lcb_ref[...]
        else:
            act = _silu(conv)
            if kind != "v":
                scale = DK ** -0.5 if kind == "q" else 1.0
                segs = []
                for h in range(HA):
                    seg = act[:, h * DK:(h + 1) * DK]
                    segs.append(seg * (lax.rsqrt(jnp.sum(seg * seg, axis=-1, keepdims=True) + EPS) * scale))
                act = jnp.concatenate(segs, axis=1)
            out = jnp.where(valid[j], act, 0.0)
        p_ref[j * CHUNK:(j + 1) * CHUNK, lo:lo + 1024] = out.astype(BF16)

    def plain(lo):
        p_ref[:, lo:lo + 1024] = jnp.dot(
            xn, w_ref[:, lo:lo + 1024], preferred_element_type=F32).astype(BF16)

    plain_cols = (P_Z, P_Y, P_G, P_G + 1024)
    project(0)
    for kidx in range(len(kinds)):
        plain(plain_cols[kidx])
        if kidx + 1 < len(kinds):
            project(kidx + 1)
        for j in range(nch):
            convolve(kidx, j)
    ba_ref[...] = jnp.dot(xn, wba_ref[...], preferred_element_type=F32)


def _shift_rows(x, d, fill):
    if d % 8 == 0:
        return jnp.concatenate([jnp.full((d, x.shape[1]), fill, x.dtype), x[:x.shape[0] - d]], axis=0)
    rows = lax.broadcasted_iota(jnp.int32, x.shape, 0)
    return jnp.where(rows >= d, pltpu.roll(x, d, 0), fill)


def _outproj_kernel(geo, final, x_ref, g_ref, oa_ref, xl_ref, h0_ref,
                    wri_ref, br_ref, bi_ref, lam_ref, wa_ref, wb_ref, wo_ref,
                    nw_ref, wgu_ref, wd_ref, fn_ref,
                    o_ref, hlast_ref, h_ref, hc_scr, ob_scr):
    i = pl.program_id(0)
    nch = ROW_TILE // CHUNK

    @pl.when(i == 0)
    def _():
        hc_scr[...] = jnp.zeros_like(hc_scr)

    sp = _softplus(-lam_ref[...])
    nulls = [geo.chunk_null(i * nch + j) for j in range(nch)]
    valid = [_valid_rows(n) for n in nulls]
    is_sample = [(i * nch + j) >= geo.prompt_chunks for j in range(nch)]
    for n in range(NB):
        sl = slice(n * BW, (n + 1) * BW)
        xb = xl_ref[:, sl]
        gates = jnp.dot(xb, wri_ref[n], preferred_element_type=F32)
        r = _sigmoid(gates[:, :BW] + br_ref[:, sl])
        ig = _sigmoid(gates[:, BW:] + bi_ref[:, sl])
        log_a = -LRU_C * r * sp[:, sl]
        a_all = jnp.exp(log_a)
        th = jnp.tanh(log_a)
        mult = jnp.sqrt(-2.0 * th / (1.0 - th))
        b_all = mult * (ig * xb.astype(F32))
        gy = _gelu_tanh(xl_ref[:, LRU_W + n * BW:LRU_W + (n + 1) * BW].astype(F32))
        carry = [hc_scr[g:g + 1, sl] for g in range(GROUP)]
        for j in range(nch):
            rows = slice(j * CHUNK, (j + 1) * CHUNK)
            a = jnp.where(valid[j], a_all[rows], 1.0)
            b = jnp.where(valid[j], b_all[rows], 0.0)
            h_in = jnp.where(nulls[j] > 0, jnp.where(is_sample[j], h0_ref[j:j + 1, sl], 0.0), carry[j % GROUP])
            hs, carry[j % GROUP] = _scan_time(a, b, h_in)
            ob_scr[rows, sl] = (hs * gy[rows]).astype(BF16)
            hlast_ref[j:j + 1, sl] = carry[j % GROUP]
        for g in range(GROUP):
            hc_scr[g:g + 1, sl] = carry[g]

    ga = _sigmoid(g_ref[:, :D_MODEL].astype(F32))
    gb = _sigmoid(g_ref[:, D_MODEL:].astype(F32))
    ma = jnp.dot(oa_ref[...], wa_ref[...], preferred_element_type=F32)
    mb = jnp.dot(ob_scr[...], wb_ref[...], preferred_element_type=F32)
    m = (ga * ma + gb * mb).astype(BF16)
    x = x_ref[...] + jnp.dot(m, wo_ref[...], preferred_element_type=F32)
    x = _ffn(x, nw_ref[...], wgu_ref, wd_ref, h_ref)
    if final:
        x = _rms(x, fn_ref[...])
    o_ref[...] = x


def _resident(shape, layer):
    nd = len(shape)
    return pl.BlockSpec((None,) + shape, lambda i: (layer,) + (0,) * nd, pipeline_mode=pl.Buffered(1))


def _params(semantics):
    return pltpu.CompilerParams(dimension_semantics=(semantics,), vmem_limit_bytes=VMEM_LIMIT)


def _ffn_call(layer, x, nw, wgu, wd):
    n = x.shape[0]
    row = lambda i: (i, 0)
    return pl.pallas_call(
        _ffn_kernel,
        grid=(n // ROW_TILE,),
        in_specs=[pl.BlockSpec((ROW_TILE, D_MODEL), row),
                  _resident((1, D_MODEL), layer),
                  _resident((D_MODEL, 2 * D_FF), layer),
                  _resident((D_FF, D_MODEL), layer)],
        out_specs=pl.BlockSpec((ROW_TILE, D_MODEL), row),
        out_shape=jax.ShapeDtypeStruct((n, D_MODEL), F32),
        scratch_shapes=[pltpu.VMEM((ROW_TILE, D_FF), BF16)],
        compiler_params=_params("parallel"),
        name="ffn",
    )(x, nw, wgu, wd)


def _inproj_call(geo, layer, x, nw, w, wba, cw, lcw, lcb, inj):
    n = x.shape[0]
    nch = INPROJ_TILE // CHUNK
    sample_tile0 = geo.prompt_chunks * CHUNK // INPROJ_TILE
    row = lambda i: (i, 0)
    return pl.pallas_call(
        functools.partial(_inproj_kernel, geo),
        grid=(n // INPROJ_TILE,),
        in_specs=[pl.BlockSpec((INPROJ_TILE, D_MODEL), row),
                  _resident((1, D_MODEL), layer),
                  _resident((D_MODEL, P_COLS), layer),
                  _resident((D_MODEL, 256), layer),
                  _resident((CONV_W, QKV_W), layer),
                  _resident((CONV_W, LRU_W), layer),
                  _resident((1, LRU_W), layer),
                  pl.BlockSpec((None, TAIL_ROWS * nch, CONV_COLS),
                               lambda i: (layer, jnp.maximum(i - sample_tile0, 0), 0))],
        out_specs=[pl.BlockSpec((INPROJ_TILE, P_COLS), row),
                   pl.BlockSpec((INPROJ_TILE, 256), row),
                   pl.BlockSpec((nch, TAIL_ROWS, CONV_COLS), lambda i: (i, 0, 0))],
        out_shape=[jax.ShapeDtypeStruct((n, P_COLS), BF16),
                   jax.ShapeDtypeStruct((n, 256), F32),
                   jax.ShapeDtypeStruct((n // CHUNK, TAIL_ROWS, CONV_COLS), F32)],
        scratch_shapes=[pltpu.VMEM((4, nch, CHUNK + TAIL_ROWS, 1024), F32),
                        pltpu.VMEM((GROUP, TAIL_ROWS, CONV_COLS), F32)],
        compiler_params=_params("arbitrary"),
        name="inproj",
    )(x, nw, w, wba, cw, lcw, lcb, inj)


def _outproj_call(geo, layer, x, p, oa, h0, wri, br, bi, lam, wa, wb, wo, nw, wgu, wd, fn, final):
    n = x.shape[0]
    nch = ROW_TILE // CHUNK
    sample_tile0 = geo.prompt_chunks * CHUNK // ROW_TILE
    row = lambda i: (i, 0)
    return pl.pallas_call(
        functools.partial(_outproj_kernel, geo, final),
        grid=(n // ROW_TILE,),
        in_specs=[pl.BlockSpec((ROW_TILE, D_MODEL), row),
                  pl.BlockSpec((ROW_TILE, 2 * D_MODEL), lambda i: (i, P_G // 2048)),
                  pl.BlockSpec((ROW_TILE, V_W), row),
                  pl.BlockSpec((ROW_TILE, 2 * LRU_W), lambda i: (i, P_X // 2048)),
                  pl.BlockSpec((None, nch, LRU_W), lambda i: (layer, jnp.maximum(i - sample_tile0, 0), 0)),
                  _resident((NB, BW, 2 * BW), layer),
                  _resident((1, LRU_W), layer),
                  _resident((1, LRU_W), layer),
                  _resident((1, LRU_W), layer),
                  _resident((V_W, D_MODEL), layer),
                  _resident((LRU_W, D_MODEL), layer),
                  _resident((D_MODEL, D_MODEL), layer),
                  _resident((1, D_MODEL), layer),
                  _resident((D_MODEL, 2 * D_FF), layer),
                  _resident((D_FF, D_MODEL), layer),
                  pl.BlockSpec((1, D_MODEL), lambda i: (0, 0), pipeline_mode=pl.Buffered(1))],
        out_specs=[pl.BlockSpec((ROW_TILE, D_MODEL), row),
                   pl.BlockSpec((nch, LRU_W), row)],
        out_shape=[jax.ShapeDtypeStruct((n, D_MODEL), F32),
                   jax.ShapeDtypeStruct((n // CHUNK, LRU_W), F32)],
        scratch_shapes=[pltpu.VMEM((ROW_TILE, D_FF), BF16),
                        pltpu.VMEM((8, LRU_W), F32),
                        pltpu.VMEM((ROW_TILE, LRU_W), BF16)],
        compiler_params=_params("arbitrary"),
        name="outproj_ffn",
    )(x, p, oa, p, h0, wri, br, bi, lam, wa, wb, wo, nw, wgu, wd, fn)


def _unit_lower_inverse(lows, c):
    ti = _row_time(lax.broadcasted_iota(jnp.int32, (c, c), 0))
    tj = _row_time(lax.broadcasted_iota(jnp.int32, (c, c), 1))
    same_block = (ti // INV_BASE) == (tj // INV_BASE)
    eye = (ti == tj).astype(F32)
    ld = [jnp.where(same_block, low, 0.0) for low in lows]
    lo = [low - d for low, d in zip(lows, ld)]
    x = [eye - d for d in ld]
    p = ld
    span = 2
    while span < INV_BASE:
        p = [_mm(t, t) for t in p]
        x = [a + _mm(a, t) for a, t in zip(x, p)]
        span *= 2
    e = [_mm(a, b) for a, b in zip(x, lo)]
    y = [eye - t for t in e]
    q = e
    span = 2
    while span < c // INV_BASE:
        q = [_mm(t, t) for t in q]
        y = [a + _mm(a, t) for a, t in zip(y, q)]
        span *= 2
    return [_mm(a, b) for a, b in zip(y, x)]


def _delta_kernel(geo, qkv_ref, z_ref, ba_ref, s0_ref, alog_ref, dtb_ref, onorm_ref, sp_in, ss_in,
                  oa_ref, sp_ref, ss_ref, s_scr):
    del sp_in, ss_in
    c = CHUNK
    i = pl.program_id(0)
    is_prompt = i < geo.prompt_steps
    pos = i % geo.cpp
    first = jnp.logical_or(jnp.logical_not(is_prompt), pos == 0)
    last = jnp.logical_or(jnp.logical_not(is_prompt), pos == geo.cpp - 1)
    null = jnp.where(first, jnp.where(is_prompt, PROMPT_NULL, SAMPLE_NULL), 0)
    valid = _valid_rows(null)

    @pl.when(jnp.logical_and(first, is_prompt))
    def _():
        s_scr[...] = jnp.zeros_like(s_scr)

    @pl.when(jnp.logical_not(is_prompt))
    def _():
        s_scr[...] = s0_ref[...]

    ti = _row_time(lax.broadcasted_iota(jnp.int32, (c, c), 0))
    tj = _row_time(lax.broadcasted_iota(jnp.int32, (c, c), 1))
    tri_incl = ti >= tj
    tri_strict = ti > tj

    beta_all, gc_all, gc_t, eg_all, ekd_all, egl_all = [], [], [], [], [], []
    for s in range(GROUP):
        ba = ba_ref[s * c:(s + 1) * c, :]
        beta_all.append(jnp.where(valid, _sigmoid(ba[:, :128]), 0.0))
        g = -jnp.exp(alog_ref[...]) * _softplus(ba[:, 128:] + dtb_ref[...])
        gc, g_last = _scan_time(None, jnp.where(valid, g, 0.0), jnp.zeros((1, 128), F32))
        gc_all.append(gc)
        gc_t.append(jnp.concatenate([gc, jnp.zeros((128 - c, 128), F32)], axis=0).T)
        eg_all.append(jnp.exp(gc))
        ekd_all.append(jnp.exp(g_last - gc))
        egl_all.append(jnp.exp(g_last))

    probs = [(s, h) for s in range(GROUP) for h in range(HA)]
    rows = lambda s: slice(s * c, (s + 1) * c)
    qb = [qkv_ref[rows(s), h * DK:(h + 1) * DK] for s, h in probs]
    kbf = [qkv_ref[rows(s), QK_W + h * DK:QK_W + (h + 1) * DK] for s, h in probs]
    vb = [qkv_ref[rows(s), 2 * QK_W + h * DV:2 * QK_W + (h + 1) * DV] for s, h in probs]
    q = [t.astype(F32) for t in qb]
    k = [t.astype(F32) for t in kbf]
    beta = [beta_all[s][:, h:h + 1] for s, h in probs]
    eg = [eg_all[s][:, h:h + 1] for s, h in probs]
    n = len(probs)
    kb = [k[p] * beta[p] for p in range(n)]
    sc = [_mm_nt(jnp.concatenate([kb[p].astype(BF16), qb[p]], axis=0), kbf[p]) for p in range(n)]
    decay = [jnp.where(tri_incl, jnp.exp(jnp.where(tri_incl, gc_all[s][:, h:h + 1] - gc_t[s][h:h + 1, :c], 0.0)), 0.0)
             for s, h in probs]
    low = [jnp.where(tri_strict, sc[p][:c] * decay[p], 0.0) for p in range(n)]
    a_intra = [sc[p][c:] * decay[p] for p in range(n)]
    t_inv = _unit_lower_inverse(low, c)
    uw = [_mm(t_inv[p], jnp.concatenate([vb[p].astype(F32) * beta[p], kb[p] * eg[p]], axis=1))
          for p in range(n)]
    s_old = [s_scr[s, h] for s, h in probs]
    wq = [_mm(jnp.concatenate([uw[p][:, DV:], q[p] * eg[p]], axis=0), s_old[p]) for p in range(n)]
    v_new = [uw[p][:, :DV] - wq[p][:c] for p in range(n)]
    o_intra = [_mm(a_intra[p], v_new[p]) for p in range(n)]
    ds = [_mm_tn(k[p] * ekd_all[s][:, h:h + 1], v_new[p]) for p, (s, h) in enumerate(probs)]
    for p, (s, h) in enumerate(probs):
        s_scr[s, h] = s_old[p] * egl_all[s][:, h:h + 1] + ds[p]
        o = wq[p][c:] + o_intra[p]
        zh = z_ref[rows(s), h * DV:(h + 1) * DV].astype(F32)
        o = o * lax.rsqrt(jnp.mean(o * o, axis=-1, keepdims=True) + EPS) * onorm_ref[...] * _silu(zh)
        oa_ref[rows(s), h * DV:(h + 1) * DV] = o.astype(oa_ref.dtype)

    @pl.when(jnp.logical_and(last, is_prompt))
    def _():
        sp_ref[...] = s_scr[...].astype(sp_ref.dtype)

    @pl.when(jnp.logical_not(is_prompt))
    def _():
        ss_ref[...] = s_scr[...].astype(ss_ref.dtype)


def _delta_call(geo, layer, p, ba, s0, alog, dtb, onorm, sp_all, ss_all):
    def prompt_group(i):
        return jnp.minimum(i, geo.prompt_steps - 1) // geo.cpp

    def sample_group(i):
        return jnp.maximum(i - geo.prompt_steps, 0)

    vec = lambda width: pl.BlockSpec((None, 1, width), lambda i: (layer, 0, 0))
    state = lambda group: pl.BlockSpec((None, GROUP, HA, DK, DV), lambda i: (layer, group(i), 0, 0, 0))
    return pl.pallas_call(
        functools.partial(_delta_kernel, geo),
        grid=(geo.steps,),
        in_specs=[pl.BlockSpec((GROUP_ROWS, QKV_W), lambda i: (i, 0)),
                  pl.BlockSpec((GROUP_ROWS, V_W), lambda i: (i, P_Z // V_W)),
                  pl.BlockSpec((GROUP_ROWS, 256), lambda i: (i, 0)),
                  state(sample_group),
                  vec(128), vec(128), vec(DV),
                  pl.BlockSpec(memory_space=pl.ANY),
                  pl.BlockSpec(memory_space=pl.ANY)],
        out_specs=[pl.BlockSpec((GROUP_ROWS, V_W), lambda i: (i, 0)),
                   state(prompt_group),
                   state(sample_group)],
        out_shape=[jax.ShapeDtypeStruct((geo.rows, V_W), BF16),
                   jax.ShapeDtypeStruct(sp_all.shape, sp_all.dtype),
                   jax.ShapeDtypeStruct(ss_all.shape, ss_all.dtype)],
        input_output_aliases={7: 1, 8: 2},
        scratch_shapes=[pltpu.VMEM((GROUP, HA, DK, DV), F32)],
        compiler_params=_params("arbitrary"),
        name="delta",
    )(p, p, ba, s0, alog, dtb, onorm, sp_all, ss_all)


def _permute_chunk(x):
    return jnp.swapaxes(x.reshape(SLAB, N_SLAB, x.shape[-1]), 0, 1).reshape(CHUNK, x.shape[-1])


def _pack_kernel(geo, xp_ref, xs_ref, meta_ref, o_ref):
    i = pl.program_id(0)
    is_prompt = i < geo.prompt_steps
    pos = i % geo.cpp
    width = o_ref.shape[1]

    @pl.when(jnp.logical_and(is_prompt, pos > 0))
    def _():
        for s in range(GROUP):
            o_ref[s * CHUNK:(s + 1) * CHUNK, :] = _permute_chunk(xp_ref[s].astype(o_ref.dtype))

    @pl.when(jnp.logical_and(is_prompt, pos == 0))
    def _():
        first = _permute_chunk(jnp.concatenate(
            [jnp.zeros((PROMPT_NULL, width), o_ref.dtype), meta_ref[...].astype(o_ref.dtype)], axis=0))
        for s in range(GROUP):
            o_ref[s * CHUNK:(s + 1) * CHUNK, :] = first

    @pl.when(jnp.logical_not(is_prompt))
    def _():
        for s in range(GROUP):
            o_ref[s * CHUNK:(s + 1) * CHUNK, :] = _permute_chunk(jnp.concatenate(
                [jnp.zeros((SAMPLE_NULL, width), o_ref.dtype), xs_ref[s].astype(o_ref.dtype)], axis=0))


def _unpack_kernel(geo, x_ref, yp_ref, ys_ref):
    i = pl.program_id(0)
    is_prompt = i < geo.prompt_steps
    pos = i % geo.cpp

    @pl.when(jnp.logical_and(is_prompt, pos > 0))
    def _():
        for s in range(GROUP):
            yp_ref[s] = _permute_chunk(x_ref[s * CHUNK:(s + 1) * CHUNK, :]).astype(yp_ref.dtype)

    @pl.when(jnp.logical_not(is_prompt))
    def _():
        for s in range(GROUP):
            ys_ref[s] = _permute_chunk(x_ref[s * CHUNK:(s + 1) * CHUNK, :])[SAMPLE_NULL:].astype(ys_ref.dtype)


def _layout_specs(geo):
    def prompt_idx(i):
        step = jnp.minimum(i, geo.prompt_steps - 1)
        return (step // geo.cpp, jnp.maximum(step % geo.cpp - 1, 0), 0)

    def sample_idx(i):
        return (jnp.maximum(i - geo.prompt_steps, 0), 0, 0)

    return (pl.BlockSpec((GROUP, CHUNK, D_MODEL), prompt_idx),
            pl.BlockSpec((GROUP, CHUNK - SAMPLE_NULL, D_MODEL), sample_idx))


def _pack_call(geo, x_prompt, x_sample, meta):
    prompt_spec, sample_spec = _layout_specs(geo)
    return pl.pallas_call(
        functools.partial(_pack_kernel, geo),
        grid=(geo.steps,),
        in_specs=[prompt_spec, sample_spec, pl.BlockSpec((N_META, D_MODEL), lambda i: (0, 0))],
        out_specs=pl.BlockSpec((GROUP_ROWS, D_MODEL), lambda i: (i, 0)),
        out_shape=jax.ShapeDtypeStruct((geo.rows, D_MODEL), F32),
        compiler_params=_params("parallel"),
        name="pack",
    )(x_prompt, x_sample, meta)


def _unpack_call(geo, x, dtype):
    prompt_spec, sample_spec = _layout_specs(geo)
    return pl.pallas_call(
        functools.partial(_unpack_kernel, geo),
        grid=(geo.steps,),
        in_specs=[pl.BlockSpec((GROUP_ROWS, D_MODEL), lambda i: (i, 0))],
        out_specs=[prompt_spec, sample_spec],
        out_shape=[jax.ShapeDtypeStruct((geo.n_prompt, (geo.cpp - 1) * CHUNK, D_MODEL), dtype),
                   jax.ShapeDtypeStruct((geo.n_sample, CHUNK - SAMPLE_NULL, D_MODEL), dtype)],
        compiler_params=_params("arbitrary"),
        name="unpack",
    )(x)


def _pad_lanes(v, width):
    return jnp.pad(v, ((0, 0), (0, width - v.shape[-1])))


def kernel(x_prompt, x_sample, state_delta_S, state_delta_conv, state_lru_h, state_lru_conv,
           meta_tokens, ffn1_norm, ffn1_w_gu, ffn1_w_down, mix_norm, w_in, delta_conv_w,
           delta_A_log, delta_dt_bias, delta_out_norm, lru_conv_w, lru_conv_b, lru_w_r, lru_b_r,
           lru_w_i, lru_b_i, lru_lambda, w_branch_a, w_branch_b, w_out, ffn2_norm, ffn2_w_gu,
           ffn2_w_down, final_norm):
    n_prompt, seq_len, _ = x_prompt.shape
    n_sample, dec_len, _ = x_sample.shape
    assert dec_len == CHUNK - SAMPLE_NULL and (N_META + seq_len) % CHUNK == N_META
    prompt_rows = PROMPT_NULL + N_META + seq_len
    cpp = prompt_rows // CHUNK
    geo = _Geometry(n_prompt, n_sample, cpp)
    assert geo.rows % ROW_TILE == 0 and (geo.prompt_chunks * CHUNK) % ROW_TILE == 0
    assert INPROJ_TILE % GROUP_ROWS == 0 and ROW_TILE % GROUP_ROWS == 0
    dt = x_prompt.dtype

    x = _pack_call(geo, x_prompt, x_sample, meta_tokens)

    cast = lambda w: w.astype(BF16)
    vec = lambda v: v.astype(F32)[:, None, :]
    wgu1, wd1, wgu2, wd2 = cast(ffn1_w_gu), cast(ffn1_w_down), cast(ffn2_w_gu), cast(ffn2_w_down)
    w_main = cast(jnp.concatenate([w_in[:, :, :OFF_BETA], w_in[:, :, OFF_LX:]], axis=-1))
    pad_heads = lambda w: jnp.pad(w, ((0, 0), (0, 0), (0, 128 - HA)))
    w_ba = cast(jnp.concatenate([pad_heads(w_in[:, :, OFF_BETA:OFF_ALPHA]),
                                 pad_heads(w_in[:, :, OFF_ALPHA:OFF_LX])], axis=-1))
    w_ri = cast(jnp.concatenate([lru_w_r, lru_w_i], axis=-1))
    wa, wb, wo = cast(w_branch_a), cast(w_branch_b), cast(w_out)
    norm1, norm_mix, norm2 = vec(ffn1_norm), vec(mix_norm), vec(ffn2_norm)
    alog, dtb, onorm = vec(_pad_lanes(delta_A_log, 128)), vec(_pad_lanes(delta_dt_bias, 128)), vec(delta_out_norm)
    lcb, b_r, b_i, lam = vec(lru_conv_b), vec(lru_b_r), vec(lru_b_i), vec(lru_lambda)
    cw, lcw = delta_conv_w.astype(F32), lru_conv_w.astype(F32)
    s0, h0 = state_delta_S.astype(F32), state_lru_h.astype(F32)

    inj = jnp.concatenate([state_delta_conv, state_lru_conv], axis=-1).astype(F32)
    null_runs = SAMPLE_NULL // N_SLAB
    inj = jnp.pad(inj[:, :, :, None, :], ((0, 0), (0, 0), (0, 0), (null_runs - 1, SLAB - null_runs), (0, 0)))
    inj = inj.reshape(DEPTH, n_sample * TAIL_ROWS, CONV_COLS)

    last_ids = geo.last_chunk_ids()
    sp_all = jnp.zeros((DEPTH, n_prompt, HA, DK, DV), dt)
    ss_all = jnp.zeros((DEPTH, n_sample, HA, DK, DV), state_delta_S.dtype)
    outs_tail, outs_h = [], []
    for l in range(DEPTH):
        x = _ffn_call(l, x, norm1, wgu1, wd1)
        p, ba, tails = _inproj_call(geo, l, x, norm_mix, w_main, w_ba, cw, lcw, lcb, inj)
        oa, sp_all, ss_all = _delta_call(geo, l, p, ba, s0, alog, dtb, onorm, sp_all, ss_all)
        x, hlast = _outproj_call(geo, l, x, p, oa, h0, w_ri, b_r, b_i, lam, wa, wb, wo, norm2, wgu2, wd2,
                                 final_norm[None].astype(F32), final=(l == DEPTH - 1))
        outs_tail.append(tails[last_ids, SLAB - 1::SLAB, :])
        outs_h.append(hlast[last_ids])

    n_prompt_rows = geo.prompt_chunks * CHUNK
    y_prompt, y_sample = _unpack_call(geo, x, dt)
    tail_all = jnp.stack(outs_tail)
    cq_all = tail_all[..., :QKV_W]
    cx_all = tail_all[..., QKV_W:]
    h_all = jnp.stack(outs_h)
    return (y_prompt.astype(dt), y_sample.astype(dt),
            sp_all, cq_all[:, :n_prompt].astype(dt),
            h_all[:, :n_prompt].astype(dt), cx_all[:, :n_prompt].astype(dt),
            ss_all,
            cq_all[:, n_prompt:].astype(state_delta_conv.dtype),
            h_all[:, n_prompt:].astype(state_lru_h.dtype),
            cx_all[:, n_prompt:].astype(state_lru_conv.dtype))
```

```python
import functools

import numpy as np
import jax
import jax.numpy as jnp
from jax import lax
from jax.experimental import pallas as pl
from jax.experimental.pallas import tpu as pltpu

F32 = jnp.float32
BF16 = jnp.bfloat16

D_MODEL = 1024
DEPTH = 4
N_META = 16
HA = 8
DK = 128
DV = 128
QK_W = HA * DK
V_W = HA * DV
QKV_W = 2 * QK_W + V_W
LRU_W = D_MODEL
NB = 8
BW = 128
CONV_W = 4
LRU_C = 8.0
D_FF = 2816
EPS = 1e-6

OFF_Z = QKV_W
OFF_BETA = OFF_Z + V_W
OFF_ALPHA = OFF_BETA + HA
OFF_LX = OFF_ALPHA + HA
OFF_LY = OFF_LX + LRU_W
OFF_GA = OFF_LY + LRU_W
OFF_GB = OFF_GA + D_MODEL
IN_COLS = OFF_GB + D_MODEL

CHUNK = 64
GROUP = 4
GROUP_ROWS = GROUP * CHUNK
PROMPT_NULL = CHUNK - N_META
SAMPLE_NULL = CHUNK // 2
INV_BASE = 16
ROW_TILE = 512
INPROJ_TILE = 256
FF_HALF = D_FF // 2
CONV_COLS = QKV_W + LRU_W
SLAB = 8
N_SLAB = CHUNK // SLAB
TAIL_ROWS = (CONV_W - 1) * SLAB
P_Z, P_X, P_Y, P_G = 3072, 4096, 5120, 6144
P_COLS = 8192
VMEM_LIMIT = 56 * 1024 * 1024


def _rms(x, w):
    return x * lax.rsqrt(jnp.mean(x * x, axis=-1, keepdims=True) + EPS) * w


def _mm(a, b):
    return jnp.dot(a.astype(BF16), b.astype(BF16), preferred_element_type=F32)


def _mm_nt(a, b):
    return lax.dot_general(a.astype(BF16), b.astype(BF16), (((1,), (1,)), ((), ())),
                           preferred_element_type=F32)


def _mm_tn(a, b):
    return lax.dot_general(a.astype(BF16), b.astype(BF16), (((0,), (0,)), ((), ())),
                           preferred_element_type=F32)


def _sigmoid(x):
    return 0.5 * (1.0 + jnp.tanh(0.5 * x))


def _silu(x):
    return x * _sigmoid(x)


def _softplus(x):
    return jnp.maximum(x, 0.0) + jnp.log1p(jnp.exp(-jnp.abs(x)))


def _gelu_tanh(x):
    return 0.5 * x * (1.0 + jnp.tanh(0.7978845608028654 * (x + 0.044715 * (x * x * x))))


class _Geometry:
    def __init__(self, n_prompt, n_sample, chunks_per_prompt):
        assert n_prompt % GROUP == 0 and n_sample % GROUP == 0
        self.n_prompt = n_prompt
        self.n_sample = n_sample
        self.cpp = chunks_per_prompt
        self.prompt_steps = (n_prompt // GROUP) * chunks_per_prompt
        self.steps = self.prompt_steps + n_sample // GROUP
        self.prompt_chunks = GROUP * self.prompt_steps
        self.chunks = GROUP * self.steps
        self.rows = self.chunks * CHUNK
        self.n_streams = n_prompt + n_sample

    def chunk_null(self, cid):
        is_prompt = cid < self.prompt_chunks
        first = jnp.logical_or(jnp.logical_not(is_prompt), (cid // GROUP) % self.cpp == 0)
        return jnp.where(first, jnp.where(is_prompt, PROMPT_NULL, SAMPLE_NULL), 0)

    def last_chunk_ids(self):
        ids = [GROUP * ((b // GROUP) * self.cpp + self.cpp - 1) + b % GROUP for b in range(self.n_prompt)]
        ids += [self.prompt_chunks + t for t in range(self.n_sample)]
        return np.asarray(ids, np.int32)


def _row_time(rows):
    return N_SLAB * (rows % SLAB) + rows // SLAB


def _valid_rows(null):
    return _row_time(lax.broadcasted_iota(jnp.int32, (CHUNK, 1), 0)) >= null


def _slabs(x):
    return [x[v * SLAB:(v + 1) * SLAB] for v in range(N_SLAB)]


def _shift_run(x, fill):
    return jnp.concatenate([fill, x[:SLAB - 1]], axis=0)


def _scan_time(a, b, h_in):
    bs = _slabs(b)
    if a is None:
        for v in range(1, N_SLAB):
            bs[v] = bs[v] + bs[v - 1]
        run = bs[N_SLAB - 1]
        d = 1
        while d < SLAB:
            run = run + _shift_rows(run, d, 0.0)
            d *= 2
        run = run + h_in
        prev = _shift_run(run, h_in)
        hs = [t + prev for t in bs]
    else:
        as_ = _slabs(a)
        for v in range(1, N_SLAB):
            bs[v] = bs[v] + as_[v] * bs[v - 1]
            as_[v] = as_[v] * as_[v - 1]
        ra, rb = as_[N_SLAB - 1], bs[N_SLAB - 1]
        d = 1
        while d < SLAB:
            rb = rb + ra * _shift_rows(rb, d, 0.0)
            ra = ra * _shift_rows(ra, d, 1.0)
            d *= 2
        run = rb + ra * h_in
        prev = _shift_run(run, h_in)
        hs = [t + u * prev for t, u in zip(bs, as_)]
    return jnp.concatenate(hs, axis=0), run[SLAB - 1:SLAB, :]


def _ffn(x, nw, wgu_ref, wd_ref, h_ref):
    xn = _rms(x, nw).astype(BF16)
    for c in range(2):
        lo = c * FF_HALF
        g = jnp.dot(xn, wgu_ref[:, lo:lo + FF_HALF], preferred_element_type=F32)
        u = jnp.dot(xn, wgu_ref[:, D_FF + lo:D_FF + lo + FF_HALF], preferred_element_type=F32)
        h_ref[:, lo:lo + FF_HALF] = (_silu(g) * u).astype(BF16)
    y = jnp.dot(h_ref[...], wd_ref[...], preferred_element_type=F32)
    return x + 0.5 * y


def _ffn_kernel(x_ref, nw_ref, wgu_ref, wd_ref, o_ref, h_ref):
    o_ref[...] = _ffn(x_ref[...], nw_ref[...], wgu_ref, wd_ref, h_ref)


def _inproj_kernel(geo, x_ref, nw_ref, wa_ref, wb_ref, wba_ref, cw_ref, lcw_ref, lcb_ref, inj_ref,
                   p_ref, ba_ref, tails_ref, stage_all, carry_scr):
    i = pl.program_id(0)
    nch = INPROJ_TILE // CHUNK
    xn = _rms(x_ref[...], nw_ref[...]).astype(BF16)

    def project_cols(lo):
        w_ref, start = (wa_ref, lo) if lo < P_COLS // 2 else (wb_ref, lo - P_COLS // 2)
        return jnp.dot(xn, w_ref[:, start:start + 1024], preferred_element_type=F32)

    @pl.when(i == 0)
    def _():
        carry_scr[...] = jnp.zeros_like(carry_scr)

    valid = [_valid_rows(geo.chunk_null(i * nch + j)) for j in range(nch)]
    in_sample = i >= geo.prompt_chunks * CHUNK // INPROJ_TILE
    kinds = (("q", 0, 0), ("k", QK_W, QK_W), ("v", 2 * QK_W, 2 * QK_W), ("x", P_X, QKV_W))
    t0 = TAIL_ROWS

    def project(kidx):
        kind, lo, col = kinds[kidx]
        stage = stage_all.at[kidx]
        pre = project_cols(lo)
        for j in range(nch):
            stage[j, t0:t0 + CHUNK, :] = pre[j * CHUNK:(j + 1) * CHUNK]
            if j >= GROUP:
                stage[j, 0:t0, :] = pre[(j - GROUP + 1) * CHUNK - t0:(j - GROUP + 1) * CHUNK]
            else:
                stage[j, 0:t0, :] = carry_scr[j, :, col:col + 1024]
            stage[j, CHUNK:CHUNK + t0, :] = stage[j, CHUNK:CHUNK + t0, :] + jnp.where(
                in_sample, inj_ref[t0 * j:t0 * (j + 1), col:col + 1024], 0.0)
            tails_ref[j, :, col:col + 1024] = stage[j, CHUNK:CHUNK + t0, :]
        for g in range(GROUP):
            carry_scr[g, :, col:col + 1024] = stage[nch - GROUP + g, CHUNK:CHUNK + t0, :]

    def convolve(kidx, j):
        kind, lo, col = kinds[kidx]
        stage = stage_all.at[kidx]
        taps = lcw_ref[...] if kind == "x" else cw_ref[:, lo:lo + 1024]
        cur = [stage[j, t0 + v * SLAB:t0 + (v + 1) * SLAB, :] for v in range(N_SLAB)]
        back = []
        for k in range(CONV_W - 1):
            prev_last = stage[j, (k + 1) * SLAB - 1:(k + 1) * SLAB, :]
            back.append(_shift_run(cur[N_SLAB - (CONV_W - 1) + k], prev_last))
        conv = []
        for v in range(N_SLAB):
            acc = cur[v] * taps[CONV_W - 1:CONV_W, :]
            for d in range(1, CONV_W):
                src = cur[v - d] if v >= d else back[CONV_W - 1 - d + v]
                acc = acc + src * taps[CONV_W - 1 - d:CONV_W - d, :]
            conv.append(acc)
        conv = jnp.concatenate(conv, axis=0)
        if kind == "x":
            out = conv + lcb_ref[...]
        else:
            act = _silu(conv)
            if kind != "v":
                scale = DK ** -0.5 if kind == "q" else 1.0
                segs = []
                for h in range(HA):
                    seg = act[:, h * DK:(h + 1) * DK]
                    segs.append(seg * (lax.rsqrt(jnp.sum(seg * seg, axis=-1, keepdims=True) + EPS) * scale))
                act = jnp.concatenate(segs, axis=1)
            out = jnp.where(valid[j], act, 0.0)
        p_ref[j * CHUNK:(j + 1) * CHUNK, lo:lo + 1024] = out.astype(BF16)

    def plain(lo):
        p_ref[:, lo:lo + 1024] = project_cols(lo).astype(BF16)

    plain_cols = (P_Z, P_Y, P_G, P_G + 1024)
    project(0)
    for kidx in range(len(kinds)):
        plain(plain_cols[kidx])
        if kidx + 1 < len(kinds):
            project(kidx + 1)
        for j in range(nch):
            convolve(kidx, j)
    ba_ref[...] = jnp.dot(xn, wba_ref[...], preferred_element_type=F32)


def _shift_rows(x, d, fill):
    if d % 8 == 0:
        return jnp.concatenate([jnp.full((d, x.shape[1]), fill, x.dtype), x[:x.shape[0] - d]], axis=0)
    rows = lax.broadcasted_iota(jnp.int32, x.shape, 0)
    return jnp.where(rows >= d, pltpu.roll(x, d, 0), fill)


def _outproj_kernel(geo, final, x_ref, g_ref, oa_ref, xl_ref, h0_ref,
                    wri_ref, br_ref, bi_ref, lam_ref, wa_ref, wb_ref, wo_ref,
                    nw_ref, wgu_ref, wd_ref, fn_ref,
                    o_ref, hlast_ref, h_ref, hc_scr, ob_scr):
    i = pl.program_id(0)
    nch = ROW_TILE // CHUNK

    @pl.when(i == 0)
    def _():
        hc_scr[...] = jnp.zeros_like(hc_scr)

    sp = _softplus(-lam_ref[...])
    nulls = [geo.chunk_null(i * nch + j) for j in range(nch)]
    valid = [_valid_rows(n) for n in nulls]
    is_sample = [(i * nch + j) >= geo.prompt_chunks for j in range(nch)]
    for n in range(NB):
        sl = slice(n * BW, (n + 1) * BW)
        xb = xl_ref[:, sl]
        gates = jnp.dot(xb, wri_ref[n], preferred_element_type=F32)
        r = _sigmoid(gates[:, :BW] + br_ref[:, sl])
        ig = _sigmoid(gates[:, BW:] + bi_ref[:, sl])
        log_a = -LRU_C * r * sp[:, sl]
        a_all = jnp.exp(log_a)
        th = jnp.tanh(log_a)
        mult = jnp.sqrt(-2.0 * th / (1.0 - th))
        b_all = mult * (ig * xb.astype(F32))
        gy = _gelu_tanh(xl_ref[:, LRU_W + n * BW:LRU_W + (n + 1) * BW].astype(F32))
        carry = [hc_scr[g:g + 1, sl] for g in range(GROUP)]
        for j in range(nch):
            rows = slice(j * CHUNK, (j + 1) * CHUNK)
            a = jnp.where(valid[j], a_all[rows], 1.0)
            b = jnp.where(valid[j], b_all[rows], 0.0)
            h_in = jnp.where(nulls[j] > 0, jnp.where(is_sample[j], h0_ref[j:j + 1, sl], 0.0), carry[j % GROUP])
            hs, carry[j % GROUP] = _scan_time(a, b, h_in)
            ob_scr[rows, sl] = (hs * gy[rows]).astype(BF16)
            hlast_ref[j:j + 1, sl] = carry[j % GROUP]
        for g in range(GROUP):
            hc_scr[g:g + 1, sl] = carry[g]

    ga = _sigmoid(g_ref[:, :D_MODEL].astype(F32))
    gb = _sigmoid(g_ref[:, D_MODEL:].astype(F32))
    ma = jnp.dot(oa_ref[...], wa_ref[...], preferred_element_type=F32)
    mb = jnp.dot(ob_scr[...], wb_ref[...], preferred_element_type=F32)
    m = (ga * ma + gb * mb).astype(BF16)
    x = x_ref[...] + jnp.dot(m, wo_ref[...], preferred_element_type=F32)
    x = _ffn(x, nw_ref[...], wgu_ref, wd_ref, h_ref)
    if final:
        x = _rms(x, fn_ref[...])
    o_ref[...] = x


def _resident(shape, layer):
    nd = len(shape)
    return pl.BlockSpec((None,) + shape, lambda i: (layer,) + (0,) * nd, pipeline_mode=pl.Buffered(1))


def _params(semantics):
    return pltpu.CompilerParams(dimension_semantics=(semantics,), vmem_limit_bytes=VMEM_LIMIT)


def _ffn_call(layer, x, nw, wgu, wd):
    n = x.shape[0]
    row = lambda i: (i, 0)
    return pl.pallas_call(
        _ffn_kernel,
        grid=(n // ROW_TILE,),
        in_specs=[pl.BlockSpec((ROW_TILE, D_MODEL), row),
                  _resident((1, D_MODEL), layer),
                  _resident((D_MODEL, 2 * D_FF), layer),
                  _resident((D_FF, D_MODEL), layer)],
        out_specs=pl.BlockSpec((ROW_TILE, D_MODEL), row),
        out_shape=jax.ShapeDtypeStruct((n, D_MODEL), F32),
        scratch_shapes=[pltpu.VMEM((ROW_TILE, D_FF), BF16)],
        compiler_params=_params("parallel"),
        name="ffn",
    )(x, nw, wgu, wd)


def _inproj_call(geo, layer, x, nw, wa, wb, wba, cw, lcw, lcb, inj):
    n = x.shape[0]
    nch = INPROJ_TILE // CHUNK
    sample_tile0 = geo.prompt_chunks * CHUNK // INPROJ_TILE
    row = lambda i: (i, 0)
    return pl.pallas_call(
        functools.partial(_inproj_kernel, geo),
        grid=(n // INPROJ_TILE,),
        in_specs=[pl.BlockSpec((INPROJ_TILE, D_MODEL), row),
                  _resident((1, D_MODEL), layer),
                  _resident((D_MODEL, P_COLS // 2), layer),
                  _resident((D_MODEL, P_COLS // 2), layer),
                  _resident((D_MODEL, 256), layer),
                  _resident((CONV_W, QKV_W), layer),
                  _resident((CONV_W, LRU_W), layer),
                  _resident((1, LRU_W), layer),
                  pl.BlockSpec((None, TAIL_ROWS * nch, CONV_COLS),
                               lambda i: (layer, jnp.maximum(i - sample_tile0, 0), 0))],
        out_specs=[pl.BlockSpec((INPROJ_TILE, P_COLS), row),
                   pl.BlockSpec((INPROJ_TILE, 256), row),
                   pl.BlockSpec((nch, TAIL_ROWS, CONV_COLS), lambda i: (i, 0, 0))],
        out_shape=[jax.ShapeDtypeStruct((n, P_COLS), BF16),
                   jax.ShapeDtypeStruct((n, 256), F32),
                   jax.ShapeDtypeStruct((n // CHUNK, TAIL_ROWS, CONV_COLS), F32)],
        scratch_shapes=[pltpu.VMEM((4, nch, CHUNK + TAIL_ROWS, 1024), F32),
                        pltpu.VMEM((GROUP, TAIL_ROWS, CONV_COLS), F32)],
        compiler_params=_params("arbitrary"),
        name="inproj",
    )(x, nw, wa, wb, wba, cw, lcw, lcb, inj)


def _outproj_call(geo, layer, x, p, oa, h0, wri, br, bi, lam, wa, wb, wo, nw, wgu, wd, fn, final):
    n = x.shape[0]
    nch = ROW_TILE // CHUNK
    sample_tile0 = geo.prompt_chunks * CHUNK // ROW_TILE
    row = lambda i: (i, 0)
    return pl.pallas_call(
        functools.partial(_outproj_kernel, geo, final),
        grid=(n // ROW_TILE,),
        in_specs=[pl.BlockSpec((ROW_TILE, D_MODEL), row),
                  pl.BlockSpec((ROW_TILE, 2 * D_MODEL), lambda i: (i, P_G // 2048)),
                  pl.BlockSpec((ROW_TILE, V_W), row),
                  pl.BlockSpec((ROW_TILE, 2 * LRU_W), lambda i: (i, P_X // 2048)),
                  pl.BlockSpec((None, nch, LRU_W), lambda i: (layer, jnp.maximum(i - sample_tile0, 0), 0)),
                  _resident((NB, BW, 2 * BW), layer),
                  _resident((1, LRU_W), layer),
                  _resident((1, LRU_W), layer),
                  _resident((1, LRU_W), layer),
                  _resident((V_W, D_MODEL), layer),
                  _resident((LRU_W, D_MODEL), layer),
                  _resident((D_MODEL, D_MODEL), layer),
                  _resident((1, D_MODEL), layer),
                  _resident((D_MODEL, 2 * D_FF), layer),
                  _resident((D_FF, D_MODEL), layer),
                  pl.BlockSpec((1, D_MODEL), lambda i: (0, 0), pipeline_mode=pl.Buffered(1))],
        out_specs=[pl.BlockSpec((ROW_TILE, D_MODEL), row),
                   pl.BlockSpec((nch, LRU_W), row)],
        out_shape=[jax.ShapeDtypeStruct((n, D_MODEL), F32),
                   jax.ShapeDtypeStruct((n // CHUNK, LRU_W), F32)],
        scratch_shapes=[pltpu.VMEM((ROW_TILE, D_FF), BF16),
                        pltpu.VMEM((8, LRU_W), F32),
                        pltpu.VMEM((ROW_TILE, LRU_W), BF16)],
        compiler_params=_params("arbitrary"),
        name="outproj_ffn",
    )(x, p, oa, p, h0, wri, br, bi, lam, wa, wb, wo, nw, wgu, wd, fn)


def _unit_lower_inverse(lows, c):
    ti = _row_time(lax.broadcasted_iota(jnp.int32, (c, c), 0))
    tj = _row_time(lax.broadcasted_iota(jnp.int32, (c, c), 1))
    same_block = (ti // INV_BASE) == (tj // INV_BASE)
    eye = (ti == tj).astype(F32)
    ld = [jnp.where(same_block, low, 0.0) for low in lows]
    lo = [low - d for low, d in zip(lows, ld)]
    x = [eye - d for d in ld]
    p = ld
    span = 2
    while span < INV_BASE:
        p = [_mm(t, t) for t in p]
        x = [a + _mm(a, t) for a, t in zip(x, p)]
        span *= 2
    e = [_mm(a, b) for a, b in zip(x, lo)]
    y = [eye - t for t in e]
    q = e
    span = 2
    while span < c // INV_BASE:
        q = [_mm(t, t) for t in q]
        y = [a + _mm(a, t) for a, t in zip(y, q)]
        span *= 2
    return [_mm(a, b) for a, b in zip(y, x)]


def _delta_kernel(geo, qkv_ref, z_ref, ba_ref, s0_ref, alog_ref, dtb_ref, onorm_ref,
                  oa_ref, sout_ref, s_scr):
    c = CHUNK
    i = pl.program_id(0)
    is_prompt = i < geo.prompt_steps
    pos = i % geo.cpp
    first = jnp.logical_or(jnp.logical_not(is_prompt), pos == 0)
    last = jnp.logical_or(jnp.logical_not(is_prompt), pos == geo.cpp - 1)
    null = jnp.where(first, jnp.where(is_prompt, PROMPT_NULL, SAMPLE_NULL), 0)
    valid = _valid_rows(null)

    @pl.when(jnp.logical_and(first, is_prompt))
    def _():
        s_scr[...] = jnp.zeros_like(s_scr)

    @pl.when(jnp.logical_not(is_prompt))
    def _():
        s_scr[...] = s0_ref[...]

    ti = _row_time(lax.broadcasted_iota(jnp.int32, (c, c), 0))
    tj = _row_time(lax.broadcasted_iota(jnp.int32, (c, c), 1))
    tri_incl = ti >= tj
    tri_strict = ti > tj

    beta_all, gc_all, gc_t, eg_all, ekd_all, egl_all = [], [], [], [], [], []
    for s in range(GROUP):
        ba = ba_ref[s * c:(s + 1) * c, :]
        beta_all.append(jnp.where(valid, _sigmoid(ba[:, :128]), 0.0))
        g = -jnp.exp(alog_ref[...]) * _softplus(ba[:, 128:] + dtb_ref[...])
        gc, g_last = _scan_time(None, jnp.where(valid, g, 0.0), jnp.zeros((1, 128), F32))
        gc_all.append(gc)
        gc_t.append(jnp.concatenate([gc, jnp.zeros((128 - c, 128), F32)], axis=0).T)
        eg_all.append(jnp.exp(gc))
        ekd_all.append(jnp.exp(g_last - gc))
        egl_all.append(jnp.exp(g_last))

    probs = [(s, h) for s in range(GROUP) for h in range(HA)]
    rows = lambda s: slice(s * c, (s + 1) * c)
    qb = [qkv_ref[rows(s), h * DK:(h + 1) * DK] for s, h in probs]
    kbf = [qkv_ref[rows(s), QK_W + h * DK:QK_W + (h + 1) * DK] for s, h in probs]
    vb = [qkv_ref[rows(s), 2 * QK_W + h * DV:2 * QK_W + (h + 1) * DV] for s, h in probs]
    q = [t.astype(F32) for t in qb]
    k = [t.astype(F32) for t in kbf]
    beta = [beta_all[s][:, h:h + 1] for s, h in probs]
    eg = [eg_all[s][:, h:h + 1] for s, h in probs]
    n = len(probs)
    kb = [k[p] * beta[p] for p in range(n)]
    sc = [_mm_nt(jnp.concatenate([kb[p].astype(BF16), qb[p]], axis=0), kbf[p]) for p in range(n)]
    decay = [jnp.where(tri_incl, jnp.exp(jnp.where(tri_incl, gc_all[s][:, h:h + 1] - gc_t[s][h:h + 1, :c], 0.0)), 0.0)
             for s, h in probs]
    low = [jnp.where(tri_strict, sc[p][:c] * decay[p], 0.0) for p in range(n)]
    a_intra = [sc[p][c:] * decay[p] for p in range(n)]
    t_inv = _unit_lower_inverse(low, c)
    uw = [_mm(t_inv[p], jnp.concatenate([vb[p].astype(F32) * beta[p], kb[p] * eg[p]], axis=1))
          for p in range(n)]
    s_old = [s_scr[s, h] for s, h in probs]
    wq = [_mm(jnp.concatenate([uw[p][:, DV:], q[p] * eg[p]], axis=0), s_old[p]) for p in range(n)]
    v_new = [uw[p][:, :DV] - wq[p][:c] for p in range(n)]
    o_intra = [_mm(a_intra[p], v_new[p]) for p in range(n)]
    ds = [_mm_tn(k[p] * ekd_all[s][:, h:h + 1], v_new[p]) for p, (s, h) in enumerate(probs)]
    for p, (s, h) in enumerate(probs):
        s_scr[s, h] = s_old[p] * egl_all[s][:, h:h + 1] + ds[p]
        o = wq[p][c:] + o_intra[p]
        zh = z_ref[rows(s), h * DV:(h + 1) * DV].astype(F32)
        o = o * lax.rsqrt(jnp.mean(o * o, axis=-1, keepdims=True) + EPS) * onorm_ref[...] * _silu(zh)
        oa_ref[rows(s), h * DV:(h + 1) * DV] = o.astype(oa_ref.dtype)

    @pl.when(last)
    def _():
        sout_ref[...] = s_scr[...]


def _delta_call(geo, layer, p, ba, s0, alog, dtb, onorm):
    n_pp = geo.n_prompt // GROUP

    def pair(i):
        return jnp.where(i < geo.prompt_steps, i // geo.cpp, i - geo.prompt_steps + n_pp)

    vec = lambda width: pl.BlockSpec((None, 1, width), lambda i: (layer, 0, 0))
    return pl.pallas_call(
        functools.partial(_delta_kernel, geo),
        grid=(geo.steps,),
        in_specs=[pl.BlockSpec((GROUP_ROWS, QKV_W), lambda i: (i, 0)),
                  pl.BlockSpec((GROUP_ROWS, V_W), lambda i: (i, P_Z // V_W)),
                  pl.BlockSpec((GROUP_ROWS, 256), lambda i: (i, 0)),
                  pl.BlockSpec((None, GROUP, HA, DK, DV),
                               lambda i: (layer, jnp.maximum(i - geo.prompt_steps, 0), 0, 0, 0)),
                  vec(128), vec(128), vec(DV)],
        out_specs=[pl.BlockSpec((GROUP_ROWS, V_W), lambda i: (i, 0)),
                   pl.BlockSpec((GROUP, HA, DK, DV), lambda i: (pair(i), 0, 0, 0))],
        out_shape=[jax.ShapeDtypeStruct((geo.rows, V_W), BF16),
                   jax.ShapeDtypeStruct((geo.n_streams, HA, DK, DV), F32)],
        scratch_shapes=[pltpu.VMEM((GROUP, HA, DK, DV), F32)],
        compiler_params=_params("arbitrary"),
        name="delta",
    )(p, p, ba, s0, alog, dtb, onorm)


def _permute_chunk(x):
    return jnp.swapaxes(x.reshape(SLAB, N_SLAB, x.shape[-1]), 0, 1).reshape(CHUNK, x.shape[-1])


def _pack_kernel(geo, xp_ref, xs_ref, meta_ref, o_ref):
    i = pl.program_id(0)
    is_prompt = i < geo.prompt_steps
    pos = i % geo.cpp
    width = o_ref.shape[1]

    @pl.when(jnp.logical_and(is_prompt, pos > 0))
    def _():
        for s in range(GROUP):
            o_ref[s * CHUNK:(s + 1) * CHUNK, :] = _permute_chunk(xp_ref[s].astype(o_ref.dtype))

    @pl.when(jnp.logical_and(is_prompt, pos == 0))
    def _():
        first = _permute_chunk(jnp.concatenate(
            [jnp.zeros((PROMPT_NULL, width), o_ref.dtype), meta_ref[...].astype(o_ref.dtype)], axis=0))
        for s in range(GROUP):
            o_ref[s * CHUNK:(s + 1) * CHUNK, :] = first

    @pl.when(jnp.logical_not(is_prompt))
    def _():
        for s in range(GROUP):
            o_ref[s * CHUNK:(s + 1) * CHUNK, :] = _permute_chunk(jnp.concatenate(
                [jnp.zeros((SAMPLE_NULL, width), o_ref.dtype), xs_ref[s].astype(o_ref.dtype)], axis=0))


def _unpack_kernel(geo, x_ref, yp_ref, ys_ref):
    i = pl.program_id(0)
    is_prompt = i < geo.prompt_steps
    pos = i % geo.cpp

    @pl.when(jnp.logical_and(is_prompt, pos > 0))
    def _():
        for s in range(GROUP):
            yp_ref[s] = _permute_chunk(x_ref[s * CHUNK:(s + 1) * CHUNK, :]).astype(yp_ref.dtype)

    @pl.when(jnp.logical_not(is_prompt))
    def _():
        for s in range(GROUP):
            ys_ref[s] = _permute_chunk(x_ref[s * CHUNK:(s + 1) * CHUNK, :])[SAMPLE_NULL:].astype(ys_ref.dtype)


def _layout_specs(geo):
    def prompt_idx(i):
        step = jnp.minimum(i, geo.prompt_steps - 1)
        return (step // geo.cpp, jnp.maximum(step % geo.cpp - 1, 0), 0)

    def sample_idx(i):
        return (jnp.maximum(i - geo.prompt_steps, 0), 0, 0)

    return (pl.BlockSpec((GROUP, CHUNK, D_MODEL), prompt_idx),
            pl.BlockSpec((GROUP, CHUNK - SAMPLE_NULL, D_MODEL), sample_idx))


def _pack_call(geo, x_prompt, x_sample, meta):
    prompt_spec, sample_spec = _layout_specs(geo)
    return pl.pallas_call(
        functools.partial(_pack_kernel, geo),
        grid=(geo.steps,),
        in_specs=[prompt_spec, sample_spec, pl.BlockSpec((N_META, D_MODEL), lambda i: (0, 0))],
        out_specs=pl.BlockSpec((GROUP_ROWS, D_MODEL), lambda i: (i, 0)),
        out_shape=jax.ShapeDtypeStruct((geo.rows, D_MODEL), F32),
        compiler_params=_params("parallel"),
        name="pack",
    )(x_prompt, x_sample, meta)


def _unpack_call(geo, x, dtype):
    prompt_spec, sample_spec = _layout_specs(geo)
    return pl.pallas_call(
        functools.partial(_unpack_kernel, geo),
        grid=(geo.steps,),
        in_specs=[pl.BlockSpec((GROUP_ROWS, D_MODEL), lambda i: (i, 0))],
        out_specs=[prompt_spec, sample_spec],
        out_shape=[jax.ShapeDtypeStruct((geo.n_prompt, (geo.cpp - 1) * CHUNK, D_MODEL), dtype),
                   jax.ShapeDtypeStruct((geo.n_sample, CHUNK - SAMPLE_NULL, D_MODEL), dtype)],
        compiler_params=_params("arbitrary"),
        name="unpack",
    )(x)


def _pad_lanes(v, width):
    return jnp.pad(v, ((0, 0), (0, width - v.shape[-1])))


def kernel(x_prompt, x_sample, state_delta_S, state_delta_conv, state_lru_h, state_lru_conv,
           meta_tokens, ffn1_norm, ffn1_w_gu, ffn1_w_down, mix_norm, w_in, delta_conv_w,
           delta_A_log, delta_dt_bias, delta_out_norm, lru_conv_w, lru_conv_b, lru_w_r, lru_b_r,
           lru_w_i, lru_b_i, lru_lambda, w_branch_a, w_branch_b, w_out, ffn2_norm, ffn2_w_gu,
           ffn2_w_down, final_norm):
    n_prompt, seq_len, _ = x_prompt.shape
    n_sample, dec_len, _ = x_sample.shape
    assert dec_len == CHUNK - SAMPLE_NULL and (N_META + seq_len) % CHUNK == N_META
    prompt_rows = PROMPT_NULL + N_META + seq_len
    cpp = prompt_rows // CHUNK
    geo = _Geometry(n_prompt, n_sample, cpp)
    assert geo.rows % ROW_TILE == 0 and (geo.prompt_chunks * CHUNK) % ROW_TILE == 0
    assert INPROJ_TILE % GROUP_ROWS == 0 and ROW_TILE % GROUP_ROWS == 0
    dt = x_prompt.dtype

    x = _pack_call(geo, x_prompt, x_sample, meta_tokens)

    cast = lambda w: w.astype(BF16)
    vec = lambda v: v.astype(F32)[:, None, :]
    wgu1, wd1, wgu2, wd2 = cast(ffn1_w_gu), cast(ffn1_w_down), cast(ffn2_w_gu), cast(ffn2_w_down)
    w_head, w_rest = cast(w_in[:, :, :OFF_BETA]), cast(w_in[:, :, OFF_LX:])
    pad_heads = lambda w: jnp.pad(w, ((0, 0), (0, 0), (0, 128 - HA)))
    w_ba = cast(jnp.concatenate([pad_heads(w_in[:, :, OFF_BETA:OFF_ALPHA]),
                                 pad_heads(w_in[:, :, OFF_ALPHA:OFF_LX])], axis=-1))
    w_ri = cast(jnp.concatenate([lru_w_r, lru_w_i], axis=-1))
    wa, wb, wo = cast(w_branch_a), cast(w_branch_b), cast(w_out)
    norm1, norm_mix, norm2 = vec(ffn1_norm), vec(mix_norm), vec(ffn2_norm)
    alog, dtb, onorm = vec(_pad_lanes(delta_A_log, 128)), vec(_pad_lanes(delta_dt_bias, 128)), vec(delta_out_norm)
    lcb, b_r, b_i, lam = vec(lru_conv_b), vec(lru_b_r), vec(lru_b_i), vec(lru_lambda)
    cw, lcw = delta_conv_w.astype(F32), lru_conv_w.astype(F32)
    s0, h0 = state_delta_S.astype(F32), state_lru_h.astype(F32)

    inj = jnp.concatenate([state_delta_conv, state_lru_conv], axis=-1).astype(F32)
    null_runs = SAMPLE_NULL // N_SLAB
    inj = jnp.pad(inj[:, :, :, None, :], ((0, 0), (0, 0), (0, 0), (null_runs - 1, SLAB - null_runs), (0, 0)))
    inj = inj.reshape(DEPTH, n_sample * TAIL_ROWS, CONV_COLS)

    last_ids = geo.last_chunk_ids()
    outs_s, outs_tail, outs_h = [], [], []
    for l in range(DEPTH):
        x = _ffn_call(l, x, norm1, wgu1, wd1)
        p, ba, tails = _inproj_call(geo, l, x, norm_mix, w_head, w_rest, w_ba, cw, lcw, lcb, inj)
        oa, s_new = _delta_call(geo, l, p, ba, s0, alog, dtb, onorm)
        x, hlast = _outproj_call(geo, l, x, p, oa, h0, w_ri, b_r, b_i, lam, wa, wb, wo, norm2, wgu2, wd2,
                                 final_norm[None].astype(F32), final=(l == DEPTH - 1))
        outs_s.append(s_new)
        outs_tail.append(tails[last_ids, SLAB - 1::SLAB, :])
        outs_h.append(hlast[last_ids])

    n_prompt_rows = geo.prompt_chunks * CHUNK
    y_prompt, y_sample = _unpack_call(geo, x, dt)
    s_all = jnp.stack(outs_s)
    tail_all = jnp.stack(outs_tail)
    cq_all = tail_all[..., :QKV_W]
    cx_all = tail_all[..., QKV_W:]
    h_all = jnp.stack(outs_h)
    return (y_prompt.astype(dt), y_sample.astype(dt),
            s_all[:, :n_prompt].astype(dt), cq_all[:, :n_prompt].astype(dt),
            h_all[:, :n_prompt].astype(dt), cx_all[:, :n_prompt].astype(dt),
            s_all[:, n_prompt:].astype(state_delta_S.dtype),
            cq_all[:, n_prompt:].astype(state_delta_conv.dtype),
            h_all[:, n_prompt:].astype(state_lru_h.dtype),
            cx_all[:, n_prompt:].astype(state_lru_conv.dtype))
```

```python
import functools

import numpy as np
import jax
import jax.numpy as jnp
from jax import lax
from jax.experimental import pallas as pl
from jax.experimental.pallas import tpu as pltpu

F32 = jnp.float32
BF16 = jnp.bfloat16

D_MODEL = 1024
DEPTH = 4
N_META = 16
HA = 8
DK = 128
DV = 128
QK_W = HA * DK
V_W = HA * DV
QKV_W = 2 * QK_W + V_W
LRU_W = D_MODEL
NB = 8
BW = 128
CONV_W = 4
LRU_C = 8.0
D_FF = 2816
EPS = 1e-6

OFF_Z = QKV_W
OFF_BETA = OFF_Z + V_W
OFF_ALPHA = OFF_BETA + HA
OFF_LX = OFF_ALPHA + HA
OFF_LY = OFF_LX + LRU_W
OFF_GA = OFF_LY + LRU_W
OFF_GB = OFF_GA + D_MODEL
IN_COLS = OFF_GB + D_MODEL

CHUNK = 64
GROUP = 4
GROUP_ROWS = GROUP * CHUNK
PROMPT_NULL = CHUNK - N_META
SAMPLE_NULL = CHUNK // 2
INV_BASE = 16
ROW_TILE = 512
INPROJ_TILE = 256
FF_HALF = D_FF // 2
CONV_COLS = QKV_W + LRU_W
SLAB = 8
N_SLAB = CHUNK // SLAB
TAIL_ROWS = (CONV_W - 1) * SLAB
P_Z, P_X, P_Y, P_G = 3072, 4096, 5120, 6144
P_COLS = 8192
VMEM_LIMIT = 56 * 1024 * 1024


def _rms(x, w):
    return x * lax.rsqrt(jnp.mean(x * x, axis=-1, keepdims=True) + EPS) * w


def _mm(a, b):
    return jnp.dot(a.astype(BF16), b.astype(BF16), preferred_element_type=F32)


def _mm_nt(a, b):
    return lax.dot_general(a.astype(BF16), b.astype(BF16), (((1,), (1,)), ((), ())),
                           preferred_element_type=F32)


def _mm_tn(a, b):
    return lax.dot_general(a.astype(BF16), b.astype(BF16), (((0,), (0,)), ((), ())),
                           preferred_element_type=F32)


def _sigmoid(x):
    return 0.5 * (1.0 + jnp.tanh(0.5 * x))


def _silu(x):
    return x * _sigmoid(x)


def _softplus(x):
    return jnp.maximum(x, 0.0) + jnp.log1p(jnp.exp(-jnp.abs(x)))


def _gelu_tanh(x):
    return 0.5 * x * (1.0 + jnp.tanh(0.7978845608028654 * (x + 0.044715 * (x * x * x))))


class _Geometry:
    def __init__(self, n_prompt, n_sample, chunks_per_prompt):
        assert n_prompt % GROUP == 0 and n_sample % GROUP == 0
        self.n_prompt = n_prompt
        self.n_sample = n_sample
        self.cpp = chunks_per_prompt
        self.prompt_steps = (n_prompt // GROUP) * chunks_per_prompt
        self.steps = self.prompt_steps + n_sample // GROUP
        self.prompt_chunks = GROUP * self.prompt_steps
        self.chunks = GROUP * self.steps
        self.rows = self.chunks * CHUNK
        self.n_streams = n_prompt + n_sample

    def chunk_null(self, cid):
        is_prompt = cid < self.prompt_chunks
        first = jnp.logical_or(jnp.logical_not(is_prompt), (cid // GROUP) % self.cpp == 0)
        return jnp.where(first, jnp.where(is_prompt, PROMPT_NULL, SAMPLE_NULL), 0)

    def last_chunk_ids(self):
        ids = [GROUP * ((b // GROUP) * self.cpp + self.cpp - 1) + b % GROUP for b in range(self.n_prompt)]
        ids += [self.prompt_chunks + t for t in range(self.n_sample)]
        return np.asarray(ids, np.int32)


def _row_time(rows):
    return N_SLAB * (rows % SLAB) + rows // SLAB


def _valid_rows(null):
    return _row_time(lax.broadcasted_iota(jnp.int32, (CHUNK, 1), 0)) >= null


def _slabs(x):
    return [x[v * SLAB:(v + 1) * SLAB] for v in range(N_SLAB)]


def _shift_run(x, fill):
    return jnp.concatenate([fill, x[:SLAB - 1]], axis=0)


def _scan_time(a, b, h_in):
    bs = _slabs(b)
    if a is None:
        for v in range(1, N_SLAB):
            bs[v] = bs[v] + bs[v - 1]
        run = bs[N_SLAB - 1]
        d = 1
        while d < SLAB:
            run = run + _shift_rows(run, d, 0.0)
            d *= 2
        run = run + h_in
        prev = _shift_run(run, h_in)
        hs = [t + prev for t in bs]
    else:
        as_ = _slabs(a)
        for v in range(1, N_SLAB):
            bs[v] = bs[v] + as_[v] * bs[v - 1]
            as_[v] = as_[v] * as_[v - 1]
        ra, rb = as_[N_SLAB - 1], bs[N_SLAB - 1]
        d = 1
        while d < SLAB:
            rb = rb + ra * _shift_rows(rb, d, 0.0)
            ra = ra * _shift_rows(ra, d, 1.0)
            d *= 2
        run = rb + ra * h_in
        prev = _shift_run(run, h_in)
        hs = [t + u * prev for t, u in zip(bs, as_)]
    return jnp.concatenate(hs, axis=0), run[SLAB - 1:SLAB, :]


def _ffn(x, nw, wgu_ref, wd_ref, h_ref):
    xn = _rms(x, nw).astype(BF16)
    for c in range(2):
        lo = c * FF_HALF
        g = jnp.dot(xn, wgu_ref[:, lo:lo + FF_HALF], preferred_element_type=F32)
        u = jnp.dot(xn, wgu_ref[:, D_FF + lo:D_FF + lo + FF_HALF], preferred_element_type=F32)
        h_ref[:, lo:lo + FF_HALF] = (_silu(g) * u).astype(BF16)
    y = jnp.dot(h_ref[...], wd_ref[...], preferred_element_type=F32)
    return x + 0.5 * y


def _ffn_kernel(x_ref, nw_ref, wgu_ref, wd_ref, o_ref, h_ref):
    o_ref[...] = _ffn(x_ref[...], nw_ref[...], wgu_ref, wd_ref, h_ref)


def _inproj_kernel(geo, x_ref, nw_ref, wa_ref, wb_ref, wba_ref, cw_ref, lcw_ref, lcb_ref, inj_ref,
                   p_ref, ba_ref, tails_ref, stage_all, carry_scr):
    i = pl.program_id(0)
    nch = INPROJ_TILE // CHUNK
    xn = _rms(x_ref[...], nw_ref[...]).astype(BF16)

    def project_cols(lo):
        w_ref, start = (wa_ref, lo) if lo < P_COLS // 2 else (wb_ref, lo - P_COLS // 2)
        return jnp.dot(xn, w_ref[:, start:start + 1024], preferred_element_type=F32)

    @pl.when(i == 0)
    def _():
        carry_scr[...] = jnp.zeros_like(carry_scr)

    valid = [_valid_rows(geo.chunk_null(i * nch + j)) for j in range(nch)]
    in_sample = i >= geo.prompt_chunks * CHUNK // INPROJ_TILE
    kinds = (("q", 0, 0), ("k", QK_W, QK_W), ("v", 2 * QK_W, 2 * QK_W), ("x", P_X, QKV_W))
    t0 = TAIL_ROWS

    def project(kidx):
        kind, lo, col = kinds[kidx]
        stage = stage_all.at[kidx]
        pre = project_cols(lo)
        for j in range(nch):
            stage[j, t0:t0 + CHUNK, :] = pre[j * CHUNK:(j + 1) * CHUNK]
            if j >= GROUP:
                stage[j, 0:t0, :] = pre[(j - GROUP + 1) * CHUNK - t0:(j - GROUP + 1) * CHUNK]
            else:
                stage[j, 0:t0, :] = carry_scr[j, :, col:col + 1024]
            stage[j, CHUNK:CHUNK + t0, :] = stage[j, CHUNK:CHUNK + t0, :] + jnp.where(
                in_sample, inj_ref[t0 * j:t0 * (j + 1), col:col + 1024], 0.0)
            tails_ref[j, :, col:col + 1024] = stage[j, CHUNK:CHUNK + t0, :]
        for g in range(GROUP):
            carry_scr[g, :, col:col + 1024] = stage[nch - GROUP + g, CHUNK:CHUNK + t0, :]

    def convolve(kidx, j):
        kind, lo, col = kinds[kidx]
        stage = stage_all.at[kidx]
        taps = lcw_ref[...] if kind == "x" else cw_ref[:, lo:lo + 1024]
        cur = [stage[j, t0 + v * SLAB:t0 + (v + 1) * SLAB, :] for v in range(N_SLAB)]
        back = []
        for k in range(CONV_W - 1):
            prev_last = stage[j, (k + 1) * SLAB - 1:(k + 1) * SLAB, :]
            back.append(_shift_run(cur[N_SLAB - (CONV_W - 1) + k], prev_last))
        conv = []
        for v in range(N_SLAB):
            acc = cur[v] * taps[CONV_W - 1:CONV_W, :]
            for d in range(1, CONV_W):
                src = cur[v - d] if v >= d else back[CONV_W - 1 - d + v]
                acc = acc + src * taps[CONV_W - 1 - d:CONV_W - d, :]
            conv.append(acc)
        conv = jnp.concatenate(conv, axis=0)
        if kind == "x":
            out = conv + lcb_ref[...]
        else:
            act = _silu(conv)
            if kind != "v":
                scale = DK ** -0.5 if kind == "q" else 1.0
                segs = []
                for h in range(HA):
                    seg = act[:, h * DK:(h + 1) * DK]
                    segs.append(seg * (lax.rsqrt(jnp.sum(seg * seg, axis=-1, keepdims=True) + EPS) * scale))
                act = jnp.concatenate(segs, axis=1)
            out = jnp.where(valid[j], act, 0.0)
        p_ref[j * CHUNK:(j + 1) * CHUNK, lo:lo + 1024] = out.astype(BF16)

    def plain(lo):
        p_ref[:, lo:lo + 1024] = project_cols(lo).astype(BF16)

    plain_cols = (P_Z, P_Y, P_G, P_G + 1024)
    project(0)
    for kidx in range(len(kinds)):
        plain(plain_cols[kidx])
        if kidx + 1 < len(kinds):
            project(kidx + 1)
        for j in range(nch):
            convolve(kidx, j)
    ba_ref[...] = jnp.dot(xn, wba_ref[...], preferred_element_type=F32)


def _shift_rows(x, d, fill):
    if d % 8 == 0:
        return jnp.concatenate([jnp.full((d, x.shape[1]), fill, x.dtype), x[:x.shape[0] - d]], axis=0)
    rows = lax.broadcasted_iota(jnp.int32, x.shape, 0)
    return jnp.where(rows >= d, pltpu.roll(x, d, 0), fill)


def _outproj_kernel(geo, final, x_ref, g_ref, oa_ref, xl_ref, h0_ref,
                    wri_ref, br_ref, bi_ref, lam_ref, wa_ref, wb_ref, wo_ref,
                    nw_ref, wgu_ref, wd_ref, fn_ref,
                    o_ref, hlast_ref, h_ref, hc_scr, ob_scr):
    i = pl.program_id(0)
    nch = ROW_TILE // CHUNK

    @pl.when(i == 0)
    def _():
        hc_scr[...] = jnp.zeros_like(hc_scr)

    sp = _softplus(-lam_ref[...])
    nulls = [geo.chunk_null(i * nch + j) for j in range(nch)]
    valid = [_valid_rows(n) for n in nulls]
    is_sample = [(i * nch + j) >= geo.prompt_chunks for j in range(nch)]
    for n in range(NB):
        sl = slice(n * BW, (n + 1) * BW)
        xb = xl_ref[:, sl]
        gates = jnp.dot(xb, wri_ref[n], preferred_element_type=F32)
        r = _sigmoid(gates[:, :BW] + br_ref[:, sl])
        ig = _sigmoid(gates[:, BW:] + bi_ref[:, sl])
        log_a = -LRU_C * r * sp[:, sl]
        a_all = jnp.exp(log_a)
        th = jnp.tanh(log_a)
        mult = jnp.sqrt(-2.0 * th / (1.0 - th))
        b_all = mult * (ig * xb.astype(F32))
        gy = _gelu_tanh(xl_ref[:, LRU_W + n * BW:LRU_W + (n + 1) * BW].astype(F32))
        carry = [hc_scr[g:g + 1, sl] for g in range(GROUP)]
        for j in range(nch):
            rows = slice(j * CHUNK, (j + 1) * CHUNK)
            a = jnp.where(valid[j], a_all[rows], 1.0)
            b = jnp.where(valid[j], b_all[rows], 0.0)
            h_in = jnp.where(nulls[j] > 0, jnp.where(is_sample[j], h0_ref[j:j + 1, sl], 0.0), carry[j % GROUP])
            hs, carry[j % GROUP] = _scan_time(a, b, h_in)
            ob_scr[rows, sl] = (hs * gy[rows]).astype(BF16)
            hlast_ref[j:j + 1, sl] = carry[j % GROUP]
        for g in range(GROUP):
            hc_scr[g:g + 1, sl] = carry[g]

    ga = _sigmoid(g_ref[:, :D_MODEL].astype(F32))
    gb = _sigmoid(g_ref[:, D_MODEL:].astype(F32))
    ma = jnp.dot(oa_ref[...], wa_ref[...], preferred_element_type=F32)
    mb = jnp.dot(ob_scr[...], wb_ref[...], preferred_element_type=F32)
    m = (ga * ma + gb * mb).astype(BF16)
    x = x_ref[...] + jnp.dot(m, wo_ref[...], preferred_element_type=F32)
    x = _ffn(x, nw_ref[...], wgu_ref, wd_ref, h_ref)
    if final:
        x = _rms(x, fn_ref[...])
    o_ref[...] = x


def _resident(shape, layer):
    nd = len(shape)
    return pl.BlockSpec((None,) + shape, lambda i: (layer,) + (0,) * nd, pipeline_mode=pl.Buffered(1))


def _params(semantics):
    return pltpu.CompilerParams(dimension_semantics=(semantics,), vmem_limit_bytes=VMEM_LIMIT)


def _ffn_call(layer, x, nw, wgu, wd):
    n = x.shape[0]
    row = lambda i: (i, 0)
    return pl.pallas_call(
        _ffn_kernel,
        grid=(n // ROW_TILE,),
        in_specs=[pl.BlockSpec((ROW_TILE, D_MODEL), row),
                  _resident((1, D_MODEL), layer),
                  _resident((D_MODEL, 2 * D_FF), layer),
                  _resident((D_FF, D_MODEL), layer)],
        out_specs=pl.BlockSpec((ROW_TILE, D_MODEL), row),
        out_shape=jax.ShapeDtypeStruct((n, D_MODEL), F32),
        scratch_shapes=[pltpu.VMEM((ROW_TILE, D_FF), BF16)],
        compiler_params=_params("parallel"),
        name="ffn",
    )(x, nw, wgu, wd)


def _inproj_call(geo, layer, x, nw, wa, wb, wba, cw, lcw, lcb, inj):
    n = x.shape[0]
    nch = INPROJ_TILE // CHUNK
    sample_tile0 = geo.prompt_chunks * CHUNK // INPROJ_TILE
    row = lambda i: (i, 0)
    return pl.pallas_call(
        functools.partial(_inproj_kernel, geo),
        grid=(n // INPROJ_TILE,),
        in_specs=[pl.BlockSpec((INPROJ_TILE, D_MODEL), row),
                  _resident((1, D_MODEL), layer),
                  _resident((D_MODEL, P_COLS // 2), layer),
                  _resident((D_MODEL, P_COLS // 2), layer),
                  _resident((D_MODEL, 256), layer),
                  _resident((CONV_W, QKV_W), layer),
                  _resident((CONV_W, LRU_W), layer),
                  _resident((1, LRU_W), layer),
                  pl.BlockSpec((None, TAIL_ROWS * nch, CONV_COLS),
                               lambda i: (layer, jnp.maximum(i - sample_tile0, 0), 0))],
        out_specs=[pl.BlockSpec((INPROJ_TILE, P_COLS), row),
                   pl.BlockSpec((INPROJ_TILE, 256), row),
                   pl.BlockSpec((nch, TAIL_ROWS, CONV_COLS), lambda i: (i, 0, 0))],
        out_shape=[jax.ShapeDtypeStruct((n, P_COLS), BF16),
                   jax.ShapeDtypeStruct((n, 256), F32),
                   jax.ShapeDtypeStruct((n // CHUNK, TAIL_ROWS, CONV_COLS), F32)],
        scratch_shapes=[pltpu.VMEM((4, nch, CHUNK + TAIL_ROWS, 1024), F32),
                        pltpu.VMEM((GROUP, TAIL_ROWS, CONV_COLS), F32)],
        compiler_params=_params("arbitrary"),
        name="inproj",
    )(x, nw, wa, wb, wba, cw, lcw, lcb, inj)


def _outproj_call(geo, layer, x, p, oa, h0, wri, br, bi, lam, wa, wb, wo, nw, wgu, wd, fn, final):
    n = x.shape[0]
    nch = ROW_TILE // CHUNK
    sample_tile0 = geo.prompt_chunks * CHUNK // ROW_TILE
    row = lambda i: (i, 0)
    return pl.pallas_call(
        functools.partial(_outproj_kernel, geo, final),
        grid=(n // ROW_TILE,),
        in_specs=[pl.BlockSpec((ROW_TILE, D_MODEL), row),
                  pl.BlockSpec((ROW_TILE, 2 * D_MODEL), lambda i: (i, P_G // 2048)),
                  pl.BlockSpec((ROW_TILE, V_W), row),
                  pl.BlockSpec((ROW_TILE, 2 * LRU_W), lambda i: (i, P_X // 2048)),
                  pl.BlockSpec((None, nch, LRU_W), lambda i: (layer, jnp.maximum(i - sample_tile0, 0), 0)),
                  _resident((NB, BW, 2 * BW), layer),
                  _resident((1, LRU_W), layer),
                  _resident((1, LRU_W), layer),
                  _resident((1, LRU_W), layer),
                  _resident((V_W, D_MODEL), layer),
                  _resident((LRU_W, D_MODEL), layer),
                  _resident((D_MODEL, D_MODEL), layer),
                  _resident((1, D_MODEL), layer),
                  _resident((D_MODEL, 2 * D_FF), layer),
                  _resident((D_FF, D_MODEL), layer),
                  pl.BlockSpec((1, D_MODEL), lambda i: (0, 0), pipeline_mode=pl.Buffered(1))],
        out_specs=[pl.BlockSpec((ROW_TILE, D_MODEL), row),
                   pl.BlockSpec((nch, LRU_W), row)],
        out_shape=[jax.ShapeDtypeStruct((n, D_MODEL), F32),
                   jax.ShapeDtypeStruct((n // CHUNK, LRU_W), F32)],
        scratch_shapes=[pltpu.VMEM((ROW_TILE, D_FF), BF16),
                        pltpu.VMEM((8, LRU_W), F32),
                        pltpu.VMEM((ROW_TILE, LRU_W), BF16)],
        compiler_params=_params("arbitrary"),
        name="outproj_ffn",
    )(x, p, oa, p, h0, wri, br, bi, lam, wa, wb, wo, nw, wgu, wd, fn)


def _unit_lower_inverse(lows, c):
    ti = _row_time(lax.broadcasted_iota(jnp.int32, (c, c), 0))
    tj = _row_time(lax.broadcasted_iota(jnp.int32, (c, c), 1))
    same_block = (ti // INV_BASE) == (tj // INV_BASE)
    eye = (ti == tj).astype(F32)
    ld = [jnp.where(same_block, low, 0.0) for low in lows]
    lo = [low - d for low, d in zip(lows, ld)]
    x = [eye - d for d in ld]
    p = ld
    span = 2
    while span < INV_BASE:
        p = [_mm(t, t) for t in p]
        x = [a + _mm(a, t) for a, t in zip(x, p)]
        span *= 2
    e = [_mm(a, b) for a, b in zip(x, lo)]
    y = [eye - t for t in e]
    q = e
    span = 2
    while span < c // INV_BASE:
        q = [_mm(t, t) for t in q]
        y = [a + _mm(a, t) for a, t in zip(y, q)]
        span *= 2
    return [_mm(a, b) for a, b in zip(y, x)]


def _delta_kernel(geo, qkv_ref, z_ref, ba_ref, s0_ref, alog_ref, dtb_ref, onorm_ref,
                  oa_ref, sout_ref, s_scr):
    c = CHUNK
    i = pl.program_id(0)
    is_prompt = i < geo.prompt_steps
    pos = i % geo.cpp
    first = jnp.logical_or(jnp.logical_not(is_prompt), pos == 0)
    last = jnp.logical_or(jnp.logical_not(is_prompt), pos == geo.cpp - 1)
    null = jnp.where(first, jnp.where(is_prompt, PROMPT_NULL, SAMPLE_NULL), 0)
    valid = _valid_rows(null)

    @pl.when(jnp.logical_and(first, is_prompt))
    def _():
        s_scr[...] = jnp.zeros_like(s_scr)

    @pl.when(jnp.logical_not(is_prompt))
    def _():
        s_scr[...] = s0_ref[...]

    ti = _row_time(lax.broadcasted_iota(jnp.int32, (c, c), 0))
    tj = _row_time(lax.broadcasted_iota(jnp.int32, (c, c), 1))
    tri_incl = ti >= tj
    tri_strict = ti > tj

    beta_all, gc_all, gc_t, eg_all, ekd_all, egl_all = [], [], [], [], [], []
    for s in range(GROUP):
        ba = ba_ref[s * c:(s + 1) * c, :]
        beta_all.append(jnp.where(valid, _sigmoid(ba[:, :128]), 0.0))
        g = -jnp.exp(alog_ref[...]) * _softplus(ba[:, 128:] + dtb_ref[...])
        gc, g_last = _scan_time(None, jnp.where(valid, g, 0.0), jnp.zeros((1, 128), F32))
        gc_all.append(gc)
        gc_t.append(jnp.concatenate([gc, jnp.zeros((128 - c, 128), F32)], axis=0).T)
        eg_all.append(jnp.exp(gc))
        ekd_all.append(jnp.exp(g_last - gc))
        egl_all.append(jnp.exp(g_last))

    probs = [(s, h) for s in range(GROUP) for h in range(HA)]
    rows = lambda s: slice(s * c, (s + 1) * c)
    qb = [qkv_ref[rows(s), h * DK:(h + 1) * DK] for s, h in probs]
    kbf = [qkv_ref[rows(s), QK_W + h * DK:QK_W + (h + 1) * DK] for s, h in probs]
    vb = [qkv_ref[rows(s), 2 * QK_W + h * DV:2 * QK_W + (h + 1) * DV] for s, h in probs]
    q = [t.astype(F32) for t in qb]
    k = [t.astype(F32) for t in kbf]
    beta = [beta_all[s][:, h:h + 1] for s, h in probs]
    eg = [eg_all[s][:, h:h + 1] for s, h in probs]
    n = len(probs)
    kb = [k[p] * beta[p] for p in range(n)]
    sc = [_mm_nt(jnp.concatenate([kb[p].astype(BF16), qb[p]], axis=0), kbf[p]) for p in range(n)]
    decay = [jnp.where(tri_incl, jnp.exp(jnp.where(tri_incl, gc_all[s][:, h:h + 1] - gc_t[s][h:h + 1, :c], 0.0)), 0.0)
             for s, h in probs]
    low = [jnp.where(tri_strict, sc[p][:c] * decay[p], 0.0) for p in range(n)]
    a_intra = [sc[p][c:] * decay[p] for p in range(n)]
    t_inv = _unit_lower_inverse(low, c)
    uw = [_mm(t_inv[p], jnp.concatenate([vb[p].astype(F32) * beta[p], kb[p] * eg[p]], axis=1))
          for p in range(n)]
    s_old = [s_scr[s, h] for s, h in probs]
    wq = [_mm(jnp.concatenate([uw[p][:, DV:], q[p] * eg[p]], axis=0), s_old[p]) for p in range(n)]
    v_new = [uw[p][:, :DV] - wq[p][:c] for p in range(n)]
    o_intra = [_mm(a_intra[p], v_new[p]) for p in range(n)]
    ds = [_mm_tn(k[p] * ekd_all[s][:, h:h + 1], v_new[p]) for p, (s, h) in enumerate(probs)]
    for p, (s, h) in enumerate(probs):
        s_scr[s, h] = s_old[p] * egl_all[s][:, h:h + 1] + ds[p]
        o = wq[p][c:] + o_intra[p]
        zh = z_ref[rows(s), h * DV:(h + 1) * DV].astype(F32)
        o = o * lax.rsqrt(jnp.mean(o * o, axis=-1, keepdims=True) + EPS) * onorm_ref[...] * _silu(zh)
        oa_ref[rows(s), h * DV:(h + 1) * DV] = o.astype(oa_ref.dtype)

    @pl.when(last)
    def _():
        sout_ref[...] = s_scr[...]


def _delta_call(geo, layer, p, ba, s0, alog, dtb, onorm):
    n_pp = geo.n_prompt // GROUP

    def pair(i):
        return jnp.where(i < geo.prompt_steps, i // geo.cpp, i - geo.prompt_steps + n_pp)

    vec = lambda width: pl.BlockSpec((None, 1, width), lambda i: (layer, 0, 0))
    return pl.pallas_call(
        functools.partial(_delta_kernel, geo),
        grid=(geo.steps,),
        in_specs=[pl.BlockSpec((GROUP_ROWS, QKV_W), lambda i: (i, 0)),
                  pl.BlockSpec((GROUP_ROWS, V_W), lambda i: (i, P_Z // V_W)),
                  pl.BlockSpec((GROUP_ROWS, 256), lambda i: (i, 0)),
                  pl.BlockSpec((None, GROUP, HA, DK, DV),
                               lambda i: (layer, jnp.maximum(i - geo.prompt_steps, 0), 0, 0, 0)),
                  vec(128), vec(128), vec(DV)],
        out_specs=[pl.BlockSpec((GROUP_ROWS, V_W), lambda i: (i, 0)),
                   pl.BlockSpec((GROUP, HA, DK, DV), lambda i: (pair(i), 0, 0, 0))],
        out_shape=[jax.ShapeDtypeStruct((geo.rows, V_W), BF16),
                   jax.ShapeDtypeStruct((geo.n_streams, HA, DK, DV), F32)],
        scratch_shapes=[pltpu.VMEM((GROUP, HA, DK, DV), F32)],
        compiler_params=_params("arbitrary"),
        name="delta",
    )(p, p, ba, s0, alog, dtb, onorm)


def _permute_chunk(x):
    return jnp.swapaxes(x.reshape(SLAB, N_SLAB, x.shape[-1]), 0, 1).reshape(CHUNK, x.shape[-1])


def _pack_kernel(geo, xp_ref, xs_ref, meta_ref, o_ref):
    i = pl.program_id(0)
    is_prompt = i < geo.prompt_steps
    pos = i % geo.cpp
    width = o_ref.shape[1]

    @pl.when(jnp.logical_and(is_prompt, pos > 0))
    def _():
        for s in range(GROUP):
            o_ref[s * CHUNK:(s + 1) * CHUNK, :] = _permute_chunk(xp_ref[s].astype(o_ref.dtype))

    @pl.when(jnp.logical_and(is_prompt, pos == 0))
    def _():
        first = _permute_chunk(jnp.concatenate(
            [jnp.zeros((PROMPT_NULL, width), o_ref.dtype), meta_ref[...].astype(o_ref.dtype)], axis=0))
        for s in range(GROUP):
            o_ref[s * CHUNK:(s + 1) * CHUNK, :] = first

    @pl.when(jnp.logical_not(is_prompt))
    def _():
        for s in range(GROUP):
            o_ref[s * CHUNK:(s + 1) * CHUNK, :] = _permute_chunk(jnp.concatenate(
                [jnp.zeros((SAMPLE_NULL, width), o_ref.dtype), xs_ref[s].astype(o_ref.dtype)], axis=0))


def _unpack_kernel(geo, x_ref, yp_ref, ys_ref):
    i = pl.program_id(0)
    is_prompt = i < geo.prompt_steps
    pos = i % geo.cpp

    @pl.when(jnp.logical_and(is_prompt, pos > 0))
    def _():
        for s in range(GROUP):
            yp_ref[s] = _permute_chunk(x_ref[s * CHUNK:(s + 1) * CHUNK, :]).astype(yp_ref.dtype)

    @pl.when(jnp.logical_not(is_prompt))
    def _():
        for s in range(GROUP):
            ys_ref[s] = _permute_chunk(x_ref[s * CHUNK:(s + 1) * CHUNK, :])[SAMPLE_NULL:].astype(ys_ref.dtype)


def _layout_specs(geo):
    def prompt_idx(i):
        step = jnp.minimum(i, geo.prompt_steps - 1)
        return (step // geo.cpp, jnp.maximum(step % geo.cpp - 1, 0), 0)

    def sample_idx(i):
        return (jnp.maximum(i - geo.prompt_steps, 0), 0, 0)

    return (pl.BlockSpec((GROUP, CHUNK, D_MODEL), prompt_idx),
            pl.BlockSpec((GROUP, CHUNK - SAMPLE_NULL, D_MODEL), sample_idx))


def _pack_call(geo, x_prompt, x_sample, meta):
    prompt_spec, sample_spec = _layout_specs(geo)
    return pl.pallas_call(
        functools.partial(_pack_kernel, geo),
        grid=(geo.steps,),
        in_specs=[prompt_spec, sample_spec, pl.BlockSpec((N_META, D_MODEL), lambda i: (0, 0))],
        out_specs=pl.BlockSpec((GROUP_ROWS, D_MODEL), lambda i: (i, 0)),
        out_shape=jax.ShapeDtypeStruct((geo.rows, D_MODEL), F32),
        compiler_params=_params("parallel"),
        name="pack",
    )(x_prompt, x_sample, meta)


def _unpack_call(geo, x, dtype):
    prompt_spec, sample_spec = _layout_specs(geo)
    return pl.pallas_call(
        functools.partial(_unpack_kernel, geo),
        grid=(geo.steps,),
        in_specs=[pl.BlockSpec((GROUP_ROWS, D_MODEL), lambda i: (i, 0))],
        out_specs=[prompt_spec, sample_spec],
        out_shape=[jax.ShapeDtypeStruct((geo.n_prompt, (geo.cpp - 1) * CHUNK, D_MODEL), dtype),
                   jax.ShapeDtypeStruct((geo.n_sample, CHUNK - SAMPLE_NULL, D_MODEL), dtype)],
        compiler_params=_params("arbitrary"),
        name="unpack",
    )(x)


def _split_w_in_kernel(w_ref, head_ref, rest_ref):
    head_ref[...] = w_ref[:, :OFF_BETA].astype(head_ref.dtype)
    rest_ref[...] = w_ref[:, OFF_LX:].astype(rest_ref.dtype)


def _split_w_in(w_in):
    depth, rows, _ = w_in.shape
    tile = 256
    out = jax.ShapeDtypeStruct((depth, rows, P_COLS // 2), BF16)
    spec = lambda width: pl.BlockSpec((None, tile, width), lambda l, r: (l, r, 0))
    return pl.pallas_call(
        _split_w_in_kernel,
        grid=(depth, rows // tile),
        in_specs=[spec(IN_COLS)],
        out_specs=[spec(P_COLS // 2), spec(P_COLS // 2)],
        out_shape=[out, out],
        compiler_params=pltpu.CompilerParams(dimension_semantics=("parallel", "parallel"),
                                             vmem_limit_bytes=VMEM_LIMIT),
        name="split_w_in",
    )(w_in)


def _pad_lanes(v, width):
    return jnp.pad(v, ((0, 0), (0, width - v.shape[-1])))


def kernel(x_prompt, x_sample, state_delta_S, state_delta_conv, state_lru_h, state_lru_conv,
           meta_tokens, ffn1_norm, ffn1_w_gu, ffn1_w_down, mix_norm, w_in, delta_conv_w,
           delta_A_log, delta_dt_bias, delta_out_norm, lru_conv_w, lru_conv_b, lru_w_r, lru_b_r,
           lru_w_i, lru_b_i, lru_lambda, w_branch_a, w_branch_b, w_out, ffn2_norm, ffn2_w_gu,
           ffn2_w_down, final_norm):
    n_prompt, seq_len, _ = x_prompt.shape
    n_sample, dec_len, _ = x_sample.shape
    assert dec_len == CHUNK - SAMPLE_NULL and (N_META + seq_len) % CHUNK == N_META
    prompt_rows = PROMPT_NULL + N_META + seq_len
    cpp = prompt_rows // CHUNK
    geo = _Geometry(n_prompt, n_sample, cpp)
    assert geo.rows % ROW_TILE == 0 and (geo.prompt_chunks * CHUNK) % ROW_TILE == 0
    assert INPROJ_TILE % GROUP_ROWS == 0 and ROW_TILE % GROUP_ROWS == 0
    dt = x_prompt.dtype

    x = _pack_call(geo, x_prompt, x_sample, meta_tokens)

    cast = lambda w: w.astype(BF16)
    vec = lambda v: v.astype(F32)[:, None, :]
    wgu1, wd1, wgu2, wd2 = cast(ffn1_w_gu), cast(ffn1_w_down), cast(ffn2_w_gu), cast(ffn2_w_down)
    w_head, w_rest = _split_w_in(w_in)
    pad_heads = lambda w: jnp.pad(w, ((0, 0), (0, 0), (0, 128 - HA)))
    w_ba = cast(jnp.concatenate([pad_heads(w_in[:, :, OFF_BETA:OFF_ALPHA]),
                                 pad_heads(w_in[:, :, OFF_ALPHA:OFF_LX])], axis=-1))
    w_ri = cast(jnp.concatenate([lru_w_r, lru_w_i], axis=-1))
    wa, wb, wo = cast(w_branch_a), cast(w_branch_b), cast(w_out)
    norm1, norm_mix, norm2 = vec(ffn1_norm), vec(mix_norm), vec(ffn2_norm)
    alog, dtb, onorm = vec(_pad_lanes(delta_A_log, 128)), vec(_pad_lanes(delta_dt_bias, 128)), vec(delta_out_norm)
    lcb, b_r, b_i, lam = vec(lru_conv_b), vec(lru_b_r), vec(lru_b_i), vec(lru_lambda)
    cw, lcw = delta_conv_w.astype(F32), lru_conv_w.astype(F32)
    s0, h0 = state_delta_S.astype(F32), state_lru_h.astype(F32)

    inj = jnp.concatenate([state_delta_conv, state_lru_conv], axis=-1).astype(F32)
    null_runs = SAMPLE_NULL // N_SLAB
    inj = jnp.pad(inj[:, :, :, None, :], ((0, 0), (0, 0), (0, 0), (null_runs - 1, SLAB - null_runs), (0, 0)))
    inj = inj.reshape(DEPTH, n_sample * TAIL_ROWS, CONV_COLS)

    last_ids = geo.last_chunk_ids()
    outs_s, outs_tail, outs_h = [], [], []
    for l in range(DEPTH):
        x = _ffn_call(l, x, norm1, wgu1, wd1)
        p, ba, tails = _inproj_call(geo, l, x, norm_mix, w_head, w_rest, w_ba, cw, lcw, lcb, inj)
        oa, s_new = _delta_call(geo, l, p, ba, s0, alog, dtb, onorm)
        x, hlast = _outproj_call(geo, l, x, p, oa, h0, w_ri, b_r, b_i, lam, wa, wb, wo, norm2, wgu2, wd2,
                                 final_norm[None].astype(F32), final=(l == DEPTH - 1))
        outs_s.append(s_new)
        outs_tail.append(tails[last_ids, SLAB - 1::SLAB, :])
        outs_h.append(hlast[last_ids])

    n_prompt_rows = geo.prompt_chunks * CHUNK
    y_prompt, y_sample = _unpack_call(geo, x, dt)
    s_all = jnp.stack(outs_s)
    tail_all = jnp.stack(outs_tail)
    cq_all = tail_all[..., :QKV_W]
    cx_all = tail_all[..., QKV_W:]
    h_all = jnp.stack(outs_h)
    return (y_prompt.astype(dt), y_sample.astype(dt),
            s_all[:, :n_prompt].astype(dt), cq_all[:, :n_prompt].astype(dt),
            h_all[:, :n_prompt].astype(dt), cx_all[:, :n_prompt].astype(dt),
            s_all[:, n_prompt:].astype(state_delta_S.dtype),
            cq_all[:, n_prompt:].astype(state_delta_conv.dtype),
            h_all[:, n_prompt:].astype(state_lru_h.dtype),
            cx_all[:, n_prompt:].astype(state_lru_conv.dtype))
```

```python
import functools

import numpy as np
import jax
import jax.numpy as jnp
from jax import lax
from jax.experimental import pallas as pl
from jax.experimental.pallas import tpu as pltpu

F32 = jnp.float32
BF16 = jnp.bfloat16

D_MODEL = 1024
DEPTH = 4
N_META = 16
HA = 8
DK = 128
DV = 128
QK_W = HA * DK
V_W = HA * DV
QKV_W = 2 * QK_W + V_W
LRU_W = D_MODEL
NB = 8
BW = 128
CONV_W = 4
LRU_C = 8.0
D_FF = 2816
EPS = 1e-6

OFF_Z = QKV_W
OFF_BETA = OFF_Z + V_W
OFF_ALPHA = OFF_BETA + HA
OFF_LX = OFF_ALPHA + HA
OFF_LY = OFF_LX + LRU_W
OFF_GA = OFF_LY + LRU_W
OFF_GB = OFF_GA + D_MODEL
IN_COLS = OFF_GB + D_MODEL

CHUNK = 64
GROUP = 4
GROUP_ROWS = GROUP * CHUNK
PROMPT_NULL = CHUNK - N_META
SAMPLE_NULL = CHUNK // 2
INV_BASE = 16
ROW_TILE = 512
INPROJ_TILE = 256
FF_HALF = D_FF // 2
CONV_COLS = QKV_W + LRU_W
SLAB = 8
N_SLAB = CHUNK // SLAB
TAIL_ROWS = (CONV_W - 1) * SLAB
P_Z, P_X, P_Y, P_G = 3072, 4096, 5120, 6144
P_COLS = 8192
VMEM_LIMIT = 56 * 1024 * 1024


def _rms(x, w):
    return x * lax.rsqrt(jnp.mean(x * x, axis=-1, keepdims=True) + EPS) * w


def _mm(a, b):
    return jnp.dot(a.astype(BF16), b.astype(BF16), preferred_element_type=F32)


def _mm_nt(a, b):
    return lax.dot_general(a.astype(BF16), b.astype(BF16), (((1,), (1,)), ((), ())),
                           preferred_element_type=F32)


def _mm_tn(a, b):
    return lax.dot_general(a.astype(BF16), b.astype(BF16), (((0,), (0,)), ((), ())),
                           preferred_element_type=F32)


def _sigmoid(x):
    return 0.5 * (1.0 + jnp.tanh(0.5 * x))


def _silu(x):
    return x * _sigmoid(x)


def _softplus(x):
    return jnp.maximum(x, 0.0) + jnp.log1p(jnp.exp(-jnp.abs(x)))


def _gelu_tanh(x):
    return 0.5 * x * (1.0 + jnp.tanh(0.7978845608028654 * (x + 0.044715 * (x * x * x))))


class _Geometry:
    def __init__(self, n_prompt, n_sample, chunks_per_prompt):
        assert n_prompt % GROUP == 0 and n_sample % GROUP == 0
        self.n_prompt = n_prompt
        self.n_sample = n_sample
        self.cpp = chunks_per_prompt
        self.prompt_steps = (n_prompt // GROUP) * chunks_per_prompt
        self.steps = self.prompt_steps + n_sample // GROUP
        self.prompt_chunks = GROUP * self.prompt_steps
        self.chunks = GROUP * self.steps
        self.rows = self.chunks * CHUNK
        self.n_streams = n_prompt + n_sample

    def chunk_null(self, cid):
        is_prompt = cid < self.prompt_chunks
        first = jnp.logical_or(jnp.logical_not(is_prompt), (cid // GROUP) % self.cpp == 0)
        return jnp.where(first, jnp.where(is_prompt, PROMPT_NULL, SAMPLE_NULL), 0)

    def last_chunk_ids(self):
        ids = [GROUP * ((b // GROUP) * self.cpp + self.cpp - 1) + b % GROUP for b in range(self.n_prompt)]
        ids += [self.prompt_chunks + t for t in range(self.n_sample)]
        return np.asarray(ids, np.int32)


def _row_time(rows):
    return N_SLAB * (rows % SLAB) + rows // SLAB


def _valid_rows(null):
    return _row_time(lax.broadcasted_iota(jnp.int32, (CHUNK, 1), 0)) >= null


def _slabs(x):
    return [x[v * SLAB:(v + 1) * SLAB] for v in range(N_SLAB)]


def _shift_run(x, fill):
    return jnp.concatenate([fill, x[:SLAB - 1]], axis=0)


def _scan_time(a, b, h_in):
    bs = _slabs(b)
    if a is None:
        for v in range(1, N_SLAB):
            bs[v] = bs[v] + bs[v - 1]
        run = bs[N_SLAB - 1]
        d = 1
        while d < SLAB:
            run = run + _shift_rows(run, d, 0.0)
            d *= 2
        run = run + h_in
        prev = _shift_run(run, h_in)
        hs = [t + prev for t in bs]
    else:
        as_ = _slabs(a)
        for v in range(1, N_SLAB):
            bs[v] = bs[v] + as_[v] * bs[v - 1]
            as_[v] = as_[v] * as_[v - 1]
        ra, rb = as_[N_SLAB - 1], bs[N_SLAB - 1]
        d = 1
        while d < SLAB:
            rb = rb + ra * _shift_rows(rb, d, 0.0)
            ra = ra * _shift_rows(ra, d, 1.0)
            d *= 2
        run = rb + ra * h_in
        prev = _shift_run(run, h_in)
        hs = [t + u * prev for t, u in zip(bs, as_)]
    return jnp.concatenate(hs, axis=0), run[SLAB - 1:SLAB, :]


def _ffn(x, nw, wgu_ref, wd_ref, h_ref):
    xn = _rms(x, nw).astype(BF16)
    for c in range(2):
        lo = c * FF_HALF
        g = jnp.dot(xn, wgu_ref[:, lo:lo + FF_HALF], preferred_element_type=F32)
        u = jnp.dot(xn, wgu_ref[:, D_FF + lo:D_FF + lo + FF_HALF], preferred_element_type=F32)
        h_ref[:, lo:lo + FF_HALF] = (_silu(g) * u).astype(BF16)
    y = jnp.dot(h_ref[...], wd_ref[...], preferred_element_type=F32)
    return x + 0.5 * y


def _ffn_kernel(x_ref, nw_ref, wgu_ref, wd_ref, o_ref, h_ref):
    o_ref[...] = _ffn(x_ref[...], nw_ref[...], wgu_ref, wd_ref, h_ref)


def _inproj_kernel(geo, x_ref, nw_ref, wa_ref, wt_ref, wl_ref, wba_ref, cw_ref, lcw_ref, lcb_ref, inj_ref,
                   p_ref, ba_ref, tails_ref, stage_all, carry_scr, wb_scr):
    i = pl.program_id(0)
    nch = INPROJ_TILE // CHUNK
    xn = _rms(x_ref[...], nw_ref[...]).astype(BF16)

    def project_cols(lo):
        w_ref, start = (wa_ref, lo) if lo < P_COLS // 2 else (wb_scr, lo - P_COLS // 2)
        return jnp.dot(xn, w_ref[:, start:start + 1024], preferred_element_type=F32)

    @pl.when(i == 0)
    def _():
        carry_scr[...] = jnp.zeros_like(carry_scr)
        skip = OFF_LX - OFF_BETA
        for c in range(P_COLS // 2 // 1024):
            nxt = wt_ref[:, (c + 1) * 1024:(c + 1) * 1024 + 128] if (c + 1) * 1024 < P_COLS // 2 else wl_ref[...]
            wide = jnp.concatenate([wt_ref[:, c * 1024:(c + 1) * 1024], nxt], axis=1)
            wb_scr[:, c * 1024:(c + 1) * 1024] = pltpu.roll(wide, wide.shape[1] - skip, 1)[:, :1024]

    valid = [_valid_rows(geo.chunk_null(i * nch + j)) for j in range(nch)]
    in_sample = i >= geo.prompt_chunks * CHUNK // INPROJ_TILE
    kinds = (("q", 0, 0), ("k", QK_W, QK_W), ("v", 2 * QK_W, 2 * QK_W), ("x", P_X, QKV_W))
    t0 = TAIL_ROWS

    def project(kidx):
        kind, lo, col = kinds[kidx]
        stage = stage_all.at[kidx]
        pre = project_cols(lo)
        for j in range(nch):
            stage[j, t0:t0 + CHUNK, :] = pre[j * CHUNK:(j + 1) * CHUNK]
            if j >= GROUP:
                stage[j, 0:t0, :] = pre[(j - GROUP + 1) * CHUNK - t0:(j - GROUP + 1) * CHUNK]
            else:
                stage[j, 0:t0, :] = carry_scr[j, :, col:col + 1024]
            stage[j, CHUNK:CHUNK + t0, :] = stage[j, CHUNK:CHUNK + t0, :] + jnp.where(
                in_sample, inj_ref[t0 * j:t0 * (j + 1), col:col + 1024], 0.0)
            tails_ref[j, :, col:col + 1024] = stage[j, CHUNK:CHUNK + t0, :]
        for g in range(GROUP):
            carry_scr[g, :, col:col + 1024] = stage[nch - GROUP + g, CHUNK:CHUNK + t0, :]

    def convolve(kidx, j):
        kind, lo, col = kinds[kidx]
        stage = stage_all.at[kidx]
        taps = lcw_ref[...] if kind == "x" else cw_ref[:, lo:lo + 1024]
        cur = [stage[j, t0 + v * SLAB:t0 + (v + 1) * SLAB, :] for v in range(N_SLAB)]
        back = []
        for k in range(CONV_W - 1):
            prev_last = stage[j, (k + 1) * SLAB - 1:(k + 1) * SLAB, :]
            back.append(_shift_run(cur[N_SLAB - (CONV_W - 1) + k], prev_last))
        conv = []
        for v in range(N_SLAB):
            acc = cur[v] * taps[CONV_W - 1:CONV_W, :]
            for d in range(1, CONV_W):
                src = cur[v - d] if v >= d else back[CONV_W - 1 - d + v]
                acc = acc + src * taps[CONV_W - 1 - d:CONV_W - d, :]
            conv.append(acc)
        conv = jnp.concatenate(conv, axis=0)
        if kind == "x":
            out = conv + lcb_ref[...]
        else:
            act = _silu(conv)
            if kind != "v":
                scale = DK ** -0.5 if kind == "q" else 1.0
                segs = []
                for h in range(HA):
                    seg = act[:, h * DK:(h + 1) * DK]
                    segs.append(seg * (lax.rsqrt(jnp.sum(seg * seg, axis=-1, keepdims=True) + EPS) * scale))
                act = jnp.concatenate(segs, axis=1)
            out = jnp.where(valid[j], act, 0.0)
        p_ref[j * CHUNK:(j + 1) * CHUNK, lo:lo + 1024] = out.astype(BF16)

    def plain(lo):
        p_ref[:, lo:lo + 1024] = project_cols(lo).astype(BF16)

    plain_cols = (P_Z, P_Y, P_G, P_G + 1024)
    project(0)
    for kidx in range(len(kinds)):
        plain(plain_cols[kidx])
        if kidx + 1 < len(kinds):
            project(kidx + 1)
        for j in range(nch):
            convolve(kidx, j)
    ba_ref[...] = jnp.dot(xn, wba_ref[...], preferred_element_type=F32)


def _shift_rows(x, d, fill):
    if d % 8 == 0:
        return jnp.concatenate([jnp.full((d, x.shape[1]), fill, x.dtype), x[:x.shape[0] - d]], axis=0)
    rows = lax.broadcasted_iota(jnp.int32, x.shape, 0)
    return jnp.where(rows >= d, pltpu.roll(x, d, 0), fill)


def _outproj_kernel(geo, final, x_ref, g_ref, oa_ref, xl_ref, h0_ref,
                    wri_ref, br_ref, bi_ref, lam_ref, wa_ref, wb_ref, wo_ref,
                    nw_ref, wgu_ref, wd_ref, fn_ref,
                    o_ref, hlast_ref, h_ref, hc_scr, ob_scr):
    i = pl.program_id(0)
    nch = ROW_TILE // CHUNK

    @pl.when(i == 0)
    def _():
        hc_scr[...] = jnp.zeros_like(hc_scr)

    sp = _softplus(-lam_ref[...])
    nulls = [geo.chunk_null(i * nch + j) for j in range(nch)]
    valid = [_valid_rows(n) for n in nulls]
    is_sample = [(i * nch + j) >= geo.prompt_chunks for j in range(nch)]
    for n in range(NB):
        sl = slice(n * BW, (n + 1) * BW)
        xb = xl_ref[:, sl]
        gates = jnp.dot(xb, wri_ref[n], preferred_element_type=F32)
        r = _sigmoid(gates[:, :BW] + br_ref[:, sl])
        ig = _sigmoid(gates[:, BW:] + bi_ref[:, sl])
        log_a = -LRU_C * r * sp[:, sl]
        a_all = jnp.exp(log_a)
        th = jnp.tanh(log_a)
        mult = jnp.sqrt(-2.0 * th / (1.0 - th))
        b_all = mult * (ig * xb.astype(F32))
        gy = _gelu_tanh(xl_ref[:, LRU_W + n * BW:LRU_W + (n + 1) * BW].astype(F32))
        carry = [hc_scr[g:g + 1, sl] for g in range(GROUP)]
        for j in range(nch):
            rows = slice(j * CHUNK, (j + 1) * CHUNK)
            a = jnp.where(valid[j], a_all[rows], 1.0)
            b = jnp.where(valid[j], b_all[rows], 0.0)
            h_in = jnp.where(nulls[j] > 0, jnp.where(is_sample[j], h0_ref[j:j + 1, sl], 0.0), carry[j % GROUP])
            hs, carry[j % GROUP] = _scan_time(a, b, h_in)
            ob_scr[rows, sl] = (hs * gy[rows]).astype(BF16)
            hlast_ref[j:j + 1, sl] = carry[j % GROUP]
        for g in range(GROUP):
            hc_scr[g:g + 1, sl] = carry[g]

    ga = _sigmoid(g_ref[:, :D_MODEL].astype(F32))
    gb = _sigmoid(g_ref[:, D_MODEL:].astype(F32))
    ma = jnp.dot(oa_ref[...], wa_ref[...], preferred_element_type=F32)
    mb = jnp.dot(ob_scr[...], wb_ref[...], preferred_element_type=F32)
    m = (ga * ma + gb * mb).astype(BF16)
    x = x_ref[...] + jnp.dot(m, wo_ref[...], preferred_element_type=F32)
    x = _ffn(x, nw_ref[...], wgu_ref, wd_ref, h_ref)
    if final:
        x = _rms(x, fn_ref[...])
    o_ref[...] = x


def _resident(shape, layer):
    nd = len(shape)
    return pl.BlockSpec((None,) + shape, lambda i: (layer,) + (0,) * nd, pipeline_mode=pl.Buffered(1))


def _params(semantics):
    return pltpu.CompilerParams(dimension_semantics=(semantics,), vmem_limit_bytes=VMEM_LIMIT)


def _ffn_call(layer, x, nw, wgu, wd):
    n = x.shape[0]
    row = lambda i: (i, 0)
    return pl.pallas_call(
        _ffn_kernel,
        grid=(n // ROW_TILE,),
        in_specs=[pl.BlockSpec((ROW_TILE, D_MODEL), row),
                  _resident((1, D_MODEL), layer),
                  _resident((D_MODEL, 2 * D_FF), layer),
                  _resident((D_FF, D_MODEL), layer)],
        out_specs=pl.BlockSpec((ROW_TILE, D_MODEL), row),
        out_shape=jax.ShapeDtypeStruct((n, D_MODEL), F32),
        scratch_shapes=[pltpu.VMEM((ROW_TILE, D_FF), BF16)],
        compiler_params=_params("parallel"),
        name="ffn",
    )(x, nw, wgu, wd)


def _inproj_call(geo, layer, x, nw, w_all, w_last, wba, cw, lcw, lcb, inj):
    n = x.shape[0]
    nch = INPROJ_TILE // CHUNK
    sample_tile0 = geo.prompt_chunks * CHUNK // INPROJ_TILE
    row = lambda i: (i, 0)
    half = P_COLS // 2
    assert OFF_BETA == half and w_all.shape[2] == 2 * half + OFF_LX - OFF_BETA and w_last.shape[2] == 128
    return pl.pallas_call(
        functools.partial(_inproj_kernel, geo),
        grid=(n // INPROJ_TILE,),
        in_specs=[pl.BlockSpec((INPROJ_TILE, D_MODEL), row),
                  _resident((1, D_MODEL), layer),
                  _resident((D_MODEL, half), layer),
                  pl.BlockSpec((None, D_MODEL, half), lambda i: (layer, 0, 1), pipeline_mode=pl.Buffered(1)),
                  _resident((D_MODEL, 128), layer),
                  _resident((D_MODEL, 256), layer),
                  _resident((CONV_W, QKV_W), layer),
                  _resident((CONV_W, LRU_W), layer),
                  _resident((1, LRU_W), layer),
                  pl.BlockSpec((None, TAIL_ROWS * nch, CONV_COLS),
                               lambda i: (layer, jnp.maximum(i - sample_tile0, 0), 0))],
        out_specs=[pl.BlockSpec((INPROJ_TILE, P_COLS), row),
                   pl.BlockSpec((INPROJ_TILE, 256), row),
                   pl.BlockSpec((nch, TAIL_ROWS, CONV_COLS), lambda i: (i, 0, 0))],
        out_shape=[jax.ShapeDtypeStruct((n, P_COLS), BF16),
                   jax.ShapeDtypeStruct((n, 256), F32),
                   jax.ShapeDtypeStruct((n // CHUNK, TAIL_ROWS, CONV_COLS), F32)],
        scratch_shapes=[pltpu.VMEM((4, nch, CHUNK + TAIL_ROWS, 1024), F32),
                        pltpu.VMEM((GROUP, TAIL_ROWS, CONV_COLS), F32),
                        pltpu.VMEM((D_MODEL, half), BF16)],
        compiler_params=_params("arbitrary"),
        name="inproj",
    )(x, nw, w_all, w_all, w_last, wba, cw, lcw, lcb, inj)


def _outproj_call(geo, layer, x, p, oa, h0, wri, br, bi, lam, wa, wb, wo, nw, wgu, wd, fn, final):
    n = x.shape[0]
    nch = ROW_TILE // CHUNK
    sample_tile0 = geo.prompt_chunks * CHUNK // ROW_TILE
    row = lambda i: (i, 0)
    return pl.pallas_call(
        functools.partial(_outproj_kernel, geo, final),
        grid=(n // ROW_TILE,),
        in_specs=[pl.BlockSpec((ROW_TILE, D_MODEL), row),
                  pl.BlockSpec((ROW_TILE, 2 * D_MODEL), lambda i: (i, P_G // 2048)),
                  pl.BlockSpec((ROW_TILE, V_W), row),
                  pl.BlockSpec((ROW_TILE, 2 * LRU_W), lambda i: (i, P_X // 2048)),
                  pl.BlockSpec((None, nch, LRU_W), lambda i: (layer, jnp.maximum(i - sample_tile0, 0), 0)),
                  _resident((NB, BW, 2 * BW), layer),
                  _resident((1, LRU_W), layer),
                  _resident((1, LRU_W), layer),
                  _resident((1, LRU_W), layer),
                  _resident((V_W, D_MODEL), layer),
                  _resident((LRU_W, D_MODEL), layer),
                  _resident((D_MODEL, D_MODEL), layer),
                  _resident((1, D_MODEL), layer),
                  _resident((D_MODEL, 2 * D_FF), layer),
                  _resident((D_FF, D_MODEL), layer),
                  pl.BlockSpec((1, D_MODEL), lambda i: (0, 0), pipeline_mode=pl.Buffered(1))],
        out_specs=[pl.BlockSpec((ROW_TILE, D_MODEL), row),
                   pl.BlockSpec((nch, LRU_W), row)],
        out_shape=[jax.ShapeDtypeStruct((n, D_MODEL), F32),
                   jax.ShapeDtypeStruct((n // CHUNK, LRU_W), F32)],
        scratch_shapes=[pltpu.VMEM((ROW_TILE, D_FF), BF16),
                        pltpu.VMEM((8, LRU_W), F32),
                        pltpu.VMEM((ROW_TILE, LRU_W), BF16)],
        compiler_params=_params("arbitrary"),
        name="outproj_ffn",
    )(x, p, oa, p, h0, wri, br, bi, lam, wa, wb, wo, nw, wgu, wd, fn)


def _unit_lower_inverse(lows, c):
    ti = _row_time(lax.broadcasted_iota(jnp.int32, (c, c), 0))
    tj = _row_time(lax.broadcasted_iota(jnp.int32, (c, c), 1))
    same_block = (ti // INV_BASE) == (tj // INV_BASE)
    eye = (ti == tj).astype(F32)
    ld = [jnp.where(same_block, low, 0.0) for low in lows]
    lo = [low - d for low, d in zip(lows, ld)]
    x = [eye - d for d in ld]
    p = ld
    span = 2
    while span < INV_BASE:
        p = [_mm(t, t) for t in p]
        x = [a + _mm(a, t) for a, t in zip(x, p)]
        span *= 2
    e = [_mm(a, b) for a, b in zip(x, lo)]
    y = [eye - t for t in e]
    q = e
    span = 2
    while span < c // INV_BASE:
        q = [_mm(t, t) for t in q]
        y = [a + _mm(a, t) for a, t in zip(y, q)]
        span *= 2
    return [_mm(a, b) for a, b in zip(y, x)]


def _delta_kernel(geo, qkv_ref, z_ref, ba_ref, s0_ref, alog_ref, dtb_ref, onorm_ref,
                  oa_ref, sout_ref, s_scr):
    c = CHUNK
    i = pl.program_id(0)
    is_prompt = i < geo.prompt_steps
    pos = i % geo.cpp
    first = jnp.logical_or(jnp.logical_not(is_prompt), pos == 0)
    last = jnp.logical_or(jnp.logical_not(is_prompt), pos == geo.cpp - 1)
    null = jnp.where(first, jnp.where(is_prompt, PROMPT_NULL, SAMPLE_NULL), 0)
    valid = _valid_rows(null)

    @pl.when(jnp.logical_and(first, is_prompt))
    def _():
        s_scr[...] = jnp.zeros_like(s_scr)

    @pl.when(jnp.logical_not(is_prompt))
    def _():
        s_scr[...] = s0_ref[...]

    ti = _row_time(lax.broadcasted_iota(jnp.int32, (c, c), 0))
    tj = _row_time(lax.broadcasted_iota(jnp.int32, (c, c), 1))
    tri_incl = ti >= tj
    tri_strict = ti > tj

    beta_all, gc_all, gc_t, eg_all, ekd_all, egl_all = [], [], [], [], [], []
    for s in range(GROUP):
        ba = ba_ref[s * c:(s + 1) * c, :]
        beta_all.append(jnp.where(valid, _sigmoid(ba[:, :128]), 0.0))
        g = -jnp.exp(alog_ref[...]) * _softplus(ba[:, 128:] + dtb_ref[...])
        gc, g_last = _scan_time(None, jnp.where(valid, g, 0.0), jnp.zeros((1, 128), F32))
        gc_all.append(gc)
        gc_t.append(jnp.concatenate([gc, jnp.zeros((128 - c, 128), F32)], axis=0).T)
        eg_all.append(jnp.exp(gc))
        ekd_all.append(jnp.exp(g_last - gc))
        egl_all.append(jnp.exp(g_last))

    probs = [(s, h) for s in range(GROUP) for h in range(HA)]
    rows = lambda s: slice(s * c, (s + 1) * c)
    qb = [qkv_ref[rows(s), h * DK:(h + 1) * DK] for s, h in probs]
    kbf = [qkv_ref[rows(s), QK_W + h * DK:QK_W + (h + 1) * DK] for s, h in probs]
    vb = [qkv_ref[rows(s), 2 * QK_W + h * DV:2 * QK_W + (h + 1) * DV] for s, h in probs]
    q = [t.astype(F32) for t in qb]
    k = [t.astype(F32) for t in kbf]
    beta = [beta_all[s][:, h:h + 1] for s, h in probs]
    eg = [eg_all[s][:, h:h + 1] for s, h in probs]
    n = len(probs)
    kb = [k[p] * beta[p] for p in range(n)]
    sc = [_mm_nt(jnp.concatenate([kb[p].astype(BF16), qb[p]], axis=0), kbf[p]) for p in range(n)]
    decay = [jnp.where(tri_incl, jnp.exp(jnp.where(tri_incl, gc_all[s][:, h:h + 1] - gc_t[s][h:h + 1, :c], 0.0)), 0.0)
             for s, h in probs]
    low = [jnp.where(tri_strict, sc[p][:c] * decay[p], 0.0) for p in range(n)]
    a_intra = [sc[p][c:] * decay[p] for p in range(n)]
    t_inv = _unit_lower_inverse(low, c)
    uw = [_mm(t_inv[p], jnp.concatenate([vb[p].astype(F32) * beta[p], kb[p] * eg[p]], axis=1))
          for p in range(n)]
    s_old = [s_scr[s, h] for s, h in probs]
    wq = [_mm(jnp.concatenate([uw[p][:, DV:], q[p] * eg[p]], axis=0), s_old[p]) for p in range(n)]
    v_new = [uw[p][:, :DV] - wq[p][:c] for p in range(n)]
    o_intra = [_mm(a_intra[p], v_new[p]) for p in range(n)]
    ds = [_mm_tn(k[p] * ekd_all[s][:, h:h + 1], v_new[p]) for p, (s, h) in enumerate(probs)]
    for p, (s, h) in enumerate(probs):
        s_scr[s, h] = s_old[p] * egl_all[s][:, h:h + 1] + ds[p]
        o = wq[p][c:] + o_intra[p]
        zh = z_ref[rows(s), h * DV:(h + 1) * DV].astype(F32)
        o = o * lax.rsqrt(jnp.mean(o * o, axis=-1, keepdims=True) + EPS) * onorm_ref[...] * _silu(zh)
        oa_ref[rows(s), h * DV:(h + 1) * DV] = o.astype(oa_ref.dtype)

    @pl.when(last)
    def _():
        sout_ref[...] = s_scr[...]


def _delta_call(geo, layer, p, ba, s0, alog, dtb, onorm):
    n_pp = geo.n_prompt // GROUP

    def pair(i):
        return jnp.where(i < geo.prompt_steps, i // geo.cpp, i - geo.prompt_steps + n_pp)

    vec = lambda width: pl.BlockSpec((None, 1, width), lambda i: (layer, 0, 0))
    return pl.pallas_call(
        functools.partial(_delta_kernel, geo),
        grid=(geo.steps,),
        in_specs=[pl.BlockSpec((GROUP_ROWS, QKV_W), lambda i: (i, 0)),
                  pl.BlockSpec((GROUP_ROWS, V_W), lambda i: (i, P_Z // V_W)),
                  pl.BlockSpec((GROUP_ROWS, 256), lambda i: (i, 0)),
                  pl.BlockSpec((None, GROUP, HA, DK, DV),
                               lambda i: (layer, jnp.maximum(i - geo.prompt_steps, 0), 0, 0, 0)),
                  vec(128), vec(128), vec(DV)],
        out_specs=[pl.BlockSpec((GROUP_ROWS, V_W), lambda i: (i, 0)),
                   pl.BlockSpec((GROUP, HA, DK, DV), lambda i: (pair(i), 0, 0, 0))],
        out_shape=[jax.ShapeDtypeStruct((geo.rows, V_W), BF16),
                   jax.ShapeDtypeStruct((geo.n_streams, HA, DK, DV), F32)],
        scratch_shapes=[pltpu.VMEM((GROUP, HA, DK, DV), F32)],
        compiler_params=_params("arbitrary"),
        name="delta",
    )(p, p, ba, s0, alog, dtb, onorm)


def _permute_chunk(x):
    return jnp.swapaxes(x.reshape(SLAB, N_SLAB, x.shape[-1]), 0, 1).reshape(CHUNK, x.shape[-1])


def _pack_kernel(geo, xp_ref, xs_ref, meta_ref, o_ref):
    i = pl.program_id(0)
    is_prompt = i < geo.prompt_steps
    pos = i % geo.cpp
    width = o_ref.shape[1]

    @pl.when(jnp.logical_and(is_prompt, pos > 0))
    def _():
        for s in range(GROUP):
            o_ref[s * CHUNK:(s + 1) * CHUNK, :] = _permute_chunk(xp_ref[s].astype(o_ref.dtype))

    @pl.when(jnp.logical_and(is_prompt, pos == 0))
    def _():
        first = _permute_chunk(jnp.concatenate(
            [jnp.zeros((PROMPT_NULL, width), o_ref.dtype), meta_ref[...].astype(o_ref.dtype)], axis=0))
        for s in range(GROUP):
            o_ref[s * CHUNK:(s + 1) * CHUNK, :] = first

    @pl.when(jnp.logical_not(is_prompt))
    def _():
        for s in range(GROUP):
            o_ref[s * CHUNK:(s + 1) * CHUNK, :] = _permute_chunk(jnp.concatenate(
                [jnp.zeros((SAMPLE_NULL, width), o_ref.dtype), xs_ref[s].astype(o_ref.dtype)], axis=0))


def _unpack_kernel(geo, x_ref, yp_ref, ys_ref):
    i = pl.program_id(0)
    is_prompt = i < geo.prompt_steps
    pos = i % geo.cpp

    @pl.when(jnp.logical_and(is_prompt, pos > 0))
    def _():
        for s in range(GROUP):
            yp_ref[s] = _permute_chunk(x_ref[s * CHUNK:(s + 1) * CHUNK, :]).astype(yp_ref.dtype)

    @pl.when(jnp.logical_not(is_prompt))
    def _():
        for s in range(GROUP):
            ys_ref[s] = _permute_chunk(x_ref[s * CHUNK:(s + 1) * CHUNK, :])[SAMPLE_NULL:].astype(ys_ref.dtype)


def _layout_specs(geo):
    def prompt_idx(i):
        step = jnp.minimum(i, geo.prompt_steps - 1)
        return (step // geo.cpp, jnp.maximum(step % geo.cpp - 1, 0), 0)

    def sample_idx(i):
        return (jnp.maximum(i - geo.prompt_steps, 0), 0, 0)

    return (pl.BlockSpec((GROUP, CHUNK, D_MODEL), prompt_idx),
            pl.BlockSpec((GROUP, CHUNK - SAMPLE_NULL, D_MODEL), sample_idx))


def _pack_call(geo, x_prompt, x_sample, meta):
    prompt_spec, sample_spec = _layout_specs(geo)
    return pl.pallas_call(
        functools.partial(_pack_kernel, geo),
        grid=(geo.steps,),
        in_specs=[prompt_spec, sample_spec, pl.BlockSpec((N_META, D_MODEL), lambda i: (0, 0))],
        out_specs=pl.BlockSpec((GROUP_ROWS, D_MODEL), lambda i: (i, 0)),
        out_shape=jax.ShapeDtypeStruct((geo.rows, D_MODEL), F32),
        compiler_params=_params("parallel"),
        name="pack",
    )(x_prompt, x_sample, meta)


def _unpack_call(geo, x, dtype):
    prompt_spec, sample_spec = _layout_specs(geo)
    return pl.pallas_call(
        functools.partial(_unpack_kernel, geo),
        grid=(geo.steps,),
        in_specs=[pl.BlockSpec((GROUP_ROWS, D_MODEL), lambda i: (i, 0))],
        out_specs=[prompt_spec, sample_spec],
        out_shape=[jax.ShapeDtypeStruct((geo.n_prompt, (geo.cpp - 1) * CHUNK, D_MODEL), dtype),
                   jax.ShapeDtypeStruct((geo.n_sample, CHUNK - SAMPLE_NULL, D_MODEL), dtype)],
        compiler_params=_params("arbitrary"),
        name="unpack",
    )(x)


def _pad_lanes(v, width):
    return jnp.pad(v, ((0, 0), (0, width - v.shape[-1])))


def kernel(x_prompt, x_sample, state_delta_S, state_delta_conv, state_lru_h, state_lru_conv,
           meta_tokens, ffn1_norm, ffn1_w_gu, ffn1_w_down, mix_norm, w_in, delta_conv_w,
           delta_A_log, delta_dt_bias, delta_out_norm, lru_conv_w, lru_conv_b, lru_w_r, lru_b_r,
           lru_w_i, lru_b_i, lru_lambda, w_branch_a, w_branch_b, w_out, ffn2_norm, ffn2_w_gu,
           ffn2_w_down, final_norm):
    n_prompt, seq_len, _ = x_prompt.shape
    n_sample, dec_len, _ = x_sample.shape
    assert dec_len == CHUNK - SAMPLE_NULL and (N_META + seq_len) % CHUNK == N_META
    prompt_rows = PROMPT_NULL + N_META + seq_len
    cpp = prompt_rows // CHUNK
    geo = _Geometry(n_prompt, n_sample, cpp)
    assert geo.rows % ROW_TILE == 0 and (geo.prompt_chunks * CHUNK) % ROW_TILE == 0
    assert INPROJ_TILE % GROUP_ROWS == 0 and ROW_TILE % GROUP_ROWS == 0
    dt = x_prompt.dtype

    x = _pack_call(geo, x_prompt, x_sample, meta_tokens)

    cast = lambda w: w.astype(BF16)
    vec = lambda v: v.astype(F32)[:, None, :]
    wgu1, wd1, wgu2, wd2 = cast(ffn1_w_gu), cast(ffn1_w_down), cast(ffn2_w_gu), cast(ffn2_w_down)
    w_all = cast(w_in)
    pad_heads = lambda w: jnp.pad(w, ((0, 0), (0, 0), (0, 128 - w.shape[-1])))
    w_last = cast(pad_heads(w_in[:, :, P_COLS:]))
    w_ba = cast(jnp.concatenate([pad_heads(w_in[:, :, OFF_BETA:OFF_ALPHA]),
                                 pad_heads(w_in[:, :, OFF_ALPHA:OFF_LX])], axis=-1))
    w_ri = cast(jnp.concatenate([lru_w_r, lru_w_i], axis=-1))
    wa, wb, wo = cast(w_branch_a), cast(w_branch_b), cast(w_out)
    norm1, norm_mix, norm2 = vec(ffn1_norm), vec(mix_norm), vec(ffn2_norm)
    alog, dtb, onorm = vec(_pad_lanes(delta_A_log, 128)), vec(_pad_lanes(delta_dt_bias, 128)), vec(delta_out_norm)
    lcb, b_r, b_i, lam = vec(lru_conv_b), vec(lru_b_r), vec(lru_b_i), vec(lru_lambda)
    cw, lcw = delta_conv_w.astype(F32), lru_conv_w.astype(F32)
    s0, h0 = state_delta_S.astype(F32), state_lru_h.astype(F32)

    inj = jnp.concatenate([state_delta_conv, state_lru_conv], axis=-1).astype(F32)
    null_runs = SAMPLE_NULL // N_SLAB
    inj = jnp.pad(inj[:, :, :, None, :], ((0, 0), (0, 0), (0, 0), (null_runs - 1, SLAB - null_runs), (0, 0)))
    inj = inj.reshape(DEPTH, n_sample * TAIL_ROWS, CONV_COLS)

    last_ids = geo.last_chunk_ids()
    outs_s, outs_tail, outs_h = [], [], []
    for l in range(DEPTH):
        x = _ffn_call(l, x, norm1, wgu1, wd1)
        p, ba, tails = _inproj_call(geo, l, x, norm_mix, w_all, w_last, w_ba, cw, lcw, lcb, inj)
        oa, s_new = _delta_call(geo, l, p, ba, s0, alog, dtb, onorm)
        x, hlast = _outproj_call(geo, l, x, p, oa, h0, w_ri, b_r, b_i, lam, wa, wb, wo, norm2, wgu2, wd2,
                                 final_norm[None].astype(F32), final=(l == DEPTH - 1))
        outs_s.append(s_new)
        outs_tail.append(tails[last_ids, SLAB - 1::SLAB, :])
        outs_h.append(hlast[last_ids])

    n_prompt_rows = geo.prompt_chunks * CHUNK
    y_prompt, y_sample = _unpack_call(geo, x, dt)
    s_all = jnp.stack(outs_s)
    tail_all = jnp.stack(outs_tail)
    cq_all = tail_all[..., :QKV_W]
    cx_all = tail_all[..., QKV_W:]
    h_all = jnp.stack(outs_h)
    return (y_prompt.astype(dt), y_sample.astype(dt),
            s_all[:, :n_prompt].astype(dt), cq_all[:, :n_prompt].astype(dt),
            h_all[:, :n_prompt].astype(dt), cx_all[:, :n_prompt].astype(dt),
            s_all[:, n_prompt:].astype(state_delta_S.dtype),
            cq_all[:, n_prompt:].astype(state_delta_conv.dtype),
            h_all[:, n_prompt:].astype(state_lru_h.dtype),
            cx_all[:, n_prompt:].astype(state_lru_conv.dtype))
```

```python
import functools

import numpy as np
import jax
import jax.numpy as jnp
from jax import lax
from jax.experimental import pallas as pl
from jax.experimental.pallas import tpu as pltpu

F32 = jnp.float32
BF16 = jnp.bfloat16

D_MODEL = 1024
DEPTH = 4
N_META = 16
HA = 8
DK = 128
DV = 128
QK_W = HA * DK
V_W = HA * DV
QKV_W = 2 * QK_W + V_W
LRU_W = D_MODEL
NB = 8
BW = 128
CONV_W = 4
LRU_C = 8.0
D_FF = 2816
EPS = 1e-6

OFF_Z = QKV_W
OFF_BETA = OFF_Z + V_W
OFF_ALPHA = OFF_BETA + HA
OFF_LX = OFF_ALPHA + HA
OFF_LY = OFF_LX + LRU_W
OFF_GA = OFF_LY + LRU_W
OFF_GB = OFF_GA + D_MODEL
IN_COLS = OFF_GB + D_MODEL

CHUNK = 64
GROUP = 4
GROUP_ROWS = GROUP * CHUNK
PROMPT_NULL = CHUNK - N_META
SAMPLE_NULL = CHUNK // 2
INV_BASE = 16
ROW_TILE = 512
INPROJ_TILE = 256
FF_HALF = D_FF // 2
CONV_COLS = QKV_W + LRU_W
SLAB = 8
N_SLAB = CHUNK // SLAB
TAIL_ROWS = (CONV_W - 1) * SLAB
P_Z, P_X, P_Y, P_G = 3072, 4096, 5120, 6144
P_COLS = 8192
VMEM_LIMIT = 56 * 1024 * 1024


def _rms(x, w):
    return x * lax.rsqrt(jnp.mean(x * x, axis=-1, keepdims=True) + EPS) * w


def _mm(a, b):
    return jnp.dot(a.astype(BF16), b.astype(BF16), preferred_element_type=F32)


def _mm_nt(a, b):
    return lax.dot_general(a.astype(BF16), b.astype(BF16), (((1,), (1,)), ((), ())),
                           preferred_element_type=F32)


def _mm_tn(a, b):
    return lax.dot_general(a.astype(BF16), b.astype(BF16), (((0,), (0,)), ((), ())),
                           preferred_element_type=F32)


def _sigmoid(x):
    return 0.5 * (1.0 + jnp.tanh(0.5 * x))


def _silu(x):
    return x * _sigmoid(x)


def _softplus(x):
    return jnp.maximum(x, 0.0) + jnp.log1p(jnp.exp(-jnp.abs(x)))


def _gelu_tanh(x):
    return 0.5 * x * (1.0 + jnp.tanh(0.7978845608028654 * (x + 0.044715 * (x * x * x))))


class _Geometry:
    def __init__(self, n_prompt, n_sample, chunks_per_prompt):
        assert n_prompt % GROUP == 0 and n_sample % GROUP == 0
        self.n_prompt = n_prompt
        self.n_sample = n_sample
        self.cpp = chunks_per_prompt
        self.prompt_steps = (n_prompt // GROUP) * chunks_per_prompt
        self.steps = self.prompt_steps + n_sample // GROUP
        self.prompt_chunks = GROUP * self.prompt_steps
        self.chunks = GROUP * self.steps
        self.rows = self.chunks * CHUNK
        self.n_streams = n_prompt + n_sample

    def chunk_null(self, cid):
        is_prompt = cid < self.prompt_chunks
        first = jnp.logical_or(jnp.logical_not(is_prompt), (cid // GROUP) % self.cpp == 0)
        return jnp.where(first, jnp.where(is_prompt, PROMPT_NULL, SAMPLE_NULL), 0)

    def last_chunk_ids(self):
        ids = [GROUP * ((b // GROUP) * self.cpp + self.cpp - 1) + b % GROUP for b in range(self.n_prompt)]
        ids += [self.prompt_chunks + t for t in range(self.n_sample)]
        return np.asarray(ids, np.int32)


def _row_time(rows):
    return N_SLAB * (rows % SLAB) + rows // SLAB


def _valid_rows(null):
    return _row_time(lax.broadcasted_iota(jnp.int32, (CHUNK, 1), 0)) >= null


def _slabs(x):
    return [x[v * SLAB:(v + 1) * SLAB] for v in range(N_SLAB)]


def _shift_run(x, fill):
    return jnp.concatenate([fill, x[:SLAB - 1]], axis=0)


def _scan_time(a, b, h_in):
    bs = _slabs(b)
    if a is None:
        for v in range(1, N_SLAB):
            bs[v] = bs[v] + bs[v - 1]
        run = bs[N_SLAB - 1]
        d = 1
        while d < SLAB:
            run = run + _shift_rows(run, d, 0.0)
            d *= 2
        run = run + h_in
        prev = _shift_run(run, h_in)
        hs = [t + prev for t in bs]
    else:
        as_ = _slabs(a)
        for v in range(1, N_SLAB):
            bs[v] = bs[v] + as_[v] * bs[v - 1]
            as_[v] = as_[v] * as_[v - 1]
        ra, rb = as_[N_SLAB - 1], bs[N_SLAB - 1]
        d = 1
        while d < SLAB:
            rb = rb + ra * _shift_rows(rb, d, 0.0)
            ra = ra * _shift_rows(ra, d, 1.0)
            d *= 2
        run = rb + ra * h_in
        prev = _shift_run(run, h_in)
        hs = [t + u * prev for t, u in zip(bs, as_)]
    return jnp.concatenate(hs, axis=0), run[SLAB - 1:SLAB, :]


def _ffn(x, nw, wgu_ref, wd_ref, h_ref):
    xn = _rms(x, nw).astype(BF16)
    for c in range(2):
        lo = c * FF_HALF
        g = jnp.dot(xn, wgu_ref[:, lo:lo + FF_HALF], preferred_element_type=F32)
        u = jnp.dot(xn, wgu_ref[:, D_FF + lo:D_FF + lo + FF_HALF], preferred_element_type=F32)
        h_ref[:, lo:lo + FF_HALF] = (_silu(g) * u).astype(BF16)
    y = jnp.dot(h_ref[...], wd_ref[...], preferred_element_type=F32)
    return x + 0.5 * y


def _ffn_kernel(x_ref, nw_ref, wgu_ref, wd_ref, o_ref, h_ref):
    o_ref[...] = _ffn(x_ref[...], nw_ref[...], wgu_ref, wd_ref, h_ref)


def _inproj_kernel(geo, x_ref, nw_ref, wt_ref, cw_ref, lcw_ref, lcb_ref, inj_ref,
                   p_ref, ba_ref, tails_ref, stage_all, carry_scr):
    i = pl.program_id(0)
    nch = INPROJ_TILE // CHUNK
    xn = _rms(x_ref[...], nw_ref[...]).astype(BF16)

    def project_cols(lo):
        src = lo if lo < OFF_BETA else lo + OFF_LX - OFF_BETA
        return lax.dot_general(xn, wt_ref[src:src + 1024, :], (((1,), (1,)), ((), ())),
                               preferred_element_type=F32)

    @pl.when(i == 0)
    def _():
        carry_scr[...] = jnp.zeros_like(carry_scr)

    valid = [_valid_rows(geo.chunk_null(i * nch + j)) for j in range(nch)]
    in_sample = i >= geo.prompt_chunks * CHUNK // INPROJ_TILE
    kinds = (("q", 0, 0), ("k", QK_W, QK_W), ("v", 2 * QK_W, 2 * QK_W), ("x", P_X, QKV_W))
    t0 = TAIL_ROWS

    def project(kidx):
        kind, lo, col = kinds[kidx]
        stage = stage_all.at[kidx]
        pre = project_cols(lo)
        for j in range(nch):
            stage[j, t0:t0 + CHUNK, :] = pre[j * CHUNK:(j + 1) * CHUNK]
            if j >= GROUP:
                stage[j, 0:t0, :] = pre[(j - GROUP + 1) * CHUNK - t0:(j - GROUP + 1) * CHUNK]
            else:
                stage[j, 0:t0, :] = carry_scr[j, :, col:col + 1024]
            stage[j, CHUNK:CHUNK + t0, :] = stage[j, CHUNK:CHUNK + t0, :] + jnp.where(
                in_sample, inj_ref[t0 * j:t0 * (j + 1), col:col + 1024], 0.0)
            tails_ref[j, :, col:col + 1024] = stage[j, CHUNK:CHUNK + t0, :]
        for g in range(GROUP):
            carry_scr[g, :, col:col + 1024] = stage[nch - GROUP + g, CHUNK:CHUNK + t0, :]

    def convolve(kidx, j):
        kind, lo, col = kinds[kidx]
        stage = stage_all.at[kidx]
        taps = lcw_ref[...] if kind == "x" else cw_ref[:, lo:lo + 1024]
        cur = [stage[j, t0 + v * SLAB:t0 + (v + 1) * SLAB, :] for v in range(N_SLAB)]
        back = []
        for k in range(CONV_W - 1):
            prev_last = stage[j, (k + 1) * SLAB - 1:(k + 1) * SLAB, :]
            back.append(_shift_run(cur[N_SLAB - (CONV_W - 1) + k], prev_last))
        conv = []
        for v in range(N_SLAB):
            acc = cur[v] * taps[CONV_W - 1:CONV_W, :]
            for d in range(1, CONV_W):
                src = cur[v - d] if v >= d else back[CONV_W - 1 - d + v]
                acc = acc + src * taps[CONV_W - 1 - d:CONV_W - d, :]
            conv.append(acc)
        conv = jnp.concatenate(conv, axis=0)
        if kind == "x":
            out = conv + lcb_ref[...]
        else:
            act = _silu(conv)
            if kind != "v":
                scale = DK ** -0.5 if kind == "q" else 1.0
                segs = []
                for h in range(HA):
                    seg = act[:, h * DK:(h + 1) * DK]
                    segs.append(seg * (lax.rsqrt(jnp.sum(seg * seg, axis=-1, keepdims=True) + EPS) * scale))
                act = jnp.concatenate(segs, axis=1)
            out = jnp.where(valid[j], act, 0.0)
        p_ref[j * CHUNK:(j + 1) * CHUNK, lo:lo + 1024] = out.astype(BF16)

    def plain(lo):
        p_ref[:, lo:lo + 1024] = project_cols(lo).astype(BF16)

    plain_cols = (P_Z, P_Y, P_G, P_G + 1024)
    project(0)
    for kidx in range(len(kinds)):
        plain(plain_cols[kidx])
        if kidx + 1 < len(kinds):
            project(kidx + 1)
        for j in range(nch):
            convolve(kidx, j)
    raw = lax.dot_general(xn, wt_ref[OFF_BETA:OFF_BETA + 128, :], (((1,), (1,)), ((), ())),
                          preferred_element_type=F32)
    head_lanes = lax.broadcasted_iota(jnp.int32, raw.shape, 1) < HA
    ba_ref[:, :128] = jnp.where(head_lanes, raw, 0.0)
    ba_ref[:, 128:] = jnp.where(head_lanes, pltpu.roll(raw, 128 - HA, 1), 0.0)


def _shift_rows(x, d, fill):
    if d % 8 == 0:
        return jnp.concatenate([jnp.full((d, x.shape[1]), fill, x.dtype), x[:x.shape[0] - d]], axis=0)
    rows = lax.broadcasted_iota(jnp.int32, x.shape, 0)
    return jnp.where(rows >= d, pltpu.roll(x, d, 0), fill)


def _outproj_kernel(geo, final, x_ref, g_ref, oa_ref, xl_ref, h0_ref,
                    wri_ref, br_ref, bi_ref, lam_ref, wa_ref, wb_ref, wo_ref,
                    nw_ref, wgu_ref, wd_ref, fn_ref,
                    o_ref, hlast_ref, h_ref, hc_scr, ob_scr):
    i = pl.program_id(0)
    nch = ROW_TILE // CHUNK

    @pl.when(i == 0)
    def _():
        hc_scr[...] = jnp.zeros_like(hc_scr)

    sp = _softplus(-lam_ref[...])
    nulls = [geo.chunk_null(i * nch + j) for j in range(nch)]
    valid = [_valid_rows(n) for n in nulls]
    is_sample = [(i * nch + j) >= geo.prompt_chunks for j in range(nch)]
    for n in range(NB):
        sl = slice(n * BW, (n + 1) * BW)
        xb = xl_ref[:, sl]
        gates = jnp.dot(xb, wri_ref[n], preferred_element_type=F32)
        r = _sigmoid(gates[:, :BW] + br_ref[:, sl])
        ig = _sigmoid(gates[:, BW:] + bi_ref[:, sl])
        log_a = -LRU_C * r * sp[:, sl]
        a_all = jnp.exp(log_a)
        th = jnp.tanh(log_a)
        mult = jnp.sqrt(-2.0 * th / (1.0 - th))
        b_all = mult * (ig * xb.astype(F32))
        gy = _gelu_tanh(xl_ref[:, LRU_W + n * BW:LRU_W + (n + 1) * BW].astype(F32))
        carry = [hc_scr[g:g + 1, sl] for g in range(GROUP)]
        for j in range(nch):
            rows = slice(j * CHUNK, (j + 1) * CHUNK)
            a = jnp.where(valid[j], a_all[rows], 1.0)
            b = jnp.where(valid[j], b_all[rows], 0.0)
            h_in = jnp.where(nulls[j] > 0, jnp.where(is_sample[j], h0_ref[j:j + 1, sl], 0.0), carry[j % GROUP])
            hs, carry[j % GROUP] = _scan_time(a, b, h_in)
            ob_scr[rows, sl] = (hs * gy[rows]).astype(BF16)
            hlast_ref[j:j + 1, sl] = carry[j % GROUP]
        for g in range(GROUP):
            hc_scr[g:g + 1, sl] = carry[g]

    ga = _sigmoid(g_ref[:, :D_MODEL].astype(F32))
    gb = _sigmoid(g_ref[:, D_MODEL:].astype(F32))
    ma = jnp.dot(oa_ref[...], wa_ref[...], preferred_element_type=F32)
    mb = jnp.dot(ob_scr[...], wb_ref[...], preferred_element_type=F32)
    m = (ga * ma + gb * mb).astype(BF16)
    x = x_ref[...] + jnp.dot(m, wo_ref[...], preferred_element_type=F32)
    x = _ffn(x, nw_ref[...], wgu_ref, wd_ref, h_ref)
    if final:
        x = _rms(x, fn_ref[...])
    o_ref[...] = x


def _resident(shape, layer):
    nd = len(shape)
    return pl.BlockSpec((None,) + shape, lambda i: (layer,) + (0,) * nd, pipeline_mode=pl.Buffered(1))


def _params(semantics):
    return pltpu.CompilerParams(dimension_semantics=(semantics,), vmem_limit_bytes=VMEM_LIMIT)


def _ffn_call(layer, x, nw, wgu, wd):
    n = x.shape[0]
    row = lambda i: (i, 0)
    return pl.pallas_call(
        _ffn_kernel,
        grid=(n // ROW_TILE,),
        in_specs=[pl.BlockSpec((ROW_TILE, D_MODEL), row),
                  _resident((1, D_MODEL), layer),
                  _resident((D_MODEL, 2 * D_FF), layer),
                  _resident((D_FF, D_MODEL), layer)],
        out_specs=pl.BlockSpec((ROW_TILE, D_MODEL), row),
        out_shape=jax.ShapeDtypeStruct((n, D_MODEL), F32),
        scratch_shapes=[pltpu.VMEM((ROW_TILE, D_FF), BF16)],
        compiler_params=_params("parallel"),
        name="ffn",
    )(x, nw, wgu, wd)


def _inproj_call(geo, layer, x, nw, wt, cw, lcw, lcb, inj):
    n = x.shape[0]
    nch = INPROJ_TILE // CHUNK
    sample_tile0 = geo.prompt_chunks * CHUNK // INPROJ_TILE
    row = lambda i: (i, 0)
    assert OFF_BETA == P_Z + V_W and P_COLS == IN_COLS - (OFF_LX - OFF_BETA)
    return pl.pallas_call(
        functools.partial(_inproj_kernel, geo),
        grid=(n // INPROJ_TILE,),
        in_specs=[pl.BlockSpec((INPROJ_TILE, D_MODEL), row),
                  _resident((1, D_MODEL), layer),
                  _resident((IN_COLS, D_MODEL), layer),
                  _resident((CONV_W, QKV_W), layer),
                  _resident((CONV_W, LRU_W), layer),
                  _resident((1, LRU_W), layer),
                  pl.BlockSpec((None, TAIL_ROWS * nch, CONV_COLS),
                               lambda i: (layer, jnp.maximum(i - sample_tile0, 0), 0))],
        out_specs=[pl.BlockSpec((INPROJ_TILE, P_COLS), row),
                   pl.BlockSpec((INPROJ_TILE, 256), row),
                   pl.BlockSpec((nch, TAIL_ROWS, CONV_COLS), lambda i: (i, 0, 0))],
        out_shape=[jax.ShapeDtypeStruct((n, P_COLS), BF16),
                   jax.ShapeDtypeStruct((n, 256), F32),
                   jax.ShapeDtypeStruct((n // CHUNK, TAIL_ROWS, CONV_COLS), F32)],
        scratch_shapes=[pltpu.VMEM((4, nch, CHUNK + TAIL_ROWS, 1024), F32),
                        pltpu.VMEM((GROUP, TAIL_ROWS, CONV_COLS), F32)],
        compiler_params=_params("arbitrary"),
        name="inproj",
    )(x, nw, wt, cw, lcw, lcb, inj)


def _outproj_call(geo, layer, x, p, oa, h0, wri, br, bi, lam, wa, wb, wo, nw, wgu, wd, fn, final):
    n = x.shape[0]
    nch = ROW_TILE // CHUNK
    sample_tile0 = geo.prompt_chunks * CHUNK // ROW_TILE
    row = lambda i: (i, 0)
    return pl.pallas_call(
        functools.partial(_outproj_kernel, geo, final),
        grid=(n // ROW_TILE,),
        in_specs=[pl.BlockSpec((ROW_TILE, D_MODEL), row),
                  pl.BlockSpec((ROW_TILE, 2 * D_MODEL), lambda i: (i, P_G // 2048)),
                  pl.BlockSpec((ROW_TILE, V_W), row),
                  pl.BlockSpec((ROW_TILE, 2 * LRU_W), lambda i: (i, P_X // 2048)),
                  pl.BlockSpec((None, nch, LRU_W), lambda i: (layer, jnp.maximum(i - sample_tile0, 0), 0)),
                  _resident((NB, BW, 2 * BW), layer),
                  _resident((1, LRU_W), layer),
                  _resident((1, LRU_W), layer),
                  _resident((1, LRU_W), layer),
                  _resident((V_W, D_MODEL), layer),
                  _resident((LRU_W, D_MODEL), layer),
                  _resident((D_MODEL, D_MODEL), layer),
                  _resident((1, D_MODEL), layer),
                  _resident((D_MODEL, 2 * D_FF), layer),
                  _resident((D_FF, D_MODEL), layer),
                  pl.BlockSpec((1, D_MODEL), lambda i: (0, 0), pipeline_mode=pl.Buffered(1))],
        out_specs=[pl.BlockSpec((ROW_TILE, D_MODEL), row),
                   pl.BlockSpec((nch, LRU_W), row)],
        out_shape=[jax.ShapeDtypeStruct((n, D_MODEL), F32),
                   jax.ShapeDtypeStruct((n // CHUNK, LRU_W), F32)],
        scratch_shapes=[pltpu.VMEM((ROW_TILE, D_FF), BF16),
                        pltpu.VMEM((8, LRU_W), F32),
                        pltpu.VMEM((ROW_TILE, LRU_W), BF16)],
        compiler_params=_params("arbitrary"),
        name="outproj_ffn",
    )(x, p, oa, p, h0, wri, br, bi, lam, wa, wb, wo, nw, wgu, wd, fn)


def _unit_lower_inverse(lows, c):
    ti = _row_time(lax.broadcasted_iota(jnp.int32, (c, c), 0))
    tj = _row_time(lax.broadcasted_iota(jnp.int32, (c, c), 1))
    same_block = (ti // INV_BASE) == (tj // INV_BASE)
    eye = (ti == tj).astype(F32)
    ld = [jnp.where(same_block, low, 0.0) for low in lows]
    lo = [low - d for low, d in zip(lows, ld)]
    x = [eye - d for d in ld]
    p = ld
    span = 2
    while span < INV_BASE:
        p = [_mm(t, t) for t in p]
        x = [a + _mm(a, t) for a, t in zip(x, p)]
        span *= 2
    e = [_mm(a, b) for a, b in zip(x, lo)]
    y = [eye - t for t in e]
    q = e
    span = 2
    while span < c // INV_BASE:
        q = [_mm(t, t) for t in q]
        y = [a + _mm(a, t) for a, t in zip(y, q)]
        span *= 2
    return [_mm(a, b) for a, b in zip(y, x)]


def _delta_kernel(geo, qkv_ref, z_ref, ba_ref, s0_ref, alog_ref, dtb_ref, onorm_ref,
                  oa_ref, sout_ref, s_scr):
    c = CHUNK
    i = pl.program_id(0)
    is_prompt = i < geo.prompt_steps
    pos = i % geo.cpp
    first = jnp.logical_or(jnp.logical_not(is_prompt), pos == 0)
    last = jnp.logical_or(jnp.logical_not(is_prompt), pos == geo.cpp - 1)
    null = jnp.where(first, jnp.where(is_prompt, PROMPT_NULL, SAMPLE_NULL), 0)
    valid = _valid_rows(null)

    @pl.when(jnp.logical_and(first, is_prompt))
    def _():
        s_scr[...] = jnp.zeros_like(s_scr)

    @pl.when(jnp.logical_not(is_prompt))
    def _():
        s_scr[...] = s0_ref[...]

    ti = _row_time(lax.broadcasted_iota(jnp.int32, (c, c), 0))
    tj = _row_time(lax.broadcasted_iota(jnp.int32, (c, c), 1))
    tri_incl = ti >= tj
    tri_strict = ti > tj

    beta_all, gc_all, gc_t, eg_all, ekd_all, egl_all = [], [], [], [], [], []
    for s in range(GROUP):
        ba = ba_ref[s * c:(s + 1) * c, :]
        beta_all.append(jnp.where(valid, _sigmoid(ba[:, :128]), 0.0))
        g = -jnp.exp(alog_ref[...]) * _softplus(ba[:, 128:] + dtb_ref[...])
        gc, g_last = _scan_time(None, jnp.where(valid, g, 0.0), jnp.zeros((1, 128), F32))
        gc_all.append(gc)
        gc_t.append(jnp.concatenate([gc, jnp.zeros((128 - c, 128), F32)], axis=0).T)
        eg_all.append(jnp.exp(gc))
        ekd_all.append(jnp.exp(g_last - gc))
        egl_all.append(jnp.exp(g_last))

    probs = [(s, h) for s in range(GROUP) for h in range(HA)]
    rows = lambda s: slice(s * c, (s + 1) * c)
    qb = [qkv_ref[rows(s), h * DK:(h + 1) * DK] for s, h in probs]
    kbf = [qkv_ref[rows(s), QK_W + h * DK:QK_W + (h + 1) * DK] for s, h in probs]
    vb = [qkv_ref[rows(s), 2 * QK_W + h * DV:2 * QK_W + (h + 1) * DV] for s, h in probs]
    q = [t.astype(F32) for t in qb]
    k = [t.astype(F32) for t in kbf]
    beta = [beta_all[s][:, h:h + 1] for s, h in probs]
    eg = [eg_all[s][:, h:h + 1] for s, h in probs]
    n = len(probs)
    kb = [k[p] * beta[p] for p in range(n)]
    sc = [_mm_nt(jnp.concatenate([kb[p].astype(BF16), qb[p]], axis=0), kbf[p]) for p in range(n)]
    decay = [jnp.where(tri_incl, jnp.exp(jnp.where(tri_incl, gc_all[s][:, h:h + 1] - gc_t[s][h:h + 1, :c], 0.0)), 0.0)
             for s, h in probs]
    low = [jnp.where(tri_strict, sc[p][:c] * decay[p], 0.0) for p in range(n)]
    a_intra = [sc[p][c:] * decay[p] for p in range(n)]
    t_inv = _unit_lower_inverse(low, c)
    uw = [_mm(t_inv[p], jnp.concatenate([vb[p].astype(F32) * beta[p], kb[p] * eg[p]], axis=1))
          for p in range(n)]
    s_old = [s_scr[s, h] for s, h in probs]
    wq = [_mm(jnp.concatenate([uw[p][:, DV:], q[p] * eg[p]], axis=0), s_old[p]) for p in range(n)]
    v_new = [uw[p][:, :DV] - wq[p][:c] for p in range(n)]
    o_intra = [_mm(a_intra[p], v_new[p]) for p in range(n)]
    ds = [_mm_tn(k[p] * ekd_all[s][:, h:h + 1], v_new[p]) for p, (s, h) in enumerate(probs)]
    for p, (s, h) in enumerate(probs):
        s_scr[s, h] = s_old[p] * egl_all[s][:, h:h + 1] + ds[p]
        o = wq[p][c:] + o_intra[p]
        zh = z_ref[rows(s), h * DV:(h + 1) * DV].astype(F32)
        o = o * lax.rsqrt(jnp.mean(o * o, axis=-1, keepdims=True) + EPS) * onorm_ref[...] * _silu(zh)
        oa_ref[rows(s), h * DV:(h + 1) * DV] = o.astype(oa_ref.dtype)

    @pl.when(last)
    def _():
        sout_ref[...] = s_scr[...]


def _delta_call(geo, layer, p, ba, s0, alog, dtb, onorm):
    n_pp = geo.n_prompt // GROUP

    def pair(i):
        return jnp.where(i < geo.prompt_steps, i // geo.cpp, i - geo.prompt_steps + n_pp)

    vec = lambda width: pl.BlockSpec((None, 1, width), lambda i: (layer, 0, 0))
    return pl.pallas_call(
        functools.partial(_delta_kernel, geo),
        grid=(geo.steps,),
        in_specs=[pl.BlockSpec((GROUP_ROWS, QKV_W), lambda i: (i, 0)),
                  pl.BlockSpec((GROUP_ROWS, V_W), lambda i: (i, P_Z // V_W)),
                  pl.BlockSpec((GROUP_ROWS, 256), lambda i: (i, 0)),
                  pl.BlockSpec((None, GROUP, HA, DK, DV),
                               lambda i: (layer, jnp.maximum(i - geo.prompt_steps, 0), 0, 0, 0)),
                  vec(128), vec(128), vec(DV)],
        out_specs=[pl.BlockSpec((GROUP_ROWS, V_W), lambda i: (i, 0)),
                   pl.BlockSpec((GROUP, HA, DK, DV), lambda i: (pair(i), 0, 0, 0))],
        out_shape=[jax.ShapeDtypeStruct((geo.rows, V_W), BF16),
                   jax.ShapeDtypeStruct((geo.n_streams, HA, DK, DV), F32)],
        scratch_shapes=[pltpu.VMEM((GROUP, HA, DK, DV), F32)],
        compiler_params=_params("arbitrary"),
        name="delta",
    )(p, p, ba, s0, alog, dtb, onorm)


def _permute_chunk(x):
    return jnp.swapaxes(x.reshape(SLAB, N_SLAB, x.shape[-1]), 0, 1).reshape(CHUNK, x.shape[-1])


def _pack_kernel(geo, xp_ref, xs_ref, meta_ref, o_ref):
    i = pl.program_id(0)
    is_prompt = i < geo.prompt_steps
    pos = i % geo.cpp
    width = o_ref.shape[1]

    @pl.when(jnp.logical_and(is_prompt, pos > 0))
    def _():
        for s in range(GROUP):
            o_ref[s * CHUNK:(s + 1) * CHUNK, :] = _permute_chunk(xp_ref[s].astype(o_ref.dtype))

    @pl.when(jnp.logical_and(is_prompt, pos == 0))
    def _():
        first = _permute_chunk(jnp.concatenate(
            [jnp.zeros((PROMPT_NULL, width), o_ref.dtype), meta_ref[...].astype(o_ref.dtype)], axis=0))
        for s in range(GROUP):
            o_ref[s * CHUNK:(s + 1) * CHUNK, :] = first

    @pl.when(jnp.logical_not(is_prompt))
    def _():
        for s in range(GROUP):
            o_ref[s * CHUNK:(s + 1) * CHUNK, :] = _permute_chunk(jnp.concatenate(
                [jnp.zeros((SAMPLE_NULL, width), o_ref.dtype), xs_ref[s].astype(o_ref.dtype)], axis=0))


def _unpack_kernel(geo, x_ref, yp_ref, ys_ref):
    i = pl.program_id(0)
    is_prompt = i < geo.prompt_steps
    pos = i % geo.cpp

    @pl.when(jnp.logical_and(is_prompt, pos > 0))
    def _():
        for s in range(GROUP):
            yp_ref[s] = _permute_chunk(x_ref[s * CHUNK:(s + 1) * CHUNK, :]).astype(yp_ref.dtype)

    @pl.when(jnp.logical_not(is_prompt))
    def _():
        for s in range(GROUP):
            ys_ref[s] = _permute_chunk(x_ref[s * CHUNK:(s + 1) * CHUNK, :])[SAMPLE_NULL:].astype(ys_ref.dtype)


def _layout_specs(geo):
    def prompt_idx(i):
        step = jnp.minimum(i, geo.prompt_steps - 1)
        return (step // geo.cpp, jnp.maximum(step % geo.cpp - 1, 0), 0)

    def sample_idx(i):
        return (jnp.maximum(i - geo.prompt_steps, 0), 0, 0)

    return (pl.BlockSpec((GROUP, CHUNK, D_MODEL), prompt_idx),
            pl.BlockSpec((GROUP, CHUNK - SAMPLE_NULL, D_MODEL), sample_idx))


def _pack_call(geo, x_prompt, x_sample, meta):
    prompt_spec, sample_spec = _layout_specs(geo)
    return pl.pallas_call(
        functools.partial(_pack_kernel, geo),
        grid=(geo.steps,),
        in_specs=[prompt_spec, sample_spec, pl.BlockSpec((N_META, D_MODEL), lambda i: (0, 0))],
        out_specs=pl.BlockSpec((GROUP_ROWS, D_MODEL), lambda i: (i, 0)),
        out_shape=jax.ShapeDtypeStruct((geo.rows, D_MODEL), F32),
        compiler_params=_params("parallel"),
        name="pack",
    )(x_prompt, x_sample, meta)


def _unpack_call(geo, x, dtype):
    prompt_spec, sample_spec = _layout_specs(geo)
    return pl.pallas_call(
        functools.partial(_unpack_kernel, geo),
        grid=(geo.steps,),
        in_specs=[pl.BlockSpec((GROUP_ROWS, D_MODEL), lambda i: (i, 0))],
        out_specs=[prompt_spec, sample_spec],
        out_shape=[jax.ShapeDtypeStruct((geo.n_prompt, (geo.cpp - 1) * CHUNK, D_MODEL), dtype),
                   jax.ShapeDtypeStruct((geo.n_sample, CHUNK - SAMPLE_NULL, D_MODEL), dtype)],
        compiler_params=_params("arbitrary"),
        name="unpack",
    )(x)


def _pad_lanes(v, width):
    return jnp.pad(v, ((0, 0), (0, width - v.shape[-1])))


def kernel(x_prompt, x_sample, state_delta_S, state_delta_conv, state_lru_h, state_lru_conv,
           meta_tokens, ffn1_norm, ffn1_w_gu, ffn1_w_down, mix_norm, w_in, delta_conv_w,
           delta_A_log, delta_dt_bias, delta_out_norm, lru_conv_w, lru_conv_b, lru_w_r, lru_b_r,
           lru_w_i, lru_b_i, lru_lambda, w_branch_a, w_branch_b, w_out, ffn2_norm, ffn2_w_gu,
           ffn2_w_down, final_norm):
    n_prompt, seq_len, _ = x_prompt.shape
    n_sample, dec_len, _ = x_sample.shape
    assert dec_len == CHUNK - SAMPLE_NULL and (N_META + seq_len) % CHUNK == N_META
    prompt_rows = PROMPT_NULL + N_META + seq_len
    cpp = prompt_rows // CHUNK
    geo = _Geometry(n_prompt, n_sample, cpp)
    assert geo.rows % ROW_TILE == 0 and (geo.prompt_chunks * CHUNK) % ROW_TILE == 0
    assert INPROJ_TILE % GROUP_ROWS == 0 and ROW_TILE % GROUP_ROWS == 0
    dt = x_prompt.dtype

    x = _pack_call(geo, x_prompt, x_sample, meta_tokens)

    cast = lambda w: w.astype(BF16)
    vec = lambda v: v.astype(F32)[:, None, :]
    wgu1, wd1, wgu2, wd2 = cast(ffn1_w_gu), cast(ffn1_w_down), cast(ffn2_w_gu), cast(ffn2_w_down)
    w_in_t = cast(jnp.swapaxes(w_in, 1, 2))
    w_ri = cast(jnp.concatenate([lru_w_r, lru_w_i], axis=-1))
    wa, wb, wo = cast(w_branch_a), cast(w_branch_b), cast(w_out)
    norm1, norm_mix, norm2 = vec(ffn1_norm), vec(mix_norm), vec(ffn2_norm)
    alog, dtb, onorm = vec(_pad_lanes(delta_A_log, 128)), vec(_pad_lanes(delta_dt_bias, 128)), vec(delta_out_norm)
    lcb, b_r, b_i, lam = vec(lru_conv_b), vec(lru_b_r), vec(lru_b_i), vec(lru_lambda)
    cw, lcw = delta_conv_w.astype(F32), lru_conv_w.astype(F32)
    s0, h0 = state_delta_S.astype(F32), state_lru_h.astype(F32)

    inj = jnp.concatenate([state_delta_conv, state_lru_conv], axis=-1).astype(F32)
    null_runs = SAMPLE_NULL // N_SLAB
    inj = jnp.pad(inj[:, :, :, None, :], ((0, 0), (0, 0), (0, 0), (null_runs - 1, SLAB - null_runs), (0, 0)))
    inj = inj.reshape(DEPTH, n_sample * TAIL_ROWS, CONV_COLS)

    last_ids = geo.last_chunk_ids()
    outs_s, outs_tail, outs_h = [], [], []
    for l in range(DEPTH):
        x = _ffn_call(l, x, norm1, wgu1, wd1)
        p, ba, tails = _inproj_call(geo, l, x, norm_mix, w_in_t, cw, lcw, lcb, inj)
        oa, s_new = _delta_call(geo, l, p, ba, s0, alog, dtb, onorm)
        x, hlast = _outproj_call(geo, l, x, p, oa, h0, w_ri, b_r, b_i, lam, wa, wb, wo, norm2, wgu2, wd2,
                                 final_norm[None].astype(F32), final=(l == DEPTH - 1))
        outs_s.append(s_new)
        outs_tail.append(tails[last_ids, SLAB - 1::SLAB, :])
        outs_h.append(hlast[last_ids])

    n_prompt_rows = geo.prompt_chunks * CHUNK
    y_prompt, y_sample = _unpack_call(geo, x, dt)
    s_all = jnp.stack(outs_s)
    tail_all = jnp.stack(outs_tail)
    cq_all = tail_all[..., :QKV_W]
    cx_all = tail_all[..., QKV_W:]
    h_all = jnp.stack(outs_h)
    return (y_prompt.astype(dt), y_sample.astype(dt),
            s_all[:, :n_prompt].astype(dt), cq_all[:, :n_prompt].astype(dt),
            h_all[:, :n_prompt].astype(dt), cx_all[:, :n_prompt].astype(dt),
            s_all[:, n_prompt:].astype(state_delta_S.dtype),
            cq_all[:, n_prompt:].astype(state_delta_conv.dtype),
            h_all[:, n_prompt:].astype(state_lru_h.dtype),
            cx_all[:, n_prompt:].astype(state_lru_conv.dtype))
```

```python
import functools

import numpy as np
import jax
import jax.numpy as jnp
from jax import lax
from jax.experimental import pallas as pl
from jax.experimental.pallas import tpu as pltpu

F32 = jnp.float32
BF16 = jnp.bfloat16

D_MODEL = 1024
DEPTH = 4
N_META = 16
HA = 8
DK = 128
DV = 128
QK_W = HA * DK
V_W = HA * DV
QKV_W = 2 * QK_W + V_W
LRU_W = D_MODEL
NB = 8
BW = 128
CONV_W = 4
LRU_C = 8.0
D_FF = 2816
EPS = 1e-6

OFF_Z = QKV_W
OFF_BETA = OFF_Z + V_W
OFF_ALPHA = OFF_BETA + HA
OFF_LX = OFF_ALPHA + HA
OFF_LY = OFF_LX + LRU_W
OFF_GA = OFF_LY + LRU_W
OFF_GB = OFF_GA + D_MODEL
IN_COLS = OFF_GB + D_MODEL

CHUNK = 64
GROUP = 4
GROUP_ROWS = GROUP * CHUNK
PROMPT_NULL = CHUNK - N_META
SAMPLE_NULL = CHUNK // 2
INV_BASE = 16
ROW_TILE = 512
INPROJ_TILE = 512
FF_HALF = D_FF // 2
CONV_COLS = QKV_W + LRU_W
SLAB = 8
N_SLAB = CHUNK // SLAB
TAIL_ROWS = (CONV_W - 1) * SLAB
P_Z, P_X, P_Y, P_G = 3072, 4096, 5120, 6144
P_COLS = 8192
VMEM_LIMIT = 56 * 1024 * 1024


def _rms(x, w):
    return x * lax.rsqrt(jnp.mean(x * x, axis=-1, keepdims=True) + EPS) * w


def _mm(a, b):
    return jnp.dot(a.astype(BF16), b.astype(BF16), preferred_element_type=F32)


def _mm_nt(a, b):
    return lax.dot_general(a.astype(BF16), b.astype(BF16), (((1,), (1,)), ((), ())),
                           preferred_element_type=F32)


def _mm_tn(a, b):
    return lax.dot_general(a.astype(BF16), b.astype(BF16), (((0,), (0,)), ((), ())),
                           preferred_element_type=F32)


def _sigmoid(x):
    return 0.5 * (1.0 + jnp.tanh(0.5 * x))


def _silu(x):
    return x * _sigmoid(x)


def _softplus(x):
    return jnp.maximum(x, 0.0) + jnp.log1p(jnp.exp(-jnp.abs(x)))


def _gelu_tanh(x):
    return 0.5 * x * (1.0 + jnp.tanh(0.7978845608028654 * (x + 0.044715 * (x * x * x))))


class _Geometry:
    def __init__(self, n_prompt, n_sample, chunks_per_prompt):
        assert n_prompt % GROUP == 0 and n_sample % GROUP == 0
        self.n_prompt = n_prompt
        self.n_sample = n_sample
        self.cpp = chunks_per_prompt
        self.prompt_steps = (n_prompt // GROUP) * chunks_per_prompt
        self.steps = self.prompt_steps + n_sample // GROUP
        self.prompt_chunks = GROUP * self.prompt_steps
        self.chunks = GROUP * self.steps
        self.rows = self.chunks * CHUNK
        self.n_streams = n_prompt + n_sample

    def chunk_null(self, cid):
        is_prompt = cid < self.prompt_chunks
        first = jnp.logical_or(jnp.logical_not(is_prompt), (cid // GROUP) % self.cpp == 0)
        return jnp.where(first, jnp.where(is_prompt, PROMPT_NULL, SAMPLE_NULL), 0)

    def last_chunk_ids(self):
        ids = [GROUP * ((b // GROUP) * self.cpp + self.cpp - 1) + b % GROUP for b in range(self.n_prompt)]
        ids += [self.prompt_chunks + t for t in range(self.n_sample)]
        return np.asarray(ids, np.int32)


def _row_time(rows):
    return N_SLAB * (rows % SLAB) + rows // SLAB


def _valid_rows(null):
    return _row_time(lax.broadcasted_iota(jnp.int32, (CHUNK, 1), 0)) >= null


def _slabs(x):
    return [x[v * SLAB:(v + 1) * SLAB] for v in range(N_SLAB)]


def _shift_run(x, fill):
    return jnp.concatenate([fill, x[:SLAB - 1]], axis=0)


def _scan_time(a, b, h_in):
    bs = _slabs(b)
    if a is None:
        for v in range(1, N_SLAB):
            bs[v] = bs[v] + bs[v - 1]
        run = bs[N_SLAB - 1]
        d = 1
        while d < SLAB:
            run = run + _shift_rows(run, d, 0.0)
            d *= 2
        run = run + h_in
        prev = _shift_run(run, h_in)
        hs = [t + prev for t in bs]
    else:
        as_ = _slabs(a)
        for v in range(1, N_SLAB):
            bs[v] = bs[v] + as_[v] * bs[v - 1]
            as_[v] = as_[v] * as_[v - 1]
        ra, rb = as_[N_SLAB - 1], bs[N_SLAB - 1]
        d = 1
        while d < SLAB:
            rb = rb + ra * _shift_rows(rb, d, 0.0)
            ra = ra * _shift_rows(ra, d, 1.0)
            d *= 2
        run = rb + ra * h_in
        prev = _shift_run(run, h_in)
        hs = [t + u * prev for t, u in zip(bs, as_)]
    return jnp.concatenate(hs, axis=0), run[SLAB - 1:SLAB, :]


def _ffn(x, nw, wgu_ref, wd_ref, h_ref):
    xn = _rms(x, nw).astype(BF16)
    for c in range(2):
        lo = c * FF_HALF
        g = jnp.dot(xn, wgu_ref[:, lo:lo + FF_HALF], preferred_element_type=F32)
        u = jnp.dot(xn, wgu_ref[:, D_FF + lo:D_FF + lo + FF_HALF], preferred_element_type=F32)
        h_ref[:, lo:lo + FF_HALF] = (_silu(g) * u).astype(BF16)
    y = jnp.dot(h_ref[...], wd_ref[...], preferred_element_type=F32)
    return x + 0.5 * y


def _ffn_kernel(x_ref, nw_ref, wgu_ref, wd_ref, o_ref, h_ref):
    o_ref[...] = _ffn(x_ref[...], nw_ref[...], wgu_ref, wd_ref, h_ref)


def _inproj_kernel(geo, x_ref, nw_ref, wt_ref, cw_ref, lcw_ref, lcb_ref, inj_ref,
                   p_ref, ba_ref, tails_ref, stage_all, carry_scr):
    i = pl.program_id(0)
    nch = INPROJ_TILE // CHUNK
    xn = _rms(x_ref[...], nw_ref[...]).astype(BF16)

    def project_cols(lo):
        src = lo if lo < OFF_BETA else lo + OFF_LX - OFF_BETA
        return lax.dot_general(xn, wt_ref[src:src + 1024, :], (((1,), (1,)), ((), ())),
                               preferred_element_type=F32)

    @pl.when(i == 0)
    def _():
        carry_scr[...] = jnp.zeros_like(carry_scr)

    valid = [_valid_rows(geo.chunk_null(i * nch + j)) for j in range(nch)]
    in_sample = i >= geo.prompt_chunks * CHUNK // INPROJ_TILE
    kinds = (("q", 0, 0), ("k", QK_W, QK_W), ("v", 2 * QK_W, 2 * QK_W), ("x", P_X, QKV_W))
    t0 = TAIL_ROWS

    def project(kidx):
        kind, lo, col = kinds[kidx]
        stage = stage_all.at[kidx % 2]
        pre = project_cols(lo)
        for j in range(nch):
            stage[j, t0:t0 + CHUNK, :] = pre[j * CHUNK:(j + 1) * CHUNK]
            if j >= GROUP:
                stage[j, 0:t0, :] = pre[(j - GROUP + 1) * CHUNK - t0:(j - GROUP + 1) * CHUNK]
            else:
                stage[j, 0:t0, :] = carry_scr[j, :, col:col + 1024]
            stage[j, CHUNK:CHUNK + t0, :] = stage[j, CHUNK:CHUNK + t0, :] + jnp.where(
                in_sample, inj_ref[t0 * j:t0 * (j + 1), col:col + 1024], 0.0)
            for k in range(CONV_W - 1):
                last = CHUNK + (k + 1) * SLAB - 1
                tails_ref[j, k:k + 1, col:col + 1024] = stage[j, last:last + 1, :]
        for g in range(GROUP):
            carry_scr[g, :, col:col + 1024] = stage[nch - GROUP + g, CHUNK:CHUNK + t0, :]

    def convolve(kidx, j):
        kind, lo, col = kinds[kidx]
        stage = stage_all.at[kidx % 2]
        taps = lcw_ref[...] if kind == "x" else cw_ref[:, lo:lo + 1024]
        cur = [stage[j, t0 + v * SLAB:t0 + (v + 1) * SLAB, :] for v in range(N_SLAB)]
        back = []
        for k in range(CONV_W - 1):
            prev_last = stage[j, (k + 1) * SLAB - 1:(k + 1) * SLAB, :]
            back.append(_shift_run(cur[N_SLAB - (CONV_W - 1) + k], prev_last))
        conv = []
        for v in range(N_SLAB):
            acc = cur[v] * taps[CONV_W - 1:CONV_W, :]
            for d in range(1, CONV_W):
                src = cur[v - d] if v >= d else back[CONV_W - 1 - d + v]
                acc = acc + src * taps[CONV_W - 1 - d:CONV_W - d, :]
            conv.append(acc)
        conv = jnp.concatenate(conv, axis=0)
        if kind == "x":
            out = conv + lcb_ref[...]
        else:
            act = _silu(conv)
            if kind != "v":
                scale = DK ** -0.5 if kind == "q" else 1.0
                segs = []
                for h in range(HA):
                    seg = act[:, h * DK:(h + 1) * DK]
                    segs.append(seg * (lax.rsqrt(jnp.sum(seg * seg, axis=-1, keepdims=True) + EPS) * scale))
                act = jnp.concatenate(segs, axis=1)
            out = jnp.where(valid[j], act, 0.0)
        p_ref[j * CHUNK:(j + 1) * CHUNK, lo:lo + 1024] = out.astype(BF16)

    def plain(lo):
        p_ref[:, lo:lo + 1024] = project_cols(lo).astype(BF16)

    plain_cols = (P_Z, P_Y, P_G, P_G + 1024)
    project(0)
    for kidx in range(len(kinds)):
        plain(plain_cols[kidx])
        if kidx + 1 < len(kinds):
            project(kidx + 1)
        for j in range(nch):
            convolve(kidx, j)
    raw = lax.dot_general(xn, wt_ref[OFF_BETA:OFF_BETA + 128, :], (((1,), (1,)), ((), ())),
                          preferred_element_type=F32)
    head_lanes = lax.broadcasted_iota(jnp.int32, raw.shape, 1) < HA
    ba_ref[:, :128] = jnp.where(head_lanes, raw, 0.0)
    ba_ref[:, 128:] = jnp.where(head_lanes, pltpu.roll(raw, 128 - HA, 1), 0.0)


def _shift_rows(x, d, fill):
    if d % 8 == 0:
        return jnp.concatenate([jnp.full((d, x.shape[1]), fill, x.dtype), x[:x.shape[0] - d]], axis=0)
    rows = lax.broadcasted_iota(jnp.int32, x.shape, 0)
    return jnp.where(rows >= d, pltpu.roll(x, d, 0), fill)


def _outproj_kernel(geo, final, x_ref, g_ref, oa_ref, xl_ref, h0_ref,
                    wri_ref, br_ref, bi_ref, lam_ref, wa_ref, wb_ref, wo_ref,
                    nw_ref, wgu_ref, wd_ref, fn_ref,
                    o_ref, hlast_ref, h_ref, hc_scr, ob_scr):
    i = pl.program_id(0)
    nch = ROW_TILE // CHUNK

    @pl.when(i == 0)
    def _():
        hc_scr[...] = jnp.zeros_like(hc_scr)

    sp = _softplus(-lam_ref[...])
    nulls = [geo.chunk_null(i * nch + j) for j in range(nch)]
    valid = [_valid_rows(n) for n in nulls]
    is_sample = [(i * nch + j) >= geo.prompt_chunks for j in range(nch)]
    for n in range(NB):
        sl = slice(n * BW, (n + 1) * BW)
        xb = xl_ref[:, sl]
        gates = jnp.dot(xb, wri_ref[n], preferred_element_type=F32)
        r = _sigmoid(gates[:, :BW] + br_ref[:, sl])
        ig = _sigmoid(gates[:, BW:] + bi_ref[:, sl])
        log_a = -LRU_C * r * sp[:, sl]
        a_all = jnp.exp(log_a)
        th = jnp.tanh(log_a)
        mult = jnp.sqrt(-2.0 * th / (1.0 - th))
        b_all = mult * (ig * xb.astype(F32))
        gy = _gelu_tanh(xl_ref[:, LRU_W + n * BW:LRU_W + (n + 1) * BW].astype(F32))
        carry = [hc_scr[g:g + 1, sl] for g in range(GROUP)]
        for j in range(nch):
            rows = slice(j * CHUNK, (j + 1) * CHUNK)
            a = jnp.where(valid[j], a_all[rows], 1.0)
            b = jnp.where(valid[j], b_all[rows], 0.0)
            h_in = jnp.where(nulls[j] > 0, jnp.where(is_sample[j], h0_ref[j:j + 1, sl], 0.0), carry[j % GROUP])
            hs, carry[j % GROUP] = _scan_time(a, b, h_in)
            ob_scr[rows, sl] = (hs * gy[rows]).astype(BF16)
            hlast_ref[j:j + 1, sl] = carry[j % GROUP]
        for g in range(GROUP):
            hc_scr[g:g + 1, sl] = carry[g]

    ga = _sigmoid(g_ref[:, :D_MODEL].astype(F32))
    gb = _sigmoid(g_ref[:, D_MODEL:].astype(F32))
    ma = jnp.dot(oa_ref[...], wa_ref[...], preferred_element_type=F32)
    mb = jnp.dot(ob_scr[...], wb_ref[...], preferred_element_type=F32)
    m = (ga * ma + gb * mb).astype(BF16)
    x = x_ref[...] + jnp.dot(m, wo_ref[...], preferred_element_type=F32)
    x = _ffn(x, nw_ref[...], wgu_ref, wd_ref, h_ref)
    if final:
        x = _rms(x, fn_ref[...])
    o_ref[...] = x


def _resident(shape, layer):
    nd = len(shape)
    return pl.BlockSpec((None,) + shape, lambda i: (layer,) + (0,) * nd, pipeline_mode=pl.Buffered(1))


def _params(semantics):
    return pltpu.CompilerParams(dimension_semantics=(semantics,), vmem_limit_bytes=VMEM_LIMIT)


def _ffn_call(layer, x, nw, wgu, wd):
    n = x.shape[0]
    row = lambda i: (i, 0)
    return pl.pallas_call(
        _ffn_kernel,
        grid=(n // ROW_TILE,),
        in_specs=[pl.BlockSpec((ROW_TILE, D_MODEL), row),
                  _resident((1, D_MODEL), layer),
                  _resident((D_MODEL, 2 * D_FF), layer),
                  _resident((D_FF, D_MODEL), layer)],
        out_specs=pl.BlockSpec((ROW_TILE, D_MODEL), row),
        out_shape=jax.ShapeDtypeStruct((n, D_MODEL), F32),
        scratch_shapes=[pltpu.VMEM((ROW_TILE, D_FF), BF16)],
        compiler_params=_params("parallel"),
        name="ffn",
    )(x, nw, wgu, wd)


def _inproj_call(geo, layer, x, nw, wt, cw, lcw, lcb, inj):
    n = x.shape[0]
    nch = INPROJ_TILE // CHUNK
    sample_tile0 = geo.prompt_chunks * CHUNK // INPROJ_TILE
    row = lambda i: (i, 0)
    assert OFF_BETA == P_Z + V_W and P_COLS == IN_COLS - (OFF_LX - OFF_BETA)
    return pl.pallas_call(
        functools.partial(_inproj_kernel, geo),
        grid=(n // INPROJ_TILE,),
        in_specs=[pl.BlockSpec((INPROJ_TILE, D_MODEL), row),
                  _resident((1, D_MODEL), layer),
                  _resident((IN_COLS, D_MODEL), layer),
                  _resident((CONV_W, QKV_W), layer),
                  _resident((CONV_W, LRU_W), layer),
                  _resident((1, LRU_W), layer),
                  pl.BlockSpec((None, TAIL_ROWS * nch, CONV_COLS),
                               lambda i: (layer, jnp.maximum(i - sample_tile0, 0), 0),
                               pipeline_mode=pl.Buffered(1))],
        out_specs=[pl.BlockSpec((INPROJ_TILE, P_COLS), row),
                   pl.BlockSpec((INPROJ_TILE, 256), row),
                   pl.BlockSpec((nch, CONV_W - 1, CONV_COLS), lambda i: (i, 0, 0))],
        out_shape=[jax.ShapeDtypeStruct((n, P_COLS), BF16),
                   jax.ShapeDtypeStruct((n, 256), F32),
                   jax.ShapeDtypeStruct((n // CHUNK, CONV_W - 1, CONV_COLS), F32)],
        scratch_shapes=[pltpu.VMEM((2, nch, CHUNK + TAIL_ROWS, 1024), F32),
                        pltpu.VMEM((GROUP, TAIL_ROWS, CONV_COLS), F32)],
        compiler_params=_params("arbitrary"),
        name="inproj",
    )(x, nw, wt, cw, lcw, lcb, inj)


def _outproj_call(geo, layer, x, p, oa, h0, wri, br, bi, lam, wa, wb, wo, nw, wgu, wd, fn, final):
    n = x.shape[0]
    nch = ROW_TILE // CHUNK
    sample_tile0 = geo.prompt_chunks * CHUNK // ROW_TILE
    row = lambda i: (i, 0)
    return pl.pallas_call(
        functools.partial(_outproj_kernel, geo, final),
        grid=(n // ROW_TILE,),
        in_specs=[pl.BlockSpec((ROW_TILE, D_MODEL), row),
                  pl.BlockSpec((ROW_TILE, 2 * D_MODEL), lambda i: (i, P_G // 2048)),
                  pl.BlockSpec((ROW_TILE, V_W), row),
                  pl.BlockSpec((ROW_TILE, 2 * LRU_W), lambda i: (i, P_X // 2048)),
                  pl.BlockSpec((None, nch, LRU_W), lambda i: (layer, jnp.maximum(i - sample_tile0, 0), 0)),
                  _resident((NB, BW, 2 * BW), layer),
                  _resident((1, LRU_W), layer),
                  _resident((1, LRU_W), layer),
                  _resident((1, LRU_W), layer),
                  _resident((V_W, D_MODEL), layer),
                  _resident((LRU_W, D_MODEL), layer),
                  _resident((D_MODEL, D_MODEL), layer),
                  _resident((1, D_MODEL), layer),
                  _resident((D_MODEL, 2 * D_FF), layer),
                  _resident((D_FF, D_MODEL), layer),
                  pl.BlockSpec((1, D_MODEL), lambda i: (0, 0), pipeline_mode=pl.Buffered(1))],
        out_specs=[pl.BlockSpec((ROW_TILE, D_MODEL), row),
                   pl.BlockSpec((nch, LRU_W), row)],
        out_shape=[jax.ShapeDtypeStruct((n, D_MODEL), F32),
                   jax.ShapeDtypeStruct((n // CHUNK, LRU_W), F32)],
        scratch_shapes=[pltpu.VMEM((ROW_TILE, D_FF), BF16),
                        pltpu.VMEM((8, LRU_W), F32),
                        pltpu.VMEM((ROW_TILE, LRU_W), BF16)],
        compiler_params=_params("arbitrary"),
        name="outproj_ffn",
    )(x, p, oa, p, h0, wri, br, bi, lam, wa, wb, wo, nw, wgu, wd, fn)


def _unit_lower_inverse(lows, c):
    ti = _row_time(lax.broadcasted_iota(jnp.int32, (c, c), 0))
    tj = _row_time(lax.broadcasted_iota(jnp.int32, (c, c), 1))
    same_block = (ti // INV_BASE) == (tj // INV_BASE)
    eye = (ti == tj).astype(F32)
    ld = [jnp.where(same_block, low, 0.0) for low in lows]
    lo = [low - d for low, d in zip(lows, ld)]
    x = [eye - d for d in ld]
    p = ld
    span = 2
    while span < INV_BASE:
        p = [_mm(t, t) for t in p]
        x = [a + _mm(a, t) for a, t in zip(x, p)]
        span *= 2
    e = [_mm(a, b) for a, b in zip(x, lo)]
    y = [eye - t for t in e]
    q = e
    span = 2
    while span < c // INV_BASE:
        q = [_mm(t, t) for t in q]
        y = [a + _mm(a, t) for a, t in zip(y, q)]
        span *= 2
    return [_mm(a, b) for a, b in zip(y, x)]


def _delta_kernel(geo, qkv_ref, z_ref, ba_ref, s0_ref, alog_ref, dtb_ref, onorm_ref,
                  oa_ref, sout_ref, s_scr):
    c = CHUNK
    i = pl.program_id(0)
    is_prompt = i < geo.prompt_steps
    pos = i % geo.cpp
    first = jnp.logical_or(jnp.logical_not(is_prompt), pos == 0)
    last = jnp.logical_or(jnp.logical_not(is_prompt), pos == geo.cpp - 1)
    null = jnp.where(first, jnp.where(is_prompt, PROMPT_NULL, SAMPLE_NULL), 0)
    valid = _valid_rows(null)

    @pl.when(jnp.logical_and(first, is_prompt))
    def _():
        s_scr[...] = jnp.zeros_like(s_scr)

    @pl.when(jnp.logical_not(is_prompt))
    def _():
        s_scr[...] = s0_ref[...]

    ti = _row_time(lax.broadcasted_iota(jnp.int32, (c, c), 0))
    tj = _row_time(lax.broadcasted_iota(jnp.int32, (c, c), 1))
    tri_incl = ti >= tj
    tri_strict = ti > tj

    beta_all, gc_all, gc_t, eg_all, ekd_all, egl_all = [], [], [], [], [], []
    for s in range(GROUP):
        ba = ba_ref[s * c:(s + 1) * c, :]
        beta_all.append(jnp.where(valid, _sigmoid(ba[:, :128]), 0.0))
        g = -jnp.exp(alog_ref[...]) * _softplus(ba[:, 128:] + dtb_ref[...])
        gc, g_last = _scan_time(None, jnp.where(valid, g, 0.0), jnp.zeros((1, 128), F32))
        gc_all.append(gc)
        gc_t.append(jnp.concatenate([gc, jnp.zeros((128 - c, 128), F32)], axis=0).T)
        eg_all.append(jnp.exp(gc))
        ekd_all.append(jnp.exp(g_last - gc))
        egl_all.append(jnp.exp(g_last))

    probs = [(s, h) for s in range(GROUP) for h in range(HA)]
    rows = lambda s: slice(s * c, (s + 1) * c)
    qb = [qkv_ref[rows(s), h * DK:(h + 1) * DK] for s, h in probs]
    kbf = [qkv_ref[rows(s), QK_W + h * DK:QK_W + (h + 1) * DK] for s, h in probs]
    vb = [qkv_ref[rows(s), 2 * QK_W + h * DV:2 * QK_W + (h + 1) * DV] for s, h in probs]
    q = [t.astype(F32) for t in qb]
    k = [t.astype(F32) for t in kbf]
    beta = [beta_all[s][:, h:h + 1] for s, h in probs]
    eg = [eg_all[s][:, h:h + 1] for s, h in probs]
    n = len(probs)
    kb = [k[p] * beta[p] for p in range(n)]
    sc = [_mm_nt(jnp.concatenate([kb[p].astype(BF16), qb[p]], axis=0), kbf[p]) for p in range(n)]
    decay = [jnp.where(tri_incl, jnp.exp(jnp.where(tri_incl, gc_all[s][:, h:h + 1] - gc_t[s][h:h + 1, :c], 0.0)), 0.0)
             for s, h in probs]
    low = [jnp.where(tri_strict, sc[p][:c] * decay[p], 0.0) for p in range(n)]
    a_intra = [sc[p][c:] * decay[p] for p in range(n)]
    t_inv = _unit_lower_inverse(low, c)
    uw = [_mm(t_inv[p], jnp.concatenate([vb[p].astype(F32) * beta[p], kb[p] * eg[p]], axis=1))
          for p in range(n)]
    s_old = [s_scr[s, h] for s, h in probs]
    wq = [_mm(jnp.concatenate([uw[p][:, DV:], q[p] * eg[p]], axis=0), s_old[p]) for p in range(n)]
    v_new = [uw[p][:, :DV] - wq[p][:c] for p in range(n)]
    o_intra = [_mm(a_intra[p], v_new[p]) for p in range(n)]
    ds = [_mm_tn(k[p] * ekd_all[s][:, h:h + 1], v_new[p]) for p, (s, h) in enumerate(probs)]
    for p, (s, h) in enumerate(probs):
        s_scr[s, h] = s_old[p] * egl_all[s][:, h:h + 1] + ds[p]
        o = wq[p][c:] + o_intra[p]
        zh = z_ref[rows(s), h * DV:(h + 1) * DV].astype(F32)
        o = o * lax.rsqrt(jnp.mean(o * o, axis=-1, keepdims=True) + EPS) * onorm_ref[...] * _silu(zh)
        oa_ref[rows(s), h * DV:(h + 1) * DV] = o.astype(oa_ref.dtype)

    @pl.when(last)
    def _():
        sout_ref[...] = s_scr[...]


def _delta_call(geo, layer, p, ba, s0, alog, dtb, onorm):
    n_pp = geo.n_prompt // GROUP

    def pair(i):
        return jnp.where(i < geo.prompt_steps, i // geo.cpp, i - geo.prompt_steps + n_pp)

    vec = lambda width: pl.BlockSpec((None, 1, width), lambda i: (layer, 0, 0))
    return pl.pallas_call(
        functools.partial(_delta_kernel, geo),
        grid=(geo.steps,),
        in_specs=[pl.BlockSpec((GROUP_ROWS, QKV_W), lambda i: (i, 0)),
                  pl.BlockSpec((GROUP_ROWS, V_W), lambda i: (i, P_Z // V_W)),
                  pl.BlockSpec((GROUP_ROWS, 256), lambda i: (i, 0)),
                  pl.BlockSpec((None, GROUP, HA, DK, DV),
                               lambda i: (layer, jnp.maximum(i - geo.prompt_steps, 0), 0, 0, 0)),
                  vec(128), vec(128), vec(DV)],
        out_specs=[pl.BlockSpec((GROUP_ROWS, V_W), lambda i: (i, 0)),
                   pl.BlockSpec((GROUP, HA, DK, DV), lambda i: (pair(i), 0, 0, 0))],
        out_shape=[jax.ShapeDtypeStruct((geo.rows, V_W), BF16),
                   jax.ShapeDtypeStruct((geo.n_streams, HA, DK, DV), F32)],
        scratch_shapes=[pltpu.VMEM((GROUP, HA, DK, DV), F32)],
        compiler_params=_params("arbitrary"),
        name="delta",
    )(p, p, ba, s0, alog, dtb, onorm)


def _permute_chunk(x):
    return jnp.swapaxes(x.reshape(SLAB, N_SLAB, x.shape[-1]), 0, 1).reshape(CHUNK, x.shape[-1])


def _pack_kernel(geo, xp_ref, xs_ref, meta_ref, o_ref):
    i = pl.program_id(0)
    is_prompt = i < geo.prompt_steps
    pos = i % geo.cpp
    width = o_ref.shape[1]

    @pl.when(jnp.logical_and(is_prompt, pos > 0))
    def _():
        for s in range(GROUP):
            o_ref[s * CHUNK:(s + 1) * CHUNK, :] = _permute_chunk(xp_ref[s].astype(o_ref.dtype))

    @pl.when(jnp.logical_and(is_prompt, pos == 0))
    def _():
        first = _permute_chunk(jnp.concatenate(
            [jnp.zeros((PROMPT_NULL, width), o_ref.dtype), meta_ref[...].astype(o_ref.dtype)], axis=0))
        for s in range(GROUP):
            o_ref[s * CHUNK:(s + 1) * CHUNK, :] = first

    @pl.when(jnp.logical_not(is_prompt))
    def _():
        for s in range(GROUP):
            o_ref[s * CHUNK:(s + 1) * CHUNK, :] = _permute_chunk(jnp.concatenate(
                [jnp.zeros((SAMPLE_NULL, width), o_ref.dtype), xs_ref[s].astype(o_ref.dtype)], axis=0))


def _unpack_kernel(geo, x_ref, yp_ref, ys_ref):
    i = pl.program_id(0)
    is_prompt = i < geo.prompt_steps
    pos = i % geo.cpp

    @pl.when(jnp.logical_and(is_prompt, pos > 0))
    def _():
        for s in range(GROUP):
            yp_ref[s] = _permute_chunk(x_ref[s * CHUNK:(s + 1) * CHUNK, :]).astype(yp_ref.dtype)

    @pl.when(jnp.logical_not(is_prompt))
    def _():
        for s in range(GROUP):
            ys_ref[s] = _permute_chunk(x_ref[s * CHUNK:(s + 1) * CHUNK, :])[SAMPLE_NULL:].astype(ys_ref.dtype)


def _layout_specs(geo):
    def prompt_idx(i):
        step = jnp.minimum(i, geo.prompt_steps - 1)
        return (step // geo.cpp, jnp.maximum(step % geo.cpp - 1, 0), 0)

    def sample_idx(i):
        return (jnp.maximum(i - geo.prompt_steps, 0), 0, 0)

    return (pl.BlockSpec((GROUP, CHUNK, D_MODEL), prompt_idx),
            pl.BlockSpec((GROUP, CHUNK - SAMPLE_NULL, D_MODEL), sample_idx))


def _pack_call(geo, x_prompt, x_sample, meta):
    prompt_spec, sample_spec = _layout_specs(geo)
    return pl.pallas_call(
        functools.partial(_pack_kernel, geo),
        grid=(geo.steps,),
        in_specs=[prompt_spec, sample_spec, pl.BlockSpec((N_META, D_MODEL), lambda i: (0, 0))],
        out_specs=pl.BlockSpec((GROUP_ROWS, D_MODEL), lambda i: (i, 0)),
        out_shape=jax.ShapeDtypeStruct((geo.rows, D_MODEL), F32),
        compiler_params=_params("parallel"),
        name="pack",
    )(x_prompt, x_sample, meta)


def _unpack_call(geo, x, dtype):
    prompt_spec, sample_spec = _layout_specs(geo)
    return pl.pallas_call(
        functools.partial(_unpack_kernel, geo),
        grid=(geo.steps,),
        in_specs=[pl.BlockSpec((GROUP_ROWS, D_MODEL), lambda i: (i, 0))],
        out_specs=[prompt_spec, sample_spec],
        out_shape=[jax.ShapeDtypeStruct((geo.n_prompt, (geo.cpp - 1) * CHUNK, D_MODEL), dtype),
                   jax.ShapeDtypeStruct((geo.n_sample, CHUNK - SAMPLE_NULL, D_MODEL), dtype)],
        compiler_params=_params("arbitrary"),
        name="unpack",
    )(x)


def _pad_lanes(v, width):
    return jnp.pad(v, ((0, 0), (0, width - v.shape[-1])))


def kernel(x_prompt, x_sample, state_delta_S, state_delta_conv, state_lru_h, state_lru_conv,
           meta_tokens, ffn1_norm, ffn1_w_gu, ffn1_w_down, mix_norm, w_in, delta_conv_w,
           delta_A_log, delta_dt_bias, delta_out_norm, lru_conv_w, lru_conv_b, lru_w_r, lru_b_r,
           lru_w_i, lru_b_i, lru_lambda, w_branch_a, w_branch_b, w_out, ffn2_norm, ffn2_w_gu,
           ffn2_w_down, final_norm):
    n_prompt, seq_len, _ = x_prompt.shape
    n_sample, dec_len, _ = x_sample.shape
    assert dec_len == CHUNK - SAMPLE_NULL and (N_META + seq_len) % CHUNK == N_META
    prompt_rows = PROMPT_NULL + N_META + seq_len
    cpp = prompt_rows // CHUNK
    geo = _Geometry(n_prompt, n_sample, cpp)
    assert geo.rows % ROW_TILE == 0 and (geo.prompt_chunks * CHUNK) % ROW_TILE == 0
    assert INPROJ_TILE % GROUP_ROWS == 0 and ROW_TILE % GROUP_ROWS == 0
    dt = x_prompt.dtype

    x = _pack_call(geo, x_prompt, x_sample, meta_tokens)

    cast = lambda w: w.astype(BF16)
    vec = lambda v: v.astype(F32)[:, None, :]
    wgu1, wd1, wgu2, wd2 = cast(ffn1_w_gu), cast(ffn1_w_down), cast(ffn2_w_gu), cast(ffn2_w_down)
    w_in_t = cast(jnp.swapaxes(w_in, 1, 2))
    w_ri = cast(jnp.concatenate([lru_w_r, lru_w_i], axis=-1))
    wa, wb, wo = cast(w_branch_a), cast(w_branch_b), cast(w_out)
    norm1, norm_mix, norm2 = vec(ffn1_norm), vec(mix_norm), vec(ffn2_norm)
    alog, dtb, onorm = vec(_pad_lanes(delta_A_log, 128)), vec(_pad_lanes(delta_dt_bias, 128)), vec(delta_out_norm)
    lcb, b_r, b_i, lam = vec(lru_conv_b), vec(lru_b_r), vec(lru_b_i), vec(lru_lambda)
    cw, lcw = delta_conv_w.astype(F32), lru_conv_w.astype(F32)
    s0, h0 = state_delta_S.astype(F32), state_lru_h.astype(F32)

    inj = jnp.concatenate([state_delta_conv, state_lru_conv], axis=-1).astype(F32)
    null_runs = SAMPLE_NULL // N_SLAB
    inj = jnp.pad(inj[:, :, :, None, :], ((0, 0), (0, 0), (0, 0), (null_runs - 1, SLAB - null_runs), (0, 0)))
    inj = inj.reshape(DEPTH, n_sample * TAIL_ROWS, CONV_COLS)

    last_ids = geo.last_chunk_ids()
    outs_s, outs_tail, outs_h = [], [], []
    for l in range(DEPTH):
        x = _ffn_call(l, x, norm1, wgu1, wd1)
        p, ba, tails = _inproj_call(geo, l, x, norm_mix, w_in_t, cw, lcw, lcb, inj)
        oa, s_new = _delta_call(geo, l, p, ba, s0, alog, dtb, onorm)
        x, hlast = _outproj_call(geo, l, x, p, oa, h0, w_ri, b_r, b_i, lam, wa, wb, wo, norm2, wgu2, wd2,
                                 final_norm[None].astype(F32), final=(l == DEPTH - 1))
        outs_s.append(s_new)
        outs_tail.append(tails[last_ids])
        outs_h.append(hlast[last_ids])

    n_prompt_rows = geo.prompt_chunks * CHUNK
    y_prompt, y_sample = _unpack_call(geo, x, dt)
    s_all = jnp.stack(outs_s)
    tail_all = jnp.stack(outs_tail)
    cq_all = tail_all[..., :QKV_W]
    cx_all = tail_all[..., QKV_W:]
    h_all = jnp.stack(outs_h)
    return (y_prompt.astype(dt), y_sample.astype(dt),
            s_all[:, :n_prompt].astype(dt), cq_all[:, :n_prompt].astype(dt),
            h_all[:, :n_prompt].astype(dt), cx_all[:, :n_prompt].astype(dt),
            s_all[:, n_prompt:].astype(state_delta_S.dtype),
            cq_all[:, n_prompt:].astype(state_delta_conv.dtype),
            h_all[:, n_prompt:].astype(state_lru_h.dtype),
            cx_all[:, n_prompt:].astype(state_lru_conv.dtype))
```

```python
import functools

import numpy as np
import jax
import jax.numpy as jnp
from jax import lax
from jax.experimental import pallas as pl
from jax.experimental.pallas import tpu as pltpu

F32 = jnp.float32
BF16 = jnp.bfloat16

D_MODEL = 1024
DEPTH = 4
N_META = 16
HA = 8
DK = 128
DV = 128
QK_W = HA * DK
V_W = HA * DV
QKV_W = 2 * QK_W + V_W
LRU_W = D_MODEL
NB = 8
BW = 128
CONV_W = 4
LRU_C = 8.0
D_FF = 2816
EPS = 1e-6

OFF_Z = QKV_W
OFF_BETA = OFF_Z + V_W
OFF_ALPHA = OFF_BETA + HA
OFF_LX = OFF_ALPHA + HA
OFF_LY = OFF_LX + LRU_W
OFF_GA = OFF_LY + LRU_W
OFF_GB = OFF_GA + D_MODEL
IN_COLS = OFF_GB + D_MODEL

CHUNK = 64
GROUP = 4
GROUP_ROWS = GROUP * CHUNK
PROMPT_NULL = CHUNK - N_META
SAMPLE_NULL = CHUNK // 2
INV_BASE = 16
ROW_TILE = 512
INPROJ_TILE = 256
FF_HALF = D_FF // 2
CONV_COLS = QKV_W + LRU_W
SLAB = 8
N_SLAB = CHUNK // SLAB
TAIL_ROWS = (CONV_W - 1) * SLAB
P_Z, P_X, P_Y, P_G = 3072, 4096, 5120, 6144
P_COLS = 8192
VMEM_LIMIT = 56 * 1024 * 1024


def _rms(x, w):
    return x * lax.rsqrt(jnp.mean(x * x, axis=-1, keepdims=True) + EPS) * w


def _mm(a, b):
    return jnp.dot(a.astype(BF16), b.astype(BF16), preferred_element_type=F32)


def _mm_nt(a, b):
    return lax.dot_general(a.astype(BF16), b.astype(BF16), (((1,), (1,)), ((), ())),
                           preferred_element_type=F32)


def _mm_tn(a, b):
    return lax.dot_general(a.astype(BF16), b.astype(BF16), (((0,), (0,)), ((), ())),
                           preferred_element_type=F32)


def _sigmoid(x):
    return 0.5 * (1.0 + jnp.tanh(0.5 * x))


def _silu(x):
    return x * _sigmoid(x)


def _softplus(x):
    return jnp.maximum(x, 0.0) + jnp.log1p(jnp.exp(-jnp.abs(x)))


def _gelu_tanh(x):
    return 0.5 * x * (1.0 + jnp.tanh(0.7978845608028654 * (x + 0.044715 * (x * x * x))))


class _Geometry:
    def __init__(self, n_prompt, n_sample, chunks_per_prompt):
        assert n_prompt % GROUP == 0 and n_sample % GROUP == 0
        self.n_prompt = n_prompt
        self.n_sample = n_sample
        self.cpp = chunks_per_prompt
        self.prompt_steps = (n_prompt // GROUP) * chunks_per_prompt
        self.steps = self.prompt_steps + n_sample // GROUP
        self.prompt_chunks = GROUP * self.prompt_steps
        self.chunks = GROUP * self.steps
        self.rows = self.chunks * CHUNK
        self.n_streams = n_prompt + n_sample

    def chunk_null(self, cid):
        is_prompt = cid < self.prompt_chunks
        first = jnp.logical_or(jnp.logical_not(is_prompt), (cid // GROUP) % self.cpp == 0)
        return jnp.where(first, jnp.where(is_prompt, PROMPT_NULL, SAMPLE_NULL), 0)

    def last_chunk_ids(self):
        ids = [GROUP * ((b // GROUP) * self.cpp + self.cpp - 1) + b % GROUP for b in range(self.n_prompt)]
        ids += [self.prompt_chunks + t for t in range(self.n_sample)]
        return np.asarray(ids, np.int32)


def _row_time(rows):
    return N_SLAB * (rows % SLAB) + rows // SLAB


def _valid_rows(null):
    return _row_time(lax.broadcasted_iota(jnp.int32, (CHUNK, 1), 0)) >= null


def _slabs(x):
    return [x[v * SLAB:(v + 1) * SLAB] for v in range(N_SLAB)]


def _shift_run(x, fill):
    return jnp.concatenate([fill, x[:SLAB - 1]], axis=0)


def _scan_time(a, b, h_in):
    bs = _slabs(b)
    if a is None:
        for v in range(1, N_SLAB):
            bs[v] = bs[v] + bs[v - 1]
        run = bs[N_SLAB - 1]
        d = 1
        while d < SLAB:
            run = run + _shift_rows(run, d, 0.0)
            d *= 2
        run = run + h_in
        prev = _shift_run(run, h_in)
        hs = [t + prev for t in bs]
    else:
        as_ = _slabs(a)
        for v in range(1, N_SLAB):
            bs[v] = bs[v] + as_[v] * bs[v - 1]
            as_[v] = as_[v] * as_[v - 1]
        ra, rb = as_[N_SLAB - 1], bs[N_SLAB - 1]
        d = 1
        while d < SLAB:
            rb = rb + ra * _shift_rows(rb, d, 0.0)
            ra = ra * _shift_rows(ra, d, 1.0)
            d *= 2
        run = rb + ra * h_in
        prev = _shift_run(run, h_in)
        hs = [t + u * prev for t, u in zip(bs, as_)]
    return jnp.concatenate(hs, axis=0), run[SLAB - 1:SLAB, :]


def _ffn(x, nw, wgu_ref, wd_ref, h_ref):
    xn = _rms(x, nw).astype(BF16)
    for c in range(2):
        lo = c * FF_HALF
        g = jnp.dot(xn, wgu_ref[:, lo:lo + FF_HALF], preferred_element_type=F32)
        u = jnp.dot(xn, wgu_ref[:, D_FF + lo:D_FF + lo + FF_HALF], preferred_element_type=F32)
        h_ref[:, lo:lo + FF_HALF] = (_silu(g) * u).astype(BF16)
    y = jnp.dot(h_ref[...], wd_ref[...], preferred_element_type=F32)
    return x + 0.5 * y


def _ffn_kernel(x_ref, nw_ref, wgu_ref, wd_ref, o_ref, h_ref):
    o_ref[...] = _ffn(x_ref[...], nw_ref[...], wgu_ref, wd_ref, h_ref)


def _inproj_kernel(geo, x_ref, nw_ref, wa_ref, wt_ref, wl_ref, wba_ref, cw_ref, lcw_ref, lcb_ref, inj_ref,
                   p_ref, ba_ref, tails_ref, stage_all, carry_scr, wb_scr):
    i = pl.program_id(0)
    nch = INPROJ_TILE // CHUNK
    xn = _rms(x_ref[...], nw_ref[...]).astype(BF16)

    def project_cols(lo, width):
        w_ref, start = (wa_ref, lo) if lo < P_COLS // 2 else (wb_scr, lo - P_COLS // 2)
        return jnp.dot(xn, w_ref[:, start:start + width], preferred_element_type=F32)

    @pl.when(i == 0)
    def _():
        carry_scr[...] = jnp.zeros_like(carry_scr)
        skip = OFF_LX - OFF_BETA
        for c in range(P_COLS // 2 // 1024):
            nxt = wt_ref[:, (c + 1) * 1024:(c + 1) * 1024 + 128] if (c + 1) * 1024 < P_COLS // 2 else wl_ref[...]
            wide = jnp.concatenate([wt_ref[:, c * 1024:(c + 1) * 1024], nxt], axis=1)
            wb_scr[:, c * 1024:(c + 1) * 1024] = pltpu.roll(wide, wide.shape[1] - skip, 1)[:, :1024]

    valid = [_valid_rows(geo.chunk_null(i * nch + j)) for j in range(nch)]
    in_sample = i >= geo.prompt_chunks * CHUNK // INPROJ_TILE
    kinds = (("q", 0, 0), ("k", QK_W, QK_W), ("v", 2 * QK_W, 2 * QK_W), ("x", P_X, QKV_W))
    t0 = TAIL_ROWS
    part_w = 1024 // nch

    def project(kidx, part):
        kind, lo, col = kinds[kidx]
        stage = stage_all.at[kidx]
        pre = project_cols(lo + part * part_w, part_w)
        sc = slice(part * part_w, (part + 1) * part_w)
        cc = slice(col + part * part_w, col + (part + 1) * part_w)
        for j in range(nch):
            stage[j, t0:t0 + CHUNK, sc] = pre[j * CHUNK:(j + 1) * CHUNK]
            if j >= GROUP:
                stage[j, 0:t0, sc] = pre[(j - GROUP + 1) * CHUNK - t0:(j - GROUP + 1) * CHUNK]
            else:
                stage[j, 0:t0, sc] = carry_scr[j, :, cc]
            stage[j, CHUNK:CHUNK + t0, sc] = stage[j, CHUNK:CHUNK + t0, sc] + jnp.where(
                in_sample, inj_ref[t0 * j:t0 * (j + 1), cc], 0.0)
            tails_ref[j, :, cc] = stage[j, CHUNK:CHUNK + t0, sc]
        for g in range(GROUP):
            carry_scr[g, :, cc] = stage[nch - GROUP + g, CHUNK:CHUNK + t0, sc]

    def convolve(kidx, j):
        kind, lo, col = kinds[kidx]
        stage = stage_all.at[kidx]
        taps = lcw_ref[...] if kind == "x" else cw_ref[:, lo:lo + 1024]
        cur = [stage[j, t0 + v * SLAB:t0 + (v + 1) * SLAB, :] for v in range(N_SLAB)]
        back = []
        for k in range(CONV_W - 1):
            prev_last = stage[j, (k + 1) * SLAB - 1:(k + 1) * SLAB, :]
            back.append(_shift_run(cur[N_SLAB - (CONV_W - 1) + k], prev_last))
        conv = []
        for v in range(N_SLAB):
            acc = cur[v] * taps[CONV_W - 1:CONV_W, :]
            for d in range(1, CONV_W):
                src = cur[v - d] if v >= d else back[CONV_W - 1 - d + v]
                acc = acc + src * taps[CONV_W - 1 - d:CONV_W - d, :]
            conv.append(acc)
        conv = jnp.concatenate(conv, axis=0)
        if kind == "x":
            out = conv + lcb_ref[...]
        else:
            act = _silu(conv)
            if kind != "v":
                scale = DK ** -0.5 if kind == "q" else 1.0
                segs = []
                for h in range(HA):
                    seg = act[:, h * DK:(h + 1) * DK]
                    segs.append(seg * (lax.rsqrt(jnp.sum(seg * seg, axis=-1, keepdims=True) + EPS) * scale))
                act = jnp.concatenate(segs, axis=1)
            out = jnp.where(valid[j], act, 0.0)
        p_ref[j * CHUNK:(j + 1) * CHUNK, lo:lo + 1024] = out.astype(BF16)

    def plain(lo, part):
        sl = slice(lo + part * part_w, lo + (part + 1) * part_w)
        p_ref[:, sl] = project_cols(sl.start, part_w).astype(BF16)

    plain_cols = (P_Z, P_Y, P_G, P_G + 1024)
    for part in range(nch):
        project(0, part)
    for kidx in range(len(kinds)):
        for j in range(nch):
            plain(plain_cols[kidx], j)
            if kidx + 1 < len(kinds):
                project(kidx + 1, j)
            convolve(kidx, j)
    ba_ref[...] = jnp.dot(xn, wba_ref[...], preferred_element_type=F32)


def _shift_rows(x, d, fill):
    if d % 8 == 0:
        return jnp.concatenate([jnp.full((d, x.shape[1]), fill, x.dtype), x[:x.shape[0] - d]], axis=0)
    rows = lax.broadcasted_iota(jnp.int32, x.shape, 0)
    return jnp.where(rows >= d, pltpu.roll(x, d, 0), fill)


def _outproj_kernel(geo, final, x_ref, g_ref, oa_ref, xl_ref, h0_ref,
                    wri_ref, br_ref, bi_ref, lam_ref, wa_ref, wb_ref, wo_ref,
                    nw_ref, wgu_ref, wd_ref, fn_ref,
                    o_ref, hlast_ref, h_ref, hc_scr, ob_scr):
    i = pl.program_id(0)
    nch = ROW_TILE // CHUNK

    @pl.when(i == 0)
    def _():
        hc_scr[...] = jnp.zeros_like(hc_scr)

    sp = _softplus(-lam_ref[...])
    nulls = [geo.chunk_null(i * nch + j) for j in range(nch)]
    valid = [_valid_rows(n) for n in nulls]
    is_sample = [(i * nch + j) >= geo.prompt_chunks for j in range(nch)]
    for n in range(NB):
        sl = slice(n * BW, (n + 1) * BW)
        xb = xl_ref[:, sl]
        gates = jnp.dot(xb, wri_ref[n], preferred_element_type=F32)
        r = _sigmoid(gates[:, :BW] + br_ref[:, sl])
        ig = _sigmoid(gates[:, BW:] + bi_ref[:, sl])
        log_a = -LRU_C * r * sp[:, sl]
        a_all = jnp.exp(log_a)
        th = jnp.tanh(log_a)
        mult = jnp.sqrt(-2.0 * th / (1.0 - th))
        b_all = mult * (ig * xb.astype(F32))
        gy = _gelu_tanh(xl_ref[:, LRU_W + n * BW:LRU_W + (n + 1) * BW].astype(F32))
        carry = [hc_scr[g:g + 1, sl] for g in range(GROUP)]
        for j in range(nch):
            rows = slice(j * CHUNK, (j + 1) * CHUNK)
            a = jnp.where(valid[j], a_all[rows], 1.0)
            b = jnp.where(valid[j], b_all[rows], 0.0)
            h_in = jnp.where(nulls[j] > 0, jnp.where(is_sample[j], h0_ref[j:j + 1, sl], 0.0), carry[j % GROUP])
            hs, carry[j % GROUP] = _scan_time(a, b, h_in)
            ob_scr[rows, sl] = (hs * gy[rows]).astype(BF16)
            hlast_ref[j:j + 1, sl] = carry[j % GROUP]
        for g in range(GROUP):
            hc_scr[g:g + 1, sl] = carry[g]

    ga = _sigmoid(g_ref[:, :D_MODEL].astype(F32))
    gb = _sigmoid(g_ref[:, D_MODEL:].astype(F32))
    ma = jnp.dot(oa_ref[...], wa_ref[...], preferred_element_type=F32)
    mb = jnp.dot(ob_scr[...], wb_ref[...], preferred_element_type=F32)
    m = (ga * ma + gb * mb).astype(BF16)
    x = x_ref[...] + jnp.dot(m, wo_ref[...], preferred_element_type=F32)
    x = _ffn(x, nw_ref[...], wgu_ref, wd_ref, h_ref)
    if final:
        x = _rms(x, fn_ref[...])
    o_ref[...] = x


def _resident(shape, layer):
    nd = len(shape)
    return pl.BlockSpec((None,) + shape, lambda i: (layer,) + (0,) * nd, pipeline_mode=pl.Buffered(1))


def _params(semantics):
    return pltpu.CompilerParams(dimension_semantics=(semantics,), vmem_limit_bytes=VMEM_LIMIT)


def _ffn_call(layer, x, nw, wgu, wd):
    n = x.shape[0]
    row = lambda i: (i, 0)
    return pl.pallas_call(
        _ffn_kernel,
        grid=(n // ROW_TILE,),
        in_specs=[pl.BlockSpec((ROW_TILE, D_MODEL), row),
                  _resident((1, D_MODEL), layer),
                  _resident((D_MODEL, 2 * D_FF), layer),
                  _resident((D_FF, D_MODEL), layer)],
        out_specs=pl.BlockSpec((ROW_TILE, D_MODEL), row),
        out_shape=jax.ShapeDtypeStruct((n, D_MODEL), F32),
        scratch_shapes=[pltpu.VMEM((ROW_TILE, D_FF), BF16)],
        compiler_params=_params("parallel"),
        name="ffn",
    )(x, nw, wgu, wd)


def _inproj_call(geo, layer, x, nw, w_all, w_last, wba, cw, lcw, lcb, inj):
    n = x.shape[0]
    nch = INPROJ_TILE // CHUNK
    sample_tile0 = geo.prompt_chunks * CHUNK // INPROJ_TILE
    row = lambda i: (i, 0)
    half = P_COLS // 2
    assert OFF_BETA == half and w_all.shape[2] == 2 * half + OFF_LX - OFF_BETA and w_last.shape[2] == 128
    return pl.pallas_call(
        functools.partial(_inproj_kernel, geo),
        grid=(n // INPROJ_TILE,),
        in_specs=[pl.BlockSpec((INPROJ_TILE, D_MODEL), row),
                  _resident((1, D_MODEL), layer),
                  _resident((D_MODEL, half), layer),
                  pl.BlockSpec((None, D_MODEL, half), lambda i: (layer, 0, 1), pipeline_mode=pl.Buffered(1)),
                  _resident((D_MODEL, 128), layer),
                  _resident((D_MODEL, 256), layer),
                  _resident((CONV_W, QKV_W), layer),
                  _resident((CONV_W, LRU_W), layer),
                  _resident((1, LRU_W), layer),
                  pl.BlockSpec((None, TAIL_ROWS * nch, CONV_COLS),
                               lambda i: (layer, jnp.maximum(i - sample_tile0, 0), 0))],
        out_specs=[pl.BlockSpec((INPROJ_TILE, P_COLS), row),
                   pl.BlockSpec((INPROJ_TILE, 256), row),
                   pl.BlockSpec((nch, TAIL_ROWS, CONV_COLS), lambda i: (i, 0, 0))],
        out_shape=[jax.ShapeDtypeStruct((n, P_COLS), BF16),
                   jax.ShapeDtypeStruct((n, 256), F32),
                   jax.ShapeDtypeStruct((n // CHUNK, TAIL_ROWS, CONV_COLS), F32)],
        scratch_shapes=[pltpu.VMEM((4, nch, CHUNK + TAIL_ROWS, 1024), F32),
                        pltpu.VMEM((GROUP, TAIL_ROWS, CONV_COLS), F32),
                        pltpu.VMEM((D_MODEL, half), BF16)],
        compiler_params=_params("arbitrary"),
        name="inproj",
    )(x, nw, w_all, w_all, w_last, wba, cw, lcw, lcb, inj)


def _outproj_call(geo, layer, x, p, oa, h0, wri, br, bi, lam, wa, wb, wo, nw, wgu, wd, fn, final):
    n = x.shape[0]
    nch = ROW_TILE // CHUNK
    sample_tile0 = geo.prompt_chunks * CHUNK // ROW_TILE
    row = lambda i: (i, 0)
    return pl.pallas_call(
        functools.partial(_outproj_kernel, geo, final),
        grid=(n // ROW_TILE,),
        in_specs=[pl.BlockSpec((ROW_TILE, D_MODEL), row),
                  pl.BlockSpec((ROW_TILE, 2 * D_MODEL), lambda i: (i, P_G // 2048)),
                  pl.BlockSpec((ROW_TILE, V_W), row),
                  pl.BlockSpec((ROW_TILE, 2 * LRU_W), lambda i: (i, P_X // 2048)),
                  pl.BlockSpec((None, nch, LRU_W), lambda i: (layer, jnp.maximum(i - sample_tile0, 0), 0)),
                  _resident((NB, BW, 2 * BW), layer),
                  _resident((1, LRU_W), layer),
                  _resident((1, LRU_W), layer),
                  _resident((1, LRU_W), layer),
                  _resident((V_W, D_MODEL), layer),
                  _resident((LRU_W, D_MODEL), layer),
                  _resident((D_MODEL, D_MODEL), layer),
                  _resident((1, D_MODEL), layer),
                  _resident((D_MODEL, 2 * D_FF), layer),
                  _resident((D_FF, D_MODEL), layer),
                  pl.BlockSpec((1, D_MODEL), lambda i: (0, 0), pipeline_mode=pl.Buffered(1))],
        out_specs=[pl.BlockSpec((ROW_TILE, D_MODEL), row),
                   pl.BlockSpec((nch, LRU_W), row)],
        out_shape=[jax.ShapeDtypeStruct((n, D_MODEL), F32),
                   jax.ShapeDtypeStruct((n // CHUNK, LRU_W), F32)],
        scratch_shapes=[pltpu.VMEM((ROW_TILE, D_FF), BF16),
                        pltpu.VMEM((8, LRU_W), F32),
                        pltpu.VMEM((ROW_TILE, LRU_W), BF16)],
        compiler_params=_params("arbitrary"),
        name="outproj_ffn",
    )(x, p, oa, p, h0, wri, br, bi, lam, wa, wb, wo, nw, wgu, wd, fn)


def _unit_lower_inverse(lows, c):
    ti = _row_time(lax.broadcasted_iota(jnp.int32, (c, c), 0))
    tj = _row_time(lax.broadcasted_iota(jnp.int32, (c, c), 1))
    same_block = (ti // INV_BASE) == (tj // INV_BASE)
    eye = (ti == tj).astype(F32)
    ld = [jnp.where(same_block, low, 0.0) for low in lows]
    lo = [low - d for low, d in zip(lows, ld)]
    x = [eye - d for d in ld]
    p = ld
    span = 2
    while span < INV_BASE:
        p = [_mm(t, t) for t in p]
        x = [a + _mm(a, t) for a, t in zip(x, p)]
        span *= 2
    e = [_mm(a, b) for a, b in zip(x, lo)]
    y = [eye - t for t in e]
    q = e
    span = 2
    while span < c // INV_BASE:
        q = [_mm(t, t) for t in q]
        y = [a + _mm(a, t) for a, t in zip(y, q)]
        span *= 2
    return [_mm(a, b) for a, b in zip(y, x)]


def _delta_kernel(geo, qkv_ref, z_ref, ba_ref, s0_ref, alog_ref, dtb_ref, onorm_ref,
                  oa_ref, sout_ref, s_scr):
    c = CHUNK
    i = pl.program_id(0)
    is_prompt = i < geo.prompt_steps
    pos = i % geo.cpp
    first = jnp.logical_or(jnp.logical_not(is_prompt), pos == 0)
    last = jnp.logical_or(jnp.logical_not(is_prompt), pos == geo.cpp - 1)
    null = jnp.where(first, jnp.where(is_prompt, PROMPT_NULL, SAMPLE_NULL), 0)
    valid = _valid_rows(null)

    @pl.when(jnp.logical_and(first, is_prompt))
    def _():
        s_scr[...] = jnp.zeros_like(s_scr)

    @pl.when(jnp.logical_not(is_prompt))
    def _():
        s_scr[...] = s0_ref[...]

    ti = _row_time(lax.broadcasted_iota(jnp.int32, (c, c), 0))
    tj = _row_time(lax.broadcasted_iota(jnp.int32, (c, c), 1))
    tri_incl = ti >= tj
    tri_strict = ti > tj

    beta_all, gc_all, gc_t, eg_all, ekd_all, egl_all = [], [], [], [], [], []
    for s in range(GROUP):
        ba = ba_ref[s * c:(s + 1) * c, :]
        beta_all.append(jnp.where(valid, _sigmoid(ba[:, :128]), 0.0))
        g = -jnp.exp(alog_ref[...]) * _softplus(ba[:, 128:] + dtb_ref[...])
        gc, g_last = _scan_time(None, jnp.where(valid, g, 0.0), jnp.zeros((1, 128), F32))
        gc_all.append(gc)
        gc_t.append(jnp.concatenate([gc, jnp.zeros((128 - c, 128), F32)], axis=0).T)
        eg_all.append(jnp.exp(gc))
        ekd_all.append(jnp.exp(g_last - gc))
        egl_all.append(jnp.exp(g_last))

    probs = [(s, h) for s in range(GROUP) for h in range(HA)]
    rows = lambda s: slice(s * c, (s + 1) * c)
    qb = [qkv_ref[rows(s), h * DK:(h + 1) * DK] for s, h in probs]
    kbf = [qkv_ref[rows(s), QK_W + h * DK:QK_W + (h + 1) * DK] for s, h in probs]
    vb = [qkv_ref[rows(s), 2 * QK_W + h * DV:2 * QK_W + (h + 1) * DV] for s, h in probs]
    q = [t.astype(F32) for t in qb]
    k = [t.astype(F32) for t in kbf]
    beta = [beta_all[s][:, h:h + 1] for s, h in probs]
    eg = [eg_all[s][:, h:h + 1] for s, h in probs]
    n = len(probs)
    kb = [k[p] * beta[p] for p in range(n)]
    sc = [_mm_nt(jnp.concatenate([kb[p].astype(BF16), qb[p]], axis=0), kbf[p]) for p in range(n)]
    decay = [jnp.where(tri_incl, jnp.exp(jnp.where(tri_incl, gc_all[s][:, h:h + 1] - gc_t[s][h:h + 1, :c], 0.0)), 0.0)
             for s, h in probs]
    low = [jnp.where(tri_strict, sc[p][:c] * decay[p], 0.0) for p in range(n)]
    a_intra = [sc[p][c:] * decay[p] for p in range(n)]
    t_inv = _unit_lower_inverse(low, c)
    uw = [_mm(t_inv[p], jnp.concatenate([vb[p].astype(F32) * beta[p], kb[p] * eg[p]], axis=1))
          for p in range(n)]
    s_old = [s_scr[s, h] for s, h in probs]
    wq = [_mm(jnp.concatenate([uw[p][:, DV:], q[p] * eg[p]], axis=0), s_old[p]) for p in range(n)]
    v_new = [uw[p][:, :DV] - wq[p][:c] for p in range(n)]
    o_intra = [_mm(a_intra[p], v_new[p]) for p in range(n)]
    ds = [_mm_tn(k[p] * ekd_all[s][:, h:h + 1], v_new[p]) for p, (s, h) in enumerate(probs)]
    for p, (s, h) in enumerate(probs):
        s_scr[s, h] = s_old[p] * egl_all[s][:, h:h + 1] + ds[p]
        o = wq[p][c:] + o_intra[p]
        zh = z_ref[rows(s), h * DV:(h + 1) * DV].astype(F32)
        o = o * lax.rsqrt(jnp.mean(o * o, axis=-1, keepdims=True) + EPS) * onorm_ref[...] * _silu(zh)
        oa_ref[rows(s), h * DV:(h + 1) * DV] = o.astype(oa_ref.dtype)

    @pl.when(last)
    def _():
        sout_ref[...] = s_scr[...]


def _delta_call(geo, layer, p, ba, s0, alog, dtb, onorm):
    n_pp = geo.n_prompt // GROUP

    def pair(i):
        return jnp.where(i < geo.prompt_steps, i // geo.cpp, i - geo.prompt_steps + n_pp)

    vec = lambda width: pl.BlockSpec((None, 1, width), lambda i: (layer, 0, 0))
    return pl.pallas_call(
        functools.partial(_delta_kernel, geo),
        grid=(geo.steps,),
        in_specs=[pl.BlockSpec((GROUP_ROWS, QKV_W), lambda i: (i, 0)),
                  pl.BlockSpec((GROUP_ROWS, V_W), lambda i: (i, P_Z // V_W)),
                  pl.BlockSpec((GROUP_ROWS, 256), lambda i: (i, 0)),
                  pl.BlockSpec((None, GROUP, HA, DK, DV),
                               lambda i: (layer, jnp.maximum(i - geo.prompt_steps, 0), 0, 0, 0)),
                  vec(128), vec(128), vec(DV)],
        out_specs=[pl.BlockSpec((GROUP_ROWS, V_W), lambda i: (i, 0)),
                   pl.BlockSpec((GROUP, HA, DK, DV), lambda i: (pair(i), 0, 0, 0))],
        out_shape=[jax.ShapeDtypeStruct((geo.rows, V_W), BF16),
                   jax.ShapeDtypeStruct((geo.n_streams, HA, DK, DV), F32)],
        scratch_shapes=[pltpu.VMEM((GROUP, HA, DK, DV), F32)],
        compiler_params=_params("arbitrary"),
        name="delta",
    )(p, p, ba, s0, alog, dtb, onorm)


def _permute_chunk(x):
    return jnp.swapaxes(x.reshape(SLAB, N_SLAB, x.shape[-1]), 0, 1).reshape(CHUNK, x.shape[-1])


def _pack_kernel(geo, xp_ref, xs_ref, meta_ref, o_ref):
    i = pl.program_id(0)
    is_prompt = i < geo.prompt_steps
    pos = i % geo.cpp
    width = o_ref.shape[1]

    @pl.when(jnp.logical_and(is_prompt, pos > 0))
    def _():
        for s in range(GROUP):
            o_ref[s * CHUNK:(s + 1) * CHUNK, :] = _permute_chunk(xp_ref[s].astype(o_ref.dtype))

    @pl.when(jnp.logical_and(is_prompt, pos == 0))
    def _():
        first = _permute_chunk(jnp.concatenate(
            [jnp.zeros((PROMPT_NULL, width), o_ref.dtype), meta_ref[...].astype(o_ref.dtype)], axis=0))
        for s in range(GROUP):
            o_ref[s * CHUNK:(s + 1) * CHUNK, :] = first

    @pl.when(jnp.logical_not(is_prompt))
    def _():
        for s in range(GROUP):
            o_ref[s * CHUNK:(s + 1) * CHUNK, :] = _permute_chunk(jnp.concatenate(
                [jnp.zeros((SAMPLE_NULL, width), o_ref.dtype), xs_ref[s].astype(o_ref.dtype)], axis=0))


def _unpack_kernel(geo, x_ref, yp_ref, ys_ref):
    i = pl.program_id(0)
    is_prompt = i < geo.prompt_steps
    pos = i % geo.cpp

    @pl.when(jnp.logical_and(is_prompt, pos > 0))
    def _():
        for s in range(GROUP):
            yp_ref[s] = _permute_chunk(x_ref[s * CHUNK:(s + 1) * CHUNK, :]).astype(yp_ref.dtype)

    @pl.when(jnp.logical_not(is_prompt))
    def _():
        for s in range(GROUP):
            ys_ref[s] = _permute_chunk(x_ref[s * CHUNK:(s + 1) * CHUNK, :])[SAMPLE_NULL:].astype(ys_ref.dtype)


def _layout_specs(geo):
    def prompt_idx(i):
        step = jnp.minimum(i, geo.prompt_steps - 1)
        return (step // geo.cpp, jnp.maximum(step % geo.cpp - 1, 0), 0)

    def sample_idx(i):
        return (jnp.maximum(i - geo.prompt_steps, 0), 0, 0)

    return (pl.BlockSpec((GROUP, CHUNK, D_MODEL), prompt_idx),
            pl.BlockSpec((GROUP, CHUNK - SAMPLE_NULL, D_MODEL), sample_idx))


def _pack_call(geo, x_prompt, x_sample, meta):
    prompt_spec, sample_spec = _layout_specs(geo)
    return pl.pallas_call(
        functools.partial(_pack_kernel, geo),
        grid=(geo.steps,),
        in_specs=[prompt_spec, sample_spec, pl.BlockSpec((N_META, D_MODEL), lambda i: (0, 0))],
        out_specs=pl.BlockSpec((GROUP_ROWS, D_MODEL), lambda i: (i, 0)),
        out_shape=jax.ShapeDtypeStruct((geo.rows, D_MODEL), F32),
        compiler_params=_params("parallel"),
        name="pack",
    )(x_prompt, x_sample, meta)


def _unpack_call(geo, x, dtype):
    prompt_spec, sample_spec = _layout_specs(geo)
    return pl.pallas_call(
        functools.partial(_unpack_kernel, geo),
        grid=(geo.steps,),
        in_specs=[pl.BlockSpec((GROUP_ROWS, D_MODEL), lambda i: (i, 0))],
        out_specs=[prompt_spec, sample_spec],
        out_shape=[jax.ShapeDtypeStruct((geo.n_prompt, (geo.cpp - 1) * CHUNK, D_MODEL), dtype),
                   jax.ShapeDtypeStruct((geo.n_sample, CHUNK - SAMPLE_NULL, D_MODEL), dtype)],
        compiler_params=_params("arbitrary"),
        name="unpack",
    )(x)


def _pad_lanes(v, width):
    return jnp.pad(v, ((0, 0), (0, width - v.shape[-1])))


def kernel(x_prompt, x_sample, state_delta_S, state_delta_conv, state_lru_h, state_lru_conv,
           meta_tokens, ffn1_norm, ffn1_w_gu, ffn1_w_down, mix_norm, w_in, delta_conv_w,
           delta_A_log, delta_dt_bias, delta_out_norm, lru_conv_w, lru_conv_b, lru_w_r, lru_b_r,
           lru_w_i, lru_b_i, lru_lambda, w_branch_a, w_branch_b, w_out, ffn2_norm, ffn2_w_gu,
           ffn2_w_down, final_norm):
    n_prompt, seq_len, _ = x_prompt.shape
    n_sample, dec_len, _ = x_sample.shape
    assert dec_len == CHUNK - SAMPLE_NULL and (N_META + seq_len) % CHUNK == N_META
    prompt_rows = PROMPT_NULL + N_META + seq_len
    cpp = prompt_rows // CHUNK
    geo = _Geometry(n_prompt, n_sample, cpp)
    assert geo.rows % ROW_TILE == 0 and (geo.prompt_chunks * CHUNK) % ROW_TILE == 0
    assert INPROJ_TILE % GROUP_ROWS == 0 and ROW_TILE % GROUP_ROWS == 0
    dt = x_prompt.dtype

    x = _pack_call(geo, x_prompt, x_sample, meta_tokens)

    cast = lambda w: w.astype(BF16)
    vec = lambda v: v.astype(F32)[:, None, :]
    wgu1, wd1, wgu2, wd2 = cast(ffn1_w_gu), cast(ffn1_w_down), cast(ffn2_w_gu), cast(ffn2_w_down)
    w_all = cast(w_in)
    pad_heads = lambda w: jnp.pad(w, ((0, 0), (0, 0), (0, 128 - w.shape[-1])))
    w_last = cast(pad_heads(w_in[:, :, P_COLS:]))
    w_ba = cast(jnp.concatenate([pad_heads(w_in[:, :, OFF_BETA:OFF_ALPHA]),
                                 pad_heads(w_in[:, :, OFF_ALPHA:OFF_LX])], axis=-1))
    w_ri = cast(jnp.concatenate([lru_w_r, lru_w_i], axis=-1))
    wa, wb, wo = cast(w_branch_a), cast(w_branch_b), cast(w_out)
    norm1, norm_mix, norm2 = vec(ffn1_norm), vec(mix_norm), vec(ffn2_norm)
    alog, dtb, onorm = vec(_pad_lanes(delta_A_log, 128)), vec(_pad_lanes(delta_dt_bias, 128)), vec(delta_out_norm)
    lcb, b_r, b_i, lam = vec(lru_conv_b), vec(lru_b_r), vec(lru_b_i), vec(lru_lambda)
    cw, lcw = delta_conv_w.astype(F32), lru_conv_w.astype(F32)
    s0, h0 = state_delta_S.astype(F32), state_lru_h.astype(F32)

    inj = jnp.concatenate([state_delta_conv, state_lru_conv], axis=-1).astype(F32)
    null_runs = SAMPLE_NULL // N_SLAB
    inj = jnp.pad(inj[:, :, :, None, :], ((0, 0), (0, 0), (0, 0), (null_runs - 1, SLAB - null_runs), (0, 0)))
    inj = inj.reshape(DEPTH, n_sample * TAIL_ROWS, CONV_COLS)

    last_ids = geo.last_chunk_ids()
    outs_s, outs_tail, outs_h = [], [], []
    for l in range(DEPTH):
        x = _ffn_call(l, x, norm1, wgu1, wd1)
        p, ba, tails = _inproj_call(geo, l, x, norm_mix, w_all, w_last, w_ba, cw, lcw, lcb, inj)
        oa, s_new = _delta_call(geo, l, p, ba, s0, alog, dtb, onorm)
        x, hlast = _outproj_call(geo, l, x, p, oa, h0, w_ri, b_r, b_i, lam, wa, wb, wo, norm2, wgu2, wd2,
                                 final_norm[None].astype(F32), final=(l == DEPTH - 1))
        outs_s.append(s_new)
        outs_tail.append(tails[last_ids, SLAB - 1::SLAB, :])
        outs_h.append(hlast[last_ids])

    n_prompt_rows = geo.prompt_chunks * CHUNK
    y_prompt, y_sample = _unpack_call(geo, x, dt)
    s_all = jnp.stack(outs_s)
    tail_all = jnp.stack(outs_tail)
    cq_all = tail_all[..., :QKV_W]
    cx_all = tail_all[..., QKV_W:]
    h_all = jnp.stack(outs_h)
    return (y_prompt.astype(dt), y_sample.astype(dt),
            s_all[:, :n_prompt].astype(dt), cq_all[:, :n_prompt].astype(dt),
            h_all[:, :n_prompt].astype(dt), cx_all[:, :n_prompt].astype(dt),
            s_all[:, n_prompt:].astype(state_delta_S.dtype),
            cq_all[:, n_prompt:].astype(state_delta_conv.dtype),
            h_all[:, n_prompt:].astype(state_lru_h.dtype),
            cx_all[:, n_prompt:].astype(state_lru_conv.dtype))
```

```python
import functools

import numpy as np
import jax
import jax.numpy as jnp
from jax import lax
from jax.experimental import pallas as pl
from jax.experimental.pallas import tpu as pltpu

F32 = jnp.float32
BF16 = jnp.bfloat16

D_MODEL = 1024
DEPTH = 4
N_META = 16
HA = 8
DK = 128
DV = 128
QK_W = HA * DK
V_W = HA * DV
QKV_W = 2 * QK_W + V_W
LRU_W = D_MODEL
NB = 8
BW = 128
CONV_W = 4
LRU_C = 8.0
D_FF = 2816
EPS = 1e-6

OFF_Z = QKV_W
OFF_BETA = OFF_Z + V_W
OFF_ALPHA = OFF_BETA + HA
OFF_LX = OFF_ALPHA + HA
OFF_LY = OFF_LX + LRU_W
OFF_GA = OFF_LY + LRU_W
OFF_GB = OFF_GA + D_MODEL
IN_COLS = OFF_GB + D_MODEL

CHUNK = 64
GROUP = 4
GROUP_ROWS = GROUP * CHUNK
PROMPT_NULL = CHUNK - N_META
SAMPLE_NULL = CHUNK // 2
INV_BASE = 16
ROW_TILE = 512
INPROJ_TILE = 256
FF_HALF = 256
CONV_COLS = QKV_W + LRU_W
SLAB = 8
N_SLAB = CHUNK // SLAB
TAIL_ROWS = (CONV_W - 1) * SLAB
P_Z, P_X, P_Y, P_G = 3072, 4096, 5120, 6144
P_COLS = 8192
VMEM_LIMIT = 56 * 1024 * 1024


def _rms(x, w):
    return x * lax.rsqrt(jnp.mean(x * x, axis=-1, keepdims=True) + EPS) * w


def _mm(a, b):
    return jnp.dot(a.astype(BF16), b.astype(BF16), preferred_element_type=F32)


def _mm_nt(a, b):
    return lax.dot_general(a.astype(BF16), b.astype(BF16), (((1,), (1,)), ((), ())),
                           preferred_element_type=F32)


def _mm_tn(a, b):
    return lax.dot_general(a.astype(BF16), b.astype(BF16), (((0,), (0,)), ((), ())),
                           preferred_element_type=F32)


def _sigmoid(x):
    return 0.5 * (1.0 + jnp.tanh(0.5 * x))


def _silu(x):
    return x * _sigmoid(x)


def _softplus(x):
    return jnp.maximum(x, 0.0) + jnp.log1p(jnp.exp(-jnp.abs(x)))


def _gelu_tanh(x):
    return 0.5 * x * (1.0 + jnp.tanh(0.7978845608028654 * (x + 0.044715 * (x * x * x))))


class _Geometry:
    def __init__(self, n_prompt, n_sample, chunks_per_prompt):
        assert n_prompt % GROUP == 0 and n_sample % GROUP == 0
        self.n_prompt = n_prompt
        self.n_sample = n_sample
        self.cpp = chunks_per_prompt
        self.prompt_steps = (n_prompt // GROUP) * chunks_per_prompt
        self.steps = self.prompt_steps + n_sample // GROUP
        self.prompt_chunks = GROUP * self.prompt_steps
        self.chunks = GROUP * self.steps
        self.rows = self.chunks * CHUNK
        self.n_streams = n_prompt + n_sample

    def chunk_null(self, cid):
        is_prompt = cid < self.prompt_chunks
        first = jnp.logical_or(jnp.logical_not(is_prompt), (cid // GROUP) % self.cpp == 0)
        return jnp.where(first, jnp.where(is_prompt, PROMPT_NULL, SAMPLE_NULL), 0)

    def last_chunk_ids(self):
        ids = [GROUP * ((b // GROUP) * self.cpp + self.cpp - 1) + b % GROUP for b in range(self.n_prompt)]
        ids += [self.prompt_chunks + t for t in range(self.n_sample)]
        return np.asarray(ids, np.int32)


def _row_time(rows):
    return N_SLAB * (rows % SLAB) + rows // SLAB


def _valid_rows(null):
    return _row_time(lax.broadcasted_iota(jnp.int32, (CHUNK, 1), 0)) >= null


def _slabs(x):
    return [x[v * SLAB:(v + 1) * SLAB] for v in range(N_SLAB)]


def _shift_run(x, fill):
    return jnp.concatenate([fill, x[:SLAB - 1]], axis=0)


def _scan_time(a, b, h_in):
    bs = _slabs(b)
    if a is None:
        for v in range(1, N_SLAB):
            bs[v] = bs[v] + bs[v - 1]
        run = bs[N_SLAB - 1]
        d = 1
        while d < SLAB:
            run = run + _shift_rows(run, d, 0.0)
            d *= 2
        run = run + h_in
        prev = _shift_run(run, h_in)
        hs = [t + prev for t in bs]
    else:
        as_ = _slabs(a)
        for v in range(1, N_SLAB):
            bs[v] = bs[v] + as_[v] * bs[v - 1]
            as_[v] = as_[v] * as_[v - 1]
        ra, rb = as_[N_SLAB - 1], bs[N_SLAB - 1]
        d = 1
        while d < SLAB:
            rb = rb + ra * _shift_rows(rb, d, 0.0)
            ra = ra * _shift_rows(ra, d, 1.0)
            d *= 2
        run = rb + ra * h_in
        prev = _shift_run(run, h_in)
        hs = [t + u * prev for t, u in zip(bs, as_)]
    return jnp.concatenate(hs, axis=0), run[SLAB - 1:SLAB, :]


def _ffn(x, nw, wgu_ref, wd_ref, h_ref):
    xn = _rms(x, nw).astype(BF16)
    for c in range(D_FF // FF_HALF):
        lo = c * FF_HALF
        g = jnp.dot(xn, wgu_ref[:, lo:lo + FF_HALF], preferred_element_type=F32)
        u = jnp.dot(xn, wgu_ref[:, D_FF + lo:D_FF + lo + FF_HALF], preferred_element_type=F32)
        h_ref[:, lo:lo + FF_HALF] = (_silu(g) * u).astype(BF16)
    y = jnp.dot(h_ref[...], wd_ref[...], preferred_element_type=F32)
    return x + 0.5 * y


def _ffn_kernel(x_ref, nw_ref, wgu_ref, wd_ref, o_ref, h_ref):
    o_ref[...] = _ffn(x_ref[...], nw_ref[...], wgu_ref, wd_ref, h_ref)


def _inproj_kernel(geo, x_ref, nw_ref, wa_ref, wt_ref, wl_ref, wba_ref, cw_ref, lcw_ref, lcb_ref, inj_ref,
                   p_ref, ba_ref, tails_ref, stage_all, carry_scr, wb_scr):
    i = pl.program_id(0)
    nch = INPROJ_TILE // CHUNK
    xn = _rms(x_ref[...], nw_ref[...]).astype(BF16)

    def project_cols(lo, width):
        w_ref, start = (wa_ref, lo) if lo < P_COLS // 2 else (wb_scr, lo - P_COLS // 2)
        return jnp.dot(xn, w_ref[:, start:start + width], preferred_element_type=F32)

    @pl.when(i == 0)
    def _():
        carry_scr[...] = jnp.zeros_like(carry_scr)
        skip = OFF_LX - OFF_BETA
        for c in range(P_COLS // 2 // 1024):
            nxt = wt_ref[:, (c + 1) * 1024:(c + 1) * 1024 + 128] if (c + 1) * 1024 < P_COLS // 2 else wl_ref[...]
            wide = jnp.concatenate([wt_ref[:, c * 1024:(c + 1) * 1024], nxt], axis=1)
            wb_scr[:, c * 1024:(c + 1) * 1024] = pltpu.roll(wide, wide.shape[1] - skip, 1)[:, :1024]

    valid = [_valid_rows(geo.chunk_null(i * nch + j)) for j in range(nch)]
    in_sample = i >= geo.prompt_chunks * CHUNK // INPROJ_TILE
    kinds = (("q", 0, 0), ("k", QK_W, QK_W), ("v", 2 * QK_W, 2 * QK_W), ("x", P_X, QKV_W))
    t0 = TAIL_ROWS
    part_w = 1024 // nch

    def project(kidx, part):
        kind, lo, col = kinds[kidx]
        stage = stage_all.at[kidx]
        pre = project_cols(lo + part * part_w, part_w)
        sc = slice(part * part_w, (part + 1) * part_w)
        cc = slice(col + part * part_w, col + (part + 1) * part_w)
        for j in range(nch):
            stage[j, t0:t0 + CHUNK, sc] = pre[j * CHUNK:(j + 1) * CHUNK]
            if j >= GROUP:
                stage[j, 0:t0, sc] = pre[(j - GROUP + 1) * CHUNK - t0:(j - GROUP + 1) * CHUNK]
            else:
                stage[j, 0:t0, sc] = carry_scr[j, :, cc]
            stage[j, CHUNK:CHUNK + t0, sc] = stage[j, CHUNK:CHUNK + t0, sc] + jnp.where(
                in_sample, inj_ref[t0 * j:t0 * (j + 1), cc], 0.0)
            tails_ref[j, :, cc] = stage[j, CHUNK:CHUNK + t0, sc]
        for g in range(GROUP):
            carry_scr[g, :, cc] = stage[nch - GROUP + g, CHUNK:CHUNK + t0, sc]

    def convolve(kidx, j):
        kind, lo, col = kinds[kidx]
        stage = stage_all.at[kidx]
        taps = lcw_ref[...] if kind == "x" else cw_ref[:, lo:lo + 1024]
        cur = [stage[j, t0 + v * SLAB:t0 + (v + 1) * SLAB, :] for v in range(N_SLAB)]
        back = []
        for k in range(CONV_W - 1):
            prev_last = stage[j, (k + 1) * SLAB - 1:(k + 1) * SLAB, :]
            back.append(_shift_run(cur[N_SLAB - (CONV_W - 1) + k], prev_last))
        conv = []
        for v in range(N_SLAB):
            acc = cur[v] * taps[CONV_W - 1:CONV_W, :]
            for d in range(1, CONV_W):
                src = cur[v - d] if v >= d else back[CONV_W - 1 - d + v]
                acc = acc + src * taps[CONV_W - 1 - d:CONV_W - d, :]
            conv.append(acc)
        conv = jnp.concatenate(conv, axis=0)
        if kind == "x":
            out = conv + lcb_ref[...]
        else:
            act = _silu(conv)
            if kind != "v":
                scale = DK ** -0.5 if kind == "q" else 1.0
                segs = []
                for h in range(HA):
                    seg = act[:, h * DK:(h + 1) * DK]
                    segs.append(seg * (lax.rsqrt(jnp.sum(seg * seg, axis=-1, keepdims=True) + EPS) * scale))
                act = jnp.concatenate(segs, axis=1)
            out = jnp.where(valid[j], act, 0.0)
        p_ref[j * CHUNK:(j + 1) * CHUNK, lo:lo + 1024] = out.astype(BF16)

    def plain(lo, part):
        sl = slice(lo + part * part_w, lo + (part + 1) * part_w)
        p_ref[:, sl] = project_cols(sl.start, part_w).astype(BF16)

    plain_cols = (P_Z, P_Y, P_G, P_G + 1024)
    for part in range(nch):
        project(0, part)
    for kidx in range(len(kinds)):
        for j in range(nch):
            plain(plain_cols[kidx], j)
            if kidx + 1 < len(kinds):
                project(kidx + 1, j)
            convolve(kidx, j)
    ba_ref[...] = jnp.dot(xn, wba_ref[...], preferred_element_type=F32)


def _shift_rows(x, d, fill):
    if d % 8 == 0:
        return jnp.concatenate([jnp.full((d, x.shape[1]), fill, x.dtype), x[:x.shape[0] - d]], axis=0)
    rows = lax.broadcasted_iota(jnp.int32, x.shape, 0)
    return jnp.where(rows >= d, pltpu.roll(x, d, 0), fill)


def _outproj_kernel(geo, final, x_ref, g_ref, oa_ref, xl_ref, h0_ref,
                    wri_ref, br_ref, bi_ref, lam_ref, wa_ref, wb_ref, wo_ref,
                    nw_ref, wgu_ref, wd_ref, fn_ref,
                    o_ref, hlast_ref, h_ref, hc_scr, ob_scr):
    i = pl.program_id(0)
    nch = ROW_TILE // CHUNK

    @pl.when(i == 0)
    def _():
        hc_scr[...] = jnp.zeros_like(hc_scr)

    sp = _softplus(-lam_ref[...])
    nulls = [geo.chunk_null(i * nch + j) for j in range(nch)]
    valid = [_valid_rows(n) for n in nulls]
    is_sample = [(i * nch + j) >= geo.prompt_chunks for j in range(nch)]
    for n in range(NB):
        sl = slice(n * BW, (n + 1) * BW)
        xb = xl_ref[:, sl]
        gates = jnp.dot(xb, wri_ref[n], preferred_element_type=F32)
        r = _sigmoid(gates[:, :BW] + br_ref[:, sl])
        ig = _sigmoid(gates[:, BW:] + bi_ref[:, sl])
        log_a = -LRU_C * r * sp[:, sl]
        a_all = jnp.exp(log_a)
        th = jnp.tanh(log_a)
        mult = jnp.sqrt(-2.0 * th / (1.0 - th))
        b_all = mult * (ig * xb.astype(F32))
        gy = _gelu_tanh(xl_ref[:, LRU_W + n * BW:LRU_W + (n + 1) * BW].astype(F32))
        carry = [hc_scr[g:g + 1, sl] for g in range(GROUP)]
        for j in range(nch):
            rows = slice(j * CHUNK, (j + 1) * CHUNK)
            a = jnp.where(valid[j], a_all[rows], 1.0)
            b = jnp.where(valid[j], b_all[rows], 0.0)
            h_in = jnp.where(nulls[j] > 0, jnp.where(is_sample[j], h0_ref[j:j + 1, sl], 0.0), carry[j % GROUP])
            hs, carry[j % GROUP] = _scan_time(a, b, h_in)
            ob_scr[rows, sl] = (hs * gy[rows]).astype(BF16)
            hlast_ref[j:j + 1, sl] = carry[j % GROUP]
        for g in range(GROUP):
            hc_scr[g:g + 1, sl] = carry[g]

    ga = _sigmoid(g_ref[:, :D_MODEL].astype(F32))
    gb = _sigmoid(g_ref[:, D_MODEL:].astype(F32))
    ma = jnp.dot(oa_ref[...], wa_ref[...], preferred_element_type=F32)
    mb = jnp.dot(ob_scr[...], wb_ref[...], preferred_element_type=F32)
    m = (ga * ma + gb * mb).astype(BF16)
    x = x_ref[...] + jnp.dot(m, wo_ref[...], preferred_element_type=F32)
    x = _ffn(x, nw_ref[...], wgu_ref, wd_ref, h_ref)
    if final:
        x = _rms(x, fn_ref[...])
    o_ref[...] = x


def _resident(shape, layer):
    nd = len(shape)
    return pl.BlockSpec((None,) + shape, lambda i: (layer,) + (0,) * nd, pipeline_mode=pl.Buffered(1))


def _params(semantics):
    return pltpu.CompilerParams(dimension_semantics=(semantics,), vmem_limit_bytes=VMEM_LIMIT)


def _ffn_call(layer, x, nw, wgu, wd):
    n = x.shape[0]
    row = lambda i: (i, 0)
    return pl.pallas_call(
        _ffn_kernel,
        grid=(n // ROW_TILE,),
        in_specs=[pl.BlockSpec((ROW_TILE, D_MODEL), row),
                  _resident((1, D_MODEL), layer),
                  _resident((D_MODEL, 2 * D_FF), layer),
                  _resident((D_FF, D_MODEL), layer)],
        out_specs=pl.BlockSpec((ROW_TILE, D_MODEL), row),
        out_shape=jax.ShapeDtypeStruct((n, D_MODEL), F32),
        scratch_shapes=[pltpu.VMEM((ROW_TILE, D_FF), BF16)],
        compiler_params=_params("parallel"),
        name="ffn",
    )(x, nw, wgu, wd)


def _inproj_call(geo, layer, x, nw, w_all, w_last, wba, cw, lcw, lcb, inj):
    n = x.shape[0]
    nch = INPROJ_TILE // CHUNK
    sample_tile0 = geo.prompt_chunks * CHUNK // INPROJ_TILE
    row = lambda i: (i, 0)
    half = P_COLS // 2
    assert OFF_BETA == half and w_all.shape[2] == 2 * half + OFF_LX - OFF_BETA and w_last.shape[2] == 128
    return pl.pallas_call(
        functools.partial(_inproj_kernel, geo),
        grid=(n // INPROJ_TILE,),
        in_specs=[pl.BlockSpec((INPROJ_TILE, D_MODEL), row),
                  _resident((1, D_MODEL), layer),
                  _resident((D_MODEL, half), layer),
                  pl.BlockSpec((None, D_MODEL, half), lambda i: (layer, 0, 1), pipeline_mode=pl.Buffered(1)),
                  _resident((D_MODEL, 128), layer),
                  _resident((D_MODEL, 256), layer),
                  _resident((CONV_W, QKV_W), layer),
                  _resident((CONV_W, LRU_W), layer),
                  _resident((1, LRU_W), layer),
                  pl.BlockSpec((None, TAIL_ROWS * nch, CONV_COLS),
                               lambda i: (layer, jnp.maximum(i - sample_tile0, 0), 0))],
        out_specs=[pl.BlockSpec((INPROJ_TILE, P_COLS), row),
                   pl.BlockSpec((INPROJ_TILE, 256), row),
                   pl.BlockSpec((nch, TAIL_ROWS, CONV_COLS), lambda i: (i, 0, 0))],
        out_shape=[jax.ShapeDtypeStruct((n, P_COLS), BF16),
                   jax.ShapeDtypeStruct((n, 256), F32),
                   jax.ShapeDtypeStruct((n // CHUNK, TAIL_ROWS, CONV_COLS), F32)],
        scratch_shapes=[pltpu.VMEM((4, nch, CHUNK + TAIL_ROWS, 1024), F32),
                        pltpu.VMEM((GROUP, TAIL_ROWS, CONV_COLS), F32),
                        pltpu.VMEM((D_MODEL, half), BF16)],
        compiler_params=_params("arbitrary"),
        name="inproj",
    )(x, nw, w_all, w_all, w_last, wba, cw, lcw, lcb, inj)


def _outproj_call(geo, layer, x, p, oa, h0, wri, br, bi, lam, wa, wb, wo, nw, wgu, wd, fn, final):
    n = x.shape[0]
    nch = ROW_TILE // CHUNK
    sample_tile0 = geo.prompt_chunks * CHUNK // ROW_TILE
    row = lambda i: (i, 0)
    return pl.pallas_call(
        functools.partial(_outproj_kernel, geo, final),
        grid=(n // ROW_TILE,),
        in_specs=[pl.BlockSpec((ROW_TILE, D_MODEL), row),
                  pl.BlockSpec((ROW_TILE, 2 * D_MODEL), lambda i: (i, P_G // 2048)),
                  pl.BlockSpec((ROW_TILE, V_W), row),
                  pl.BlockSpec((ROW_TILE, 2 * LRU_W), lambda i: (i, P_X // 2048)),
                  pl.BlockSpec((None, nch, LRU_W), lambda i: (layer, jnp.maximum(i - sample_tile0, 0), 0)),
                  _resident((NB, BW, 2 * BW), layer),
                  _resident((1, LRU_W), layer),
                  _resident((1, LRU_W), layer),
                  _resident((1, LRU_W), layer),
                  _resident((V_W, D_MODEL), layer),
                  _resident((LRU_W, D_MODEL), layer),
                  _resident((D_MODEL, D_MODEL), layer),
                  _resident((1, D_MODEL), layer),
                  _resident((D_MODEL, 2 * D_FF), layer),
                  _resident((D_FF, D_MODEL), layer),
                  pl.BlockSpec((1, D_MODEL), lambda i: (0, 0), pipeline_mode=pl.Buffered(1))],
        out_specs=[pl.BlockSpec((ROW_TILE, D_MODEL), row),
                   pl.BlockSpec((nch, LRU_W), row)],
        out_shape=[jax.ShapeDtypeStruct((n, D_MODEL), F32),
                   jax.ShapeDtypeStruct((n // CHUNK, LRU_W), F32)],
        scratch_shapes=[pltpu.VMEM((ROW_TILE, D_FF), BF16),
                        pltpu.VMEM((8, LRU_W), F32),
                        pltpu.VMEM((ROW_TILE, LRU_W), BF16)],
        compiler_params=_params("arbitrary"),
        name="outproj_ffn",
    )(x, p, oa, p, h0, wri, br, bi, lam, wa, wb, wo, nw, wgu, wd, fn)


def _unit_lower_inverse(lows, c):
    ti = _row_time(lax.broadcasted_iota(jnp.int32, (c, c), 0))
    tj = _row_time(lax.broadcasted_iota(jnp.int32, (c, c), 1))
    same_block = (ti // INV_BASE) == (tj // INV_BASE)
    eye = (ti == tj).astype(F32)
    ld = [jnp.where(same_block, low, 0.0) for low in lows]
    lo = [low - d for low, d in zip(lows, ld)]
    x = [eye - d for d in ld]
    p = ld
    span = 2
    while span < INV_BASE:
        p = [_mm(t, t) for t in p]
        x = [a + _mm(a, t) for a, t in zip(x, p)]
        span *= 2
    e = [_mm(a, b) for a, b in zip(x, lo)]
    y = [eye - t for t in e]
    q = e
    span = 2
    while span < c // INV_BASE:
        q = [_mm(t, t) for t in q]
        y = [a + _mm(a, t) for a, t in zip(y, q)]
        span *= 2
    return [_mm(a, b) for a, b in zip(y, x)]


def _delta_kernel(geo, qkv_ref, z_ref, ba_ref, s0_ref, alog_ref, dtb_ref, onorm_ref,
                  oa_ref, sout_ref, s_scr):
    c = CHUNK
    i = pl.program_id(0)
    is_prompt = i < geo.prompt_steps
    pos = i % geo.cpp
    first = jnp.logical_or(jnp.logical_not(is_prompt), pos == 0)
    last = jnp.logical_or(jnp.logical_not(is_prompt), pos == geo.cpp - 1)
    null = jnp.where(first, jnp.where(is_prompt, PROMPT_NULL, SAMPLE_NULL), 0)
    valid = _valid_rows(null)

    @pl.when(jnp.logical_and(first, is_prompt))
    def _():
        s_scr[...] = jnp.zeros_like(s_scr)

    @pl.when(jnp.logical_not(is_prompt))
    def _():
        s_scr[...] = s0_ref[...]

    ti = _row_time(lax.broadcasted_iota(jnp.int32, (c, c), 0))
    tj = _row_time(lax.broadcasted_iota(jnp.int32, (c, c), 1))
    tri_incl = ti >= tj
    tri_strict = ti > tj

    beta_all, gc_all, gc_t, eg_all, ekd_all, egl_all = [], [], [], [], [], []
    for s in range(GROUP):
        ba = ba_ref[s * c:(s + 1) * c, :]
        beta_all.append(jnp.where(valid, _sigmoid(ba[:, :128]), 0.0))
        g = -jnp.exp(alog_ref[...]) * _softplus(ba[:, 128:] + dtb_ref[...])
        gc, g_last = _scan_time(None, jnp.where(valid, g, 0.0), jnp.zeros((1, 128), F32))
        gc_all.append(gc)
        gc_t.append(jnp.concatenate([gc, jnp.zeros((128 - c, 128), F32)], axis=0).T)
        eg_all.append(jnp.exp(gc))
        ekd_all.append(jnp.exp(g_last - gc))
        egl_all.append(jnp.exp(g_last))

    probs = [(s, h) for s in range(GROUP) for h in range(HA)]
    rows = lambda s: slice(s * c, (s + 1) * c)
    qb = [qkv_ref[rows(s), h * DK:(h + 1) * DK] for s, h in probs]
    kbf = [qkv_ref[rows(s), QK_W + h * DK:QK_W + (h + 1) * DK] for s, h in probs]
    vb = [qkv_ref[rows(s), 2 * QK_W + h * DV:2 * QK_W + (h + 1) * DV] for s, h in probs]
    q = [t.astype(F32) for t in qb]
    k = [t.astype(F32) for t in kbf]
    beta = [beta_all[s][:, h:h + 1] for s, h in probs]
    eg = [eg_all[s][:, h:h + 1] for s, h in probs]
    n = len(probs)
    kb = [k[p] * beta[p] for p in range(n)]
    sc = [_mm_nt(jnp.concatenate([kb[p].astype(BF16), qb[p]], axis=0), kbf[p]) for p in range(n)]
    decay = [jnp.where(tri_incl, jnp.exp(jnp.where(tri_incl, gc_all[s][:, h:h + 1] - gc_t[s][h:h + 1, :c], 0.0)), 0.0)
             for s, h in probs]
    low = [jnp.where(tri_strict, sc[p][:c] * decay[p], 0.0) for p in range(n)]
    a_intra = [sc[p][c:] * decay[p] for p in range(n)]
    t_inv = _unit_lower_inverse(low, c)
    uw = [_mm(t_inv[p], jnp.concatenate([vb[p].astype(F32) * beta[p], kb[p] * eg[p]], axis=1))
          for p in range(n)]
    s_old = [s_scr[s, h] for s, h in probs]
    wq = [_mm(jnp.concatenate([uw[p][:, DV:], q[p] * eg[p]], axis=0), s_old[p]) for p in range(n)]
    v_new = [uw[p][:, :DV] - wq[p][:c] for p in range(n)]
    o_intra = [_mm(a_intra[p], v_new[p]) for p in range(n)]
    ds = [_mm_tn(k[p] * ekd_all[s][:, h:h + 1], v_new[p]) for p, (s, h) in enumerate(probs)]
    for p, (s, h) in enumerate(probs):
        s_scr[s, h] = s_old[p] * egl_all[s][:, h:h + 1] + ds[p]
        o = wq[p][c:] + o_intra[p]
        zh = z_ref[rows(s), h * DV:(h + 1) * DV].astype(F32)
        o = o * lax.rsqrt(jnp.mean(o * o, axis=-1, keepdims=True) + EPS) * onorm_ref[...] * _silu(zh)
        oa_ref[rows(s), h * DV:(h + 1) * DV] = o.astype(oa_ref.dtype)

    @pl.when(last)
    def _():
        sout_ref[...] = s_scr[...]


def _delta_call(geo, layer, p, ba, s0, alog, dtb, onorm):
    n_pp = geo.n_prompt // GROUP

    def pair(i):
        return jnp.where(i < geo.prompt_steps, i // geo.cpp, i - geo.prompt_steps + n_pp)

    vec = lambda width: pl.BlockSpec((None, 1, width), lambda i: (layer, 0, 0))
    return pl.pallas_call(
        functools.partial(_delta_kernel, geo),
        grid=(geo.steps,),
        in_specs=[pl.BlockSpec((GROUP_ROWS, QKV_W), lambda i: (i, 0)),
                  pl.BlockSpec((GROUP_ROWS, V_W), lambda i: (i, P_Z // V_W)),
                  pl.BlockSpec((GROUP_ROWS, 256), lambda i: (i, 0)),
                  pl.BlockSpec((None, GROUP, HA, DK, DV),
                               lambda i: (layer, jnp.maximum(i - geo.prompt_steps, 0), 0, 0, 0)),
                  vec(128), vec(128), vec(DV)],
        out_specs=[pl.BlockSpec((GROUP_ROWS, V_W), lambda i: (i, 0)),
                   pl.BlockSpec((GROUP, HA, DK, DV), lambda i: (pair(i), 0, 0, 0))],
        out_shape=[jax.ShapeDtypeStruct((geo.rows, V_W), BF16),
                   jax.ShapeDtypeStruct((geo.n_streams, HA, DK, DV), F32)],
        scratch_shapes=[pltpu.VMEM((GROUP, HA, DK, DV), F32)],
        compiler_params=_params("arbitrary"),
        name="delta",
    )(p, p, ba, s0, alog, dtb, onorm)


def _permute_chunk(x):
    return jnp.swapaxes(x.reshape(SLAB, N_SLAB, x.shape[-1]), 0, 1).reshape(CHUNK, x.shape[-1])


def _pack_kernel(geo, xp_ref, xs_ref, meta_ref, o_ref):
    i = pl.program_id(0)
    is_prompt = i < geo.prompt_steps
    pos = i % geo.cpp
    width = o_ref.shape[1]

    @pl.when(jnp.logical_and(is_prompt, pos > 0))
    def _():
        for s in range(GROUP):
            o_ref[s * CHUNK:(s + 1) * CHUNK, :] = _permute_chunk(xp_ref[s].astype(o_ref.dtype))

    @pl.when(jnp.logical_and(is_prompt, pos == 0))
    def _():
        first = _permute_chunk(jnp.concatenate(
            [jnp.zeros((PROMPT_NULL, width), o_ref.dtype), meta_ref[...].astype(o_ref.dtype)], axis=0))
        for s in range(GROUP):
            o_ref[s * CHUNK:(s + 1) * CHUNK, :] = first

    @pl.when(jnp.logical_not(is_prompt))
    def _():
        for s in range(GROUP):
            o_ref[s * CHUNK:(s + 1) * CHUNK, :] = _permute_chunk(jnp.concatenate(
                [jnp.zeros((SAMPLE_NULL, width), o_ref.dtype), xs_ref[s].astype(o_ref.dtype)], axis=0))


def _unpack_kernel(geo, x_ref, yp_ref, ys_ref):
    i = pl.program_id(0)
    is_prompt = i < geo.prompt_steps
    pos = i % geo.cpp

    @pl.when(jnp.logical_and(is_prompt, pos > 0))
    def _():
        for s in range(GROUP):
            yp_ref[s] = _permute_chunk(x_ref[s * CHUNK:(s + 1) * CHUNK, :]).astype(yp_ref.dtype)

    @pl.when(jnp.logical_not(is_prompt))
    def _():
        for s in range(GROUP):
            ys_ref[s] = _permute_chunk(x_ref[s * CHUNK:(s + 1) * CHUNK, :])[SAMPLE_NULL:].astype(ys_ref.dtype)


def _layout_specs(geo):
    def prompt_idx(i):
        step = jnp.minimum(i, geo.prompt_steps - 1)
        return (step // geo.cpp, jnp.maximum(step % geo.cpp - 1, 0), 0)

    def sample_idx(i):
        return (jnp.maximum(i - geo.prompt_steps, 0), 0, 0)

    return (pl.BlockSpec((GROUP, CHUNK, D_MODEL), prompt_idx),
            pl.BlockSpec((GROUP, CHUNK - SAMPLE_NULL, D_MODEL), sample_idx))


def _pack_call(geo, x_prompt, x_sample, meta):
    prompt_spec, sample_spec = _layout_specs(geo)
    return pl.pallas_call(
        functools.partial(_pack_kernel, geo),
        grid=(geo.steps,),
        in_specs=[prompt_spec, sample_spec, pl.BlockSpec((N_META, D_MODEL), lambda i: (0, 0))],
        out_specs=pl.BlockSpec((GROUP_ROWS, D_MODEL), lambda i: (i, 0)),
        out_shape=jax.ShapeDtypeStruct((geo.rows, D_MODEL), F32),
        compiler_params=_params("parallel"),
        name="pack",
    )(x_prompt, x_sample, meta)


def _unpack_call(geo, x, dtype):
    prompt_spec, sample_spec = _layout_specs(geo)
    return pl.pallas_call(
        functools.partial(_unpack_kernel, geo),
        grid=(geo.steps,),
        in_specs=[pl.BlockSpec((GROUP_ROWS, D_MODEL), lambda i: (i, 0))],
        out_specs=[prompt_spec, sample_spec],
        out_shape=[jax.ShapeDtypeStruct((geo.n_prompt, (geo.cpp - 1) * CHUNK, D_MODEL), dtype),
                   jax.ShapeDtypeStruct((geo.n_sample, CHUNK - SAMPLE_NULL, D_MODEL), dtype)],
        compiler_params=_params("arbitrary"),
        name="unpack",
    )(x)


def _pad_lanes(v, width):
    return jnp.pad(v, ((0, 0), (0, width - v.shape[-1])))


def kernel(x_prompt, x_sample, state_delta_S, state_delta_conv, state_lru_h, state_lru_conv,
           meta_tokens, ffn1_norm, ffn1_w_gu, ffn1_w_down, mix_norm, w_in, delta_conv_w,
           delta_A_log, delta_dt_bias, delta_out_norm, lru_conv_w, lru_conv_b, lru_w_r, lru_b_r,
           lru_w_i, lru_b_i, lru_lambda, w_branch_a, w_branch_b, w_out, ffn2_norm, ffn2_w_gu,
           ffn2_w_down, final_norm):
    n_prompt, seq_len, _ = x_prompt.shape
    n_sample, dec_len, _ = x_sample.shape
    assert dec_len == CHUNK - SAMPLE_NULL and (N_META + seq_len) % CHUNK == N_META
    prompt_rows = PROMPT_NULL + N_META + seq_len
    cpp = prompt_rows // CHUNK
    geo = _Geometry(n_prompt, n_sample, cpp)
    assert geo.rows % ROW_TILE == 0 and (geo.prompt_chunks * CHUNK) % ROW_TILE == 0
    assert INPROJ_TILE % GROUP_ROWS == 0 and ROW_TILE % GROUP_ROWS == 0
    dt = x_prompt.dtype

    x = _pack_call(geo, x_prompt, x_sample, meta_tokens)

    cast = lambda w: w.astype(BF16)
    vec = lambda v: v.astype(F32)[:, None, :]
    wgu1, wd1, wgu2, wd2 = cast(ffn1_w_gu), cast(ffn1_w_down), cast(ffn2_w_gu), cast(ffn2_w_down)
    w_all = cast(w_in)
    pad_heads = lambda w: jnp.pad(w, ((0, 0), (0, 0), (0, 128 - w.shape[-1])))
    w_last = cast(pad_heads(w_in[:, :, P_COLS:]))
    w_ba = cast(jnp.concatenate([pad_heads(w_in[:, :, OFF_BETA:OFF_ALPHA]),
                                 pad_heads(w_in[:, :, OFF_ALPHA:OFF_LX])], axis=-1))
    w_ri = cast(jnp.concatenate([lru_w_r, lru_w_i], axis=-1))
    wa, wb, wo = cast(w_branch_a), cast(w_branch_b), cast(w_out)
    norm1, norm_mix, norm2 = vec(ffn1_norm), vec(mix_norm), vec(ffn2_norm)
    alog, dtb, onorm = vec(_pad_lanes(delta_A_log, 128)), vec(_pad_lanes(delta_dt_bias, 128)), vec(delta_out_norm)
    lcb, b_r, b_i, lam = vec(lru_conv_b), vec(lru_b_r), vec(lru_b_i), vec(lru_lambda)
    cw, lcw = delta_conv_w.astype(F32), lru_conv_w.astype(F32)
    s0, h0 = state_delta_S.astype(F32), state_lru_h.astype(F32)

    inj = jnp.concatenate([state_delta_conv, state_lru_conv], axis=-1).astype(F32)
    null_runs = SAMPLE_NULL // N_SLAB
    inj = jnp.pad(inj[:, :, :, None, :], ((0, 0), (0, 0), (0, 0), (null_runs - 1, SLAB - null_runs), (0, 0)))
    inj = inj.reshape(DEPTH, n_sample * TAIL_ROWS, CONV_COLS)

    last_ids = geo.last_chunk_ids()
    outs_s, outs_tail, outs_h = [], [], []
    for l in range(DEPTH):
        x = _ffn_call(l, x, norm1, wgu1, wd1)
        p, ba, tails = _inproj_call(geo, l, x, norm_mix, w_all, w_last, w_ba, cw, lcw, lcb, inj)
        oa, s_new = _delta_call(geo, l, p, ba, s0, alog, dtb, onorm)
        x, hlast = _outproj_call(geo, l, x, p, oa, h0, w_ri, b_r, b_i, lam, wa, wb, wo, norm2, wgu2, wd2,
                                 final_norm[None].astype(F32), final=(l == DEPTH - 1))
        outs_s.append(s_new)
        outs_tail.append(tails[last_ids, SLAB - 1::SLAB, :])
        outs_h.append(hlast[last_ids])

    n_prompt_rows = geo.prompt_chunks * CHUNK
    y_prompt, y_sample = _unpack_call(geo, x, dt)
    s_all = jnp.stack(outs_s)
    tail_all = jnp.stack(outs_tail)
    cq_all = tail_all[..., :QKV_W]
    cx_all = tail_all[..., QKV_W:]
    h_all = jnp.stack(outs_h)
    return (y_prompt.astype(dt), y_sample.astype(dt),
            s_all[:, :n_prompt].astype(dt), cq_all[:, :n_prompt].astype(dt),
            h_all[:, :n_prompt].astype(dt), cx_all[:, :n_prompt].astype(dt),
            s_all[:, n_prompt:].astype(state_delta_S.dtype),
            cq_all[:, n_prompt:].astype(state_delta_conv.dtype),
            h_all[:, n_prompt:].astype(state_lru_h.dtype),
            cx_all[:, n_prompt:].astype(state_lru_conv.dtype))
```

```python
import functools

import numpy as np
import jax
import jax.numpy as jnp
from jax import lax
from jax.experimental import pallas as pl
from jax.experimental.pallas import tpu as pltpu

F32 = jnp.float32
BF16 = jnp.bfloat16

D_MODEL = 1024
DEPTH = 4
N_META = 16
HA = 8
DK = 128
DV = 128
QK_W = HA * DK
V_W = HA * DV
QKV_W = 2 * QK_W + V_W
LRU_W = D_MODEL
NB = 8
BW = 128
CONV_W = 4
LRU_C = 8.0
D_FF = 2816
EPS = 1e-6

OFF_Z = QKV_W
OFF_BETA = OFF_Z + V_W
OFF_ALPHA = OFF_BETA + HA
OFF_LX = OFF_ALPHA + HA
OFF_LY = OFF_LX + LRU_W
OFF_GA = OFF_LY + LRU_W
OFF_GB = OFF_GA + D_MODEL
IN_COLS = OFF_GB + D_MODEL

CHUNK = 64
GROUP = 4
GROUP_ROWS = GROUP * CHUNK
PROMPT_NULL = CHUNK - N_META
SAMPLE_NULL = CHUNK // 2
INV_BASE = 16
ROW_TILE = 512
INPROJ_TILE = 256
FF_HALF = 256
CONV_COLS = QKV_W + LRU_W
SLAB = 8
N_SLAB = CHUNK // SLAB
TAIL_ROWS = (CONV_W - 1) * SLAB
P_Z, P_X, P_Y, P_G = 3072, 4096, 5120, 6144
P_COLS = 8192
VMEM_LIMIT = 56 * 1024 * 1024


def _rms(x, w):
    return x * lax.rsqrt(jnp.mean(x * x, axis=-1, keepdims=True) + EPS) * w


def _mm(a, b):
    return jnp.dot(a.astype(BF16), b.astype(BF16), preferred_element_type=F32)


def _mm_nt(a, b):
    return lax.dot_general(a.astype(BF16), b.astype(BF16), (((1,), (1,)), ((), ())),
                           preferred_element_type=F32)


def _mm_tn(a, b):
    return lax.dot_general(a.astype(BF16), b.astype(BF16), (((0,), (0,)), ((), ())),
                           preferred_element_type=F32)


def _sigmoid(x):
    return 0.5 * (1.0 + jnp.tanh(0.5 * x))


def _silu(x):
    return x * _sigmoid(x)


def _softplus(x):
    return jnp.maximum(x, 0.0) + jnp.log1p(jnp.exp(-jnp.abs(x)))


def _gelu_tanh(x):
    return 0.5 * x * (1.0 + jnp.tanh(0.7978845608028654 * (x + 0.044715 * (x * x * x))))


class _Geometry:
    def __init__(self, n_prompt, n_sample, chunks_per_prompt):
        assert n_prompt % GROUP == 0 and n_sample % GROUP == 0
        self.n_prompt = n_prompt
        self.n_sample = n_sample
        self.cpp = chunks_per_prompt
        self.prompt_steps = (n_prompt // GROUP) * chunks_per_prompt
        self.steps = self.prompt_steps + n_sample // GROUP
        self.prompt_chunks = GROUP * self.prompt_steps
        self.chunks = GROUP * self.steps
        self.rows = self.chunks * CHUNK
        self.n_streams = n_prompt + n_sample

    def chunk_null(self, cid):
        is_prompt = cid < self.prompt_chunks
        first = jnp.logical_or(jnp.logical_not(is_prompt), (cid // GROUP) % self.cpp == 0)
        return jnp.where(first, jnp.where(is_prompt, PROMPT_NULL, SAMPLE_NULL), 0)

    def last_chunk_ids(self):
        ids = [GROUP * ((b // GROUP) * self.cpp + self.cpp - 1) + b % GROUP for b in range(self.n_prompt)]
        ids += [self.prompt_chunks + t for t in range(self.n_sample)]
        return np.asarray(ids, np.int32)


def _row_time(rows):
    return N_SLAB * (rows % SLAB) + rows // SLAB


def _valid_rows(null):
    return _row_time(lax.broadcasted_iota(jnp.int32, (CHUNK, 1), 0)) >= null


def _slabs(x):
    return [x[v * SLAB:(v + 1) * SLAB] for v in range(N_SLAB)]


def _shift_run(x, fill):
    return jnp.concatenate([fill, x[:SLAB - 1]], axis=0)


def _scan_time(a, b, h_in):
    bs = _slabs(b)
    if a is None:
        for v in range(1, N_SLAB):
            bs[v] = bs[v] + bs[v - 1]
        run = bs[N_SLAB - 1]
        d = 1
        while d < SLAB:
            run = run + _shift_rows(run, d, 0.0)
            d *= 2
        run = run + h_in
        prev = _shift_run(run, h_in)
        hs = [t + prev for t in bs]
    else:
        as_ = _slabs(a)
        for v in range(1, N_SLAB):
            bs[v] = bs[v] + as_[v] * bs[v - 1]
            as_[v] = as_[v] * as_[v - 1]
        ra, rb = as_[N_SLAB - 1], bs[N_SLAB - 1]
        d = 1
        while d < SLAB:
            rb = rb + ra * _shift_rows(rb, d, 0.0)
            ra = ra * _shift_rows(ra, d, 1.0)
            d *= 2
        run = rb + ra * h_in
        prev = _shift_run(run, h_in)
        hs = [t + u * prev for t, u in zip(bs, as_)]
    return jnp.concatenate(hs, axis=0), run[SLAB - 1:SLAB, :]


def _ffn(x, nw, wgu_ref, wd_ref, h_ref, between=None):
    xn = _rms(x, nw).astype(BF16)
    for c in range(D_FF // FF_HALF):
        lo = c * FF_HALF
        g = jnp.dot(xn, wgu_ref[:, lo:lo + FF_HALF], preferred_element_type=F32)
        u = jnp.dot(xn, wgu_ref[:, D_FF + lo:D_FF + lo + FF_HALF], preferred_element_type=F32)
        h_ref[:, lo:lo + FF_HALF] = (_silu(g) * u).astype(BF16)
        if between is not None and c % 2 == 0:
            between()
    y = jnp.dot(h_ref[...], wd_ref[...], preferred_element_type=F32)
    return x + 0.5 * y


def _ffn_kernel(x_ref, nw_ref, wgu_ref, wd_ref, o_ref, h_ref):
    o_ref[...] = _ffn(x_ref[...], nw_ref[...], wgu_ref, wd_ref, h_ref)


def _inproj_kernel(geo, x_ref, nw_ref, wa_ref, wt_ref, wl_ref, wba_ref, cw_ref, lcw_ref, lcb_ref, inj_ref,
                   p_ref, ba_ref, tails_ref, stage_all, carry_scr, wb_scr):
    i = pl.program_id(0)
    nch = INPROJ_TILE // CHUNK
    xn = _rms(x_ref[...], nw_ref[...]).astype(BF16)

    def project_cols(lo, width):
        w_ref, start = (wa_ref, lo) if lo < P_COLS // 2 else (wb_scr, lo - P_COLS // 2)
        return jnp.dot(xn, w_ref[:, start:start + width], preferred_element_type=F32)

    @pl.when(i == 0)
    def _():
        carry_scr[...] = jnp.zeros_like(carry_scr)
        skip = OFF_LX - OFF_BETA
        for c in range(P_COLS // 2 // 1024):
            nxt = wt_ref[:, (c + 1) * 1024:(c + 1) * 1024 + 128] if (c + 1) * 1024 < P_COLS // 2 else wl_ref[...]
            wide = jnp.concatenate([wt_ref[:, c * 1024:(c + 1) * 1024], nxt], axis=1)
            wb_scr[:, c * 1024:(c + 1) * 1024] = pltpu.roll(wide, wide.shape[1] - skip, 1)[:, :1024]

    valid = [_valid_rows(geo.chunk_null(i * nch + j)) for j in range(nch)]
    in_sample = i >= geo.prompt_chunks * CHUNK // INPROJ_TILE
    kinds = (("q", 0, 0), ("k", QK_W, QK_W), ("v", 2 * QK_W, 2 * QK_W), ("x", P_X, QKV_W))
    t0 = TAIL_ROWS
    part_w = 1024 // nch

    def project(kidx, part):
        kind, lo, col = kinds[kidx]
        stage = stage_all.at[kidx]
        pre = project_cols(lo + part * part_w, part_w)
        sc = slice(part * part_w, (part + 1) * part_w)
        cc = slice(col + part * part_w, col + (part + 1) * part_w)
        for j in range(nch):
            stage[j, t0:t0 + CHUNK, sc] = pre[j * CHUNK:(j + 1) * CHUNK]
            if j >= GROUP:
                stage[j, 0:t0, sc] = pre[(j - GROUP + 1) * CHUNK - t0:(j - GROUP + 1) * CHUNK]
            else:
                stage[j, 0:t0, sc] = carry_scr[j, :, cc]
            stage[j, CHUNK:CHUNK + t0, sc] = stage[j, CHUNK:CHUNK + t0, sc] + jnp.where(
                in_sample, inj_ref[t0 * j:t0 * (j + 1), cc], 0.0)
            tails_ref[j, :, cc] = stage[j, CHUNK:CHUNK + t0, sc]
        for g in range(GROUP):
            carry_scr[g, :, cc] = stage[nch - GROUP + g, CHUNK:CHUNK + t0, sc]

    def convolve(kidx, j):
        kind, lo, col = kinds[kidx]
        stage = stage_all.at[kidx]
        taps = lcw_ref[...] if kind == "x" else cw_ref[:, lo:lo + 1024]
        cur = [stage[j, t0 + v * SLAB:t0 + (v + 1) * SLAB, :] for v in range(N_SLAB)]
        back = []
        for k in range(CONV_W - 1):
            prev_last = stage[j, (k + 1) * SLAB - 1:(k + 1) * SLAB, :]
            back.append(_shift_run(cur[N_SLAB - (CONV_W - 1) + k], prev_last))
        conv = []
        for v in range(N_SLAB):
            acc = cur[v] * taps[CONV_W - 1:CONV_W, :]
            for d in range(1, CONV_W):
                src = cur[v - d] if v >= d else back[CONV_W - 1 - d + v]
                acc = acc + src * taps[CONV_W - 1 - d:CONV_W - d, :]
            conv.append(acc)
        conv = jnp.concatenate(conv, axis=0)
        if kind == "x":
            out = conv + lcb_ref[...]
        else:
            act = _silu(conv)
            if kind != "v":
                scale = DK ** -0.5 if kind == "q" else 1.0
                segs = []
                for h in range(HA):
                    seg = act[:, h * DK:(h + 1) * DK]
                    segs.append(seg * (lax.rsqrt(jnp.sum(seg * seg, axis=-1, keepdims=True) + EPS) * scale))
                act = jnp.concatenate(segs, axis=1)
            out = jnp.where(valid[j], act, 0.0)
        p_ref[j * CHUNK:(j + 1) * CHUNK, lo:lo + 1024] = out.astype(BF16)

    def plain(lo, part):
        sl = slice(lo + part * part_w, lo + (part + 1) * part_w)
        p_ref[:, sl] = project_cols(sl.start, part_w).astype(BF16)

    plain_cols = (P_Z, P_Y, P_G, P_G + 1024)
    for part in range(nch):
        project(0, part)
    for kidx in range(len(kinds)):
        for j in range(nch):
            plain(plain_cols[kidx], j)
            if kidx + 1 < len(kinds):
                project(kidx + 1, j)
            convolve(kidx, j)
    ba_ref[...] = jnp.dot(xn, wba_ref[...], preferred_element_type=F32)


def _shift_rows(x, d, fill):
    if d % 8 == 0:
        return jnp.concatenate([jnp.full((d, x.shape[1]), fill, x.dtype), x[:x.shape[0] - d]], axis=0)
    rows = lax.broadcasted_iota(jnp.int32, x.shape, 0)
    return jnp.where(rows >= d, pltpu.roll(x, d, 0), fill)


def _outproj_kernel(geo, final, n_tiles, x_ref, g_ref, oa_ref, xl_ref, h0_ref,
                    wri_ref, br_ref, bi_ref, lam_ref, wa_ref, wb_ref, wo_ref,
                    nw_ref, wgu_ref, wd_ref, fn_ref,
                    o_ref, hlast_ref, h_ref, hc_scr, ob_scr, hl_scr):
    s = pl.program_id(0)
    nch = ROW_TILE // CHUNK
    live = s < n_tiles
    i = jnp.minimum(s, n_tiles - 1)
    new, old = s % 2, (s + 1) % 2

    @pl.when(s == 0)
    def _():
        hc_scr[...] = jnp.zeros_like(hc_scr)
        ob_scr[...] = jnp.zeros_like(ob_scr)
        hl_scr[...] = jnp.zeros_like(hl_scr)

    sp = _softplus(-lam_ref[...])
    nulls = [geo.chunk_null(i * nch + j) for j in range(nch)]
    valid = [_valid_rows(n) for n in nulls]
    is_sample = [(i * nch + j) >= geo.prompt_chunks for j in range(nch)]

    def scan_block(n):
        sl = slice(n * BW, (n + 1) * BW)
        xb = xl_ref[:, sl]
        gates = jnp.dot(xb, wri_ref[n], preferred_element_type=F32)
        r = _sigmoid(gates[:, :BW] + br_ref[:, sl])
        ig = _sigmoid(gates[:, BW:] + bi_ref[:, sl])
        log_a = -LRU_C * r * sp[:, sl]
        a_all = jnp.exp(log_a)
        th = jnp.tanh(log_a)
        mult = jnp.sqrt(-2.0 * th / (1.0 - th))
        b_all = mult * (ig * xb.astype(F32))
        gy = _gelu_tanh(xl_ref[:, LRU_W + n * BW:LRU_W + (n + 1) * BW].astype(F32))
        carry = [hc_scr[g:g + 1, sl] for g in range(GROUP)]
        for j in range(nch):
            rows = slice(j * CHUNK, (j + 1) * CHUNK)
            a = jnp.where(valid[j], a_all[rows], 1.0)
            b = jnp.where(valid[j], b_all[rows], 0.0)
            h_in = jnp.where(nulls[j] > 0, jnp.where(is_sample[j], h0_ref[j:j + 1, sl], 0.0), carry[j % GROUP])
            hs, carry[j % GROUP] = _scan_time(a, b, h_in)
            ob_scr[new, rows, sl] = (hs * gy[rows]).astype(BF16)
            hl_scr[j:j + 1, sl] = jnp.where(live, carry[j % GROUP], hl_scr[j:j + 1, sl])
            hlast_ref[j:j + 1, sl] = hl_scr[j:j + 1, sl]
        for g in range(GROUP):
            hc_scr[g:g + 1, sl] = carry[g]

    blocks = iter(range(NB))

    def next_scan_block():
        n = next(blocks, None)
        if n is not None:
            scan_block(n)

    ga = _sigmoid(g_ref[:, :D_MODEL].astype(F32))
    gb = _sigmoid(g_ref[:, D_MODEL:].astype(F32))
    ma = jnp.dot(oa_ref[...], wa_ref[...], preferred_element_type=F32)
    next_scan_block()
    mb = jnp.dot(ob_scr[old], wb_ref[...], preferred_element_type=F32)
    next_scan_block()
    m = (ga * ma + gb * mb).astype(BF16)
    x = x_ref[...] + jnp.dot(m, wo_ref[...], preferred_element_type=F32)
    next_scan_block()
    x = _ffn(x, nw_ref[...], wgu_ref, wd_ref, h_ref, between=next_scan_block)
    for n in blocks:
        scan_block(n)
    if final:
        x = _rms(x, fn_ref[...])
    o_ref[...] = x


def _resident(shape, layer):
    nd = len(shape)
    return pl.BlockSpec((None,) + shape, lambda i: (layer,) + (0,) * nd, pipeline_mode=pl.Buffered(1))


def _params(semantics):
    return pltpu.CompilerParams(dimension_semantics=(semantics,), vmem_limit_bytes=VMEM_LIMIT)


def _ffn_call(layer, x, nw, wgu, wd):
    n = x.shape[0]
    row = lambda i: (i, 0)
    return pl.pallas_call(
        _ffn_kernel,
        grid=(n // ROW_TILE,),
        in_specs=[pl.BlockSpec((ROW_TILE, D_MODEL), row),
                  _resident((1, D_MODEL), layer),
                  _resident((D_MODEL, 2 * D_FF), layer),
                  _resident((D_FF, D_MODEL), layer)],
        out_specs=pl.BlockSpec((ROW_TILE, D_MODEL), row),
        out_shape=jax.ShapeDtypeStruct((n, D_MODEL), F32),
        scratch_shapes=[pltpu.VMEM((ROW_TILE, D_FF), BF16)],
        compiler_params=_params("parallel"),
        name="ffn",
    )(x, nw, wgu, wd)


def _inproj_call(geo, layer, x, nw, w_all, w_last, wba, cw, lcw, lcb, inj):
    n = x.shape[0]
    nch = INPROJ_TILE // CHUNK
    sample_tile0 = geo.prompt_chunks * CHUNK // INPROJ_TILE
    row = lambda i: (i, 0)
    half = P_COLS // 2
    assert OFF_BETA == half and w_all.shape[2] == 2 * half + OFF_LX - OFF_BETA and w_last.shape[2] == 128
    return pl.pallas_call(
        functools.partial(_inproj_kernel, geo),
        grid=(n // INPROJ_TILE,),
        in_specs=[pl.BlockSpec((INPROJ_TILE, D_MODEL), row),
                  _resident((1, D_MODEL), layer),
                  _resident((D_MODEL, half), layer),
                  pl.BlockSpec((None, D_MODEL, half), lambda i: (layer, 0, 1), pipeline_mode=pl.Buffered(1)),
                  _resident((D_MODEL, 128), layer),
                  _resident((D_MODEL, 256), layer),
                  _resident((CONV_W, QKV_W), layer),
                  _resident((CONV_W, LRU_W), layer),
                  _resident((1, LRU_W), layer),
                  pl.BlockSpec((None, TAIL_ROWS * nch, CONV_COLS),
                               lambda i: (layer, jnp.maximum(i - sample_tile0, 0), 0))],
        out_specs=[pl.BlockSpec((INPROJ_TILE, P_COLS), row),
                   pl.BlockSpec((INPROJ_TILE, 256), row),
                   pl.BlockSpec((nch, TAIL_ROWS, CONV_COLS), lambda i: (i, 0, 0))],
        out_shape=[jax.ShapeDtypeStruct((n, P_COLS), BF16),
                   jax.ShapeDtypeStruct((n, 256), F32),
                   jax.ShapeDtypeStruct((n // CHUNK, TAIL_ROWS, CONV_COLS), F32)],
        scratch_shapes=[pltpu.VMEM((4, nch, CHUNK + TAIL_ROWS, 1024), F32),
                        pltpu.VMEM((GROUP, TAIL_ROWS, CONV_COLS), F32),
                        pltpu.VMEM((D_MODEL, half), BF16)],
        compiler_params=_params("arbitrary"),
        name="inproj",
    )(x, nw, w_all, w_all, w_last, wba, cw, lcw, lcb, inj)


def _outproj_call(geo, layer, x, p, oa, h0, wri, br, bi, lam, wa, wb, wo, nw, wgu, wd, fn, final):
    n = x.shape[0]
    nch = ROW_TILE // CHUNK
    sample_tile0 = geo.prompt_chunks * CHUNK // ROW_TILE
    n_tiles = n // ROW_TILE
    scan_tile = lambda s: jnp.minimum(s, n_tiles - 1)
    row = lambda s: (jnp.maximum(s - 1, 0), 0)
    return pl.pallas_call(
        functools.partial(_outproj_kernel, geo, final, n_tiles),
        grid=(n_tiles + 1,),
        in_specs=[pl.BlockSpec((ROW_TILE, D_MODEL), row),
                  pl.BlockSpec((ROW_TILE, 2 * D_MODEL), lambda s: (jnp.maximum(s - 1, 0), P_G // 2048)),
                  pl.BlockSpec((ROW_TILE, V_W), row),
                  pl.BlockSpec((ROW_TILE, 2 * LRU_W), lambda s: (scan_tile(s), P_X // 2048)),
                  pl.BlockSpec((None, nch, LRU_W),
                               lambda s: (layer, jnp.maximum(scan_tile(s) - sample_tile0, 0), 0)),
                  _resident((NB, BW, 2 * BW), layer),
                  _resident((1, LRU_W), layer),
                  _resident((1, LRU_W), layer),
                  _resident((1, LRU_W), layer),
                  _resident((V_W, D_MODEL), layer),
                  _resident((LRU_W, D_MODEL), layer),
                  _resident((D_MODEL, D_MODEL), layer),
                  _resident((1, D_MODEL), layer),
                  _resident((D_MODEL, 2 * D_FF), layer),
                  _resident((D_FF, D_MODEL), layer),
                  pl.BlockSpec((1, D_MODEL), lambda i: (0, 0), pipeline_mode=pl.Buffered(1))],
        out_specs=[pl.BlockSpec((ROW_TILE, D_MODEL), row),
                   pl.BlockSpec((nch, LRU_W), lambda s: (scan_tile(s), 0))],
        out_shape=[jax.ShapeDtypeStruct((n, D_MODEL), F32),
                   jax.ShapeDtypeStruct((n // CHUNK, LRU_W), F32)],
        scratch_shapes=[pltpu.VMEM((ROW_TILE, D_FF), BF16),
                        pltpu.VMEM((8, LRU_W), F32),
                        pltpu.VMEM((2, ROW_TILE, LRU_W), BF16),
                        pltpu.VMEM((nch, LRU_W), F32)],
        compiler_params=_params("arbitrary"),
        name="outproj_ffn",
    )(x, p, oa, p, h0, wri, br, bi, lam, wa, wb, wo, nw, wgu, wd, fn)


def _unit_lower_inverse(lows, c):
    ti = _row_time(lax.broadcasted_iota(jnp.int32, (c, c), 0))
    tj = _row_time(lax.broadcasted_iota(jnp.int32, (c, c), 1))
    same_block = (ti // INV_BASE) == (tj // INV_BASE)
    eye = (ti == tj).astype(F32)
    ld = [jnp.where(same_block, low, 0.0) for low in lows]
    lo = [low - d for low, d in zip(lows, ld)]
    x = [eye - d for d in ld]
    p = ld
    span = 2
    while span < INV_BASE:
        p = [_mm(t, t) for t in p]
        x = [a + _mm(a, t) for a, t in zip(x, p)]
        span *= 2
    e = [_mm(a, b) for a, b in zip(x, lo)]
    y = [eye - t for t in e]
    q = e
    span = 2
    while span < c // INV_BASE:
        q = [_mm(t, t) for t in q]
        y = [a + _mm(a, t) for a, t in zip(y, q)]
        span *= 2
    return [_mm(a, b) for a, b in zip(y, x)]


def _delta_kernel(geo, qkv_ref, z_ref, ba_ref, s0_ref, alog_ref, dtb_ref, onorm_ref,
                  oa_ref, sout_ref, s_scr):
    c = CHUNK
    i = pl.program_id(0)
    is_prompt = i < geo.prompt_steps
    pos = i % geo.cpp
    first = jnp.logical_or(jnp.logical_not(is_prompt), pos == 0)
    last = jnp.logical_or(jnp.logical_not(is_prompt), pos == geo.cpp - 1)
    null = jnp.where(first, jnp.where(is_prompt, PROMPT_NULL, SAMPLE_NULL), 0)
    valid = _valid_rows(null)

    @pl.when(jnp.logical_and(first, is_prompt))
    def _():
        s_scr[...] = jnp.zeros_like(s_scr)

    @pl.when(jnp.logical_not(is_prompt))
    def _():
        s_scr[...] = s0_ref[...]

    ti = _row_time(lax.broadcasted_iota(jnp.int32, (c, c), 0))
    tj = _row_time(lax.broadcasted_iota(jnp.int32, (c, c), 1))
    tri_incl = ti >= tj
    tri_strict = ti > tj

    beta_all, gc_all, gc_t, eg_all, ekd_all, egl_all = [], [], [], [], [], []
    for s in range(GROUP):
        ba = ba_ref[s * c:(s + 1) * c, :]
        beta_all.append(jnp.where(valid, _sigmoid(ba[:, :128]), 0.0))
        g = -jnp.exp(alog_ref[...]) * _softplus(ba[:, 128:] + dtb_ref[...])
        gc, g_last = _scan_time(None, jnp.where(valid, g, 0.0), jnp.zeros((1, 128), F32))
        gc_all.append(gc)
        gc_t.append(jnp.concatenate([gc, jnp.zeros((128 - c, 128), F32)], axis=0).T)
        eg_all.append(jnp.exp(gc))
        ekd_all.append(jnp.exp(g_last - gc))
        egl_all.append(jnp.exp(g_last))

    probs = [(s, h) for s in range(GROUP) for h in range(HA)]
    rows = lambda s: slice(s * c, (s + 1) * c)
    qb = [qkv_ref[rows(s), h * DK:(h + 1) * DK] for s, h in probs]
    kbf = [qkv_ref[rows(s), QK_W + h * DK:QK_W + (h + 1) * DK] for s, h in probs]
    vb = [qkv_ref[rows(s), 2 * QK_W + h * DV:2 * QK_W + (h + 1) * DV] for s, h in probs]
    q = [t.astype(F32) for t in qb]
    k = [t.astype(F32) for t in kbf]
    beta = [beta_all[s][:, h:h + 1] for s, h in probs]
    eg = [eg_all[s][:, h:h + 1] for s, h in probs]
    n = len(probs)
    kb = [k[p] * beta[p] for p in range(n)]
    sc = [_mm_nt(jnp.concatenate([kb[p].astype(BF16), qb[p]], axis=0), kbf[p]) for p in range(n)]
    decay = [jnp.where(tri_incl, jnp.exp(jnp.where(tri_incl, gc_all[s][:, h:h + 1] - gc_t[s][h:h + 1, :c], 0.0)), 0.0)
             for s, h in probs]
    low = [jnp.where(tri_strict, sc[p][:c] * decay[p], 0.0) for p in range(n)]
    a_intra = [sc[p][c:] * decay[p] for p in range(n)]
    t_inv = _unit_lower_inverse(low, c)
    uw = [_mm(t_inv[p], jnp.concatenate([vb[p].astype(F32) * beta[p], kb[p] * eg[p]], axis=1))
          for p in range(n)]
    s_old = [s_scr[s, h] for s, h in probs]
    wq = [_mm(jnp.concatenate([uw[p][:, DV:], q[p] * eg[p]], axis=0), s_old[p]) for p in range(n)]
    v_new = [uw[p][:, :DV] - wq[p][:c] for p in range(n)]
    o_intra = [_mm(a_intra[p], v_new[p]) for p in range(n)]
    ds = [_mm_tn(k[p] * ekd_all[s][:, h:h + 1], v_new[p]) for p, (s, h) in enumerate(probs)]
    for p, (s, h) in enumerate(probs):
        s_scr[s, h] = s_old[p] * egl_all[s][:, h:h + 1] + ds[p]
        o = wq[p][c:] + o_intra[p]
        zh = z_ref[rows(s), h * DV:(h + 1) * DV].astype(F32)
        o = o * lax.rsqrt(jnp.mean(o * o, axis=-1, keepdims=True) + EPS) * onorm_ref[...] * _silu(zh)
        oa_ref[rows(s), h * DV:(h + 1) * DV] = o.astype(oa_ref.dtype)

    @pl.when(last)
    def _():
        sout_ref[...] = s_scr[...]


def _delta_call(geo, layer, p, ba, s0, alog, dtb, onorm):
    n_pp = geo.n_prompt // GROUP

    def pair(i):
        return jnp.where(i < geo.prompt_steps, i // geo.cpp, i - geo.prompt_steps + n_pp)

    vec = lambda width: pl.BlockSpec((None, 1, width), lambda i: (layer, 0, 0))
    return pl.pallas_call(
        functools.partial(_delta_kernel, geo),
        grid=(geo.steps,),
        in_specs=[pl.BlockSpec((GROUP_ROWS, QKV_W), lambda i: (i, 0)),
                  pl.BlockSpec((GROUP_ROWS, V_W), lambda i: (i, P_Z // V_W)),
                  pl.BlockSpec((GROUP_ROWS, 256), lambda i: (i, 0)),
                  pl.BlockSpec((None, GROUP, HA, DK, DV),
                               lambda i: (layer, jnp.maximum(i - geo.prompt_steps, 0), 0, 0, 0)),
                  vec(128), vec(128), vec(DV)],
        out_specs=[pl.BlockSpec((GROUP_ROWS, V_W), lambda i: (i, 0)),
                   pl.BlockSpec((GROUP, HA, DK, DV), lambda i: (pair(i), 0, 0, 0))],
        out_shape=[jax.ShapeDtypeStruct((geo.rows, V_W), BF16),
                   jax.ShapeDtypeStruct((geo.n_streams, HA, DK, DV), F32)],
        scratch_shapes=[pltpu.VMEM((GROUP, HA, DK, DV), F32)],
        compiler_params=_params("arbitrary"),
        name="delta",
    )(p, p, ba, s0, alog, dtb, onorm)


def _permute_chunk(x):
    return jnp.swapaxes(x.reshape(SLAB, N_SLAB, x.shape[-1]), 0, 1).reshape(CHUNK, x.shape[-1])


def _pack_kernel(geo, xp_ref, xs_ref, meta_ref, o_ref):
    i = pl.program_id(0)
    is_prompt = i < geo.prompt_steps
    pos = i % geo.cpp
    width = o_ref.shape[1]

    @pl.when(jnp.logical_and(is_prompt, pos > 0))
    def _():
        for s in range(GROUP):
            o_ref[s * CHUNK:(s + 1) * CHUNK, :] = _permute_chunk(xp_ref[s].astype(o_ref.dtype))

    @pl.when(jnp.logical_and(is_prompt, pos == 0))
    def _():
        first = _permute_chunk(jnp.concatenate(
            [jnp.zeros((PROMPT_NULL, width), o_ref.dtype), meta_ref[...].astype(o_ref.dtype)], axis=0))
        for s in range(GROUP):
            o_ref[s * CHUNK:(s + 1) * CHUNK, :] = first

    @pl.when(jnp.logical_not(is_prompt))
    def _():
        for s in range(GROUP):
            o_ref[s * CHUNK:(s + 1) * CHUNK, :] = _permute_chunk(jnp.concatenate(
                [jnp.zeros((SAMPLE_NULL, width), o_ref.dtype), xs_ref[s].astype(o_ref.dtype)], axis=0))


def _unpack_kernel(geo, x_ref, yp_ref, ys_ref):
    i = pl.program_id(0)
    is_prompt = i < geo.prompt_steps
    pos = i % geo.cpp

    @pl.when(jnp.logical_and(is_prompt, pos > 0))
    def _():
        for s in range(GROUP):
            yp_ref[s] = _permute_chunk(x_ref[s * CHUNK:(s + 1) * CHUNK, :]).astype(yp_ref.dtype)

    @pl.when(jnp.logical_not(is_prompt))
    def _():
        for s in range(GROUP):
            ys_ref[s] = _permute_chunk(x_ref[s * CHUNK:(s + 1) * CHUNK, :])[SAMPLE_NULL:].astype(ys_ref.dtype)


def _layout_specs(geo):
    def prompt_idx(i):
        step = jnp.minimum(i, geo.prompt_steps - 1)
        return (step // geo.cpp, jnp.maximum(step % geo.cpp - 1, 0), 0)

    def sample_idx(i):
        return (jnp.maximum(i - geo.prompt_steps, 0), 0, 0)

    return (pl.BlockSpec((GROUP, CHUNK, D_MODEL), prompt_idx),
            pl.BlockSpec((GROUP, CHUNK - SAMPLE_NULL, D_MODEL), sample_idx))


def _pack_call(geo, x_prompt, x_sample, meta):
    prompt_spec, sample_spec = _layout_specs(geo)
    return pl.pallas_call(
        functools.partial(_pack_kernel, geo),
        grid=(geo.steps,),
        in_specs=[prompt_spec, sample_spec, pl.BlockSpec((N_META, D_MODEL), lambda i: (0, 0))],
        out_specs=pl.BlockSpec((GROUP_ROWS, D_MODEL), lambda i: (i, 0)),
        out_shape=jax.ShapeDtypeStruct((geo.rows, D_MODEL), F32),
        compiler_params=_params("parallel"),
        name="pack",
    )(x_prompt, x_sample, meta)


def _unpack_call(geo, x, dtype):
    prompt_spec, sample_spec = _layout_specs(geo)
    return pl.pallas_call(
        functools.partial(_unpack_kernel, geo),
        grid=(geo.steps,),
        in_specs=[pl.BlockSpec((GROUP_ROWS, D_MODEL), lambda i: (i, 0))],
        out_specs=[prompt_spec, sample_spec],
        out_shape=[jax.ShapeDtypeStruct((geo.n_prompt, (geo.cpp - 1) * CHUNK, D_MODEL), dtype),
                   jax.ShapeDtypeStruct((geo.n_sample, CHUNK - SAMPLE_NULL, D_MODEL), dtype)],
        compiler_params=_params("arbitrary"),
        name="unpack",
    )(x)


def _pad_lanes(v, width):
    return jnp.pad(v, ((0, 0), (0, width - v.shape[-1])))


def kernel(x_prompt, x_sample, state_delta_S, state_delta_conv, state_lru_h, state_lru_conv,
           meta_tokens, ffn1_norm, ffn1_w_gu, ffn1_w_down, mix_norm, w_in, delta_conv_w,
           delta_A_log, delta_dt_bias, delta_out_norm, lru_conv_w, lru_conv_b, lru_w_r, lru_b_r,
           lru_w_i, lru_b_i, lru_lambda, w_branch_a, w_branch_b, w_out, ffn2_norm, ffn2_w_gu,
           ffn2_w_down, final_norm):
    n_prompt, seq_len, _ = x_prompt.shape
    n_sample, dec_len, _ = x_sample.shape
    assert dec_len == CHUNK - SAMPLE_NULL and (N_META + seq_len) % CHUNK == N_META
    prompt_rows = PROMPT_NULL + N_META + seq_len
    cpp = prompt_rows // CHUNK
    geo = _Geometry(n_prompt, n_sample, cpp)
    assert geo.rows % ROW_TILE == 0 and (geo.prompt_chunks * CHUNK) % ROW_TILE == 0
    assert INPROJ_TILE % GROUP_ROWS == 0 and ROW_TILE % GROUP_ROWS == 0
    dt = x_prompt.dtype

    x = _pack_call(geo, x_prompt, x_sample, meta_tokens)

    cast = lambda w: w.astype(BF16)
    vec = lambda v: v.astype(F32)[:, None, :]
    wgu1, wd1, wgu2, wd2 = cast(ffn1_w_gu), cast(ffn1_w_down), cast(ffn2_w_gu), cast(ffn2_w_down)
    w_all = cast(w_in)
    pad_heads = lambda w: jnp.pad(w, ((0, 0), (0, 0), (0, 128 - w.shape[-1])))
    w_last = cast(pad_heads(w_in[:, :, P_COLS:]))
    w_ba = cast(jnp.concatenate([pad_heads(w_in[:, :, OFF_BETA:OFF_ALPHA]),
                                 pad_heads(w_in[:, :, OFF_ALPHA:OFF_LX])], axis=-1))
    w_ri = cast(jnp.concatenate([lru_w_r, lru_w_i], axis=-1))
    wa, wb, wo = cast(w_branch_a), cast(w_branch_b), cast(w_out)
    norm1, norm_mix, norm2 = vec(ffn1_norm), vec(mix_norm), vec(ffn2_norm)
    alog, dtb, onorm = vec(_pad_lanes(delta_A_log, 128)), vec(_pad_lanes(delta_dt_bias, 128)), vec(delta_out_norm)
    lcb, b_r, b_i, lam = vec(lru_conv_b), vec(lru_b_r), vec(lru_b_i), vec(lru_lambda)
    cw, lcw = delta_conv_w.astype(F32), lru_conv_w.astype(F32)
    s0, h0 = state_delta_S.astype(F32), state_lru_h.astype(F32)

    inj = jnp.concatenate([state_delta_conv, state_lru_conv], axis=-1).astype(F32)
    null_runs = SAMPLE_NULL // N_SLAB
    inj = jnp.pad(inj[:, :, :, None, :], ((0, 0), (0, 0), (0, 0), (null_runs - 1, SLAB - null_runs), (0, 0)))
    inj = inj.reshape(DEPTH, n_sample * TAIL_ROWS, CONV_COLS)

    last_ids = geo.last_chunk_ids()
    outs_s, outs_tail, outs_h = [], [], []
    for l in range(DEPTH):
        x = _ffn_call(l, x, norm1, wgu1, wd1)
        p, ba, tails = _inproj_call(geo, l, x, norm_mix, w_all, w_last, w_ba, cw, lcw, lcb, inj)
        oa, s_new = _delta_call(geo, l, p, ba, s0, alog, dtb, onorm)
        x, hlast = _outproj_call(geo, l, x, p, oa, h0, w_ri, b_r, b_i, lam, wa, wb, wo, norm2, wgu2, wd2,
                                 final_norm[None].astype(F32), final=(l == DEPTH - 1))
        outs_s.append(s_new)
        outs_tail.append(tails[last_ids, SLAB - 1::SLAB, :])
        outs_h.append(hlast[last_ids])

    n_prompt_rows = geo.prompt_chunks * CHUNK
    y_prompt, y_sample = _unpack_call(geo, x, dt)
    s_all = jnp.stack(outs_s)
    tail_all = jnp.stack(outs_tail)
    cq_all = tail_all[..., :QKV_W]
    cx_all = tail_all[..., QKV_W:]
    h_all = jnp.stack(outs_h)
    return (y_prompt.astype(dt), y_sample.astype(dt),
            s_all[:, :n_prompt].astype(dt), cq_all[:, :n_prompt].astype(dt),
            h_all[:, :n_prompt].astype(dt), cx_all[:, :n_prompt].astype(dt),
            s_all[:, n_prompt:].astype(state_delta_S.dtype),
            cq_all[:, n_prompt:].astype(state_delta_conv.dtype),
            h_all[:, n_prompt:].astype(state_lru_h.dtype),
            cx_all[:, n_prompt:].astype(state_lru_conv.dtype))
```

```python
import functools

import numpy as np
import jax
import jax.numpy as jnp
from jax import lax
from jax.experimental import pallas as pl
from jax.experimental.pallas import tpu as pltpu

F32 = jnp.float32
BF16 = jnp.bfloat16

D_MODEL = 1024
DEPTH = 4
N_META = 16
HA = 8
DK = 128
DV = 128
QK_W = HA * DK
V_W = HA * DV
QKV_W = 2 * QK_W + V_W
LRU_W = D_MODEL
NB = 8
BW = 128
CONV_W = 4
LRU_C = 8.0
D_FF = 2816
EPS = 1e-6

OFF_Z = QKV_W
OFF_BETA = OFF_Z + V_W
OFF_ALPHA = OFF_BETA + HA
OFF_LX = OFF_ALPHA + HA
OFF_LY = OFF_LX + LRU_W
OFF_GA = OFF_LY + LRU_W
OFF_GB = OFF_GA + D_MODEL
IN_COLS = OFF_GB + D_MODEL

CHUNK = 64
GROUP = 4
GROUP_ROWS = GROUP * CHUNK
PROMPT_NULL = CHUNK - N_META
SAMPLE_NULL = CHUNK // 2
INV_BASE = 16
ROW_TILE = 512
INPROJ_TILE = 256
FF_HALF = 256
CONV_PART = 512
CONV_COLS = QKV_W + LRU_W
SLAB = 8
N_SLAB = CHUNK // SLAB
TAIL_ROWS = (CONV_W - 1) * SLAB
P_Z, P_X, P_Y, P_G = 3072, 4096, 5120, 6144
P_COLS = 8192
VMEM_LIMIT = 56 * 1024 * 1024


def _rms(x, w):
    return x * lax.rsqrt(jnp.mean(x * x, axis=-1, keepdims=True) + EPS) * w


def _mm(a, b):
    return jnp.dot(a.astype(BF16), b.astype(BF16), preferred_element_type=F32)


def _mm_nt(a, b):
    return lax.dot_general(a.astype(BF16), b.astype(BF16), (((1,), (1,)), ((), ())),
                           preferred_element_type=F32)


def _mm_tn(a, b):
    return lax.dot_general(a.astype(BF16), b.astype(BF16), (((0,), (0,)), ((), ())),
                           preferred_element_type=F32)


def _sigmoid(x):
    return 0.5 * (1.0 + jnp.tanh(0.5 * x))


def _silu(x):
    return x * _sigmoid(x)


def _softplus(x):
    return jnp.maximum(x, 0.0) + jnp.log1p(jnp.exp(-jnp.abs(x)))


def _gelu_tanh(x):
    return 0.5 * x * (1.0 + jnp.tanh(0.7978845608028654 * (x + 0.044715 * (x * x * x))))


class _Geometry:
    def __init__(self, n_prompt, n_sample, chunks_per_prompt):
        assert n_prompt % GROUP == 0 and n_sample % GROUP == 0
        self.n_prompt = n_prompt
        self.n_sample = n_sample
        self.cpp = chunks_per_prompt
        self.prompt_steps = (n_prompt // GROUP) * chunks_per_prompt
        self.steps = self.prompt_steps + n_sample // GROUP
        self.prompt_chunks = GROUP * self.prompt_steps
        self.chunks = GROUP * self.steps
        self.rows = self.chunks * CHUNK
        self.n_streams = n_prompt + n_sample

    def chunk_null(self, cid):
        is_prompt = cid < self.prompt_chunks
        first = jnp.logical_or(jnp.logical_not(is_prompt), (cid // GROUP) % self.cpp == 0)
        return jnp.where(first, jnp.where(is_prompt, PROMPT_NULL, SAMPLE_NULL), 0)

    def last_chunk_ids(self):
        ids = [GROUP * ((b // GROUP) * self.cpp + self.cpp - 1) + b % GROUP for b in range(self.n_prompt)]
        ids += [self.prompt_chunks + t for t in range(self.n_sample)]
        return np.asarray(ids, np.int32)


def _row_time(rows):
    return N_SLAB * (rows % SLAB) + rows // SLAB


def _valid_rows(null):
    return _row_time(lax.broadcasted_iota(jnp.int32, (CHUNK, 1), 0)) >= null


def _slabs(x):
    return [x[v * SLAB:(v + 1) * SLAB] for v in range(N_SLAB)]


def _shift_run(x, fill):
    return jnp.concatenate([fill, x[:SLAB - 1]], axis=0)


def _scan_time(a, b, h_in):
    bs = _slabs(b)
    if a is None:
        for v in range(1, N_SLAB):
            bs[v] = bs[v] + bs[v - 1]
        run = bs[N_SLAB - 1]
        d = 1
        while d < SLAB:
            run = run + _shift_rows(run, d, 0.0)
            d *= 2
        run = run + h_in
        prev = _shift_run(run, h_in)
        hs = [t + prev for t in bs]
    else:
        as_ = _slabs(a)
        for v in range(1, N_SLAB):
            bs[v] = bs[v] + as_[v] * bs[v - 1]
            as_[v] = as_[v] * as_[v - 1]
        ra, rb = as_[N_SLAB - 1], bs[N_SLAB - 1]
        d = 1
        while d < SLAB:
            rb = rb + ra * _shift_rows(rb, d, 0.0)
            ra = ra * _shift_rows(ra, d, 1.0)
            d *= 2
        run = rb + ra * h_in
        prev = _shift_run(run, h_in)
        hs = [t + u * prev for t, u in zip(bs, as_)]
    return jnp.concatenate(hs, axis=0), run[SLAB - 1:SLAB, :]


def _ffn(x, nw, wgu_ref, wd_ref, h_ref):
    xn = _rms(x, nw).astype(BF16)
    for c in range(D_FF // FF_HALF):
        lo = c * FF_HALF
        g = jnp.dot(xn, wgu_ref[:, lo:lo + FF_HALF], preferred_element_type=F32)
        u = jnp.dot(xn, wgu_ref[:, D_FF + lo:D_FF + lo + FF_HALF], preferred_element_type=F32)
        h_ref[:, lo:lo + FF_HALF] = (_silu(g) * u).astype(BF16)
    y = jnp.dot(h_ref[...], wd_ref[...], preferred_element_type=F32)
    return x + 0.5 * y


def _ffn_kernel(x_ref, nw_ref, wgu_ref, wd_ref, o_ref, h_ref):
    o_ref[...] = _ffn(x_ref[...], nw_ref[...], wgu_ref, wd_ref, h_ref)


def _inproj_kernel(geo, x_ref, nw_ref, wa_ref, wt_ref, wl_ref, wba_ref, cw_ref, lcw_ref, lcb_ref, inj_ref,
                   p_ref, ba_ref, tails_ref, stage_all, carry_scr, wb_scr):
    i = pl.program_id(0)
    nch = INPROJ_TILE // CHUNK
    xn = _rms(x_ref[...], nw_ref[...]).astype(BF16)

    def project_cols(lo, width):
        w_ref, start = (wa_ref, lo) if lo < P_COLS // 2 else (wb_scr, lo - P_COLS // 2)
        return jnp.dot(xn, w_ref[:, start:start + width], preferred_element_type=F32)

    @pl.when(i == 0)
    def _():
        carry_scr[...] = jnp.zeros_like(carry_scr)
        skip = OFF_LX - OFF_BETA
        for c in range(P_COLS // 2 // 1024):
            nxt = wt_ref[:, (c + 1) * 1024:(c + 1) * 1024 + 128] if (c + 1) * 1024 < P_COLS // 2 else wl_ref[...]
            wide = jnp.concatenate([wt_ref[:, c * 1024:(c + 1) * 1024], nxt], axis=1)
            wb_scr[:, c * 1024:(c + 1) * 1024] = pltpu.roll(wide, wide.shape[1] - skip, 1)[:, :1024]

    valid = [_valid_rows(geo.chunk_null(i * nch + j)) for j in range(nch)]
    in_sample = i >= geo.prompt_chunks * CHUNK // INPROJ_TILE
    kinds = (("q", 0, 0), ("k", QK_W, QK_W), ("v", 2 * QK_W, 2 * QK_W), ("x", P_X, QKV_W))
    t0 = TAIL_ROWS
    part_w = 1024 // nch

    def project(kidx, part):
        kind, lo, col = kinds[kidx]
        stage = stage_all.at[kidx]
        pre = project_cols(lo + part * part_w, part_w)
        sc = slice(part * part_w, (part + 1) * part_w)
        cc = slice(col + part * part_w, col + (part + 1) * part_w)
        for j in range(nch):
            stage[j, t0:t0 + CHUNK, sc] = pre[j * CHUNK:(j + 1) * CHUNK]
            if j >= GROUP:
                stage[j, 0:t0, sc] = pre[(j - GROUP + 1) * CHUNK - t0:(j - GROUP + 1) * CHUNK]
            else:
                stage[j, 0:t0, sc] = carry_scr[j, :, cc]
            stage[j, CHUNK:CHUNK + t0, sc] = stage[j, CHUNK:CHUNK + t0, sc] + jnp.where(
                in_sample, inj_ref[t0 * j:t0 * (j + 1), cc], 0.0)
            tails_ref[j, :, cc] = stage[j, CHUNK:CHUNK + t0, sc]
        for g in range(GROUP):
            carry_scr[g, :, cc] = stage[nch - GROUP + g, CHUNK:CHUNK + t0, sc]

    def convolve(kidx, j, half):
        kind, lo, col = kinds[kidx]
        stage = stage_all.at[kidx]
        hs = slice(half * CONV_PART, (half + 1) * CONV_PART)
        taps = lcw_ref[:, hs] if kind == "x" else cw_ref[:, lo + hs.start:lo + hs.stop]
        cur = [stage[j, t0 + v * SLAB:t0 + (v + 1) * SLAB, hs] for v in range(N_SLAB)]
        back = []
        for k in range(CONV_W - 1):
            prev_last = stage[j, (k + 1) * SLAB - 1:(k + 1) * SLAB, hs]
            back.append(_shift_run(cur[N_SLAB - (CONV_W - 1) + k], prev_last))
        conv = []
        for v in range(N_SLAB):
            acc = cur[v] * taps[CONV_W - 1:CONV_W, :]
            for d in range(1, CONV_W):
                src = cur[v - d] if v >= d else back[CONV_W - 1 - d + v]
                acc = acc + src * taps[CONV_W - 1 - d:CONV_W - d, :]
            conv.append(acc)
        conv = jnp.concatenate(conv, axis=0)
        if kind == "x":
            out = conv + lcb_ref[:, hs]
        else:
            act = _silu(conv)
            if kind != "v":
                scale = DK ** -0.5 if kind == "q" else 1.0
                segs = []
                for h in range(CONV_PART // DK):
                    seg = act[:, h * DK:(h + 1) * DK]
                    segs.append(seg * (lax.rsqrt(jnp.sum(seg * seg, axis=-1, keepdims=True) + EPS) * scale))
                act = jnp.concatenate(segs, axis=1)
            out = jnp.where(valid[j], act, 0.0)
        p_ref[j * CHUNK:(j + 1) * CHUNK, lo + hs.start:lo + hs.stop] = out.astype(BF16)

    def plain(lo, part):
        sl = slice(lo + part * part_w, lo + (part + 1) * part_w)
        p_ref[:, sl] = project_cols(sl.start, part_w).astype(BF16)

    plain_cols = (P_Z, P_Y, P_G, P_G + 1024)
    for part in range(nch):
        project(0, part)
    for kidx in range(len(kinds)):
        for j in range(nch):
            plain(plain_cols[kidx], j)
            convolve(kidx, j, 0)
            if kidx + 1 < len(kinds):
                project(kidx + 1, j)
            convolve(kidx, j, 1)
    ba_ref[...] = jnp.dot(xn, wba_ref[...], preferred_element_type=F32)


def _shift_rows(x, d, fill):
    if d % 8 == 0:
        return jnp.concatenate([jnp.full((d, x.shape[1]), fill, x.dtype), x[:x.shape[0] - d]], axis=0)
    rows = lax.broadcasted_iota(jnp.int32, x.shape, 0)
    return jnp.where(rows >= d, pltpu.roll(x, d, 0), fill)


def _outproj_kernel(geo, final, x_ref, g_ref, oa_ref, xl_ref, h0_ref,
                    wri_ref, br_ref, bi_ref, lam_ref, wa_ref, wb_ref, wo_ref,
                    nw_ref, wgu_ref, wd_ref, fn_ref,
                    o_ref, hlast_ref, h_ref, hc_scr, ob_scr):
    i = pl.program_id(0)
    nch = ROW_TILE // CHUNK

    @pl.when(i == 0)
    def _():
        hc_scr[...] = jnp.zeros_like(hc_scr)

    sp = _softplus(-lam_ref[...])
    nulls = [geo.chunk_null(i * nch + j) for j in range(nch)]
    valid = [_valid_rows(n) for n in nulls]
    is_sample = [(i * nch + j) >= geo.prompt_chunks for j in range(nch)]
    for n in range(NB):
        sl = slice(n * BW, (n + 1) * BW)
        xb = xl_ref[:, sl]
        gates = jnp.dot(xb, wri_ref[n], preferred_element_type=F32)
        r = _sigmoid(gates[:, :BW] + br_ref[:, sl])
        ig = _sigmoid(gates[:, BW:] + bi_ref[:, sl])
        log_a = -LRU_C * r * sp[:, sl]
        a_all = jnp.exp(log_a)
        th = jnp.tanh(log_a)
        mult = jnp.sqrt(-2.0 * th / (1.0 - th))
        b_all = mult * (ig * xb.astype(F32))
        gy = _gelu_tanh(xl_ref[:, LRU_W + n * BW:LRU_W + (n + 1) * BW].astype(F32))
        carry = [hc_scr[g:g + 1, sl] for g in range(GROUP)]
        for j in range(nch):
            rows = slice(j * CHUNK, (j + 1) * CHUNK)
            a = jnp.where(valid[j], a_all[rows], 1.0)
            b = jnp.where(valid[j], b_all[rows], 0.0)
            h_in = jnp.where(nulls[j] > 0, jnp.where(is_sample[j], h0_ref[j:j + 1, sl], 0.0), carry[j % GROUP])
            hs, carry[j % GROUP] = _scan_time(a, b, h_in)
            ob_scr[rows, sl] = (hs * gy[rows]).astype(BF16)
            hlast_ref[j:j + 1, sl] = carry[j % GROUP]
        for g in range(GROUP):
            hc_scr[g:g + 1, sl] = carry[g]

    ga = _sigmoid(g_ref[:, :D_MODEL].astype(F32))
    gb = _sigmoid(g_ref[:, D_MODEL:].astype(F32))
    ma = jnp.dot(oa_ref[...], wa_ref[...], preferred_element_type=F32)
    mb = jnp.dot(ob_scr[...], wb_ref[...], preferred_element_type=F32)
    m = (ga * ma + gb * mb).astype(BF16)
    x = x_ref[...] + jnp.dot(m, wo_ref[...], preferred_element_type=F32)
    x = _ffn(x, nw_ref[...], wgu_ref, wd_ref, h_ref)
    if final:
        x = _rms(x, fn_ref[...])
    o_ref[...] = x


def _resident(shape, layer):
    nd = len(shape)
    return pl.BlockSpec((None,) + shape, lambda i: (layer,) + (0,) * nd, pipeline_mode=pl.Buffered(1))


def _params(semantics):
    return pltpu.CompilerParams(dimension_semantics=(semantics,), vmem_limit_bytes=VMEM_LIMIT)


def _ffn_call(layer, x, nw, wgu, wd):
    n = x.shape[0]
    row = lambda i: (i, 0)
    return pl.pallas_call(
        _ffn_kernel,
        grid=(n // ROW_TILE,),
        in_specs=[pl.BlockSpec((ROW_TILE, D_MODEL), row),
                  _resident((1, D_MODEL), layer),
                  _resident((D_MODEL, 2 * D_FF), layer),
                  _resident((D_FF, D_MODEL), layer)],
        out_specs=pl.BlockSpec((ROW_TILE, D_MODEL), row),
        out_shape=jax.ShapeDtypeStruct((n, D_MODEL), F32),
        scratch_shapes=[pltpu.VMEM((ROW_TILE, D_FF), BF16)],
        compiler_params=_params("parallel"),
        name="ffn",
    )(x, nw, wgu, wd)


def _inproj_call(geo, layer, x, nw, w_all, w_last, wba, cw, lcw, lcb, inj):
    n = x.shape[0]
    nch = INPROJ_TILE // CHUNK
    sample_tile0 = geo.prompt_chunks * CHUNK // INPROJ_TILE
    row = lambda i: (i, 0)
    half = P_COLS // 2
    assert OFF_BETA == half and w_all.shape[2] == 2 * half + OFF_LX - OFF_BETA and w_last.shape[2] == 128
    return pl.pallas_call(
        functools.partial(_inproj_kernel, geo),
        grid=(n // INPROJ_TILE,),
        in_specs=[pl.BlockSpec((INPROJ_TILE, D_MODEL), row),
                  _resident((1, D_MODEL), layer),
                  _resident((D_MODEL, half), layer),
                  pl.BlockSpec((None, D_MODEL, half), lambda i: (layer, 0, 1), pipeline_mode=pl.Buffered(1)),
                  _resident((D_MODEL, 128), layer),
                  _resident((D_MODEL, 256), layer),
                  _resident((CONV_W, QKV_W), layer),
                  _resident((CONV_W, LRU_W), layer),
                  _resident((1, LRU_W), layer),
                  pl.BlockSpec((None, TAIL_ROWS * nch, CONV_COLS),
                               lambda i: (layer, jnp.maximum(i - sample_tile0, 0), 0))],
        out_specs=[pl.BlockSpec((INPROJ_TILE, P_COLS), row),
                   pl.BlockSpec((INPROJ_TILE, 256), row),
                   pl.BlockSpec((nch, TAIL_ROWS, CONV_COLS), lambda i: (i, 0, 0))],
        out_shape=[jax.ShapeDtypeStruct((n, P_COLS), BF16),
                   jax.ShapeDtypeStruct((n, 256), F32),
                   jax.ShapeDtypeStruct((n // CHUNK, TAIL_ROWS, CONV_COLS), F32)],
        scratch_shapes=[pltpu.VMEM((4, nch, CHUNK + TAIL_ROWS, 1024), F32),
                        pltpu.VMEM((GROUP, TAIL_ROWS, CONV_COLS), F32),
                        pltpu.VMEM((D_MODEL, half), BF16)],
        compiler_params=_params("arbitrary"),
        name="inproj",
    )(x, nw, w_all, w_all, w_last, wba, cw, lcw, lcb, inj)


def _outproj_call(geo, layer, x, p, oa, h0, wri, br, bi, lam, wa, wb, wo, nw, wgu, wd, fn, final):
    n = x.shape[0]
    nch = ROW_TILE // CHUNK
    sample_tile0 = geo.prompt_chunks * CHUNK // ROW_TILE
    row = lambda i: (i, 0)
    return pl.pallas_call(
        functools.partial(_outproj_kernel, geo, final),
        grid=(n // ROW_TILE,),
        in_specs=[pl.BlockSpec((ROW_TILE, D_MODEL), row),
                  pl.BlockSpec((ROW_TILE, 2 * D_MODEL), lambda i: (i, P_G // 2048)),
                  pl.BlockSpec((ROW_TILE, V_W), row),
                  pl.BlockSpec((ROW_TILE, 2 * LRU_W), lambda i: (i, P_X // 2048)),
                  pl.BlockSpec((None, nch, LRU_W), lambda i: (layer, jnp.maximum(i - sample_tile0, 0), 0)),
                  _resident((NB, BW, 2 * BW), layer),
                  _resident((1, LRU_W), layer),
                  _resident((1, LRU_W), layer),
                  _resident((1, LRU_W), layer),
                  _resident((V_W, D_MODEL), layer),
                  _resident((LRU_W, D_MODEL), layer),
                  _resident((D_MODEL, D_MODEL), layer),
                  _resident((1, D_MODEL), layer),
                  _resident((D_MODEL, 2 * D_FF), layer),
                  _resident((D_FF, D_MODEL), layer),
                  pl.BlockSpec((1, D_MODEL), lambda i: (0, 0), pipeline_mode=pl.Buffered(1))],
        out_specs=[pl.BlockSpec((ROW_TILE, D_MODEL), row),
                   pl.BlockSpec((nch, LRU_W), row)],
        out_shape=[jax.ShapeDtypeStruct((n, D_MODEL), F32),
                   jax.ShapeDtypeStruct((n // CHUNK, LRU_W), F32)],
        scratch_shapes=[pltpu.VMEM((ROW_TILE, D_FF), BF16),
                        pltpu.VMEM((8, LRU_W), F32),
                        pltpu.VMEM((ROW_TILE, LRU_W), BF16)],
        compiler_params=_params("arbitrary"),
        name="outproj_ffn",
    )(x, p, oa, p, h0, wri, br, bi, lam, wa, wb, wo, nw, wgu, wd, fn)


def _unit_lower_inverse(lows, c):
    ti = _row_time(lax.broadcasted_iota(jnp.int32, (c, c), 0))
    tj = _row_time(lax.broadcasted_iota(jnp.int32, (c, c), 1))
    same_block = (ti // INV_BASE) == (tj // INV_BASE)
    eye = (ti == tj).astype(F32)
    ld = [jnp.where(same_block, low, 0.0) for low in lows]
    lo = [low - d for low, d in zip(lows, ld)]
    x = [eye - d for d in ld]
    p = ld
    span = 2
    while span < INV_BASE:
        p = [_mm(t, t) for t in p]
        x = [a + _mm(a, t) for a, t in zip(x, p)]
        span *= 2
    e = [_mm(a, b) for a, b in zip(x, lo)]
    y = [eye - t for t in e]
    q = e
    span = 2
    while span < c // INV_BASE:
        q = [_mm(t, t) for t in q]
        y = [a + _mm(a, t) for a, t in zip(y, q)]
        span *= 2
    return [_mm(a, b) for a, b in zip(y, x)]


def _delta_kernel(geo, qkv_ref, z_ref, ba_ref, s0_ref, alog_ref, dtb_ref, onorm_ref,
                  oa_ref, sout_ref, s_scr):
    c = CHUNK
    i = pl.program_id(0)
    is_prompt = i < geo.prompt_steps
    pos = i % geo.cpp
    first = jnp.logical_or(jnp.logical_not(is_prompt), pos == 0)
    last = jnp.logical_or(jnp.logical_not(is_prompt), pos == geo.cpp - 1)
    null = jnp.where(first, jnp.where(is_prompt, PROMPT_NULL, SAMPLE_NULL), 0)
    valid = _valid_rows(null)

    @pl.when(jnp.logical_and(first, is_prompt))
    def _():
        s_scr[...] = jnp.zeros_like(s_scr)

    @pl.when(jnp.logical_not(is_prompt))
    def _():
        s_scr[...] = s0_ref[...]

    ti = _row_time(lax.broadcasted_iota(jnp.int32, (c, c), 0))
    tj = _row_time(lax.broadcasted_iota(jnp.int32, (c, c), 1))
    tri_incl = ti >= tj
    tri_strict = ti > tj

    beta_all, gc_all, gc_t, eg_all, ekd_all, egl_all = [], [], [], [], [], []
    for s in range(GROUP):
        ba = ba_ref[s * c:(s + 1) * c, :]
        beta_all.append(jnp.where(valid, _sigmoid(ba[:, :128]), 0.0))
        g = -jnp.exp(alog_ref[...]) * _softplus(ba[:, 128:] + dtb_ref[...])
        gc, g_last = _scan_time(None, jnp.where(valid, g, 0.0), jnp.zeros((1, 128), F32))
        gc_all.append(gc)
        gc_t.append(jnp.concatenate([gc, jnp.zeros((128 - c, 128), F32)], axis=0).T)
        eg_all.append(jnp.exp(gc))
        ekd_all.append(jnp.exp(g_last - gc))
        egl_all.append(jnp.exp(g_last))

    probs = [(s, h) for s in range(GROUP) for h in range(HA)]
    rows = lambda s: slice(s * c, (s + 1) * c)
    qb = [qkv_ref[rows(s), h * DK:(h + 1) * DK] for s, h in probs]
    kbf = [qkv_ref[rows(s), QK_W + h * DK:QK_W + (h + 1) * DK] for s, h in probs]
    vb = [qkv_ref[rows(s), 2 * QK_W + h * DV:2 * QK_W + (h + 1) * DV] for s, h in probs]
    q = [t.astype(F32) for t in qb]
    k = [t.astype(F32) for t in kbf]
    beta = [beta_all[s][:, h:h + 1] for s, h in probs]
    eg = [eg_all[s][:, h:h + 1] for s, h in probs]
    n = len(probs)
    kb = [k[p] * beta[p] for p in range(n)]
    sc = [_mm_nt(jnp.concatenate([kb[p].astype(BF16), qb[p]], axis=0), kbf[p]) for p in range(n)]
    decay = [jnp.where(tri_incl, jnp.exp(jnp.where(tri_incl, gc_all[s][:, h:h + 1] - gc_t[s][h:h + 1, :c], 0.0)), 0.0)
             for s, h in probs]
    low = [jnp.where(tri_strict, sc[p][:c] * decay[p], 0.0) for p in range(n)]
    a_intra = [sc[p][c:] * decay[p] for p in range(n)]
    t_inv = _unit_lower_inverse(low, c)
    uw = [_mm(t_inv[p], jnp.concatenate([vb[p].astype(F32) * beta[p], kb[p] * eg[p]], axis=1))
          for p in range(n)]
    s_old = [s_scr[s, h] for s, h in probs]
    wq = [_mm(jnp.concatenate([uw[p][:, DV:], q[p] * eg[p]], axis=0), s_old[p]) for p in range(n)]
    v_new = [uw[p][:, :DV] - wq[p][:c] for p in range(n)]
    o_intra = [_mm(a_intra[p], v_new[p]) for p in range(n)]
    ds = [_mm_tn(k[p] * ekd_all[s][:, h:h + 1], v_new[p]) for p, (s, h) in enumerate(probs)]
    for p, (s, h) in enumerate(probs):
        s_scr[s, h] = s_old[p] * egl_all[s][:, h:h + 1] + ds[p]
        o = wq[p][c:] + o_intra[p]
        zh = z_ref[rows(s), h * DV:(h + 1) * DV].astype(F32)
        o = o * lax.rsqrt(jnp.mean(o * o, axis=-1, keepdims=True) + EPS) * onorm_ref[...] * _silu(zh)
        oa_ref[rows(s), h * DV:(h + 1) * DV] = o.astype(oa_ref.dtype)

    @pl.when(last)
    def _():
        sout_ref[...] = s_scr[...]


def _delta_call(geo, layer, p, ba, s0, alog, dtb, onorm):
    n_pp = geo.n_prompt // GROUP

    def pair(i):
        return jnp.where(i < geo.prompt_steps, i // geo.cpp, i - geo.prompt_steps + n_pp)

    vec = lambda width: pl.BlockSpec((None, 1, width), lambda i: (layer, 0, 0))
    return pl.pallas_call(
        functools.partial(_delta_kernel, geo),
        grid=(geo.steps,),
        in_specs=[pl.BlockSpec((GROUP_ROWS, QKV_W), lambda i: (i, 0)),
                  pl.BlockSpec((GROUP_ROWS, V_W), lambda i: (i, P_Z // V_W)),
                  pl.BlockSpec((GROUP_ROWS, 256), lambda i: (i, 0)),
                  pl.BlockSpec((None, GROUP, HA, DK, DV),
                               lambda i: (layer, jnp.maximum(i - geo.prompt_steps, 0), 0, 0, 0)),
                  vec(128), vec(128), vec(DV)],
        out_specs=[pl.BlockSpec((GROUP_ROWS, V_W), lambda i: (i, 0)),
                   pl.BlockSpec((GROUP, HA, DK, DV), lambda i: (pair(i), 0, 0, 0))],
        out_shape=[jax.ShapeDtypeStruct((geo.rows, V_W), BF16),
                   jax.ShapeDtypeStruct((geo.n_streams, HA, DK, DV), F32)],
        scratch_shapes=[pltpu.VMEM((GROUP, HA, DK, DV), F32)],
        compiler_params=_params("arbitrary"),
        name="delta",
    )(p, p, ba, s0, alog, dtb, onorm)


def _permute_chunk(x):
    return jnp.swapaxes(x.reshape(SLAB, N_SLAB, x.shape[-1]), 0, 1).reshape(CHUNK, x.shape[-1])


def _pack_kernel(geo, xp_ref, xs_ref, meta_ref, o_ref):
    i = pl.program_id(0)
    is_prompt = i < geo.prompt_steps
    pos = i % geo.cpp
    width = o_ref.shape[1]

    @pl.when(jnp.logical_and(is_prompt, pos > 0))
    def _():
        for s in range(GROUP):
            o_ref[s * CHUNK:(s + 1) * CHUNK, :] = _permute_chunk(xp_ref[s].astype(o_ref.dtype))

    @pl.when(jnp.logical_and(is_prompt, pos == 0))
    def _():
        first = _permute_chunk(jnp.concatenate(
            [jnp.zeros((PROMPT_NULL, width), o_ref.dtype), meta_ref[...].astype(o_ref.dtype)], axis=0))
        for s in range(GROUP):
            o_ref[s * CHUNK:(s + 1) * CHUNK, :] = first

    @pl.when(jnp.logical_not(is_prompt))
    def _():
        for s in range(GROUP):
            o_ref[s * CHUNK:(s + 1) * CHUNK, :] = _permute_chunk(jnp.concatenate(
                [jnp.zeros((SAMPLE_NULL, width), o_ref.dtype), xs_ref[s].astype(o_ref.dtype)], axis=0))


def _unpack_kernel(geo, x_ref, yp_ref, ys_ref):
    i = pl.program_id(0)
    is_prompt = i < geo.prompt_steps
    pos = i % geo.cpp

    @pl.when(jnp.logical_and(is_prompt, pos > 0))
    def _():
        for s in range(GROUP):
            yp_ref[s] = _permute_chunk(x_ref[s * CHUNK:(s + 1) * CHUNK, :]).astype(yp_ref.dtype)

    @pl.when(jnp.logical_not(is_prompt))
    def _():
        for s in range(GROUP):
            ys_ref[s] = _permute_chunk(x_ref[s * CHUNK:(s + 1) * CHUNK, :])[SAMPLE_NULL:].astype(ys_ref.dtype)


def _layout_specs(geo):
    def prompt_idx(i):
        step = jnp.minimum(i, geo.prompt_steps - 1)
        return (step // geo.cpp, jnp.maximum(step % geo.cpp - 1, 0), 0)

    def sample_idx(i):
        return (jnp.maximum(i - geo.prompt_steps, 0), 0, 0)

    return (pl.BlockSpec((GROUP, CHUNK, D_MODEL), prompt_idx),
            pl.BlockSpec((GROUP, CHUNK - SAMPLE_NULL, D_MODEL), sample_idx))


def _pack_call(geo, x_prompt, x_sample, meta):
    prompt_spec, sample_spec = _layout_specs(geo)
    return pl.pallas_call(
        functools.partial(_pack_kernel, geo),
        grid=(geo.steps,),
        in_specs=[prompt_spec, sample_spec, pl.BlockSpec((N_META, D_MODEL), lambda i: (0, 0))],
        out_specs=pl.BlockSpec((GROUP_ROWS, D_MODEL), lambda i: (i, 0)),
        out_shape=jax.ShapeDtypeStruct((geo.rows, D_MODEL), F32),
        compiler_params=_params("parallel"),
        name="pack",
    )(x_prompt, x_sample, meta)


def _unpack_call(geo, x, dtype):
    prompt_spec, sample_spec = _layout_specs(geo)
    return pl.pallas_call(
        functools.partial(_unpack_kernel, geo),
        grid=(geo.steps,),
        in_specs=[pl.BlockSpec((GROUP_ROWS, D_MODEL), lambda i: (i, 0))],
        out_specs=[prompt_spec, sample_spec],
        out_shape=[jax.ShapeDtypeStruct((geo.n_prompt, (geo.cpp - 1) * CHUNK, D_MODEL), dtype),
                   jax.ShapeDtypeStruct((geo.n_sample, CHUNK - SAMPLE_NULL, D_MODEL), dtype)],
        compiler_params=_params("arbitrary"),
        name="unpack",
    )(x)


def _pad_lanes(v, width):
    return jnp.pad(v, ((0, 0), (0, width - v.shape[-1])))


def kernel(x_prompt, x_sample, state_delta_S, state_delta_conv, state_lru_h, state_lru_conv,
           meta_tokens, ffn1_norm, ffn1_w_gu, ffn1_w_down, mix_norm, w_in, delta_conv_w,
           delta_A_log, delta_dt_bias, delta_out_norm, lru_conv_w, lru_conv_b, lru_w_r, lru_b_r,
           lru_w_i, lru_b_i, lru_lambda, w_branch_a, w_branch_b, w_out, ffn2_norm, ffn2_w_gu,
           ffn2_w_down, final_norm):
    n_prompt, seq_len, _ = x_prompt.shape
    n_sample, dec_len, _ = x_sample.shape
    assert dec_len == CHUNK - SAMPLE_NULL and (N_META + seq_len) % CHUNK == N_META
    prompt_rows = PROMPT_NULL + N_META + seq_len
    cpp = prompt_rows // CHUNK
    geo = _Geometry(n_prompt, n_sample, cpp)
    assert geo.rows % ROW_TILE == 0 and (geo.prompt_chunks * CHUNK) % ROW_TILE == 0
    assert INPROJ_TILE % GROUP_ROWS == 0 and ROW_TILE % GROUP_ROWS == 0
    dt = x_prompt.dtype

    x = _pack_call(geo, x_prompt, x_sample, meta_tokens)

    cast = lambda w: w.astype(BF16)
    vec = lambda v: v.astype(F32)[:, None, :]
    wgu1, wd1, wgu2, wd2 = cast(ffn1_w_gu), cast(ffn1_w_down), cast(ffn2_w_gu), cast(ffn2_w_down)
    w_all = cast(w_in)
    pad_heads = lambda w: jnp.pad(w, ((0, 0), (0, 0), (0, 128 - w.shape[-1])))
    w_last = cast(pad_heads(w_in[:, :, P_COLS:]))
    w_ba = cast(jnp.concatenate([pad_heads(w_in[:, :, OFF_BETA:OFF_ALPHA]),
                                 pad_heads(w_in[:, :, OFF_ALPHA:OFF_LX])], axis=-1))
    w_ri = cast(jnp.concatenate([lru_w_r, lru_w_i], axis=-1))
    wa, wb, wo = cast(w_branch_a), cast(w_branch_b), cast(w_out)
    norm1, norm_mix, norm2 = vec(ffn1_norm), vec(mix_norm), vec(ffn2_norm)
    alog, dtb, onorm = vec(_pad_lanes(delta_A_log, 128)), vec(_pad_lanes(delta_dt_bias, 128)), vec(delta_out_norm)
    lcb, b_r, b_i, lam = vec(lru_conv_b), vec(lru_b_r), vec(lru_b_i), vec(lru_lambda)
    cw, lcw = delta_conv_w.astype(F32), lru_conv_w.astype(F32)
    s0, h0 = state_delta_S.astype(F32), state_lru_h.astype(F32)

    inj = jnp.concatenate([state_delta_conv, state_lru_conv], axis=-1).astype(F32)
    null_runs = SAMPLE_NULL // N_SLAB
    inj = jnp.pad(inj[:, :, :, None, :], ((0, 0), (0, 0), (0, 0), (null_runs - 1, SLAB - null_runs), (0, 0)))
    inj = inj.reshape(DEPTH, n_sample * TAIL_ROWS, CONV_COLS)

    last_ids = geo.last_chunk_ids()
    outs_s, outs_tail, outs_h = [], [], []
    for l in range(DEPTH):
        x = _ffn_call(l, x, norm1, wgu1, wd1)
        p, ba, tails = _inproj_call(geo, l, x, norm_mix, w_all, w_last, w_ba, cw, lcw, lcb, inj)
        oa, s_new = _delta_call(geo, l, p, ba, s0, alog, dtb, onorm)
        x, hlast = _outproj_call(geo, l, x, p, oa, h0, w_ri, b_r, b_i, lam, wa, wb, wo, norm2, wgu2, wd2,
                                 final_norm[None].astype(F32), final=(l == DEPTH - 1))
        outs_s.append(s_new)
        outs_tail.append(tails[last_ids, SLAB - 1::SLAB, :])
        outs_h.append(hlast[last_ids])

    n_prompt_rows = geo.prompt_chunks * CHUNK
    y_prompt, y_sample = _unpack_call(geo, x, dt)
    s_all = jnp.stack(outs_s)
    tail_all = jnp.stack(outs_tail)
    cq_all = tail_all[..., :QKV_W]
    cx_all = tail_all[..., QKV_W:]
    h_all = jnp.stack(outs_h)
    return (y_prompt.astype(dt), y_sample.astype(dt),
            s_all[:, :n_prompt].astype(dt), cq_all[:, :n_prompt].astype(dt),
            h_all[:, :n_prompt].astype(dt), cx_all[:, :n_prompt].astype(dt),
            s_all[:, n_prompt:].astype(state_delta_S.dtype),
            cq_all[:, n_prompt:].astype(state_delta_conv.dtype),
            h_all[:, n_prompt:].astype(state_lru_h.dtype),
            cx_all[:, n_prompt:].astype(state_lru_conv.dtype))
```

```python
import functools

import numpy as np
import jax
import jax.numpy as jnp
from jax import lax
from jax.experimental import pallas as pl
from jax.experimental.pallas import tpu as pltpu

F32 = jnp.float32
BF16 = jnp.bfloat16

D_MODEL = 1024
DEPTH = 4
N_META = 16
HA = 8
DK = 128
DV = 128
QK_W = HA * DK
V_W = HA * DV
QKV_W = 2 * QK_W + V_W
LRU_W = D_MODEL
NB = 8
BW = 128
CONV_W = 4
LRU_C = 8.0
D_FF = 2816
EPS = 1e-6

OFF_Z = QKV_W
OFF_BETA = OFF_Z + V_W
OFF_ALPHA = OFF_BETA + HA
OFF_LX = OFF_ALPHA + HA
OFF_LY = OFF_LX + LRU_W
OFF_GA = OFF_LY + LRU_W
OFF_GB = OFF_GA + D_MODEL
IN_COLS = OFF_GB + D_MODEL

CHUNK = 64
GROUP = 4
GROUP_ROWS = GROUP * CHUNK
PROMPT_NULL = CHUNK - N_META
SAMPLE_NULL = CHUNK // 2
INV_BASE = 16
ROW_TILE = 512
INPROJ_TILE = 256
FF_HALF = 256
CONV_PART = 512
CONV_COLS = QKV_W + LRU_W
SLAB = 8
N_SLAB = CHUNK // SLAB
TAIL_ROWS = (CONV_W - 1) * SLAB
P_Z, P_X, P_Y, P_G = 3072, 4096, 5120, 6144
P_COLS = 8192
VMEM_LIMIT = 56 * 1024 * 1024


def _rms(x, w):
    return x * lax.rsqrt(jnp.mean(x * x, axis=-1, keepdims=True) + EPS) * w


def _mm(a, b):
    return jnp.dot(a.astype(BF16), b.astype(BF16), preferred_element_type=F32)


def _mm_nt(a, b):
    return lax.dot_general(a.astype(BF16), b.astype(BF16), (((1,), (1,)), ((), ())),
                           preferred_element_type=F32)


def _mm_tn(a, b):
    return lax.dot_general(a.astype(BF16), b.astype(BF16), (((0,), (0,)), ((), ())),
                           preferred_element_type=F32)


def _sigmoid(x):
    return 0.5 * (1.0 + jnp.tanh(0.5 * x))


def _silu(x):
    return x * _sigmoid(x)


def _softplus(x):
    return jnp.maximum(x, 0.0) + jnp.log1p(jnp.exp(-jnp.abs(x)))


def _gelu_tanh(x):
    return 0.5 * x * (1.0 + jnp.tanh(0.7978845608028654 * (x + 0.044715 * (x * x * x))))


class _Geometry:
    def __init__(self, n_prompt, n_sample, chunks_per_prompt):
        assert n_prompt % GROUP == 0 and n_sample % GROUP == 0
        self.n_prompt = n_prompt
        self.n_sample = n_sample
        self.cpp = chunks_per_prompt
        self.prompt_steps = (n_prompt // GROUP) * chunks_per_prompt
        self.steps = self.prompt_steps + n_sample // GROUP
        self.prompt_chunks = GROUP * self.prompt_steps
        self.chunks = GROUP * self.steps
        self.rows = self.chunks * CHUNK
        self.n_streams = n_prompt + n_sample

    def chunk_null(self, cid):
        is_prompt = cid < self.prompt_chunks
        first = jnp.logical_or(jnp.logical_not(is_prompt), (cid // GROUP) % self.cpp == 0)
        return jnp.where(first, jnp.where(is_prompt, PROMPT_NULL, SAMPLE_NULL), 0)

    def last_chunk_ids(self):
        ids = [GROUP * ((b // GROUP) * self.cpp + self.cpp - 1) + b % GROUP for b in range(self.n_prompt)]
        ids += [self.prompt_chunks + t for t in range(self.n_sample)]
        return np.asarray(ids, np.int32)


def _row_time(rows):
    return N_SLAB * (rows % SLAB) + rows // SLAB


def _valid_rows(null):
    return _row_time(lax.broadcasted_iota(jnp.int32, (CHUNK, 1), 0)) >= null


def _slabs(x):
    return [x[v * SLAB:(v + 1) * SLAB] for v in range(N_SLAB)]


def _shift_run(x, fill):
    return jnp.concatenate([fill, x[:SLAB - 1]], axis=0)


def _scan_time(a, b, h_in):
    bs = _slabs(b)
    if a is None:
        for v in range(1, N_SLAB):
            bs[v] = bs[v] + bs[v - 1]
        run = bs[N_SLAB - 1]
        d = 1
        while d < SLAB:
            run = run + _shift_rows(run, d, 0.0)
            d *= 2
        run = run + h_in
        prev = _shift_run(run, h_in)
        hs = [t + prev for t in bs]
    else:
        as_ = _slabs(a)
        for v in range(1, N_SLAB):
            bs[v] = bs[v] + as_[v] * bs[v - 1]
            as_[v] = as_[v] * as_[v - 1]
        ra, rb = as_[N_SLAB - 1], bs[N_SLAB - 1]
        d = 1
        while d < SLAB:
            rb = rb + ra * _shift_rows(rb, d, 0.0)
            ra = ra * _shift_rows(ra, d, 1.0)
            d *= 2
        run = rb + ra * h_in
        prev = _shift_run(run, h_in)
        hs = [t + u * prev for t, u in zip(bs, as_)]
    return jnp.concatenate(hs, axis=0), run[SLAB - 1:SLAB, :]


def _ffn(x, nw, wgu_ref, wd_ref, h_ref):
    xn = _rms(x, nw).astype(BF16)
    for c in range(D_FF // FF_HALF):
        lo = c * FF_HALF
        g = jnp.dot(xn, wgu_ref[:, lo:lo + FF_HALF], preferred_element_type=F32)
        u = jnp.dot(xn, wgu_ref[:, D_FF + lo:D_FF + lo + FF_HALF], preferred_element_type=F32)
        h_ref[:, lo:lo + FF_HALF] = (_silu(g) * u).astype(BF16)
    y = jnp.dot(h_ref[...], wd_ref[...], preferred_element_type=F32)
    return x + 0.5 * y


def _ffn_kernel(x_ref, nw_ref, wgu_ref, wd_ref, o_ref, h_ref):
    o_ref[...] = _ffn(x_ref[...], nw_ref[...], wgu_ref, wd_ref, h_ref)


def _inproj_kernel(geo, x_ref, nw_ref, wa_ref, wt_ref, wl_ref, wba_ref, cw_ref, lcw_ref, lcb_ref, inj_ref,
                   p_ref, ba_ref, tails_ref, stage_all, carry_scr, wb_scr):
    i = pl.program_id(0)
    nch = INPROJ_TILE // CHUNK
    xn = _rms(x_ref[...], nw_ref[...]).astype(BF16)

    def project_cols(lo, width):
        w_ref, start = (wa_ref, lo) if lo < P_COLS // 2 else (wb_scr, lo - P_COLS // 2)
        return jnp.dot(xn, w_ref[:, start:start + width], preferred_element_type=F32)

    @pl.when(i == 0)
    def _():
        carry_scr[...] = jnp.zeros_like(carry_scr)
        skip = OFF_LX - OFF_BETA
        for c in range(P_COLS // 2 // 1024):
            nxt = wt_ref[:, (c + 1) * 1024:(c + 1) * 1024 + 128] if (c + 1) * 1024 < P_COLS // 2 else wl_ref[...]
            wide = jnp.concatenate([wt_ref[:, c * 1024:(c + 1) * 1024], nxt], axis=1)
            wb_scr[:, c * 1024:(c + 1) * 1024] = pltpu.roll(wide, wide.shape[1] - skip, 1)[:, :1024]

    valid = [_valid_rows(geo.chunk_null(i * nch + j)) for j in range(nch)]
    in_sample = i >= geo.prompt_chunks * CHUNK // INPROJ_TILE
    kinds = (("q", 0, 0), ("k", QK_W, QK_W), ("v", 2 * QK_W, 2 * QK_W), ("x", P_X, QKV_W))
    t0 = TAIL_ROWS
    part_w = 1024 // nch

    def project(kidx, part):
        kind, lo, col = kinds[kidx]
        stage = stage_all.at[kidx]
        pre = project_cols(lo + part * part_w, part_w)
        sc = slice(part * part_w, (part + 1) * part_w)
        cc = slice(col + part * part_w, col + (part + 1) * part_w)
        for j in range(nch):
            stage[j, t0:t0 + CHUNK, sc] = pre[j * CHUNK:(j + 1) * CHUNK]
            if j >= GROUP:
                stage[j, 0:t0, sc] = pre[(j - GROUP + 1) * CHUNK - t0:(j - GROUP + 1) * CHUNK]
            else:
                stage[j, 0:t0, sc] = carry_scr[j, :, cc]
            stage[j, CHUNK:CHUNK + t0, sc] = stage[j, CHUNK:CHUNK + t0, sc] + jnp.where(
                in_sample, inj_ref[t0 * j:t0 * (j + 1), cc], 0.0)
            tails_ref[j, :, cc] = stage[j, CHUNK:CHUNK + t0, sc]
        for g in range(GROUP):
            carry_scr[g, :, cc] = stage[nch - GROUP + g, CHUNK:CHUNK + t0, sc]

    def convolve(kidx, j, half):
        kind, lo, col = kinds[kidx]
        stage = stage_all.at[kidx]
        hs = slice(half * CONV_PART, (half + 1) * CONV_PART)
        taps = lcw_ref[:, hs] if kind == "x" else cw_ref[:, lo + hs.start:lo + hs.stop]
        cur = [stage[j, t0 + v * SLAB:t0 + (v + 1) * SLAB, hs] for v in range(N_SLAB)]
        back = []
        for k in range(CONV_W - 1):
            prev_last = stage[j, (k + 1) * SLAB - 1:(k + 1) * SLAB, hs]
            back.append(_shift_run(cur[N_SLAB - (CONV_W - 1) + k], prev_last))
        conv = []
        for v in range(N_SLAB):
            acc = cur[v] * taps[CONV_W - 1:CONV_W, :]
            for d in range(1, CONV_W):
                src = cur[v - d] if v >= d else back[CONV_W - 1 - d + v]
                acc = acc + src * taps[CONV_W - 1 - d:CONV_W - d, :]
            conv.append(acc)
        conv = jnp.concatenate(conv, axis=0)
        if kind == "x":
            out = conv + lcb_ref[:, hs]
        else:
            act = _silu(conv)
            if kind != "v":
                scale = DK ** -0.5 if kind == "q" else 1.0
                segs = []
                for h in range(CONV_PART // DK):
                    seg = act[:, h * DK:(h + 1) * DK]
                    segs.append(seg * (lax.rsqrt(jnp.sum(seg * seg, axis=-1, keepdims=True) + EPS) * scale))
                act = jnp.concatenate(segs, axis=1)
            out = jnp.where(valid[j], act, 0.0)
        p_ref[j * CHUNK:(j + 1) * CHUNK, lo + hs.start:lo + hs.stop] = out.astype(BF16)

    def plain(lo, part):
        sl = slice(lo + part * part_w, lo + (part + 1) * part_w)
        p_ref[:, sl] = project_cols(sl.start, part_w).astype(BF16)

    plain_cols = (P_Z, P_Y, P_G, P_G + 1024)
    for part in range(nch):
        project(0, part)
    for kidx in range(len(kinds)):
        for j in range(nch):
            plain(plain_cols[kidx], j)
            convolve(kidx, j, 0)
            if kidx + 1 < len(kinds):
                project(kidx + 1, j)
            convolve(kidx, j, 1)
    ba_ref[...] = jnp.dot(xn, wba_ref[...], preferred_element_type=F32)


def _shift_rows(x, d, fill):
    if d % 8 == 0:
        return jnp.concatenate([jnp.full((d, x.shape[1]), fill, x.dtype), x[:x.shape[0] - d]], axis=0)
    rows = lax.broadcasted_iota(jnp.int32, x.shape, 0)
    return jnp.where(rows >= d, pltpu.roll(x, d, 0), fill)


def _outproj_kernel(geo, final, x_ref, g_ref, oa_ref, xl_ref, h0_ref,
                    wri_ref, br_ref, bi_ref, lam_ref, wa_ref, wb_ref, wo_ref,
                    nw_ref, wgu_ref, wd_ref, fn_ref,
                    o_ref, hlast_ref, h_ref, hc_scr, ob_scr):
    i = pl.program_id(0)
    nch = ROW_TILE // CHUNK

    @pl.when(i == 0)
    def _():
        hc_scr[...] = jnp.zeros_like(hc_scr)

    sp = _softplus(-lam_ref[...])
    nulls = [geo.chunk_null(i * nch + j) for j in range(nch)]
    valid = [_valid_rows(n) for n in nulls]
    is_sample = [(i * nch + j) >= geo.prompt_chunks for j in range(nch)]
    for n in range(NB):
        sl = slice(n * BW, (n + 1) * BW)
        xb = xl_ref[:, sl]
        gates = jnp.dot(xb, wri_ref[n], preferred_element_type=F32)
        r = _sigmoid(gates[:, :BW] + br_ref[:, sl])
        ig = _sigmoid(gates[:, BW:] + bi_ref[:, sl])
        log_a = -LRU_C * r * sp[:, sl]
        a_all = jnp.exp(log_a)
        mult = jnp.sqrt(-jnp.tanh(log_a) * (1.0 + a_all * a_all))
        b_all = mult * (ig * xb.astype(F32))
        gy = _gelu_tanh(xl_ref[:, LRU_W + n * BW:LRU_W + (n + 1) * BW].astype(F32))
        carry = [hc_scr[g:g + 1, sl] for g in range(GROUP)]
        for j in range(nch):
            rows = slice(j * CHUNK, (j + 1) * CHUNK)
            a = jnp.where(valid[j], a_all[rows], 1.0)
            b = jnp.where(valid[j], b_all[rows], 0.0)
            h_in = jnp.where(nulls[j] > 0, jnp.where(is_sample[j], h0_ref[j:j + 1, sl], 0.0), carry[j % GROUP])
            hs, carry[j % GROUP] = _scan_time(a, b, h_in)
            ob_scr[rows, sl] = (hs * gy[rows]).astype(BF16)
            hlast_ref[j:j + 1, sl] = carry[j % GROUP]
        for g in range(GROUP):
            hc_scr[g:g + 1, sl] = carry[g]

    ga = _sigmoid(g_ref[:, :D_MODEL].astype(F32))
    gb = _sigmoid(g_ref[:, D_MODEL:].astype(F32))
    ma = jnp.dot(oa_ref[...], wa_ref[...], preferred_element_type=F32)
    mb = jnp.dot(ob_scr[...], wb_ref[...], preferred_element_type=F32)
    m = (ga * ma + gb * mb).astype(BF16)
    x = x_ref[...] + jnp.dot(m, wo_ref[...], preferred_element_type=F32)
    x = _ffn(x, nw_ref[...], wgu_ref, wd_ref, h_ref)
    if final:
        x = _rms(x, fn_ref[...])
    o_ref[...] = x


def _resident(shape, layer):
    nd = len(shape)
    return pl.BlockSpec((None,) + shape, lambda i: (layer,) + (0,) * nd, pipeline_mode=pl.Buffered(1))


def _params(semantics):
    return pltpu.CompilerParams(dimension_semantics=(semantics,), vmem_limit_bytes=VMEM_LIMIT)


def _ffn_call(layer, x, nw, wgu, wd):
    n = x.shape[0]
    row = lambda i: (i, 0)
    return pl.pallas_call(
        _ffn_kernel,
        grid=(n // ROW_TILE,),
        in_specs=[pl.BlockSpec((ROW_TILE, D_MODEL), row),
                  _resident((1, D_MODEL), layer),
                  _resident((D_MODEL, 2 * D_FF), layer),
                  _resident((D_FF, D_MODEL), layer)],
        out_specs=pl.BlockSpec((ROW_TILE, D_MODEL), row),
        out_shape=jax.ShapeDtypeStruct((n, D_MODEL), F32),
        scratch_shapes=[pltpu.VMEM((ROW_TILE, D_FF), BF16)],
        compiler_params=_params("parallel"),
        name="ffn",
    )(x, nw, wgu, wd)


def _inproj_call(geo, layer, x, nw, w_all, w_last, wba, cw, lcw, lcb, inj):
    n = x.shape[0]
    nch = INPROJ_TILE // CHUNK
    sample_tile0 = geo.prompt_chunks * CHUNK // INPROJ_TILE
    row = lambda i: (i, 0)
    half = P_COLS // 2
    assert OFF_BETA == half and w_all.shape[2] == 2 * half + OFF_LX - OFF_BETA and w_last.shape[2] == 128
    return pl.pallas_call(
        functools.partial(_inproj_kernel, geo),
        grid=(n // INPROJ_TILE,),
        in_specs=[pl.BlockSpec((INPROJ_TILE, D_MODEL), row),
                  _resident((1, D_MODEL), layer),
                  _resident((D_MODEL, half), layer),
                  pl.BlockSpec((None, D_MODEL, half), lambda i: (layer, 0, 1), pipeline_mode=pl.Buffered(1)),
                  _resident((D_MODEL, 128), layer),
                  _resident((D_MODEL, 256), layer),
                  _resident((CONV_W, QKV_W), layer),
                  _resident((CONV_W, LRU_W), layer),
                  _resident((1, LRU_W), layer),
                  pl.BlockSpec((None, TAIL_ROWS * nch, CONV_COLS),
                               lambda i: (layer, jnp.maximum(i - sample_tile0, 0), 0))],
        out_specs=[pl.BlockSpec((INPROJ_TILE, P_COLS), row),
                   pl.BlockSpec((INPROJ_TILE, 256), row),
                   pl.BlockSpec((nch, TAIL_ROWS, CONV_COLS), lambda i: (i, 0, 0))],
        out_shape=[jax.ShapeDtypeStruct((n, P_COLS), BF16),
                   jax.ShapeDtypeStruct((n, 256), F32),
                   jax.ShapeDtypeStruct((n // CHUNK, TAIL_ROWS, CONV_COLS), F32)],
        scratch_shapes=[pltpu.VMEM((4, nch, CHUNK + TAIL_ROWS, 1024), F32),
                        pltpu.VMEM((GROUP, TAIL_ROWS, CONV_COLS), F32),
                        pltpu.VMEM((D_MODEL, half), BF16)],
        compiler_params=_params("arbitrary"),
        name="inproj",
    )(x, nw, w_all, w_all, w_last, wba, cw, lcw, lcb, inj)


def _outproj_call(geo, layer, x, p, oa, h0, wri, br, bi, lam, wa, wb, wo, nw, wgu, wd, fn, final):
    n = x.shape[0]
    nch = ROW_TILE // CHUNK
    sample_tile0 = geo.prompt_chunks * CHUNK // ROW_TILE
    row = lambda i: (i, 0)
    return pl.pallas_call(
        functools.partial(_outproj_kernel, geo, final),
        grid=(n // ROW_TILE,),
        in_specs=[pl.BlockSpec((ROW_TILE, D_MODEL), row),
                  pl.BlockSpec((ROW_TILE, 2 * D_MODEL), lambda i: (i, P_G // 2048)),
                  pl.BlockSpec((ROW_TILE, V_W), row),
                  pl.BlockSpec((ROW_TILE, 2 * LRU_W), lambda i: (i, P_X // 2048)),
                  pl.BlockSpec((None, nch, LRU_W), lambda i: (layer, jnp.maximum(i - sample_tile0, 0), 0)),
                  _resident((NB, BW, 2 * BW), layer),
                  _resident((1, LRU_W), layer),
                  _resident((1, LRU_W), layer),
                  _resident((1, LRU_W), layer),
                  _resident((V_W, D_MODEL), layer),
                  _resident((LRU_W, D_MODEL), layer),
                  _resident((D_MODEL, D_MODEL), layer),
                  _resident((1, D_MODEL), layer),
                  _resident((D_MODEL, 2 * D_FF), layer),
                  _resident((D_FF, D_MODEL), layer),
                  pl.BlockSpec((1, D_MODEL), lambda i: (0, 0), pipeline_mode=pl.Buffered(1))],
        out_specs=[pl.BlockSpec((ROW_TILE, D_MODEL), row),
                   pl.BlockSpec((nch, LRU_W), row)],
        out_shape=[jax.ShapeDtypeStruct((n, D_MODEL), F32),
                   jax.ShapeDtypeStruct((n // CHUNK, LRU_W), F32)],
        scratch_shapes=[pltpu.VMEM((ROW_TILE, D_FF), BF16),
                        pltpu.VMEM((8, LRU_W), F32),
                        pltpu.VMEM((ROW_TILE, LRU_W), BF16)],
        compiler_params=_params("arbitrary"),
        name="outproj_ffn",
    )(x, p, oa, p, h0, wri, br, bi, lam, wa, wb, wo, nw, wgu, wd, fn)


def _unit_lower_inverse(lows, c):
    ti = _row_time(lax.broadcasted_iota(jnp.int32, (c, c), 0))
    tj = _row_time(lax.broadcasted_iota(jnp.int32, (c, c), 1))
    same_block = (ti // INV_BASE) == (tj // INV_BASE)
    eye = (ti == tj).astype(F32)
    ld = [jnp.where(same_block, low, 0.0) for low in lows]
    lo = [low - d for low, d in zip(lows, ld)]
    x = [eye - d for d in ld]
    p = ld
    span = 2
    while span < INV_BASE:
        p = [_mm(t, t) for t in p]
        x = [a + _mm(a, t) for a, t in zip(x, p)]
        span *= 2
    e = [_mm(a, b) for a, b in zip(x, lo)]
    y = [eye - t for t in e]
    q = e
    span = 2
    while span < c // INV_BASE:
        q = [_mm(t, t) for t in q]
        y = [a + _mm(a, t) for a, t in zip(y, q)]
        span *= 2
    return [_mm(a, b) for a, b in zip(y, x)]


def _delta_kernel(geo, qkv_ref, z_ref, ba_ref, s0_ref, alog_ref, dtb_ref, onorm_ref,
                  oa_ref, sout_ref, s_scr):
    c = CHUNK
    i = pl.program_id(0)
    is_prompt = i < geo.prompt_steps
    pos = i % geo.cpp
    first = jnp.logical_or(jnp.logical_not(is_prompt), pos == 0)
    last = jnp.logical_or(jnp.logical_not(is_prompt), pos == geo.cpp - 1)
    null = jnp.where(first, jnp.where(is_prompt, PROMPT_NULL, SAMPLE_NULL), 0)
    valid = _valid_rows(null)

    @pl.when(jnp.logical_and(first, is_prompt))
    def _():
        s_scr[...] = jnp.zeros_like(s_scr)

    @pl.when(jnp.logical_not(is_prompt))
    def _():
        s_scr[...] = s0_ref[...]

    ti = _row_time(lax.broadcasted_iota(jnp.int32, (c, c), 0))
    tj = _row_time(lax.broadcasted_iota(jnp.int32, (c, c), 1))
    tri_incl = ti >= tj
    tri_strict = ti > tj

    beta_all, gc_all, gc_t, eg_all, ekd_all, egl_all = [], [], [], [], [], []
    for s in range(GROUP):
        ba = ba_ref[s * c:(s + 1) * c, :]
        beta_all.append(jnp.where(valid, _sigmoid(ba[:, :128]), 0.0))
        g = -jnp.exp(alog_ref[...]) * _softplus(ba[:, 128:] + dtb_ref[...])
        gc, g_last = _scan_time(None, jnp.where(valid, g, 0.0), jnp.zeros((1, 128), F32))
        gc_all.append(gc)
        gc_t.append(jnp.concatenate([gc, jnp.zeros((128 - c, 128), F32)], axis=0).T)
        eg_all.append(jnp.exp(gc))
        ekd_all.append(jnp.exp(g_last - gc))
        egl_all.append(jnp.exp(g_last))

    probs = [(s, h) for s in range(GROUP) for h in range(HA)]
    rows = lambda s: slice(s * c, (s + 1) * c)
    qb = [qkv_ref[rows(s), h * DK:(h + 1) * DK] for s, h in probs]
    kbf = [qkv_ref[rows(s), QK_W + h * DK:QK_W + (h + 1) * DK] for s, h in probs]
    vb = [qkv_ref[rows(s), 2 * QK_W + h * DV:2 * QK_W + (h + 1) * DV] for s, h in probs]
    q = [t.astype(F32) for t in qb]
    k = [t.astype(F32) for t in kbf]
    beta = [beta_all[s][:, h:h + 1] for s, h in probs]
    eg = [eg_all[s][:, h:h + 1] for s, h in probs]
    n = len(probs)
    kb = [k[p] * beta[p] for p in range(n)]
    sc = [_mm_nt(jnp.concatenate([kb[p].astype(BF16), qb[p]], axis=0), kbf[p]) for p in range(n)]
    decay = [jnp.where(tri_incl, jnp.exp(jnp.where(tri_incl, gc_all[s][:, h:h + 1] - gc_t[s][h:h + 1, :c], 0.0)), 0.0)
             for s, h in probs]
    low = [jnp.where(tri_strict, sc[p][:c] * decay[p], 0.0) for p in range(n)]
    a_intra = [sc[p][c:] * decay[p] for p in range(n)]
    t_inv = _unit_lower_inverse(low, c)
    uw = [_mm(t_inv[p], jnp.concatenate([vb[p].astype(F32) * beta[p], kb[p] * eg[p]], axis=1))
          for p in range(n)]
    s_old = [s_scr[s, h] for s, h in probs]
    wq = [_mm(jnp.concatenate([uw[p][:, DV:], q[p] * eg[p]], axis=0), s_old[p]) for p in range(n)]
    v_new = [uw[p][:, :DV] - wq[p][:c] for p in range(n)]
    o_intra = [_mm(a_intra[p], v_new[p]) for p in range(n)]
    ds = [_mm_tn(k[p] * ekd_all[s][:, h:h + 1], v_new[p]) for p, (s, h) in enumerate(probs)]
    for p, (s, h) in enumerate(probs):
        s_scr[s, h] = s_old[p] * egl_all[s][:, h:h + 1] + ds[p]
        o = wq[p][c:] + o_intra[p]
        zh = z_ref[rows(s), h * DV:(h + 1) * DV].astype(F32)
        o = o * lax.rsqrt(jnp.mean(o * o, axis=-1, keepdims=True) + EPS) * onorm_ref[...] * _silu(zh)
        oa_ref[rows(s), h * DV:(h + 1) * DV] = o.astype(oa_ref.dtype)

    @pl.when(last)
    def _():
        sout_ref[...] = s_scr[...]


def _delta_call(geo, layer, p, ba, s0, alog, dtb, onorm):
    n_pp = geo.n_prompt // GROUP

    def pair(i):
        return jnp.where(i < geo.prompt_steps, i // geo.cpp, i - geo.prompt_steps + n_pp)

    vec = lambda width: pl.BlockSpec((None, 1, width), lambda i: (layer, 0, 0))
    return pl.pallas_call(
        functools.partial(_delta_kernel, geo),
        grid=(geo.steps,),
        in_specs=[pl.BlockSpec((GROUP_ROWS, QKV_W), lambda i: (i, 0)),
                  pl.BlockSpec((GROUP_ROWS, V_W), lambda i: (i, P_Z // V_W)),
                  pl.BlockSpec((GROUP_ROWS, 256), lambda i: (i, 0)),
                  pl.BlockSpec((None, GROUP, HA, DK, DV),
                               lambda i: (layer, jnp.maximum(i - geo.prompt_steps, 0), 0, 0, 0)),
                  vec(128), vec(128), vec(DV)],
        out_specs=[pl.BlockSpec((GROUP_ROWS, V_W), lambda i: (i, 0)),
                   pl.BlockSpec((GROUP, HA, DK, DV), lambda i: (pair(i), 0, 0, 0))],
        out_shape=[jax.ShapeDtypeStruct((geo.rows, V_W), BF16),
                   jax.ShapeDtypeStruct((geo.n_streams, HA, DK, DV), F32)],
        scratch_shapes=[pltpu.VMEM((GROUP, HA, DK, DV), F32)],
        compiler_params=_params("arbitrary"),
        name="delta",
    )(p, p, ba, s0, alog, dtb, onorm)


def _permute_chunk(x):
    return jnp.swapaxes(x.reshape(SLAB, N_SLAB, x.shape[-1]), 0, 1).reshape(CHUNK, x.shape[-1])


def _pack_kernel(geo, xp_ref, xs_ref, meta_ref, o_ref):
    i = pl.program_id(0)
    is_prompt = i < geo.prompt_steps
    pos = i % geo.cpp
    width = o_ref.shape[1]

    @pl.when(jnp.logical_and(is_prompt, pos > 0))
    def _():
        for s in range(GROUP):
            o_ref[s * CHUNK:(s + 1) * CHUNK, :] = _permute_chunk(xp_ref[s].astype(o_ref.dtype))

    @pl.when(jnp.logical_and(is_prompt, pos == 0))
    def _():
        first = _permute_chunk(jnp.concatenate(
            [jnp.zeros((PROMPT_NULL, width), o_ref.dtype), meta_ref[...].astype(o_ref.dtype)], axis=0))
        for s in range(GROUP):
            o_ref[s * CHUNK:(s + 1) * CHUNK, :] = first

    @pl.when(jnp.logical_not(is_prompt))
    def _():
        for s in range(GROUP):
            o_ref[s * CHUNK:(s + 1) * CHUNK, :] = _permute_chunk(jnp.concatenate(
                [jnp.zeros((SAMPLE_NULL, width), o_ref.dtype), xs_ref[s].astype(o_ref.dtype)], axis=0))


def _unpack_kernel(geo, x_ref, yp_ref, ys_ref):
    i = pl.program_id(0)
    is_prompt = i < geo.prompt_steps
    pos = i % geo.cpp

    @pl.when(jnp.logical_and(is_prompt, pos > 0))
    def _():
        for s in range(GROUP):
            yp_ref[s] = _permute_chunk(x_ref[s * CHUNK:(s + 1) * CHUNK, :]).astype(yp_ref.dtype)

    @pl.when(jnp.logical_not(is_prompt))
    def _():
        for s in range(GROUP):
            ys_ref[s] = _permute_chunk(x_ref[s * CHUNK:(s + 1) * CHUNK, :])[SAMPLE_NULL:].astype(ys_ref.dtype)


def _layout_specs(geo):
    def prompt_idx(i):
        step = jnp.minimum(i, geo.prompt_steps - 1)
        return (step // geo.cpp, jnp.maximum(step % geo.cpp - 1, 0), 0)

    def sample_idx(i):
        return (jnp.maximum(i - geo.prompt_steps, 0), 0, 0)

    return (pl.BlockSpec((GROUP, CHUNK, D_MODEL), prompt_idx),
            pl.BlockSpec((GROUP, CHUNK - SAMPLE_NULL, D_MODEL), sample_idx))


def _pack_call(geo, x_prompt, x_sample, meta):
    prompt_spec, sample_spec = _layout_specs(geo)
    return pl.pallas_call(
        functools.partial(_pack_kernel, geo),
        grid=(geo.steps,),
        in_specs=[prompt_spec, sample_spec, pl.BlockSpec((N_META, D_MODEL), lambda i: (0, 0))],
        out_specs=pl.BlockSpec((GROUP_ROWS, D_MODEL), lambda i: (i, 0)),
        out_shape=jax.ShapeDtypeStruct((geo.rows, D_MODEL), F32),
        compiler_params=_params("parallel"),
        name="pack",
    )(x_prompt, x_sample, meta)


def _unpack_call(geo, x, dtype):
    prompt_spec, sample_spec = _layout_specs(geo)
    return pl.pallas_call(
        functools.partial(_unpack_kernel, geo),
        grid=(geo.steps,),
        in_specs=[pl.BlockSpec((GROUP_ROWS, D_MODEL), lambda i: (i, 0))],
        out_specs=[prompt_spec, sample_spec],
        out_shape=[jax.ShapeDtypeStruct((geo.n_prompt, (geo.cpp - 1) * CHUNK, D_MODEL), dtype),
                   jax.ShapeDtypeStruct((geo.n_sample, CHUNK - SAMPLE_NULL, D_MODEL), dtype)],
        compiler_params=_params("arbitrary"),
        name="unpack",
    )(x)


def _pad_lanes(v, width):
    return jnp.pad(v, ((0, 0), (0, width - v.shape[-1])))


def kernel(x_prompt, x_sample, state_delta_S, state_delta_conv, state_lru_h, state_lru_conv,
           meta_tokens, ffn1_norm, ffn1_w_gu, ffn1_w_down, mix_norm, w_in, delta_conv_w,
           delta_A_log, delta_dt_bias, delta_out_norm, lru_conv_w, lru_conv_b, lru_w_r, lru_b_r,
           lru_w_i, lru_b_i, lru_lambda, w_branch_a, w_branch_b, w_out, ffn2_norm, ffn2_w_gu,
           ffn2_w_down, final_norm):
    n_prompt, seq_len, _ = x_prompt.shape
    n_sample, dec_len, _ = x_sample.shape
    assert dec_len == CHUNK - SAMPLE_NULL and (N_META + seq_len) % CHUNK == N_META
    prompt_rows = PROMPT_NULL + N_META + seq_len
    cpp = prompt_rows // CHUNK
    geo = _Geometry(n_prompt, n_sample, cpp)
    assert geo.rows % ROW_TILE == 0 and (geo.prompt_chunks * CHUNK) % ROW_TILE == 0
    assert INPROJ_TILE % GROUP_ROWS == 0 and ROW_TILE % GROUP_ROWS == 0
    dt = x_prompt.dtype

    x = _pack_call(geo, x_prompt, x_sample, meta_tokens)

    cast = lambda w: w.astype(BF16)
    vec = lambda v: v.astype(F32)[:, None, :]
    wgu1, wd1, wgu2, wd2 = cast(ffn1_w_gu), cast(ffn1_w_down), cast(ffn2_w_gu), cast(ffn2_w_down)
    w_all = cast(w_in)
    pad_heads = lambda w: jnp.pad(w, ((0, 0), (0, 0), (0, 128 - w.shape[-1])))
    w_last = cast(pad_heads(w_in[:, :, P_COLS:]))
    w_ba = cast(jnp.concatenate([pad_heads(w_in[:, :, OFF_BETA:OFF_ALPHA]),
                                 pad_heads(w_in[:, :, OFF_ALPHA:OFF_LX])], axis=-1))
    w_ri = cast(jnp.concatenate([lru_w_r, lru_w_i], axis=-1))
    wa, wb, wo = cast(w_branch_a), cast(w_branch_b), cast(w_out)
    norm1, norm_mix, norm2 = vec(ffn1_norm), vec(mix_norm), vec(ffn2_norm)
    alog, dtb, onorm = vec(_pad_lanes(delta_A_log, 128)), vec(_pad_lanes(delta_dt_bias, 128)), vec(delta_out_norm)
    lcb, b_r, b_i, lam = vec(lru_conv_b), vec(lru_b_r), vec(lru_b_i), vec(lru_lambda)
    cw, lcw = delta_conv_w.astype(F32), lru_conv_w.astype(F32)
    s0, h0 = state_delta_S.astype(F32), state_lru_h.astype(F32)

    inj = jnp.concatenate([state_delta_conv, state_lru_conv], axis=-1).astype(F32)
    null_runs = SAMPLE_NULL // N_SLAB
    inj = jnp.pad(inj[:, :, :, None, :], ((0, 0), (0, 0), (0, 0), (null_runs - 1, SLAB - null_runs), (0, 0)))
    inj = inj.reshape(DEPTH, n_sample * TAIL_ROWS, CONV_COLS)

    last_ids = geo.last_chunk_ids()
    outs_s, outs_tail, outs_h = [], [], []
    for l in range(DEPTH):
        x = _ffn_call(l, x, norm1, wgu1, wd1)
        p, ba, tails = _inproj_call(geo, l, x, norm_mix, w_all, w_last, w_ba, cw, lcw, lcb, inj)
        oa, s_new = _delta_call(geo, l, p, ba, s0, alog, dtb, onorm)
        x, hlast = _outproj_call(geo, l, x, p, oa, h0, w_ri, b_r, b_i, lam, wa, wb, wo, norm2, wgu2, wd2,
                                 final_norm[None].astype(F32), final=(l == DEPTH - 1))
        outs_s.append(s_new)
        outs_tail.append(tails[last_ids, SLAB - 1::SLAB, :])
        outs_h.append(hlast[last_ids])

    n_prompt_rows = geo.prompt_chunks * CHUNK
    y_prompt, y_sample = _unpack_call(geo, x, dt)
    s_all = jnp.stack(outs_s)
    tail_all = jnp.stack(outs_tail)
    cq_all = tail_all[..., :QKV_W]
    cx_all = tail_all[..., QKV_W:]
    h_all = jnp.stack(outs_h)
    return (y_prompt.astype(dt), y_sample.astype(dt),
            s_all[:, :n_prompt].astype(dt), cq_all[:, :n_prompt].astype(dt),
            h_all[:, :n_prompt].astype(dt), cx_all[:, :n_prompt].astype(dt),
            s_all[:, n_prompt:].astype(state_delta_S.dtype),
            cq_all[:, n_prompt:].astype(state_delta_conv.dtype),
            h_all[:, n_prompt:].astype(state_lru_h.dtype),
            cx_all[:, n_prompt:].astype(state_lru_conv.dtype))
```

```python
import functools

import numpy as np
import jax
import jax.numpy as jnp
from jax import lax
from jax.experimental import pallas as pl
from jax.experimental.pallas import tpu as pltpu

F32 = jnp.float32
BF16 = jnp.bfloat16

D_MODEL = 1024
DEPTH = 4
N_META = 16
HA = 8
DK = 128
DV = 128
QK_W = HA * DK
V_W = HA * DV
QKV_W = 2 * QK_W + V_W
LRU_W = D_MODEL
NB = 8
BW = 128
CONV_W = 4
LRU_C = 8.0
D_FF = 2816
EPS = 1e-6

OFF_Z = QKV_W
OFF_BETA = OFF_Z + V_W
OFF_ALPHA = OFF_BETA + HA
OFF_LX = OFF_ALPHA + HA
OFF_LY = OFF_LX + LRU_W
OFF_GA = OFF_LY + LRU_W
OFF_GB = OFF_GA + D_MODEL
IN_COLS = OFF_GB + D_MODEL

CHUNK = 64
GROUP = 4
GROUP_ROWS = GROUP * CHUNK
PROMPT_NULL = CHUNK - N_META
SAMPLE_NULL = CHUNK // 2
INV_BASE = 16
ROW_TILE = 512
INPROJ_TILE = 256
FF_HALF = 256
CONV_PART = 512
CONV_COLS = QKV_W + LRU_W
SLAB = 8
N_SLAB = CHUNK // SLAB
TAIL_ROWS = (CONV_W - 1) * SLAB
P_Z, P_X, P_Y, P_G = 3072, 4096, 5120, 6144
P_COLS = 8192
VMEM_LIMIT = 56 * 1024 * 1024


def _rms(x, w):
    return x * lax.rsqrt(jnp.mean(x * x, axis=-1, keepdims=True) + EPS) * w


def _mm(a, b):
    return jnp.dot(a.astype(BF16), b.astype(BF16), preferred_element_type=F32)


def _mm_nt(a, b):
    return lax.dot_general(a.astype(BF16), b.astype(BF16), (((1,), (1,)), ((), ())),
                           preferred_element_type=F32)


def _mm_tn(a, b):
    return lax.dot_general(a.astype(BF16), b.astype(BF16), (((0,), (0,)), ((), ())),
                           preferred_element_type=F32)


def _sigmoid(x):
    return 0.5 * (1.0 + jnp.tanh(0.5 * x))


def _silu(x):
    return x * _sigmoid(x)


def _softplus(x):
    return jnp.maximum(x, 0.0) + jnp.log1p(jnp.exp(-jnp.abs(x)))


def _gelu_tanh(x):
    return 0.5 * x * (1.0 + jnp.tanh(0.7978845608028654 * (x + 0.044715 * (x * x * x))))


class _Geometry:
    def __init__(self, n_prompt, n_sample, chunks_per_prompt):
        assert n_prompt % GROUP == 0 and n_sample % GROUP == 0
        self.n_prompt = n_prompt
        self.n_sample = n_sample
        self.cpp = chunks_per_prompt
        self.prompt_steps = (n_prompt // GROUP) * chunks_per_prompt
        self.steps = self.prompt_steps + n_sample // GROUP
        self.prompt_chunks = GROUP * self.prompt_steps
        self.chunks = GROUP * self.steps
        self.rows = self.chunks * CHUNK
        self.n_streams = n_prompt + n_sample

    def chunk_null(self, cid):
        is_prompt = cid < self.prompt_chunks
        first = jnp.logical_or(jnp.logical_not(is_prompt), (cid // GROUP) % self.cpp == 0)
        return jnp.where(first, jnp.where(is_prompt, PROMPT_NULL, SAMPLE_NULL), 0)

    def last_chunk_ids(self):
        ids = [GROUP * ((b // GROUP) * self.cpp + self.cpp - 1) + b % GROUP for b in range(self.n_prompt)]
        ids += [self.prompt_chunks + t for t in range(self.n_sample)]
        return np.asarray(ids, np.int32)


def _row_time(rows):
    return N_SLAB * (rows % SLAB) + rows // SLAB


def _valid_rows(null):
    return _row_time(lax.broadcasted_iota(jnp.int32, (CHUNK, 1), 0)) >= null


def _slabs(x):
    return [x[v * SLAB:(v + 1) * SLAB] for v in range(N_SLAB)]


def _shift_run(x, fill):
    return jnp.concatenate([fill, x[:SLAB - 1]], axis=0)


def _scan_time(a, b, h_in):
    bs = _slabs(b)
    if a is None:
        for v in range(1, N_SLAB):
            bs[v] = bs[v] + bs[v - 1]
        run = bs[N_SLAB - 1]
        d = 1
        while d < SLAB:
            run = run + _shift_rows(run, d, 0.0)
            d *= 2
        run = run + h_in
        prev = _shift_run(run, h_in)
        hs = [t + prev for t in bs]
    else:
        as_ = _slabs(a)
        for v in range(1, N_SLAB):
            bs[v] = bs[v] + as_[v] * bs[v - 1]
            as_[v] = as_[v] * as_[v - 1]
        ra, rb = as_[N_SLAB - 1], bs[N_SLAB - 1]
        d = 1
        while d < SLAB:
            rb = rb + ra * _shift_rows(rb, d, 0.0)
            ra = ra * _shift_rows(ra, d, 1.0)
            d *= 2
        run = rb + ra * h_in
        prev = _shift_run(run, h_in)
        hs = [t + u * prev for t, u in zip(bs, as_)]
    return jnp.concatenate(hs, axis=0), run[SLAB - 1:SLAB, :]


def _ffn(x, nw, wgu_ref, wd_ref, h_ref):
    xn = _rms(x, nw).astype(BF16)
    for c in range(D_FF // FF_HALF):
        lo = c * FF_HALF
        g = jnp.dot(xn, wgu_ref[:, lo:lo + FF_HALF], preferred_element_type=F32)
        u = jnp.dot(xn, wgu_ref[:, D_FF + lo:D_FF + lo + FF_HALF], preferred_element_type=F32)
        h_ref[:, lo:lo + FF_HALF] = (_silu(g) * u).astype(BF16)
    y = jnp.dot(h_ref[...], wd_ref[...], preferred_element_type=F32)
    return x + 0.5 * y


def _ffn_kernel(x_ref, nw_ref, wgu_ref, wd_ref, o_ref, h_ref):
    o_ref[...] = _ffn(x_ref[...], nw_ref[...], wgu_ref, wd_ref, h_ref)


def _inproj_kernel(geo, x_ref, nw_ref, wa_ref, wt_ref, wl_ref, wba_ref, cw_ref, lcw_ref, lcb_ref, inj_ref,
                   p_ref, ba_ref, tails_ref, stage_all, carry_scr, wb_scr):
    i = pl.program_id(0)
    nch = INPROJ_TILE // CHUNK
    xn = _rms(x_ref[...], nw_ref[...]).astype(BF16)

    def project_cols(lo, width):
        w_ref, start = (wa_ref, lo) if lo < P_COLS // 2 else (wb_scr, lo - P_COLS // 2)
        return jnp.dot(xn, w_ref[:, start:start + width], preferred_element_type=F32)

    @pl.when(i == 0)
    def _():
        carry_scr[...] = jnp.zeros_like(carry_scr)
        skip = OFF_LX - OFF_BETA
        for c in range(P_COLS // 2 // 1024):
            nxt = wt_ref[:, (c + 1) * 1024:(c + 1) * 1024 + 128] if (c + 1) * 1024 < P_COLS // 2 else wl_ref[...]
            wide = jnp.concatenate([wt_ref[:, c * 1024:(c + 1) * 1024], nxt], axis=1)
            wb_scr[:, c * 1024:(c + 1) * 1024] = pltpu.roll(wide, wide.shape[1] - skip, 1)[:, :1024]

    valid = [_valid_rows(geo.chunk_null(i * nch + j)) for j in range(nch)]
    in_sample = i >= geo.prompt_chunks * CHUNK // INPROJ_TILE
    kinds = (("q", 0, 0), ("k", QK_W, QK_W), ("v", 2 * QK_W, 2 * QK_W), ("x", P_X, QKV_W))
    t0 = TAIL_ROWS
    part_w = 1024 // nch

    def project(kidx, part):
        kind, lo, col = kinds[kidx]
        stage = stage_all.at[kidx]
        pre = project_cols(lo + part * part_w, part_w)
        sc = slice(part * part_w, (part + 1) * part_w)
        cc = slice(col + part * part_w, col + (part + 1) * part_w)
        for j in range(nch):
            stage[j, t0:t0 + CHUNK, sc] = pre[j * CHUNK:(j + 1) * CHUNK]
            if j >= GROUP:
                stage[j, 0:t0, sc] = pre[(j - GROUP + 1) * CHUNK - t0:(j - GROUP + 1) * CHUNK]
            else:
                stage[j, 0:t0, sc] = carry_scr[j, :, cc]
            stage[j, CHUNK:CHUNK + t0, sc] = stage[j, CHUNK:CHUNK + t0, sc] + jnp.where(
                in_sample, inj_ref[t0 * j:t0 * (j + 1), cc], 0.0)
            tails_ref[j, :, cc] = stage[j, CHUNK:CHUNK + t0, sc]
        for g in range(GROUP):
            carry_scr[g, :, cc] = stage[nch - GROUP + g, CHUNK:CHUNK + t0, sc]

    def convolve(kidx, j, half):
        kind, lo, col = kinds[kidx]
        stage = stage_all.at[kidx]
        hs = slice(half * CONV_PART, (half + 1) * CONV_PART)
        taps = lcw_ref[:, hs] if kind == "x" else cw_ref[:, lo + hs.start:lo + hs.stop]
        cur = [stage[j, t0 + v * SLAB:t0 + (v + 1) * SLAB, hs] for v in range(N_SLAB)]
        back = []
        for k in range(CONV_W - 1):
            prev_last = stage[j, (k + 1) * SLAB - 1:(k + 1) * SLAB, hs]
            back.append(_shift_run(cur[N_SLAB - (CONV_W - 1) + k], prev_last))
        conv = []
        for v in range(N_SLAB):
            acc = cur[v] * taps[CONV_W - 1:CONV_W, :]
            for d in range(1, CONV_W):
                src = cur[v - d] if v >= d else back[CONV_W - 1 - d + v]
                acc = acc + src * taps[CONV_W - 1 - d:CONV_W - d, :]
            conv.append(acc)
        conv = jnp.concatenate(conv, axis=0)
        if kind == "x":
            out = conv + lcb_ref[:, hs]
        else:
            act = _silu(conv)
            if kind != "v":
                scale = DK ** -0.5 if kind == "q" else 1.0
                segs = []
                for h in range(CONV_PART // DK):
                    seg = act[:, h * DK:(h + 1) * DK]
                    segs.append(seg * (lax.rsqrt(jnp.sum(seg * seg, axis=-1, keepdims=True) + EPS) * scale))
                act = jnp.concatenate(segs, axis=1)
            out = jnp.where(valid[j], act, 0.0)
        p_ref[j * CHUNK:(j + 1) * CHUNK, lo + hs.start:lo + hs.stop] = out.astype(BF16)

    def plain(lo, part):
        sl = slice(lo + part * part_w, lo + (part + 1) * part_w)
        p_ref[:, sl] = project_cols(sl.start, part_w).astype(BF16)

    plain_cols = (P_Z, P_Y, P_G, P_G + 1024)
    for part in range(nch):
        project(0, part)
    for kidx in range(len(kinds)):
        for j in range(nch):
            plain(plain_cols[kidx], j)
            convolve(kidx, j, 0)
            if kidx + 1 < len(kinds):
                project(kidx + 1, j)
            convolve(kidx, j, 1)
    ba_ref[...] = jnp.dot(xn, wba_ref[...], preferred_element_type=F32)


def _shift_rows(x, d, fill):
    if d % 8 == 0:
        return jnp.concatenate([jnp.full((d, x.shape[1]), fill, x.dtype), x[:x.shape[0] - d]], axis=0)
    rows = lax.broadcasted_iota(jnp.int32, x.shape, 0)
    return jnp.where(rows >= d, pltpu.roll(x, d, 0), fill)


def _outproj_kernel(geo, final, x_ref, g_ref, oa_ref, xl_ref, h0_ref,
                    wri_ref, br_ref, bi_ref, lam_ref, wa_ref, wb_ref, wo_ref,
                    nw_ref, wgu_ref, wd_ref, fn_ref,
                    o_ref, hlast_ref, h_ref, hc_scr, ob_scr):
    i = pl.program_id(0)
    nch = ROW_TILE // CHUNK

    @pl.when(i == 0)
    def _():
        hc_scr[...] = jnp.zeros_like(hc_scr)

    sp = _softplus(-lam_ref[...])
    nulls = [geo.chunk_null(i * nch + j) for j in range(nch)]
    valid = [_valid_rows(n) for n in nulls]
    is_sample = [(i * nch + j) >= geo.prompt_chunks for j in range(nch)]
    for n in range(NB):
        sl = slice(n * BW, (n + 1) * BW)
        gates = jnp.dot(xl_ref[:, sl], wri_ref[n], preferred_element_type=F32)
        carry = [hc_scr[g:g + 1, sl] for g in range(GROUP)]
        for j in range(nch):
            rows = slice(j * CHUNK, (j + 1) * CHUNK)
            xb = xl_ref[rows, sl].astype(F32)
            r = _sigmoid(gates[rows, :BW] + br_ref[:, sl])
            ig = _sigmoid(gates[rows, BW:] + bi_ref[:, sl])
            log_a = -LRU_C * r * sp[:, sl]
            th = jnp.tanh(log_a)
            mult = jnp.sqrt(-2.0 * th / (1.0 - th))
            a = jnp.where(valid[j], jnp.exp(log_a), 1.0)
            b = jnp.where(valid[j], mult * (ig * xb), 0.0)
            gy = _gelu_tanh(xl_ref[rows, LRU_W + n * BW:LRU_W + (n + 1) * BW].astype(F32))
            h_in = jnp.where(nulls[j] > 0, jnp.where(is_sample[j], h0_ref[j:j + 1, sl], 0.0), carry[j % GROUP])
            hs, carry[j % GROUP] = _scan_time(a, b, h_in)
            ob_scr[rows, sl] = (hs * gy).astype(BF16)
            hlast_ref[j:j + 1, sl] = carry[j % GROUP]
        for g in range(GROUP):
            hc_scr[g:g + 1, sl] = carry[g]

    ga = _sigmoid(g_ref[:, :D_MODEL].astype(F32))
    gb = _sigmoid(g_ref[:, D_MODEL:].astype(F32))
    ma = jnp.dot(oa_ref[...], wa_ref[...], preferred_element_type=F32)
    mb = jnp.dot(ob_scr[...], wb_ref[...], preferred_element_type=F32)
    m = (ga * ma + gb * mb).astype(BF16)
    x = x_ref[...] + jnp.dot(m, wo_ref[...], preferred_element_type=F32)
    x = _ffn(x, nw_ref[...], wgu_ref, wd_ref, h_ref)
    if final:
        x = _rms(x, fn_ref[...])
    o_ref[...] = x


def _resident(shape, layer):
    nd = len(shape)
    return pl.BlockSpec((None,) + shape, lambda i: (layer,) + (0,) * nd, pipeline_mode=pl.Buffered(1))


def _params(semantics):
    return pltpu.CompilerParams(dimension_semantics=(semantics,), vmem_limit_bytes=VMEM_LIMIT)


def _ffn_call(layer, x, nw, wgu, wd):
    n = x.shape[0]
    row = lambda i: (i, 0)
    return pl.pallas_call(
        _ffn_kernel,
        grid=(n // ROW_TILE,),
        in_specs=[pl.BlockSpec((ROW_TILE, D_MODEL), row),
                  _resident((1, D_MODEL), layer),
                  _resident((D_MODEL, 2 * D_FF), layer),
                  _resident((D_FF, D_MODEL), layer)],
        out_specs=pl.BlockSpec((ROW_TILE, D_MODEL), row),
        out_shape=jax.ShapeDtypeStruct((n, D_MODEL), F32),
        scratch_shapes=[pltpu.VMEM((ROW_TILE, D_FF), BF16)],
        compiler_params=_params("parallel"),
        name="ffn",
    )(x, nw, wgu, wd)


def _inproj_call(geo, layer, x, nw, w_all, w_last, wba, cw, lcw, lcb, inj):
    n = x.shape[0]
    nch = INPROJ_TILE // CHUNK
    sample_tile0 = geo.prompt_chunks * CHUNK // INPROJ_TILE
    row = lambda i: (i, 0)
    half = P_COLS // 2
    assert OFF_BETA == half and w_all.shape[2] == 2 * half + OFF_LX - OFF_BETA and w_last.shape[2] == 128
    return pl.pallas_call(
        functools.partial(_inproj_kernel, geo),
        grid=(n // INPROJ_TILE,),
        in_specs=[pl.BlockSpec((INPROJ_TILE, D_MODEL), row),
                  _resident((1, D_MODEL), layer),
                  _resident((D_MODEL, half), layer),
                  pl.BlockSpec((None, D_MODEL, half), lambda i: (layer, 0, 1), pipeline_mode=pl.Buffered(1)),
                  _resident((D_MODEL, 128), layer),
                  _resident((D_MODEL, 256), layer),
                  _resident((CONV_W, QKV_W), layer),
                  _resident((CONV_W, LRU_W), layer),
                  _resident((1, LRU_W), layer),
                  pl.BlockSpec((None, TAIL_ROWS * nch, CONV_COLS),
                               lambda i: (layer, jnp.maximum(i - sample_tile0, 0), 0))],
        out_specs=[pl.BlockSpec((INPROJ_TILE, P_COLS), row),
                   pl.BlockSpec((INPROJ_TILE, 256), row),
                   pl.BlockSpec((nch, TAIL_ROWS, CONV_COLS), lambda i: (i, 0, 0))],
        out_shape=[jax.ShapeDtypeStruct((n, P_COLS), BF16),
                   jax.ShapeDtypeStruct((n, 256), F32),
                   jax.ShapeDtypeStruct((n // CHUNK, TAIL_ROWS, CONV_COLS), F32)],
        scratch_shapes=[pltpu.VMEM((4, nch, CHUNK + TAIL_ROWS, 1024), F32),
                        pltpu.VMEM((GROUP, TAIL_ROWS, CONV_COLS), F32),
                        pltpu.VMEM((D_MODEL, half), BF16)],
        compiler_params=_params("arbitrary"),
        name="inproj",
    )(x, nw, w_all, w_all, w_last, wba, cw, lcw, lcb, inj)


def _outproj_call(geo, layer, x, p, oa, h0, wri, br, bi, lam, wa, wb, wo, nw, wgu, wd, fn, final):
    n = x.shape[0]
    nch = ROW_TILE // CHUNK
    sample_tile0 = geo.prompt_chunks * CHUNK // ROW_TILE
    row = lambda i: (i, 0)
    return pl.pallas_call(
        functools.partial(_outproj_kernel, geo, final),
        grid=(n // ROW_TILE,),
        in_specs=[pl.BlockSpec((ROW_TILE, D_MODEL), row),
                  pl.BlockSpec((ROW_TILE, 2 * D_MODEL), lambda i: (i, P_G // 2048)),
                  pl.BlockSpec((ROW_TILE, V_W), row),
                  pl.BlockSpec((ROW_TILE, 2 * LRU_W), lambda i: (i, P_X // 2048)),
                  pl.BlockSpec((None, nch, LRU_W), lambda i: (layer, jnp.maximum(i - sample_tile0, 0), 0)),
                  _resident((NB, BW, 2 * BW), layer),
                  _resident((1, LRU_W), layer),
                  _resident((1, LRU_W), layer),
                  _resident((1, LRU_W), layer),
                  _resident((V_W, D_MODEL), layer),
                  _resident((LRU_W, D_MODEL), layer),
                  _resident((D_MODEL, D_MODEL), layer),
                  _resident((1, D_MODEL), layer),
                  _resident((D_MODEL, 2 * D_FF), layer),
                  _resident((D_FF, D_MODEL), layer),
                  pl.BlockSpec((1, D_MODEL), lambda i: (0, 0), pipeline_mode=pl.Buffered(1))],
        out_specs=[pl.BlockSpec((ROW_TILE, D_MODEL), row),
                   pl.BlockSpec((nch, LRU_W), row)],
        out_shape=[jax.ShapeDtypeStruct((n, D_MODEL), F32),
                   jax.ShapeDtypeStruct((n // CHUNK, LRU_W), F32)],
        scratch_shapes=[pltpu.VMEM((ROW_TILE, D_FF), BF16),
                        pltpu.VMEM((8, LRU_W), F32),
                        pltpu.VMEM((ROW_TILE, LRU_W), BF16)],
        compiler_params=_params("arbitrary"),
        name="outproj_ffn",
    )(x, p, oa, p, h0, wri, br, bi, lam, wa, wb, wo, nw, wgu, wd, fn)


def _unit_lower_inverse(lows, c):
    ti = _row_time(lax.broadcasted_iota(jnp.int32, (c, c), 0))
    tj = _row_time(lax.broadcasted_iota(jnp.int32, (c, c), 1))
    same_block = (ti // INV_BASE) == (tj // INV_BASE)
    eye = (ti == tj).astype(F32)
    ld = [jnp.where(same_block, low, 0.0) for low in lows]
    lo = [low - d for low, d in zip(lows, ld)]
    x = [eye - d for d in ld]
    p = ld
    span = 2
    while span < INV_BASE:
        p = [_mm(t, t) for t in p]
        x = [a + _mm(a, t) for a, t in zip(x, p)]
        span *= 2
    e = [_mm(a, b) for a, b in zip(x, lo)]
    y = [eye - t for t in e]
    q = e
    span = 2
    while span < c // INV_BASE:
        q = [_mm(t, t) for t in q]
        y = [a + _mm(a, t) for a, t in zip(y, q)]
        span *= 2
    return [_mm(a, b) for a, b in zip(y, x)]


def _delta_kernel(geo, qkv_ref, z_ref, ba_ref, s0_ref, alog_ref, dtb_ref, onorm_ref,
                  oa_ref, sout_ref, s_scr):
    c = CHUNK
    i = pl.program_id(0)
    is_prompt = i < geo.prompt_steps
    pos = i % geo.cpp
    first = jnp.logical_or(jnp.logical_not(is_prompt), pos == 0)
    last = jnp.logical_or(jnp.logical_not(is_prompt), pos == geo.cpp - 1)
    null = jnp.where(first, jnp.where(is_prompt, PROMPT_NULL, SAMPLE_NULL), 0)
    valid = _valid_rows(null)

    @pl.when(jnp.logical_and(first, is_prompt))
    def _():
        s_scr[...] = jnp.zeros_like(s_scr)

    @pl.when(jnp.logical_not(is_prompt))
    def _():
        s_scr[...] = s0_ref[...]

    ti = _row_time(lax.broadcasted_iota(jnp.int32, (c, c), 0))
    tj = _row_time(lax.broadcasted_iota(jnp.int32, (c, c), 1))
    tri_incl = ti >= tj
    tri_strict = ti > tj

    beta_all, gc_all, gc_t, eg_all, ekd_all, egl_all = [], [], [], [], [], []
    for s in range(GROUP):
        ba = ba_ref[s * c:(s + 1) * c, :]
        beta_all.append(jnp.where(valid, _sigmoid(ba[:, :128]), 0.0))
        g = -jnp.exp(alog_ref[...]) * _softplus(ba[:, 128:] + dtb_ref[...])
        gc, g_last = _scan_time(None, jnp.where(valid, g, 0.0), jnp.zeros((1, 128), F32))
        gc_all.append(gc)
        gc_t.append(jnp.concatenate([gc, jnp.zeros((128 - c, 128), F32)], axis=0).T)
        eg_all.append(jnp.exp(gc))
        ekd_all.append(jnp.exp(g_last - gc))
        egl_all.append(jnp.exp(g_last))

    probs = [(s, h) for s in range(GROUP) for h in range(HA)]
    rows = lambda s: slice(s * c, (s + 1) * c)
    qb = [qkv_ref[rows(s), h * DK:(h + 1) * DK] for s, h in probs]
    kbf = [qkv_ref[rows(s), QK_W + h * DK:QK_W + (h + 1) * DK] for s, h in probs]
    vb = [qkv_ref[rows(s), 2 * QK_W + h * DV:2 * QK_W + (h + 1) * DV] for s, h in probs]
    q = [t.astype(F32) for t in qb]
    k = [t.astype(F32) for t in kbf]
    beta = [beta_all[s][:, h:h + 1] for s, h in probs]
    eg = [eg_all[s][:, h:h + 1] for s, h in probs]
    n = len(probs)
    kb = [k[p] * beta[p] for p in range(n)]
    sc = [_mm_nt(jnp.concatenate([kb[p].astype(BF16), qb[p]], axis=0), kbf[p]) for p in range(n)]
    decay = [jnp.where(tri_incl, jnp.exp(jnp.where(tri_incl, gc_all[s][:, h:h + 1] - gc_t[s][h:h + 1, :c], 0.0)), 0.0)
             for s, h in probs]
    low = [jnp.where(tri_strict, sc[p][:c] * decay[p], 0.0) for p in range(n)]
    a_intra = [sc[p][c:] * decay[p] for p in range(n)]
    t_inv = _unit_lower_inverse(low, c)
    uw = [_mm(t_inv[p], jnp.concatenate([vb[p].astype(F32) * beta[p], kb[p] * eg[p]], axis=1))
          for p in range(n)]
    s_old = [s_scr[s, h] for s, h in probs]
    wq = [_mm(jnp.concatenate([uw[p][:, DV:], q[p] * eg[p]], axis=0), s_old[p]) for p in range(n)]
    v_new = [uw[p][:, :DV] - wq[p][:c] for p in range(n)]
    o_intra = [_mm(a_intra[p], v_new[p]) for p in range(n)]
    ds = [_mm_tn(k[p] * ekd_all[s][:, h:h + 1], v_new[p]) for p, (s, h) in enumerate(probs)]
    for p, (s, h) in enumerate(probs):
        s_scr[s, h] = s_old[p] * egl_all[s][:, h:h + 1] + ds[p]
        o = wq[p][c:] + o_intra[p]
        zh = z_ref[rows(s), h * DV:(h + 1) * DV].astype(F32)
        o = o * lax.rsqrt(jnp.mean(o * o, axis=-1, keepdims=True) + EPS) * onorm_ref[...] * _silu(zh)
        oa_ref[rows(s), h * DV:(h + 1) * DV] = o.astype(oa_ref.dtype)

    @pl.when(last)
    def _():
        sout_ref[...] = s_scr[...]


def _delta_call(geo, layer, p, ba, s0, alog, dtb, onorm):
    n_pp = geo.n_prompt // GROUP

    def pair(i):
        return jnp.where(i < geo.prompt_steps, i // geo.cpp, i - geo.prompt_steps + n_pp)

    vec = lambda width: pl.BlockSpec((None, 1, width), lambda i: (layer, 0, 0))
    return pl.pallas_call(
        functools.partial(_delta_kernel, geo),
        grid=(geo.steps,),
        in_specs=[pl.BlockSpec((GROUP_ROWS, QKV_W), lambda i: (i, 0)),
                  pl.BlockSpec((GROUP_ROWS, V_W), lambda i: (i, P_Z // V_W)),
                  pl.BlockSpec((GROUP_ROWS, 256), lambda i: (i, 0)),
                  pl.BlockSpec((None, GROUP, HA, DK, DV),
                               lambda i: (layer, jnp.maximum(i - geo.prompt_steps, 0), 0, 0, 0)),
                  vec(128), vec(128), vec(DV)],
        out_specs=[pl.BlockSpec((GROUP_ROWS, V_W), lambda i: (i, 0)),
                   pl.BlockSpec((GROUP, HA, DK, DV), lambda i: (pair(i), 0, 0, 0))],
        out_shape=[jax.ShapeDtypeStruct((geo.rows, V_W), BF16),
                   jax.ShapeDtypeStruct((geo.n_streams, HA, DK, DV), F32)],
        scratch_shapes=[pltpu.VMEM((GROUP, HA, DK, DV), F32)],
        compiler_params=_params("arbitrary"),
        name="delta",
    )(p, p, ba, s0, alog, dtb, onorm)


def _permute_chunk(x):
    return jnp.swapaxes(x.reshape(SLAB, N_SLAB, x.shape[-1]), 0, 1).reshape(CHUNK, x.shape[-1])


def _pack_kernel(geo, xp_ref, xs_ref, meta_ref, o_ref):
    i = pl.program_id(0)
    is_prompt = i < geo.prompt_steps
    pos = i % geo.cpp
    width = o_ref.shape[1]

    @pl.when(jnp.logical_and(is_prompt, pos > 0))
    def _():
        for s in range(GROUP):
            o_ref[s * CHUNK:(s + 1) * CHUNK, :] = _permute_chunk(xp_ref[s].astype(o_ref.dtype))

    @pl.when(jnp.logical_and(is_prompt, pos == 0))
    def _():
        first = _permute_chunk(jnp.concatenate(
            [jnp.zeros((PROMPT_NULL, width), o_ref.dtype), meta_ref[...].astype(o_ref.dtype)], axis=0))
        for s in range(GROUP):
            o_ref[s * CHUNK:(s + 1) * CHUNK, :] = first

    @pl.when(jnp.logical_not(is_prompt))
    def _():
        for s in range(GROUP):
            o_ref[s * CHUNK:(s + 1) * CHUNK, :] = _permute_chunk(jnp.concatenate(
                [jnp.zeros((SAMPLE_NULL, width), o_ref.dtype), xs_ref[s].astype(o_ref.dtype)], axis=0))


def _unpack_kernel(geo, x_ref, yp_ref, ys_ref):
    i = pl.program_id(0)
    is_prompt = i < geo.prompt_steps
    pos = i % geo.cpp

    @pl.when(jnp.logical_and(is_prompt, pos > 0))
    def _():
        for s in range(GROUP):
            yp_ref[s] = _permute_chunk(x_ref[s * CHUNK:(s + 1) * CHUNK, :]).astype(yp_ref.dtype)

    @pl.when(jnp.logical_not(is_prompt))
    def _():
        for s in range(GROUP):
            ys_ref[s] = _permute_chunk(x_ref[s * CHUNK:(s + 1) * CHUNK, :])[SAMPLE_NULL:].astype(ys_ref.dtype)


def _layout_specs(geo):
    def prompt_idx(i):
        step = jnp.minimum(i, geo.prompt_steps - 1)
        return (step // geo.cpp, jnp.maximum(step % geo.cpp - 1, 0), 0)

    def sample_idx(i):
        return (jnp.maximum(i - geo.prompt_steps, 0), 0, 0)

    return (pl.BlockSpec((GROUP, CHUNK, D_MODEL), prompt_idx),
            pl.BlockSpec((GROUP, CHUNK - SAMPLE_NULL, D_MODEL), sample_idx))


def _pack_call(geo, x_prompt, x_sample, meta):
    prompt_spec, sample_spec = _layout_specs(geo)
    return pl.pallas_call(
        functools.partial(_pack_kernel, geo),
        grid=(geo.steps,),
        in_specs=[prompt_spec, sample_spec, pl.BlockSpec((N_META, D_MODEL), lambda i: (0, 0))],
        out_specs=pl.BlockSpec((GROUP_ROWS, D_MODEL), lambda i: (i, 0)),
        out_shape=jax.ShapeDtypeStruct((geo.rows, D_MODEL), F32),
        compiler_params=_params("parallel"),
        name="pack",
    )(x_prompt, x_sample, meta)


def _unpack_call(geo, x, dtype):
    prompt_spec, sample_spec = _layout_specs(geo)
    return pl.pallas_call(
        functools.partial(_unpack_kernel, geo),
        grid=(geo.steps,),
        in_specs=[pl.BlockSpec((GROUP_ROWS, D_MODEL), lambda i: (i, 0))],
        out_specs=[prompt_spec, sample_spec],
        out_shape=[jax.ShapeDtypeStruct((geo.n_prompt, (geo.cpp - 1) * CHUNK, D_MODEL), dtype),
                   jax.ShapeDtypeStruct((geo.n_sample, CHUNK - SAMPLE_NULL, D_MODEL), dtype)],
        compiler_params=_params("arbitrary"),
        name="unpack",
    )(x)


def _pad_lanes(v, width):
    return jnp.pad(v, ((0, 0), (0, width - v.shape[-1])))


def kernel(x_prompt, x_sample, state_delta_S, state_delta_conv, state_lru_h, state_lru_conv,
           meta_tokens, ffn1_norm, ffn1_w_gu, ffn1_w_down, mix_norm, w_in, delta_conv_w,
           delta_A_log, delta_dt_bias, delta_out_norm, lru_conv_w, lru_conv_b, lru_w_r, lru_b_r,
           lru_w_i, lru_b_i, lru_lambda, w_branch_a, w_branch_b, w_out, ffn2_norm, ffn2_w_gu,
           ffn2_w_down, final_norm):
    n_prompt, seq_len, _ = x_prompt.shape
    n_sample, dec_len, _ = x_sample.shape
    assert dec_len == CHUNK - SAMPLE_NULL and (N_META + seq_len) % CHUNK == N_META
    prompt_rows = PROMPT_NULL + N_META + seq_len
    cpp = prompt_rows // CHUNK
    geo = _Geometry(n_prompt, n_sample, cpp)
    assert geo.rows % ROW_TILE == 0 and (geo.prompt_chunks * CHUNK) % ROW_TILE == 0
    assert INPROJ_TILE % GROUP_ROWS == 0 and ROW_TILE % GROUP_ROWS == 0
    dt = x_prompt.dtype

    x = _pack_call(geo, x_prompt, x_sample, meta_tokens)

    cast = lambda w: w.astype(BF16)
    vec = lambda v: v.astype(F32)[:, None, :]
    wgu1, wd1, wgu2, wd2 = cast(ffn1_w_gu), cast(ffn1_w_down), cast(ffn2_w_gu), cast(ffn2_w_down)
    w_all = cast(w_in)
    pad_heads = lambda w: jnp.pad(w, ((0, 0), (0, 0), (0, 128 - w.shape[-1])))
    w_last = cast(pad_heads(w_in[:, :, P_COLS:]))
    w_ba = cast(jnp.concatenate([pad_heads(w_in[:, :, OFF_BETA:OFF_ALPHA]),
                                 pad_heads(w_in[:, :, OFF_ALPHA:OFF_LX])], axis=-1))
    w_ri = cast(jnp.concatenate([lru_w_r, lru_w_i], axis=-1))
    wa, wb, wo = cast(w_branch_a), cast(w_branch_b), cast(w_out)
    norm1, norm_mix, norm2 = vec(ffn1_norm), vec(mix_norm), vec(ffn2_norm)
    alog, dtb, onorm = vec(_pad_lanes(delta_A_log, 128)), vec(_pad_lanes(delta_dt_bias, 128)), vec(delta_out_norm)
    lcb, b_r, b_i, lam = vec(lru_conv_b), vec(lru_b_r), vec(lru_b_i), vec(lru_lambda)
    cw, lcw = delta_conv_w.astype(F32), lru_conv_w.astype(F32)
    s0, h0 = state_delta_S.astype(F32), state_lru_h.astype(F32)

    inj = jnp.concatenate([state_delta_conv, state_lru_conv], axis=-1).astype(F32)
    null_runs = SAMPLE_NULL // N_SLAB
    inj = jnp.pad(inj[:, :, :, None, :], ((0, 0), (0, 0), (0, 0), (null_runs - 1, SLAB - null_runs), (0, 0)))
    inj = inj.reshape(DEPTH, n_sample * TAIL_ROWS, CONV_COLS)

    last_ids = geo.last_chunk_ids()
    outs_s, outs_tail, outs_h = [], [], []
    for l in range(DEPTH):
        x = _ffn_call(l, x, norm1, wgu1, wd1)
        p, ba, tails = _inproj_call(geo, l, x, norm_mix, w_all, w_last, w_ba, cw, lcw, lcb, inj)
        oa, s_new = _delta_call(geo, l, p, ba, s0, alog, dtb, onorm)
        x, hlast = _outproj_call(geo, l, x, p, oa, h0, w_ri, b_r, b_i, lam, wa, wb, wo, norm2, wgu2, wd2,
                                 final_norm[None].astype(F32), final=(l == DEPTH - 1))
        outs_s.append(s_new)
        outs_tail.append(tails[last_ids, SLAB - 1::SLAB, :])
        outs_h.append(hlast[last_ids])

    n_prompt_rows = geo.prompt_chunks * CHUNK
    y_prompt, y_sample = _unpack_call(geo, x, dt)
    s_all = jnp.stack(outs_s)
    tail_all = jnp.stack(outs_tail)
    cq_all = tail_all[..., :QKV_W]
    cx_all = tail_all[..., QKV_W:]
    h_all = jnp.stack(outs_h)
    return (y_prompt.astype(dt), y_sample.astype(dt),
            s_all[:, :n_prompt].astype(dt), cq_all[:, :n_prompt].astype(dt),
            h_all[:, :n_prompt].astype(dt), cx_all[:, :n_prompt].astype(dt),
            s_all[:, n_prompt:].astype(state_delta_S.dtype),
            cq_all[:, n_prompt:].astype(state_delta_conv.dtype),
            h_all[:, n_prompt:].astype(state_lru_h.dtype),
            cx_all[:, n_prompt:].astype(state_lru_conv.dtype))
```
